```python
import math
import jax, jax.numpy as jnp
from jax import lax
import numpy as np

D_MODEL = 1024
BATCH = 2
SEQ = 8192
DEPTH = 2
DEC_BATCH = 128
DEC_SEQ = 1
PAST_LEN = 2048
PAGE_SIZE = 128

N_EVEN = (DEPTH + 1) // 2
N_ODD = DEPTH // 2
HEAD_DIM = 64
D_A = D_MODEL // 2
H_A = D_A // HEAD_DIM
D_B = D_MODEL // 2
H_B = D_B // HEAD_DIM
LORA_W = 64
LORA_A = 64
LORA_G = 128
SHIFT_W = 3 * D_A + LORA_W + LORA_A + LORA_G
H_IDX = 8
D_IDX = 64
TOPK_MAX = 256
Q_BLOCK = 128
IN_W = SHIFT_W + 3 * D_B + H_IDX * D_IDX + D_IDX + H_IDX
N_BUCKETS = 32
MAX_DIST = 128
CH_G = 16
G_C = D_MODEL // CH_G
P_C = 64
D_FF = 2816
N_EXP = 8
TOP_E = 2
D_FF_E = 3584
EPS = 1e-6
LNX_EPS = 64e-5

kernel_name = "hybrid_rwkv7_dsa_s5_decoder_step"


def _rmsnorm(x, g):
    xf = x.astype(jnp.float32)
    return xf * lax.rsqrt(jnp.mean(xf * xf, axis=-1, keepdims=True) + EPS) * g


def _modulate(h, shift, scale):
    return h * (1.0 + scale[:, None, :]) + shift[:, None, :]


def _swiglu(h, w13, w2):
    g, u = jnp.split(h @ w13, 2, axis=-1)
    return (jax.nn.silu(g) * u) @ w2


def _moe(h, router_w, router_b, w13, w2):
    logits = (h @ router_w).astype(jnp.float32) + router_b
    top_v, top_i = lax.top_k(logits, TOP_E)
    gates = jax.nn.softmax(top_v, axis=-1)
    dense_gate = jnp.sum(jax.nn.one_hot(top_i, N_EXP, dtype=jnp.float32) * gates[..., None], axis=-2)
    out = jnp.zeros(h.shape[:-1] + (w2.shape[-1],), jnp.float32)
    for e in range(N_EXP):
        out = out + dense_gate[..., e:e + 1] * _swiglu(h, w13[e], w2[e])
    return out


def _rows(a, i):
    return a[i]


def _rel_bucket(dist):
    max_exact = N_BUCKETS // 2
    n = jnp.maximum(dist, 0)
    nf = jnp.maximum(n, 1).astype(jnp.float32)
    large = max_exact + (jnp.log(nf / max_exact) / math.log(MAX_DIST / max_exact) * (N_BUCKETS - max_exact)).astype(jnp.int32)
    return jnp.where(n < max_exact, n, jnp.minimum(large, N_BUCKETS - 1))


def _rwkv7(p, prev, s0, mu, w0, w2, a0, a2, g2, k_k, k_a, r_k, lnx_w, lnx_b):
    n, t, _ = p.shape
    p_prev = jnp.concatenate([prev[:, None, :].astype(p.dtype), p[:, :-1]], axis=1)
    ps = p + (p_prev - p) * mu
    r, k, v, xw, xa, xg = jnp.split(ps, [D_A, 2 * D_A, 3 * D_A, 3 * D_A + LORA_W, 3 * D_A + LORA_W + LORA_A], axis=-1)
    w_log = -jax.nn.softplus(-(w0 + jnp.tanh(xw) @ w2)) - 0.5
    decay = jnp.exp(-jnp.exp(w_log.astype(jnp.float32)))
    a = jax.nn.sigmoid(a0 + xa @ a2)
    g = jax.nn.sigmoid(xg) @ g2

    def hd(z):
        return z.reshape(n, t, H_A, HEAD_DIM).astype(jnp.float32)

    kk = hd(k * k_k)
    kk = kk / jnp.maximum(jnp.sqrt(jnp.sum(kk * kk, axis=-1, keepdims=True)), 1e-12)
    k_mod = hd(k * (1.0 + (a - 1.0) * k_a))
    r_h, v_h, w_h, a_h = hd(r), hd(v), hd(decay), hd(a)
    a_vec = -kk
    b_vec = kk * a_h

    def step(s, inp):
        r_t, w_t, k_t, v_t, av_t, bv_t = inp
        sa = jnp.einsum('nhij,nhj->nhi', s, av_t)
        s = s * w_t[:, :, None, :] + sa[..., None] * bv_t[:, :, None, :] + v_t[..., None] * k_t[:, :, None, :]
        return s, jnp.einsum('nhij,nhj->nhi', s, r_t)

    xs = tuple(jnp.moveaxis(z, 1, 0) for z in (r_h, w_h, k_mod, v_h, a_vec, b_vec))
    s_fin, o = lax.scan(step, s0.astype(jnp.float32), xs)
    o = jnp.moveaxis(o, 0, 1)
    mean = jnp.mean(o, axis=-1, keepdims=True)
    var = jnp.mean(jnp.square(o - mean), axis=-1, keepdims=True)
    o = ((o - mean) * lax.rsqrt(var + LNX_EPS)).reshape(n, t, D_A) * lnx_w + lnx_b
    bonus = jnp.sum(r_h * k_mod * r_k, axis=-1, keepdims=True) * v_h
    o = (o + bonus.reshape(n, t, D_A)) * g
    return o, p[:, -1], s_fin


def _indexer_scores(q_idx, w_idx, k_idx):
    s = jnp.einsum('nqhd,nld->nqhl', q_idx, k_idx).astype(jnp.float32) * (D_IDX ** -0.5)
    return jnp.einsum('nqh,nqhl->nql', w_idx.astype(jnp.float32) * (H_IDX ** -0.5), jax.nn.relu(s))


def _select_and_attend(q, qpos, scores, gather, rel_bias, topk):
    L = scores.shape[-1]
    kpos = jnp.arange(L, dtype=jnp.int32)
    admissible = kpos[None, None, :] <= qpos[None, :, None]
    _, idx = lax.top_k(jnp.where(admissible, scores, -jnp.inf), topk)
    k_sel, v_sel = gather(idx)
    logits = jnp.einsum('nqhd,nqkhd->nqhk', q, k_sel).astype(jnp.float32) * (HEAD_DIM ** -0.5)
    dist = qpos[None, :, None] - idx
    logits = logits + jnp.moveaxis(rel_bias[_rel_bucket(dist)], -1, -2)
    logits = jnp.where((dist >= 0)[:, :, None, :], logits, -jnp.inf)
    prob = jax.nn.softmax(logits, axis=-1)
    return jnp.einsum('nqhk,nqkhd->nqhd', prob, v_sel.astype(jnp.float32))


def _dsa_prompt(q, k, v, q_idx, k_idx, w_idx, rel_bias):
    n, t = q.shape[:2]
    nb = t // Q_BLOCK
    topk = min(TOPK_MAX, t // 4)

    def blk(z):
        return jnp.moveaxis(z.reshape((n, nb, Q_BLOCK) + z.shape[2:]), 1, 0)

    qpos = jnp.arange(t, dtype=jnp.int32).reshape(nb, Q_BLOCK)

    def gather(idx):
        return jax.vmap(_rows)(k, idx), jax.vmap(_rows)(v, idx)

    def one(args):
        qb, qib, wb, pb = args
        return _select_and_attend(qb, pb, _indexer_scores(qib, wb, k_idx), gather, rel_bias, topk)

    o = lax.map(one, (blk(q), blk(q_idx), blk(w_idx), qpos))
    return jnp.moveaxis(o, 0, 1).reshape(n, t, D_B)


def _dsa_sample(q, k_new, v_new, q_idx, k_idx_new, w_idx, ck, cv, ckidx, page_table, rel_bias):
    n, t = q.shape[:2]
    past = page_table.shape[1] * PAGE_SIZE
    topk = min(TOPK_MAX, (past + t) // 4)
    kidx_past = ckidx[page_table].reshape(n, past, D_IDX)
    k_idx_all = jnp.concatenate([kidx_past.astype(k_idx_new.dtype), k_idx_new], axis=1)
    qpos = past + jnp.arange(t, dtype=jnp.int32)

    def gather(idx):
        pidx = jnp.minimum(idx, past - 1)
        phys = jax.vmap(_rows)(page_table, pidx // PAGE_SIZE)
        off = pidx % PAGE_SIZE
        nidx = jnp.clip(idx - past, 0, t - 1)
        is_new = (idx >= past)[..., None, None]
        k_sel = jnp.where(is_new, jax.vmap(_rows)(k_new, nidx), ck[phys, off])
        v_sel = jnp.where(is_new, jax.vmap(_rows)(v_new, nidx), cv[phys, off])
        return k_sel, v_sel

    o = _select_and_attend(q, qpos, _indexer_scores(q_idx, w_idx, k_idx_all), gather, rel_bias, topk)
    return o.reshape(n, t, D_B)


def _even_mix(h, shift_prev, wkv0, attn_fn, w_in, mu, w0, w2, a0, a2, g2, k_k, k_a, r_k, lnx_w, lnx_b, w_out):
    n, t, _ = h.shape
    p = h @ w_in
    c1 = SHIFT_W + 3 * D_B
    p_a, q, k, v, qi, ki, wi = jnp.split(p, [SHIFT_W, SHIFT_W + D_B, SHIFT_W + 2 * D_B, c1, c1 + H_IDX * D_IDX, c1 + H_IDX * D_IDX + D_IDX], axis=-1)
    o_a, shift_new, wkv_new = _rwkv7(p_a, shift_prev, wkv0, mu, w0, w2, a0, a2, g2, k_k, k_a, r_k, lnx_w, lnx_b)
    qh = q.reshape(n, t, H_B, HEAD_DIM)
    kh = k.reshape(n, t, H_B, HEAD_DIM)
    vh = v.reshape(n, t, H_B, HEAD_DIM)
    o_b = attn_fn(qh, kh, vh, qi.reshape(n, t, H_IDX, D_IDX), ki, wi)
    y = jnp.concatenate([o_a, o_b], axis=-1) @ w_out
    return y, shift_new, wkv_new, kh, vh, ki


def _cplx_combine(e1, e2):
    a1r, a1i, b1r, b1i = e1
    a2r, a2i, b2r, b2i = e2
    return (a2r * a1r - a2i * a1i, a2r * a1i + a2i * a1r,
            a2r * b1r - a2i * b1i + b2r, a2r * b1i + a2i * b1r + b2i)


def _s5(u, h0_re, h0_im, a_re, a_im, log_step, b_re, b_im, c_re, c_im, d, w_glu):
    n, t, _ = u.shape
    uf = u.astype(jnp.float32).reshape(n, t, G_C, CH_G)
    dt = jnp.exp(log_step.astype(jnp.float32))[:, None]
    mag = jnp.exp(dt * a_re)
    ab_re = mag * jnp.cos(dt * a_im)
    ab_im = mag * jnp.sin(dt * a_im)
    den = a_re * a_re + a_im * a_im
    nr = ab_re - 1.0
    co_re = (nr * a_re + ab_im * a_im) / den
    co_im = (ab_im * a_re - nr * a_im) / den
    bu_re = jnp.einsum('gpc,ntgc->ntgp', b_re, uf)
    bu_im = jnp.einsum('gpc,ntgc->ntgp', b_im, uf)
    bb_re = co_re * bu_re - co_im * bu_im
    bb_im = co_re * bu_im + co_im * bu_re
    aa_re = jnp.broadcast_to(ab_re, (1, t, G_C, P_C))
    aa_im = jnp.broadcast_to(ab_im, (1, t, G_C, P_C))
    ca_re, ca_im, hs_re, hs_im = lax.associative_scan(_cplx_combine, (aa_re, aa_im, bb_re, bb_im), axis=1)
    h0r = h0_re.astype(jnp.float32)[:, None]
    h0i = h0_im.astype(jnp.float32)[:, None]
    h_re = hs_re + ca_re * h0r - ca_im * h0i
    h_im = hs_im + ca_re * h0i + ca_im * h0r
    y = jnp.einsum('gcp,ntgp->ntgc', c_re, h_re) - jnp.einsum('gcp,ntgp->ntgc', c_im, h_im)
    y = y.reshape(n, t, D_MODEL) + d * u
    zl, zr = jnp.split(jax.nn.gelu(y) @ w_glu, 2, axis=-1)
    return zl * jax.nn.sigmoid(zr), h_re[:, -1], h_im[:, -1]


def setup_inputs(seed: int = 0) -> dict:
    key = jax.random.key(seed)
    ks = iter(jax.random.split(key, 64))

    def nrm(shape, scale=1.0):
        return scale * jax.random.normal(next(ks), shape, jnp.float32)

    def uni(shape, lo, hi):
        return jax.random.uniform(next(ks), shape, jnp.float32, lo, hi)

    n_pages = PAST_LEN // PAGE_SIZE
    n_phys = (DEC_BATCH * n_pages * 5) // 4
    page_table = jax.random.permutation(next(ks), n_phys)[:DEC_BATCH * n_pages].reshape(DEC_BATCH, n_pages).astype(jnp.int32)
    a_im0 = jnp.broadcast_to(jnp.pi * jnp.arange(P_C, dtype=jnp.float32), (N_ODD, G_C, P_C))
    return {
        "x_prompt": nrm((BATCH, SEQ, D_MODEL)),
        "x_sample": nrm((DEC_BATCH, DEC_SEQ, D_MODEL)),
        "cache_k": nrm((N_EVEN, n_phys, PAGE_SIZE, H_B, HEAD_DIM)),
        "cache_v": nrm((N_EVEN, n_phys, PAGE_SIZE, H_B, HEAD_DIM)),
        "cache_kidx": nrm((N_EVEN, n_phys, PAGE_SIZE, D_IDX)),
        "state_shift": nrm((N_EVEN, DEC_BATCH, SHIFT_W)),
        "state_wkv": nrm((N_EVEN, DEC_BATCH, H_A, HEAD_DIM, HEAD_DIM), 0.5),
        "state_ssm_re": nrm((N_ODD, DEC_BATCH, G_C, P_C), 0.3),
        "state_ssm_im": nrm((N_ODD, DEC_BATCH, G_C, P_C), 0.3),
        "page_table": page_table,
        "c_prompt": nrm((BATCH, D_MODEL)),
        "c_sample": nrm((DEC_BATCH, D_MODEL)),
        "norm_mix": 1.0 + nrm((DEPTH, D_MODEL), 0.02),
        "norm_ffn": 1.0 + nrm((DEPTH, D_MODEL), 0.02),
        "ada_w": nrm((DEPTH, D_MODEL, 6 * D_MODEL), 0.5 * D_MODEL ** -0.5),
        "ada_b": nrm((DEPTH, 6 * D_MODEL), 0.02),
        "rel_bias": nrm((N_BUCKETS, H_B), 0.3),
        "norm_final": 1.0 + nrm((D_MODEL,), 0.02),
        "e_w_in": nrm((N_EVEN, D_MODEL, IN_W), D_MODEL ** -0.5),
        "e_mu": uni((N_EVEN, SHIFT_W), 0.0, 1.0),
        "e_w0": uni((N_EVEN, D_A), -6.0, 1.0),
        "e_w2": nrm((N_EVEN, LORA_W, D_A), 0.1 * LORA_W ** -0.5),
        "e_a0": nrm((N_EVEN, D_A), 0.1),
        "e_a2": nrm((N_EVEN, LORA_A, D_A), 0.1 * LORA_A ** -0.5),
        "e_g2": nrm((N_EVEN, LORA_G, D_A), LORA_G ** -0.5),
        "e_k_k": 0.85 + nrm((N_EVEN, D_A), 0.02),
        "e_k_a": 1.0 + nrm((N_EVEN, D_A), 0.02),
        "e_r_k": nrm((N_EVEN, H_A, HEAD_DIM), 0.1),
        "e_lnx_w": 1.0 + nrm((N_EVEN, D_A), 0.02),
        "e_lnx_b": nrm((N_EVEN, D_A), 0.02),
        "e_w_out": nrm((N_EVEN, D_A + D_B, D_MODEL), (D_A + D_B) ** -0.5),
        "e_ffn_w13": nrm((N_EVEN, D_MODEL, 2 * D_FF), D_MODEL ** -0.5),
        "e_ffn_w2": nrm((N_EVEN, D_FF, D_MODEL), D_FF ** -0.5),
        "o_a_re": -0.5 + nrm((N_ODD, G_C, P_C), 0.01),
        "o_a_im": a_im0 + nrm((N_ODD, G_C, P_C), 0.01),
        "o_log_step": uni((N_ODD, G_C), math.log(0.001), math.log(0.1)),
        "o_b_re": nrm((N_ODD, G_C, P_C, CH_G), (2 * CH_G) ** -0.5),
        "o_b_im": nrm((N_ODD, G_C, P_C, CH_G), (2 * CH_G) ** -0.5),
        "o_c_re": nrm((N_ODD, G_C, CH_G, P_C), P_C ** -0.5),
        "o_c_im": nrm((N_ODD, G_C, CH_G, P_C), P_C ** -0.5),
        "o_d": nrm((N_ODD, D_MODEL)),
        "o_w_glu": nrm((N_ODD, D_MODEL, 2 * D_MODEL), D_MODEL ** -0.5),
        "o_router_w": nrm((N_ODD, D_MODEL, N_EXP), D_MODEL ** -0.5),
        "o_router_b": nrm((N_ODD, N_EXP), 0.01),
        "o_moe_w13": nrm((N_ODD, N_EXP, D_MODEL, 2 * D_FF_E), D_MODEL ** -0.5),
        "o_moe_w2": nrm((N_ODD, N_EXP, D_FF_E, D_MODEL), D_FF_E ** -0.5),
    }


def reference(x_prompt, x_sample, cache_k, cache_v, cache_kidx, state_shift, state_wkv,
              state_ssm_re, state_ssm_im, page_table, c_prompt, c_sample,
              norm_mix, norm_ffn, ada_w, ada_b, rel_bias, norm_final,
              e_w_in, e_mu, e_w0, e_w2, e_a0, e_a2, e_g2, e_k_k, e_k_a, e_r_k,
              e_lnx_w, e_lnx_b, e_w_out, e_ffn_w13, e_ffn_w2,
              o_a_re, o_a_im, o_log_step, o_b_re, o_b_im, o_c_re, o_c_im, o_d, o_w_glu,
              o_router_w, o_router_b, o_moe_w13, o_moe_w2):

    def run(x, c, shift0, wkv0, ssm0_re, ssm0_im, attn_for_layer):
        out_dtype = x.dtype
        n_shift, n_wkv, n_k, n_v, n_kidx, n_re, n_im = [], [], [], [], [], [], []
        for l in range(DEPTH):
            sh_m, sc_m, g_m, sh_f, sc_f, g_f = jnp.split(jax.nn.silu(c) @ ada_w[l] + ada_b[l], 6, axis=-1)
            h = _modulate(_rmsnorm(x, norm_mix[l]), sh_m, sc_m)
            i = l // 2
            if l % 2 == 0:
                y, s_new, w_new, kr, vr, kir = _even_mix(
                    h, shift0[i], wkv0[i], attn_for_layer(i), e_w_in[i], e_mu[i], e_w0[i], e_w2[i],
                    e_a0[i], e_a2[i], e_g2[i], e_k_k[i], e_k_a[i], e_r_k[i], e_lnx_w[i], e_lnx_b[i], e_w_out[i])
                n_shift.append(s_new); n_wkv.append(w_new); n_k.append(kr); n_v.append(vr); n_kidx.append(kir)
                x = x + g_m[:, None, :] * y
                h = _modulate(_rmsnorm(x, norm_ffn[l]), sh_f, sc_f)
                x = x + g_f[:, None, :] * _swiglu(h, e_ffn_w13[i], e_ffn_w2[i])
            else:
                y, hr, hi = _s5(h, ssm0_re[i], ssm0_im[i], o_a_re[i], o_a_im[i], o_log_step[i],
                                o_b_re[i], o_b_im[i], o_c_re[i], o_c_im[i], o_d[i], o_w_glu[i])
                n_re.append(hr); n_im.append(hi)
                x = x + g_m[:, None, :] * y
                h = _modulate(_rmsnorm(x, norm_ffn[l]), sh_f, sc_f)
                x = x + g_f[:, None, :] * _moe(h, o_router_w[i], o_router_b[i], o_moe_w13[i], o_moe_w2[i])
        y_out = _rmsnorm(x, norm_final).astype(out_dtype)
        return (y_out, jnp.stack(n_shift), jnp.stack(n_wkv), jnp.stack(n_k), jnp.stack(n_v),
                jnp.stack(n_kidx), jnp.stack(n_re), jnp.stack(n_im))

    def prompt_attn(i):
        def fn(q, k, v, qi, ki, wi):
            return _dsa_prompt(q, k, v, qi, ki, wi, rel_bias)
        return fn

    def sample_attn(i):
        def fn(q, k, v, qi, ki, wi):
            return _dsa_sample(q, k, v, qi, ki, wi, cache_k[i], cache_v[i], cache_kidx[i], page_table, rel_bias)
        return fn

    nb = x_prompt.shape[0]
    zeros_shift = jnp.zeros((N_EVEN, nb, SHIFT_W), x_prompt.dtype)
    zeros_wkv = jnp.zeros((N_EVEN, nb, H_A, HEAD_DIM, HEAD_DIM), jnp.float32)
    zeros_ssm = jnp.zeros((N_ODD, nb, G_C, P_C), jnp.float32)
    y_prompt, p_shift, p_wkv, p_k, p_v, p_kidx, p_ssm_re, p_ssm_im = run(
        x_prompt, c_prompt, zeros_shift, zeros_wkv, zeros_ssm, zeros_ssm, prompt_attn)
    y_sample, s_shift, s_wkv, s_k, s_v, s_kidx, s_ssm_re, s_ssm_im = run(
        x_sample, c_sample, state_shift, state_wkv, state_ssm_re, state_ssm_im, sample_attn)
    return (y_prompt, y_sample, p_shift, p_wkv, p_k, p_v, p_kidx, p_ssm_re, p_ssm_im,
            s_shift, s_wkv, s_k, s_v, s_kidx, s_ssm_re, s_ssm_im)
```

```python
import functools
import math

import jax
import jax.numpy as jnp
from jax import lax
from jax.experimental import pallas as pl
from jax.experimental.pallas import tpu as pltpu

HEAD_DIM = 64
LORA_W = 64
LORA_A = 64
LORA_G = 128
H_IDX = 8
D_IDX = 64
TOPK_MAX = 256
N_BUCKETS = 32
MAX_DIST = 128
CH_G = 16
P_C = 64
TOP_E = 2
EPS = 1e-6
LNX_EPS = 64e-5

LANES = 128
SUBLANES = 8
VMEM_LIMIT = 56 * 1024 * 1024
KEY_BLOCK = 512
NEG = -2.0e30
M_INIT = -1.0e30

_bf16 = jnp.bfloat16
_f32 = jnp.float32


def _bf(x):
    return x.astype(_bf16)


def _dot(a, b):
    return jnp.dot(a, b, preferred_element_type=_f32)


def _dot_nt(a, b):
    return lax.dot_general(a, b, (((1,), (1,)), ((), ())), preferred_element_type=_f32)


def _params(*sem):
    return pltpu.CompilerParams(dimension_semantics=sem, vmem_limit_bytes=VMEM_LIMIT)


def _split_bf16(x, parts):
    out = []
    for _ in range(parts - 1):
        hi = _bf(x)
        out.append(hi)
        x = x - hi.astype(_f32)
    out.append(_bf(x))
    return out


def _segsum(x, ones_bd):
    hi, lo = _split_bf16(x, 2)
    return _dot(hi, ones_bd) + _dot(lo, ones_bd)


def _norm_mod(x, g, shift, scale):
    ms = jnp.mean(x * x, axis=-1, keepdims=True)
    return (x * lax.rsqrt(ms + EPS) * g) * (1.0 + scale) + shift


def _silu(x):
    return x * jax.nn.sigmoid(x)


def _gelu_tanh(x):
    return 0.5 * x * (1.0 + jnp.tanh(math.sqrt(2.0 / math.pi) * (x + 0.044715 * (x * x * x))))


def _softplus(x):
    return jnp.maximum(x, 0.0) + jnp.log(1.0 + jnp.exp(-jnp.abs(x)))


def _rel_bucket(dist):
    max_exact = N_BUCKETS // 2
    n = jnp.maximum(dist, 0)
    nf = jnp.maximum(n, 1).astype(_f32)
    large = max_exact + (jnp.log(nf / max_exact) / math.log(MAX_DIST / max_exact) * (N_BUCKETS - max_exact)).astype(jnp.int32)
    return jnp.where(n < max_exact, n, jnp.minimum(large, N_BUCKETS - 1))


def _group_spec(rows_per_block, rows_per_group, width):
    def spec(r):
        return pl.BlockSpec((1, r, width), lambda i, *_: ((i * rows_per_block) // rows_per_group, 0, 0))
    return spec


def _ada_body(c_ref, w_ref, b_ref, o_ref):
    o_ref[0] = _dot(_bf(_silu(c_ref[...])), _bf(w_ref[0])) + b_ref[0]


def _ada(c, ada_w, ada_b):
    depth, d, n6 = ada_w.shape
    rows = c.shape[0]
    tn = n6 // 4
    return pl.pallas_call(
        _ada_body,
        grid=(depth, n6 // tn),
        in_specs=[pl.BlockSpec((rows, d), lambda l, j: (0, 0)),
                  pl.BlockSpec((1, d, tn), lambda l, j: (l, 0, j)),
                  pl.BlockSpec((1, 1, tn), lambda l, j: (l, 0, j))],
        out_specs=pl.BlockSpec((1, rows, tn), lambda l, j: (l, 0, j)),
        out_shape=jax.ShapeDtypeStruct((depth, rows, n6), _f32),
        compiler_params=_params("arbitrary", "arbitrary"),
        name="ada_mod",
    )(c, ada_w, ada_b.reshape(depth, 1, n6))


def _norm_mod_body(x_ref, g_ref, sh_ref, sc_ref, o_ref):
    o_ref[...] = _norm_mod(x_ref[...], g_ref[...], sh_ref[0], sc_ref[0])


def _norm_mod_call(x, g, shift, scale, tm, rows_per_group):
    m, d = x.shape
    gs = _group_spec(tm, rows_per_group, d)
    return pl.pallas_call(
        _norm_mod_body,
        grid=(m // tm,),
        in_specs=[pl.BlockSpec((tm, d), lambda i: (i, 0)),
                  pl.BlockSpec((1, d), lambda i: (0, 0)),
                  gs(shift.shape[1]), gs(scale.shape[1])],
        out_specs=pl.BlockSpec((tm, d), lambda i: (i, 0)),
        out_shape=jax.ShapeDtypeStruct((m, d), _f32),
        compiler_params=_params("arbitrary"),
        name="norm_mod",
    )(x, g, shift, scale)


def _in_proj_body(x_ref, g_ref, sh_ref, sc_ref, wa_ref, wq_ref, wk_ref, wv_ref, wqi_ref, wkw_ref,
                  pa_ref, q_ref, k_ref, v_ref, qi_ref, ki_ref, wi_ref):
    h = _bf(_norm_mod(x_ref[...], g_ref[...], sh_ref[0], sc_ref[0]))
    pa_ref[...] = _dot(h, wa_ref[...])
    q_ref[...] = _bf(_dot(h, wq_ref[...]))
    k_ref[...] = _dot(h, wk_ref[...])
    v_ref[...] = _dot(h, wv_ref[...])
    qi_ref[...] = _bf(_dot(h, wqi_ref[...]))
    kw = _dot(h, wkw_ref[...])
    ki_ref[...] = kw[:, :D_IDX]
    wi_ref[...] = kw[:, D_IDX:D_IDX + H_IDX]


def _in_proj(x, g, shift, scale, w_in, tm, rows_per_group):
    m, d = x.shape
    d_a = d // 2
    d_b = d // 2
    shift_w = 3 * d_a + LORA_W + LORA_A + LORA_G
    c1 = shift_w + 3 * d_b
    c2 = c1 + H_IDX * D_IDX
    wa = _bf(w_in[:, :shift_w])
    wq = _bf(w_in[:, shift_w:shift_w + d_b])
    wk = _bf(w_in[:, shift_w + d_b:shift_w + 2 * d_b])
    wv = _bf(w_in[:, shift_w + 2 * d_b:c1])
    wqi = _bf(w_in[:, c1:c2])
    wkw = _bf(jnp.pad(w_in[:, c2:], ((0, 0), (0, LANES - D_IDX - H_IDX))))
    gs = _group_spec(tm, rows_per_group, d)
    full = lambda a: pl.BlockSpec(a.shape, lambda i: (0, 0))
    row = lambda w: pl.BlockSpec((tm, w), lambda i: (i, 0))
    widths = [(shift_w, _f32), (d_b, _bf16), (d_b, _f32), (d_b, _f32), (H_IDX * D_IDX, _bf16),
              (D_IDX, _f32), (H_IDX, _f32)]
    return pl.pallas_call(
        _in_proj_body,
        grid=(m // tm,),
        in_specs=[row(d), pl.BlockSpec((1, d), lambda i: (0, 0)), gs(shift.shape[1]), gs(scale.shape[1]),
                  full(wa), full(wq), full(wk), full(wv), full(wqi), full(wkw)],
        out_specs=[row(w) for w, _ in widths],
        out_shape=[jax.ShapeDtypeStruct((m, w), dt) for w, dt in widths],
        compiler_params=_params("arbitrary"),
        name="in_proj",
    )(x, g, shift, scale, wa, wq, wk, wv, wqi, wkw)


def _rwkv_prep_body(seq_is_one, rows_per_group, tm,
                    p_ref, pprev_ref, prev0_ref, mu_ref, w0_ref, a0_ref, kk_ref, ka_ref, rk_ref,
                    wwa_ref, g2_ref, ones_ref,
                    r_ref, w_ref, k_ref, v_ref, av_ref, bv_ref, g_ref, bonus_ref):
    d_a = r_ref.shape[1]
    p = p_ref[...]
    if seq_is_one:
        p_prev = prev0_ref[0]
    else:
        i = pl.program_id(0)
        first = (i * tm) % rows_per_group == 0
        prev_row = jnp.where(first, prev0_ref[0], pprev_ref[SUBLANES - 1:SUBLANES, :])
        rolled = pltpu.roll(p, 1, axis=0)
        row_id = lax.broadcasted_iota(jnp.int32, p.shape, 0)
        p_prev = jnp.where(row_id == 0, prev_row, rolled)
    ps = p + (p_prev - p) * mu_ref[...]
    r = ps[:, :d_a]
    k = ps[:, d_a:2 * d_a]
    v = ps[:, 2 * d_a:3 * d_a]
    xwa = ps[:, 3 * d_a:3 * d_a + LORA_W + LORA_A]
    xg = ps[:, 3 * d_a + LORA_W + LORA_A:]
    lane = lax.broadcasted_iota(jnp.int32, xwa.shape, 1)
    xwa = jnp.where(lane < LORA_W, jnp.tanh(xwa), xwa)
    lwa = _dot(_bf(xwa), wwa_ref[...])
    w_log = -_softplus(-(w0_ref[...] + lwa[:, :d_a])) - 0.5
    decay = jnp.exp(-jnp.exp(w_log))
    a = jax.nn.sigmoid(a0_ref[...] + lwa[:, d_a:])
    g_ref[...] = _dot(_bf(jax.nn.sigmoid(xg)), g2_ref[...])
    ones_bd = ones_ref[...]
    kk = k * kk_ref[...]
    kk = kk / jnp.maximum(jnp.sqrt(_segsum(kk * kk, ones_bd)), 1e-12)
    k_mod = k * (1.0 + (a - 1.0) * ka_ref[...])
    r_ref[...] = r
    w_ref[...] = decay
    k_ref[...] = k_mod
    v_ref[...] = v
    av_ref[...] = -kk
    bv_ref[...] = kk * a
    bonus_ref[...] = _segsum(r * k_mod * rk_ref[...], ones_bd) * v


def _ones_blockdiag(n, group):
    idx = jnp.arange(n, dtype=jnp.int32) // group
    return (idx[:, None] == idx[None, :]).astype(_bf16)


def _rwkv_prep(p_a, prev0, mu, w0, w2, a0, a2, g2, k_k, k_a, r_k, tm, rows_per_group):
    m, shift_w = p_a.shape
    d_a = w0.shape[-1]
    seq_is_one = rows_per_group == tm and prev0.shape[1] == tm
    wwa = jnp.zeros((LORA_W + LORA_A, 2 * d_a), _f32)
    wwa = _bf(wwa.at[:LORA_W, :d_a].set(w2).at[LORA_W:, d_a:].set(a2))
    ones_bd = _ones_blockdiag(d_a, HEAD_DIM)
    vec = lambda a: a.reshape(1, -1)
    gs = _group_spec(tm, rows_per_group, shift_w)
    full = lambda a: pl.BlockSpec(a.shape, lambda i: (0, 0))
    row = pl.BlockSpec((tm, d_a), lambda i: (i, 0))
    ins = [p_a, p_a, prev0, vec(mu), vec(w0), vec(a0), vec(k_k), vec(k_a), vec(r_k), wwa, _bf(g2), ones_bd]
    in_specs = [pl.BlockSpec((tm, shift_w), lambda i: (i, 0)),
                pl.BlockSpec((SUBLANES, shift_w), lambda i: (jnp.maximum(i * (tm // SUBLANES) - 1, 0), 0)),
                gs(prev0.shape[1])] + [full(a) for a in ins[3:]]
    return pl.pallas_call(
        functools.partial(_rwkv_prep_body, seq_is_one, rows_per_group, tm),
        grid=(m // tm,),
        in_specs=in_specs,
        out_specs=[row] * 8,
        out_shape=[jax.ShapeDtypeStruct((m, d_a), _f32)] * 8,
        compiler_params=_params("arbitrary"),
        name="rwkv_prep",
    )(*ins)


def _rwkv_scan_body(nb, n_pairs, tc,
                    r_ref, w_ref, k_ref, v_ref, av_ref, bv_ref, s0_ref, ones2_ref, ones3_ref, eye_ref,
                    o_ref, s_ref):
    t_blk = pl.program_id(1)

    @pl.when(t_blk == 0)
    def _():
        s_ref[...] = s0_ref[...]

    ones2 = ones2_ref[...]
    ones3 = ones3_ref[...]
    eye = eye_ref[...] > 0.5
    units = [(n, hp) for n in range(nb) for hp in range(n_pairs)]

    def rowsum(xs, ones, parts):
        lhs = jnp.concatenate([jnp.concatenate(_split_bf16(x, parts), axis=1) for x in xs], axis=0)
        res = _dot(lhs, ones)
        return [res[u * HEAD_DIM:(u + 1) * HEAD_DIM] for u in range(len(xs))]

    sub = min(SUBLANES, tc)
    row_id = lax.broadcasted_iota(jnp.int32, (sub, LANES), 0)

    def tile_steps(j, carry):
        t0 = pl.multiple_of(j * sub, sub)
        tiles = {}
        for (n, hp) in units:
            sl = pl.ds(hp * LANES, LANES)
            tiles[(n, hp)] = tuple(ref[n, pl.ds(t0, sub), sl] for ref in (r_ref, w_ref, k_ref, v_ref, av_ref, bv_ref))
        states = [s_ref[n, hp] for (n, hp) in units]
        o_tiles = [jnp.zeros((sub, LANES), _f32) for _ in units]
        for tt in range(sub):
            rows = {u: tuple(x[tt:tt + 1, :] for x in tiles[u]) for u in units}
            sa = rowsum([s * rows[u][4] for s, u in zip(states, units)], ones2, 2)
            vdiag = [jnp.where(eye, jnp.broadcast_to(rows[u][3], (HEAD_DIM, LANES)), 0.0) for u in units]
            vcol = rowsum(vdiag, ones3, 3)
            for idx, u in enumerate(units):
                r_t, w_t, k_t, v_t, av_t, bv_t = rows[u]
                states[idx] = states[idx] * w_t + sa[idx] * bv_t + vcol[idx] * k_t
            ob = rowsum([s * rows[u][0] for s, u in zip(states, units)], ones2, 2)
            for idx in range(len(units)):
                o_row = jnp.sum(jnp.where(eye, ob[idx], 0.0), axis=0, keepdims=True)
                o_tiles[idx] = jnp.where(row_id == tt, jnp.broadcast_to(o_row, (sub, LANES)), o_tiles[idx])
        for idx, (n, hp) in enumerate(units):
            s_ref[n, hp] = states[idx]
            o_ref[n, pl.ds(t0, sub), pl.ds(hp * LANES, LANES)] = o_tiles[idx]
        return carry

    lax.fori_loop(0, tc // sub, tile_steps, 0)


def _rwkv_scan(r, w, k, v, av, bv, s0, nb, tc):
    n, t, d_a = r.shape
    n_pairs = d_a // LANES
    half = (jnp.arange(LANES, dtype=jnp.int32) // HEAD_DIM)
    ones_blk = (half[:, None] == half[None, :]).astype(_bf16)
    ones2 = jnp.concatenate([ones_blk] * 2, axis=0)
    ones3 = jnp.concatenate([ones_blk] * 3, axis=0)
    eye = (jnp.arange(HEAD_DIM, dtype=jnp.int32)[:, None]
           == (jnp.arange(LANES, dtype=jnp.int32) % HEAD_DIM)[None, :]).astype(_f32)
    seq = pl.BlockSpec((nb, tc, d_a), lambda b, j: (b, j, 0))
    st = pl.BlockSpec((nb, n_pairs, HEAD_DIM, LANES), lambda b, j: (b, 0, 0, 0))
    full = lambda a: pl.BlockSpec(a.shape, lambda b, j: (0, 0))
    return pl.pallas_call(
        functools.partial(_rwkv_scan_body, nb, n_pairs, tc),
        grid=(n // nb, t // tc),
        in_specs=[seq] * 6 + [st, full(ones2), full(ones3), full(eye)],
        out_specs=[seq, st],
        out_shape=[jax.ShapeDtypeStruct((n, t, d_a), _f32),
                   jax.ShapeDtypeStruct(s0.shape, _f32)],
        compiler_params=_params("arbitrary", "arbitrary"),
        name="rwkv_scan",
    )(r, w, k, v, av, bv, s0, ones2, ones3, eye)


def _pack_state(s):
    n, h = s.shape[:2]
    return s.reshape(n, h // 2, 2, HEAD_DIM, HEAD_DIM).transpose(0, 1, 3, 2, 4).reshape(n, h // 2, HEAD_DIM, LANES)


def _unpack_state(s):
    n, hp = s.shape[:2]
    return s.reshape(n, hp, HEAD_DIM, 2, HEAD_DIM).transpose(0, 1, 3, 2, 4).reshape(n, hp * 2, HEAD_DIM, HEAD_DIM)


def _mix_out_body(o_ref, bonus_ref, g_ref, ob_ref, x_ref, gate_ref, lw_ref, lb_ref, ones_ref, wa_ref, wb_ref,
                  out_ref):
    ones_bd = ones_ref[...]
    o = o_ref[...]
    inv = 1.0 / HEAD_DIM
    mean = _segsum(o, ones_bd) * inv
    dlt = o - mean
    var = _segsum(dlt * dlt, ones_bd) * inv
    on = dlt * lax.rsqrt(var + LNX_EPS) * lw_ref[...] + lb_ref[...]
    oa = (on + bonus_ref[...]) * g_ref[...]
    y = _dot(_bf(oa), wa_ref[...]) + _dot(ob_ref[...], wb_ref[...])
    out_ref[...] = x_ref[...] + gate_ref[0] * y


def _mix_out(o, bonus, g, o_b, x, gate, lnx_w, lnx_b, w_out, tm, rows_per_group):
    m, d = x.shape
    d_a = o.shape[1]
    ones_bd = _ones_blockdiag(d_a, HEAD_DIM)
    wa = _bf(w_out[:d_a])
    wb = _bf(w_out[d_a:])
    gs = _group_spec(tm, rows_per_group, d)
    full = lambda a: pl.BlockSpec(a.shape, lambda i: (0, 0))
    rowa = pl.BlockSpec((tm, d_a), lambda i: (i, 0))
    rowd = pl.BlockSpec((tm, d), lambda i: (i, 0))
    lw = lnx_w.reshape(1, -1)
    lb = lnx_b.reshape(1, -1)
    return pl.pallas_call(
        _mix_out_body,
        grid=(m // tm,),
        in_specs=[rowa, rowa, rowa, pl.BlockSpec((tm, o_b.shape[1]), lambda i: (i, 0)), rowd, gs(gate.shape[1]),
                  full(lw), full(lb), full(ones_bd), full(wa), full(wb)],
        out_specs=rowd,
        out_shape=jax.ShapeDtypeStruct((m, d), _f32),
        compiler_params=_params("arbitrary"),
        name="mix_out",
    )(o, bonus, g, o_b, x, gate, lw, lb, ones_bd, wa, wb)


def _key_to_float(u):
    key = u ^ jnp.int32(-2147483648)
    bits = jnp.where(key >= 0, key, key ^ jnp.int32(0x7FFFFFFF))
    return lax.bitcast_convert_type(bits, _f32)


def _kth_largest(count_ge, k_row, shape):
    def body(it, u):
        bit = jnp.left_shift(jnp.int32(1), 31 - it)
        cand_u = u | bit
        ok = count_ge(_key_to_float(cand_u)) >= k_row
        return jnp.where(ok, cand_u, u)

    u = lax.fori_loop(0, 32, body, jnp.zeros(shape, jnp.int32))
    return _key_to_float(u)


def _tie_cut(count_eq_lt, budget, n_bits, shape):
    def body(it, c):
        cand = c | jnp.left_shift(jnp.int32(1), n_bits - 1 - it)
        ok = count_eq_lt(cand) <= budget
        return jnp.where(ok, cand, c)

    return lax.fori_loop(0, n_bits, body, jnp.zeros(shape, jnp.int32))


def _dsa_prompt_body(n_heads, k_sel, t_len,
                     far_ref, q_ref, qi_ref, wi_ref, kt_ref, v_ref, kit_ref, near_ref,
                     o_ref, sc_ref, cut_ref, m_ref, l_ref, acc_ref):
    qb = q_ref.shape[1]
    i = pl.program_id(1)
    q0 = i * qb
    qpos = q0 + lax.broadcasted_iota(jnp.int32, (qb, 1), 0)
    n_kb = (q0 + qb + KEY_BLOCK - 1) // KEY_BLOCK
    lane_kb = lax.broadcasted_iota(jnp.int32, (qb, KEY_BLOCK), 1)

    w8 = wi_ref[0] * (H_IDX ** -0.5)
    qi = qi_ref[0]

    def score_block(kb, carry):
        c0 = pl.multiple_of(kb * KEY_BLOCK, KEY_BLOCK)
        kit = kit_ref[0, :, pl.ds(c0, KEY_BLOCK)]
        acc = jnp.zeros((qb, KEY_BLOCK), _f32)
        for h in range(H_IDX):
            s = _dot(qi[:, h * D_IDX:(h + 1) * D_IDX], kit) * (D_IDX ** -0.5)
            acc = acc + w8[:, h:h + 1] * jnp.maximum(s, 0.0)
        sc_ref[:, pl.ds(c0, KEY_BLOCK)] = jnp.where(c0 + lane_kb <= qpos, acc, -jnp.inf)
        return carry

    lax.fori_loop(0, n_kb, score_block, 0)

    def count(pred):
        def body(kb, acc):
            c0 = pl.multiple_of(kb * KEY_BLOCK, KEY_BLOCK)
            hit = jnp.where(pred(sc_ref[:, pl.ds(c0, KEY_BLOCK)], c0 + lane_kb), 1.0, 0.0)
            for j in range(KEY_BLOCK // LANES):
                acc = acc + hit[:, j * LANES:(j + 1) * LANES]
            return acc
        acc = lax.fori_loop(0, n_kb, body, jnp.zeros((qb, LANES), _f32))
        return jnp.sum(acc, axis=1, keepdims=True)

    k_row = jnp.minimum(k_sel, qpos + 1).astype(_f32)
    thr = _kth_largest(lambda cand: count(lambda s, pos: s >= cand), k_row, (qb, 1))
    n_gt = count(lambda s, pos: s > thr)
    n_eq = count(lambda s, pos: s == thr)
    budget = k_row - n_gt
    cut_ref[...] = jnp.full((qb, 1), 2 * t_len, jnp.int32)

    @pl.when(jnp.max(n_eq - budget) > 0.5)
    def _():
        cut_ref[...] = _tie_cut(lambda c: count(lambda s, pos: (s == thr) & (pos < c)), budget,
                                (2 * t_len).bit_length(), (qb, 1))

    cut = cut_ref[...]

    m_ref[...] = jnp.full(m_ref.shape, M_INIT, _f32)
    l_ref[...] = jnp.zeros(l_ref.shape, _f32)
    acc_ref[...] = jnp.zeros(acc_ref.shape, _f32)
    q = q_ref[0]
    scale = HEAD_DIM ** -0.5

    def attend(c0, width, bias_of_head):
        scb = sc_ref[:, pl.ds(c0, width)]
        pos = c0 + lax.broadcasted_iota(jnp.int32, (qb, width), 1)
        sel = (scb > thr) | ((scb == thr) & (pos < cut))
        for h in range(n_heads):
            kt = kt_ref[0, h * HEAD_DIM:(h + 1) * HEAD_DIM, pl.ds(c0, width)]
            s = _dot(q[:, h * HEAD_DIM:(h + 1) * HEAD_DIM], kt) * scale + bias_of_head(h)
            s = jnp.where(sel, s, NEG)
            m_old = m_ref[h]
            m_new = jnp.maximum(m_old, jnp.max(s, axis=1, keepdims=True))
            alpha = jnp.exp(m_old - m_new)
            p = jnp.exp(s - m_new)
            l_ref[h] = alpha * l_ref[h] + jnp.sum(p, axis=1, keepdims=True)
            hp = h // 2
            pv = _dot(_bf(p), v_ref[0, pl.ds(c0, width), hp * LANES:(hp + 1) * LANES])
            acc_ref[h] = alpha * acc_ref[h] + pv
            m_ref[h] = m_new

    n_far = jnp.maximum(i - 1, 0) // (KEY_BLOCK // qb)

    def far_block(kb, carry):
        attend(pl.multiple_of(kb * KEY_BLOCK, KEY_BLOCK), KEY_BLOCK, lambda h: far_ref[h])
        return carry

    lax.fori_loop(0, n_far, far_block, 0)

    def near_block(j, carry):
        attend(pl.multiple_of(j * qb, qb), qb, lambda h: near_ref[i - j, h])
        return carry

    lax.fori_loop(n_far * (KEY_BLOCK // qb), i + 1, near_block, 0)

    lane = lax.broadcasted_iota(jnp.int32, (qb, LANES), 1)
    for hp in range(n_heads // 2):
        lo = acc_ref[2 * hp] / l_ref[2 * hp]
        hi = acc_ref[2 * hp + 1] / l_ref[2 * hp + 1]
        o_ref[0, :, hp * LANES:(hp + 1) * LANES] = _bf(jnp.where(lane < HEAD_DIM, lo, hi))


def _dsa_prompt(q, k, v, qi, ki, wi, rel_bias, qb):
    n, t, d_b = q.shape
    n_heads = d_b // HEAD_DIM
    k_sel = min(TOPK_MAX, t // 4)
    kt = _bf(k).transpose(0, 2, 1)
    kit = _bf(ki).transpose(0, 2, 1)
    vb = _bf(v)
    n_near = KEY_BLOCK // qb + 1
    qq = jnp.arange(qb, dtype=jnp.int32)
    dist = (jnp.arange(n_near, dtype=jnp.int32)[:, None, None] * qb + qq[None, :, None] - qq[None, None, :])
    near = jnp.moveaxis(rel_bias[_rel_bucket(dist)], -1, 1)
    far = rel_bias[N_BUCKETS - 1]
    res = lambda shape: pl.BlockSpec(shape, lambda b, i, *_: (b, 0, 0))
    blk = lambda w: pl.BlockSpec((1, qb, w), lambda b, i, *_: (b, i, 0))
    grid_spec = pltpu.PrefetchScalarGridSpec(
        num_scalar_prefetch=0,
        grid=(n, t // qb),
        in_specs=[pl.BlockSpec(memory_space=pltpu.SMEM),
                  blk(d_b), blk(H_IDX * D_IDX), blk(H_IDX),
                  res((1, d_b, t)), res((1, t, d_b)), res((1, D_IDX, t)),
                  pl.BlockSpec(near.shape, lambda b, i, *_: (0, 0, 0, 0))],
        out_specs=blk(d_b),
        scratch_shapes=[pltpu.VMEM((qb, t), _f32), pltpu.VMEM((qb, 1), jnp.int32),
                        pltpu.VMEM((n_heads, qb, 1), _f32), pltpu.VMEM((n_heads, qb, 1), _f32),
                        pltpu.VMEM((n_heads, qb, LANES), _f32)],
    )
    return pl.pallas_call(
        functools.partial(_dsa_prompt_body, n_heads, k_sel, t),
        grid_spec=grid_spec,
        out_shape=jax.ShapeDtypeStruct((n, t, d_b), _bf16),
        compiler_params=_params("arbitrary", "arbitrary"),
        name="dsa_prompt",
    )(far, q, qi, wi, kt, vb, kit, near)


def _dsa_sample_body(n_heads, k_sel, n_pages, page,
                     pt_ref, q_ref, qi_ref, wi_ref, kn_ref, vn_ref, kin_ref, ck_ref, cv_ref, cki_ref,
                     btab_ref, bnew_ref, o_ref, sc_ref, lg_ref, vs_ref):
    p = pl.program_id(1)
    past = n_pages * page
    d_b = n_heads * HEAD_DIM
    c0 = pl.multiple_of(p * page, page)
    w8 = wi_ref[0] * (H_IDX ** -0.5)
    qi8 = qi_ref[0]
    head_row = lax.broadcasted_iota(jnp.int32, (n_heads, d_b), 0)
    head_lane = lax.broadcasted_iota(jnp.int32, (n_heads, d_b), 1) // HEAD_DIM
    q_blk = jnp.where(head_row == head_lane, jnp.broadcast_to(q_ref[0].astype(_f32), (n_heads, d_b)), 0.0)

    s8 = _dot_nt(qi8, _bf(cki_ref[0])) * (D_IDX ** -0.5)
    sc_ref[:, pl.ds(c0, page)] = jnp.sum(w8 * jnp.maximum(s8, 0.0), axis=0, keepdims=True)
    lg = _dot_nt(_bf(q_blk), _bf(ck_ref[0])) * (HEAD_DIM ** -0.5)
    lg_ref[:, pl.ds(c0, page)] = lg + btab_ref[:, pl.ds(c0, page)]
    vs_ref[pl.ds(c0, page), :] = _bf(cv_ref[0])

    @pl.when(p == n_pages - 1)
    def _():
        sc = sc_ref[...]
        s_new8 = jnp.sum(qi8.astype(_f32) * _bf(kin_ref[0]).astype(_f32), axis=1, keepdims=True) * (D_IDX ** -0.5)
        s_new = jnp.sum(w8 * jnp.maximum(s_new8, 0.0), axis=0, keepdims=True)
        pos = lax.broadcasted_iota(jnp.int32, sc.shape, 1)

        def count(pred):
            hits = jnp.sum(jnp.where(pred(sc, pos), 1.0, 0.0), axis=1, keepdims=True)
            return hits + jnp.where(pred(s_new, past), 1.0, 0.0)

        k_row = jnp.full((1, 1), float(k_sel), _f32)
        thr = _kth_largest(lambda cand: count(lambda s, ps: s >= cand), k_row, (1, 1))
        budget = k_row - count(lambda s, ps: s > thr)
        cut = _tie_cut(lambda c: count(lambda s, ps: (s == thr) & (ps < c)), budget,
                       (2 * (past + 1)).bit_length(), (1, 1))
        sel = (sc > thr) | ((sc == thr) & (pos < cut))
        sel_new = (s_new > thr) | ((s_new == thr) & (past < cut))

        kn = _bf(kn_ref[0]).astype(_f32)
        lg_new = jnp.sum(_bf(q_blk).astype(_f32) * kn, axis=1, keepdims=True) * (HEAD_DIM ** -0.5) + bnew_ref[...]
        lg_new = jnp.where(sel_new, lg_new, NEG)
        s_all = jnp.where(sel, lg_ref[...], NEG)
        m = jnp.maximum(jnp.maximum(jnp.max(s_all, axis=1, keepdims=True), lg_new), M_INIT)
        pr = jnp.exp(s_all - m)
        pr_new = jnp.exp(lg_new - m)
        l = jnp.sum(pr, axis=1, keepdims=True) + pr_new
        out8 = (_dot(_bf(pr), vs_ref[...]) + pr_new * _bf(vn_ref[0]).astype(_f32)) / l
        o_ref[0] = _bf(jnp.sum(jnp.where(head_row == head_lane, out8, 0.0), axis=0, keepdims=True))


def _dsa_sample(q, k_new, v_new, qi, ki_new, wi, ck, cv, cki, page_table, rel_bias):
    b, _, d_b = q.shape
    n_heads = d_b // HEAD_DIM
    n_pages = page_table.shape[1]
    page = ck.shape[1]
    past = n_pages * page
    k_sel = min(TOPK_MAX, (past + 1) // 4)
    kpos = jnp.arange(past, dtype=jnp.int32)
    btab = rel_bias[_rel_bucket(past - kpos)].T
    bnew = rel_bias[_rel_bucket(jnp.zeros((1,), jnp.int32))].T
    qi8 = qi.reshape(b, H_IDX, D_IDX)
    wi8 = wi.reshape(b, H_IDX, 1)
    row = lambda w: pl.BlockSpec((1, 1, w), lambda i, p, pt: (i, 0, 0))
    paged = lambda w: pl.BlockSpec((1, page, w), lambda i, p, pt: (pt[i * n_pages + p], 0, 0))
    grid_spec = pltpu.PrefetchScalarGridSpec(
        num_scalar_prefetch=1,
        grid=(b, n_pages),
        in_specs=[row(d_b),
                  pl.BlockSpec((1, H_IDX, D_IDX), lambda i, p, pt: (i, 0, 0)),
                  pl.BlockSpec((1, H_IDX, 1), lambda i, p, pt: (i, 0, 0)),
                  row(d_b), row(d_b), row(D_IDX),
                  paged(d_b), paged(d_b), paged(D_IDX),
                  pl.BlockSpec(btab.shape, lambda i, p, pt: (0, 0)),
                  pl.BlockSpec(bnew.shape, lambda i, p, pt: (0, 0))],
        out_specs=row(d_b),
        scratch_shapes=[pltpu.VMEM((1, past), _f32), pltpu.VMEM((n_heads, past), _f32),
                        pltpu.VMEM((past, d_b), _bf16)],
    )
    return pl.pallas_call(
        functools.partial(_dsa_sample_body, n_heads, k_sel, n_pages, page),
        grid_spec=grid_spec,
        out_shape=jax.ShapeDtypeStruct((b, 1, d_b), _bf16),
        compiler_params=_params("arbitrary", "arbitrary"),
        name="dsa_sample",
    )(page_table.reshape(-1), q, qi8, wi8, k_new, v_new, ki_new, ck, cv, cki, btab, bnew)


def _ffn_body(routed, final_norm, n_exp,
              x_ref, g_ref, sh_ref, sc_ref, gate_ref, rw_ref, rb_ref, w1_ref, w3_ref, w2_ref, gf_ref,
              o_ref, h_ref, acc_ref, dg_ref):
    e = pl.program_id(1)
    f = pl.program_id(2)
    first = (e == 0) & (f == 0)
    last = (e == pl.num_programs(1) - 1) & (f == pl.num_programs(2) - 1)

    @pl.when(first)
    def _():
        h = _norm_mod(x_ref[...], g_ref[...], sh_ref[0], sc_ref[0])
        h_ref[...] = _bf(h)
        acc_ref[...] = jnp.zeros(acc_ref.shape, _f32)
        if routed:
            logits = _dot(_bf(h), rw_ref[...]) + rb_ref[...]
            lane = lax.broadcasted_iota(jnp.int32, logits.shape, 1).astype(_f32)
            logits = jnp.where(lane < n_exp, logits, -jnp.inf)
            v1 = jnp.max(logits, axis=1, keepdims=True)
            i1 = jnp.min(jnp.where(logits == v1, lane, float(LANES)), axis=1, keepdims=True)
            rest = jnp.where(lane == i1, -jnp.inf, logits)
            v2 = jnp.max(rest, axis=1, keepdims=True)
            i2 = jnp.min(jnp.where(rest == v2, lane, float(LANES)), axis=1, keepdims=True)
            e2 = jnp.exp(v2 - v1)
            den = 1.0 + e2
            dg_ref[...] = jnp.where(lane == i1, 1.0 / den, 0.0) + jnp.where(lane == i2, e2 / den, 0.0)

    h = h_ref[...]
    a = _silu(_dot(h, w1_ref[0])) * _dot(h, w3_ref[0])
    if routed:
        lane = lax.broadcasted_iota(jnp.int32, dg_ref.shape, 1)
        a = a * jnp.sum(jnp.where(lane == e, dg_ref[...], 0.0), axis=1, keepdims=True)
    acc_ref[...] += _dot(_bf(a), w2_ref[0])

    @pl.when(last)
    def _():
        y = x_ref[...] + gate_ref[0] * acc_ref[...]
        if final_norm:
            y = y * lax.rsqrt(jnp.mean(y * y, axis=-1, keepdims=True) + EPS) * gf_ref[...]
        o_ref[...] = y


def _ffn(x, g, shift, scale, gate, w13, w2, tm, tf, rows_per_group, router=None, final_g=None):
    m, d = x.shape
    n_exp, ff, _ = w2.shape
    routed = router is not None
    final_norm = final_g is not None
    if routed:
        rw, rb = router
        rw = _bf(jnp.pad(rw, ((0, 0), (0, LANES - n_exp))))
        rb = jnp.pad(rb.reshape(1, -1), ((0, 0), (0, LANES - n_exp)))
    else:
        rw = jnp.zeros((d, LANES), _bf16)
        rb = jnp.zeros((1, LANES), _f32)
    gf = final_g.reshape(1, -1) if final_norm else jnp.ones((1, d), _f32)
    nf = ff // tf
    gs = _group_spec(tm, rows_per_group, d)
    rowd = pl.BlockSpec((tm, d), lambda i, e, f: (i, 0))
    full = lambda a: pl.BlockSpec(a.shape, lambda i, e, f: (0, 0))
    return pl.pallas_call(
        functools.partial(_ffn_body, routed, final_norm, n_exp),
        grid=(m // tm, n_exp, nf),
        in_specs=[rowd, full(g), gs(shift.shape[1]), gs(scale.shape[1]), gs(gate.shape[1]), full(rw), full(rb),
                  pl.BlockSpec((1, d, tf), lambda i, e, f: (e, 0, f)),
                  pl.BlockSpec((1, d, tf), lambda i, e, f: (e, 0, f + nf)),
                  pl.BlockSpec((1, tf, d), lambda i, e, f: (e, f, 0)),
                  full(gf)],
        out_specs=rowd,
        out_shape=jax.ShapeDtypeStruct((m, d), _f32),
        scratch_shapes=[pltpu.VMEM((tm, d), _bf16), pltpu.VMEM((tm, d), _f32), pltpu.VMEM((tm, LANES), _f32)],
        compiler_params=_params("arbitrary", "arbitrary", "arbitrary"),
        name="moe_ffn" if routed else "dense_ffn",
    )(x, g, shift, scale, gate, rw, rb, w13, w13, w2, gf)


def _cmul(ar, ai, br, bi):
    return ar * br - ai * bi, ar * bi + ai * br


def _s5_io(u, b_ref, c_ref, d_ref, h_of_bb):
    half = b_ref.shape[2] // 2
    bb = _dot(_bf(u), b_ref[0])
    h_re, h_im = h_of_bb(bb[:, :half], bb[:, half:])
    y = _dot(_bf(jnp.concatenate([h_re, h_im], axis=1)), c_ref[0]) + d_ref[0] * u
    return _bf(_gelu_tanh(y)), h_re, h_im


def _s5_scan_body(tc, u_ref, b_ref, c_ref, d_ref, are_ref, aim_ref, h0r_ref, h0i_ref,
                  y_ref, hr_ref, hi_ref, xr_ref, xi_ref):
    t_blk = pl.program_id(2)

    @pl.when(t_blk == 0)
    def _():
        hr_ref[0, 0] = h0r_ref[0, 0]
        hi_ref[0, 0] = h0i_ref[0, 0]

    width = are_ref.shape[2]
    a1 = (jnp.broadcast_to(are_ref[0], (SUBLANES, width)), jnp.broadcast_to(aim_ref[0], (SUBLANES, width)))
    a2 = _cmul(*a1, *a1)
    a3 = _cmul(*a2, *a1)
    a4 = _cmul(*a2, *a2)
    a5 = _cmul(*a4, *a1)
    a6 = _cmul(*a4, *a2)
    a7 = _cmul(*a4, *a3)
    a8 = _cmul(*a4, *a4)
    row = lax.broadcasted_iota(jnp.int32, (SUBLANES, width), 0)
    lvl = [tuple(jnp.where(row >= s, c, 0.0) for c in a) for s, a in ((1, a1), (2, a2), (4, a4))]
    pw = []
    for comp in range(2):
        acc = a8[comp]
        for s, a in ((6, a7), (5, a6), (4, a5), (3, a4), (2, a3), (1, a2), (0, a1)):
            acc = jnp.where(row == s, a[comp], acc)
        pw.append(acc)

    def scan_chunk(bb_re, bb_im):
        xr_ref[...] = bb_re
        xi_ref[...] = bb_im

        def tile(j, carry):
            cr, ci = carry
            r0 = pl.multiple_of(j * SUBLANES, SUBLANES)
            xr = xr_ref[pl.ds(r0, SUBLANES), :]
            xi = xi_ref[pl.ds(r0, SUBLANES), :]
            for (s, (fr, fi)) in zip((1, 2, 4), lvl):
                sr = pltpu.roll(xr, s, axis=0)
                si = pltpu.roll(xi, s, axis=0)
                pr, pi = _cmul(fr, fi, sr, si)
                xr = xr + pr
                xi = xi + pi
            pr, pi = _cmul(pw[0], pw[1], jnp.broadcast_to(cr, xr.shape), jnp.broadcast_to(ci, xi.shape))
            xr = xr + pr
            xi = xi + pi
            xr_ref[pl.ds(r0, SUBLANES), :] = xr
            xi_ref[pl.ds(r0, SUBLANES), :] = xi
            return xr[SUBLANES - 1:SUBLANES], xi[SUBLANES - 1:SUBLANES]

        cr, ci = lax.fori_loop(0, tc // SUBLANES, tile, (hr_ref[0, 0], hi_ref[0, 0]))
        hr_ref[0, 0] = cr
        hi_ref[0, 0] = ci
        return xr_ref[...], xi_ref[...]

    y, _, _ = _s5_io(u_ref[0], b_ref, c_ref, d_ref, scan_chunk)
    y_ref[0] = y


def _s5_step_body(u_ref, b_ref, c_ref, d_ref, are_ref, aim_ref, h0r_ref, h0i_ref, y_ref, hr_ref, hi_ref):
    def one_step(bb_re, bb_im):
        pr, pi = _cmul(are_ref[0], aim_ref[0], h0r_ref[0], h0i_ref[0])
        return pr + bb_re, pi + bb_im

    y, h_re, h_im = _s5_io(u_ref[...], b_ref, c_ref, d_ref, one_step)
    y_ref[...] = y
    hr_ref[0] = h_re
    hi_ref[0] = h_im


def _s5_tables(a_re, a_im, log_step, b_re, b_im, c_re, c_im, d):
    g_c, p_c, ch = b_re.shape
    gs = LANES // ch
    ns = g_c // gs
    dt = jnp.exp(log_step.astype(_f32))[:, None]
    mag = jnp.exp(dt * a_re)
    ab_re = mag * jnp.cos(dt * a_im)
    ab_im = mag * jnp.sin(dt * a_im)
    den = a_re * a_re + a_im * a_im
    nr = ab_re - 1.0
    co_re = (nr * a_re + ab_im * a_im) / den
    co_im = (ab_im * a_re - nr * a_im) / den
    bf_re = co_re[..., None] * b_re - co_im[..., None] * b_im
    bf_im = co_re[..., None] * b_im + co_im[..., None] * b_re
    eye = jnp.eye(gs, dtype=_f32)

    def pack_b(b):
        b = b.reshape(ns, gs, p_c, ch)
        return jnp.einsum('sgpc,gh->sgchp', b, eye).reshape(ns, gs * ch, gs * p_c)

    def pack_c(c):
        c = c.reshape(ns, gs, ch, p_c)
        return jnp.einsum('sgcp,gh->sgphc', c, eye).reshape(ns, gs * p_c, gs * ch)

    b_pack = _bf(jnp.concatenate([pack_b(bf_re), pack_b(bf_im)], axis=2))
    c_pack = _bf(jnp.concatenate([pack_c(c_re), -pack_c(c_im)], axis=1))
    vec = lambda a: a.reshape(ns, 1, gs * p_c)
    return b_pack, c_pack, d.reshape(ns, 1, gs * ch), vec(ab_re), vec(ab_im)


def _s5_scan(u, h0_re, h0_im, tables, tc):
    n, t, d = u.shape
    b_pack, c_pack, dvec, are, aim = tables
    ns, _, w2 = b_pack.shape
    w = w2 // 2
    h0r = h0_re.reshape(n, ns, 1, w)
    h0i = h0_im.reshape(n, ns, 1, w)
    tab = lambda a: pl.BlockSpec((1,) + a.shape[1:], lambda b, s, j: (s, 0, 0))
    st = pl.BlockSpec((1, 1, 1, w), lambda b, s, j: (b, s, 0, 0))
    seq = pl.BlockSpec((1, tc, LANES), lambda b, s, j: (b, j, s))
    y, hr, hi = pl.pallas_call(
        functools.partial(_s5_scan_body, tc),
        grid=(n, ns, t // tc),
        in_specs=[seq, tab(b_pack), tab(c_pack), tab(dvec), tab(are), tab(aim), st, st],
        out_specs=[seq, st, st],
        out_shape=[jax.ShapeDtypeStruct((n, t, d), _bf16),
                   jax.ShapeDtypeStruct((n, ns, 1, w), _f32), jax.ShapeDtypeStruct((n, ns, 1, w), _f32)],
        scratch_shapes=[pltpu.VMEM((tc, w), _f32), pltpu.VMEM((tc, w), _f32)],
        compiler_params=_params("arbitrary", "arbitrary", "arbitrary"),
        name="s5_scan",
    )(u, b_pack, c_pack, dvec, are, aim, h0r, h0i)
    return y, hr.reshape(n, ns * w), hi.reshape(n, ns * w)


def _s5_step(u, h0_re, h0_im, tables):
    b, d = u.shape
    b_pack, c_pack, dvec, are, aim = tables
    ns, _, w2 = b_pack.shape
    w = w2 // 2
    h0r = h0_re.reshape(b, ns, w).transpose(1, 0, 2)
    h0i = h0_im.reshape(b, ns, w).transpose(1, 0, 2)
    tab = lambda a: pl.BlockSpec((1,) + a.shape[1:], lambda s: (s, 0, 0))
    st = pl.BlockSpec((1, b, w), lambda s: (s, 0, 0))
    col = pl.BlockSpec((b, LANES), lambda s: (0, s))
    y, hr, hi = pl.pallas_call(
        _s5_step_body,
        grid=(ns,),
        in_specs=[col, tab(b_pack), tab(c_pack), tab(dvec), tab(are), tab(aim), st, st],
        out_specs=[col, st, st],
        out_shape=[jax.ShapeDtypeStruct((b, d), _bf16),
                   jax.ShapeDtypeStruct((ns, b, w), _f32), jax.ShapeDtypeStruct((ns, b, w), _f32)],
        compiler_params=_params("arbitrary"),
        name="s5_step",
    )(u, b_pack, c_pack, dvec, are, aim, h0r, h0i)
    return y, hr.transpose(1, 0, 2).reshape(b, ns * w), hi.transpose(1, 0, 2).reshape(b, ns * w)


def _glu_out_body(y_ref, x_ref, gate_ref, wl_ref, wr_ref, o_ref):
    y = y_ref[...]
    z = _dot(y, wl_ref[...]) * jax.nn.sigmoid(_dot(y, wr_ref[...]))
    o_ref[...] = x_ref[...] + gate_ref[0] * z


def _glu_out(yg, x, gate, w_glu, tm, rows_per_group):
    m, d = x.shape
    wl = _bf(w_glu[:, :d])
    wr = _bf(w_glu[:, d:])
    gs = _group_spec(tm, rows_per_group, d)
    rowd = pl.BlockSpec((tm, d), lambda i: (i, 0))
    full = lambda a: pl.BlockSpec(a.shape, lambda i: (0, 0))
    return pl.pallas_call(
        _glu_out_body,
        grid=(m // tm,),
        in_specs=[rowd, rowd, gs(gate.shape[1]), full(wl), full(wr)],
        out_specs=rowd,
        out_shape=jax.ShapeDtypeStruct((m, d), _f32),
        compiler_params=_params("arbitrary"),
        name="glu_out",
    )(yg, x, gate, wl, wr)


def _tile(m, target):
    if m <= target:
        return m
    t = target
    while m % t:
        t -= SUBLANES
    return t


def _run_group(x, mods, is_prompt, st, wts):
    n, t, d = x.shape
    m = n * t
    depth = wts["norm_mix"].shape[0]
    d_a = d // 2
    d_b = d // 2
    n_heads = d_a // HEAD_DIM
    rows_per_group = t if is_prompt else m
    tm = _tile(t, 256) if is_prompt else m
    xf = x.reshape(m, d).astype(_f32)

    def mod_vecs(l):
        parts = jnp.split(mods[l], 6, axis=-1)
        if is_prompt:
            return [p.reshape(n, 1, d) for p in parts]
        return [p.reshape(1, m, d) for p in parts]

    outs = {k: [] for k in ("shift", "wkv", "k", "v", "kidx", "re", "im")}
    for l in range(depth):
        sh_m, sc_m, g_m, sh_f, sc_f, g_f = mod_vecs(l)
        i = l // 2
        last = l == depth - 1
        final_g = wts["norm_final"] if last else None
        norm_mix = wts["norm_mix"][l].reshape(1, d)
        norm_ffn = wts["norm_ffn"][l].reshape(1, d)
        if l % 2 == 0:
            p_a, q, k, v, qi, ki, wi = _in_proj(xf, norm_mix, sh_m, sc_m, wts["e_w_in"][i], tm, rows_per_group)
            if is_prompt:
                prev0 = st["shift"][i].reshape(n, 1, -1)
            else:
                prev0 = st["shift"][i].reshape(1, m, -1)
            r, w, k_mod, v_a, av, bv, g, bonus = _rwkv_prep(
                p_a, prev0.astype(_f32), wts["e_mu"][i], wts["e_w0"][i], wts["e_w2"][i], wts["e_a0"][i],
                wts["e_a2"][i], wts["e_g2"][i], wts["e_k_k"][i], wts["e_k_a"][i], wts["e_r_k"][i],
                tm, rows_per_group)
            seq = lambda a: a.reshape(n, t, d_a)
            nb = n if is_prompt else _tile(n, 4)
            tc = _tile(t, 256)
            o, s_fin = _rwkv_scan(seq(r), seq(w), seq(k_mod), seq(v_a), seq(av), seq(bv),
                                  _pack_state(st["wkv"][i].astype(_f32)), nb, tc)
            if is_prompt:
                o_b = _dsa_prompt(q.reshape(n, t, d_b), k.reshape(n, t, d_b), v.reshape(n, t, d_b),
                                  qi.reshape(n, t, -1), ki.reshape(n, t, -1), wi.reshape(n, t, -1),
                                  wts["rel_bias"], 128)
            else:
                n_phys, page = st["cache_k"].shape[1:3]
                o_b = _dsa_sample(q.reshape(n, t, d_b), k.reshape(n, t, d_b), v.reshape(n, t, d_b),
                                  qi.reshape(n, t, -1), ki.reshape(n, t, -1), wi.reshape(n, t, -1),
                                  st["cache_k"][i].reshape(n_phys, page, d_b),
                                  st["cache_v"][i].reshape(n_phys, page, d_b),
                                  st["cache_kidx"][i], st["page_table"], wts["rel_bias"])
            xf = _mix_out(o.reshape(m, d_a), bonus, g, o_b.reshape(m, d_b), xf, g_m,
                          wts["e_lnx_w"][i], wts["e_lnx_b"][i], wts["e_w_out"][i], tm, rows_per_group)
            outs["shift"].append(p_a.reshape(n, t, -1)[:, -1].astype(x.dtype))
            outs["wkv"].append(_unpack_state(s_fin))
            outs["k"].append(k.reshape(n, t, n_heads, HEAD_DIM))
            outs["v"].append(v.reshape(n, t, n_heads, HEAD_DIM))
            outs["kidx"].append(ki.reshape(n, t, D_IDX))
            tm_f = _tile(t, 1024) if is_prompt else m
            ff = wts["ffn_w2"][i].shape[0]
            xf = _ffn(xf, norm_ffn, sh_f, sc_f, g_f, wts["ffn_w13"][i][None], wts["ffn_w2"][i][None],
                      tm_f, _tile(ff, 256) if ff % 256 == 0 else ff, rows_per_group, final_g=final_g)
        else:
            tables = _s5_tables(wts["o_a_re"][i], wts["o_a_im"][i], wts["o_log_step"][i], wts["o_b_re"][i],
                                wts["o_b_im"][i], wts["o_c_re"][i], wts["o_c_im"][i], wts["o_d"][i])
            u = _norm_mod_call(xf, norm_mix, sh_m, sc_m, tm, rows_per_group)
            h0r = st["ssm_re"][i].reshape(n, -1).astype(_f32)
            h0i = st["ssm_im"][i].reshape(n, -1).astype(_f32)
            if is_prompt:
                yg, hr, hi = _s5_scan(u.reshape(n, t, d), h0r, h0i, tables, _tile(t, 256))
            else:
                yg, hr, hi = _s5_step(u, h0r, h0i, tables)
            xf = _glu_out(yg.reshape(m, d), xf, g_m, wts["o_w_glu"][i], tm, rows_per_group)
            g_c = d // CH_G
            outs["re"].append(hr.reshape(n, g_c, P_C))
            outs["im"].append(hi.reshape(n, g_c, P_C))
            tm_f = _tile(t, 1024) if is_prompt else m
            ff = wts["moe_w2"][i].shape[1]
            xf = _ffn(xf, norm_ffn, sh_f, sc_f, g_f, wts["moe_w13"][i], wts["moe_w2"][i],
                      tm_f, _tile(ff, 512) if ff % 512 == 0 else ff, rows_per_group,
                      router=(wts["o_router_w"][i], wts["o_router_b"][i]), final_g=final_g)
    y = xf.reshape(n, t, d).astype(x.dtype)
    return (y, jnp.stack(outs["shift"]), jnp.stack(outs["wkv"]), jnp.stack(outs["k"]), jnp.stack(outs["v"]),
            jnp.stack(outs["kidx"]), jnp.stack(outs["re"]), jnp.stack(outs["im"]))


def kernel(x_prompt, x_sample, cache_k, cache_v, cache_kidx, state_shift, state_wkv, state_ssm_re, state_ssm_im, page_table, c_prompt, c_sample, norm_mix, norm_ffn, ada_w, ada_b, rel_bias, norm_final, e_w_in, e_mu, e_w0, e_w2, e_a0, e_a2, e_g2, e_k_k, e_k_a, e_r_k, e_lnx_w, e_lnx_b, e_w_out, e_ffn_w13, e_ffn_w2, o_a_re, o_a_im, o_log_step, o_b_re, o_b_im, o_c_re, o_c_im, o_d, o_w_glu, o_router_w, o_router_b, o_moe_w13, o_moe_w2):
    nb, _, d = x_prompt.shape
    n_dec = x_sample.shape[0]
    n_even = e_w_in.shape[0]
    n_odd = o_a_re.shape[0]
    d_a = d // 2
    n_heads = d_a // HEAD_DIM
    shift_w = state_shift.shape[-1]

    c_all = jnp.concatenate([c_prompt, c_sample], axis=0)
    pad = (-c_all.shape[0]) % SUBLANES
    mods = _ada(jnp.pad(c_all, ((0, pad), (0, 0))), ada_w, ada_b)
    mods_p = mods[:, :nb]
    mods_s = mods[:, nb:nb + n_dec]

    wts = dict(norm_mix=norm_mix, norm_ffn=norm_ffn, norm_final=norm_final, rel_bias=rel_bias,
               e_w_in=e_w_in, e_mu=e_mu, e_w0=e_w0, e_w2=e_w2, e_a0=e_a0, e_a2=e_a2, e_g2=e_g2, e_k_k=e_k_k,
               e_k_a=e_k_a, e_r_k=e_r_k, e_lnx_w=e_lnx_w, e_lnx_b=e_lnx_b, e_w_out=e_w_out,
               ffn_w13=_bf(e_ffn_w13), ffn_w2=_bf(e_ffn_w2),
               o_a_re=o_a_re, o_a_im=o_a_im, o_log_step=o_log_step, o_b_re=o_b_re, o_b_im=o_b_im,
               o_c_re=o_c_re, o_c_im=o_c_im, o_d=o_d, o_w_glu=o_w_glu, o_router_w=o_router_w,
               o_router_b=o_router_b, moe_w13=_bf(o_moe_w13), moe_w2=_bf(o_moe_w2))

    st_p = dict(shift=jnp.zeros((n_even, nb, shift_w), x_prompt.dtype),
                wkv=jnp.zeros((n_even, nb, n_heads, HEAD_DIM, HEAD_DIM), _f32),
                ssm_re=jnp.zeros((n_odd, nb, d // CH_G, P_C), _f32),
                ssm_im=jnp.zeros((n_odd, nb, d // CH_G, P_C), _f32))
    st_s = dict(shift=state_shift, wkv=state_wkv, ssm_re=state_ssm_re, ssm_im=state_ssm_im,
                cache_k=cache_k, cache_v=cache_v, cache_kidx=cache_kidx, page_table=page_table)
    out_p = _run_group(x_prompt, mods_p, True, st_p, wts)
    out_s = _run_group(x_sample, mods_s, False, st_s, wts)
    return (out_p[0], out_s[0]) + out_p[1:] + out_s[1:]
```

```python
import functools
import math

import jax
import jax.numpy as jnp
from jax import lax
from jax.experimental import pallas as pl
from jax.experimental.pallas import tpu as pltpu

HEAD_DIM = 64
LORA_W = 64
LORA_A = 64
LORA_G = 128
H_IDX = 8
D_IDX = 64
TOPK_MAX = 256
N_BUCKETS = 32
MAX_DIST = 128
CH_G = 16
P_C = 64
TOP_E = 2
EPS = 1e-6
LNX_EPS = 64e-5

LANES = 128
SUBLANES = 8
VMEM_LIMIT = 56 * 1024 * 1024
KEY_BLOCK = 256
COUNT_BLOCK = 1024
COUNT_ROWS = 64
NEG = -2.0e30
M_INIT = -1.0e30

_bf16 = jnp.bfloat16
_f32 = jnp.float32


def _bf(x):
    return x.astype(_bf16)


def _dot(a, b):
    return jnp.dot(a, b, preferred_element_type=_f32)


def _dot_nt(a, b):
    return lax.dot_general(a, b, (((1,), (1,)), ((), ())), preferred_element_type=_f32)


def _params(*sem):
    return pltpu.CompilerParams(dimension_semantics=sem, vmem_limit_bytes=VMEM_LIMIT)


def _split_bf16(x, parts):
    out = []
    for _ in range(parts - 1):
        hi = _bf(x)
        out.append(hi)
        x = x - hi.astype(_f32)
    out.append(_bf(x))
    return out


def _segsum(x, ones_bd):
    hi, lo = _split_bf16(x, 2)
    return _dot(hi, ones_bd) + _dot(lo, ones_bd)


def _norm_mod(x, g, shift, scale):
    ms = jnp.mean(x * x, axis=-1, keepdims=True)
    return (x * lax.rsqrt(ms + EPS) * g) * (1.0 + scale) + shift


def _silu(x):
    return x * jax.nn.sigmoid(x)


def _gelu_tanh(x):
    return 0.5 * x * (1.0 + jnp.tanh(math.sqrt(2.0 / math.pi) * (x + 0.044715 * (x * x * x))))


def _softplus(x):
    return jnp.maximum(x, 0.0) + jnp.log(1.0 + jnp.exp(-jnp.abs(x)))


def _rel_bucket(dist):
    max_exact = N_BUCKETS // 2
    n = jnp.maximum(dist, 0)
    nf = jnp.maximum(n, 1).astype(_f32)
    large = max_exact + (jnp.log(nf / max_exact) / math.log(MAX_DIST / max_exact) * (N_BUCKETS - max_exact)).astype(jnp.int32)
    return jnp.where(n < max_exact, n, jnp.minimum(large, N_BUCKETS - 1))


def _group_spec(rows_per_block, rows_per_group, width):
    def spec(r):
        return pl.BlockSpec((1, r, width), lambda i, *_: ((i * rows_per_block) // rows_per_group, 0, 0))
    return spec


def _ada_body(c_ref, w_ref, b_ref, o_ref):
    o_ref[0] = _dot(_bf(_silu(c_ref[...])), _bf(w_ref[0])) + b_ref[0]


def _ada(c, ada_w, ada_b):
    depth, d, n6 = ada_w.shape
    rows = c.shape[0]
    tn = n6 // 4
    return pl.pallas_call(
        _ada_body,
        grid=(depth, n6 // tn),
        in_specs=[pl.BlockSpec((rows, d), lambda l, j: (0, 0)),
                  pl.BlockSpec((1, d, tn), lambda l, j: (l, 0, j)),
                  pl.BlockSpec((1, 1, tn), lambda l, j: (l, 0, j))],
        out_specs=pl.BlockSpec((1, rows, tn), lambda l, j: (l, 0, j)),
        out_shape=jax.ShapeDtypeStruct((depth, rows, n6), _f32),
        compiler_params=_params("arbitrary", "arbitrary"),
        name="ada_mod",
    )(c, ada_w, ada_b.reshape(depth, 1, n6))


def _norm_mod_body(x_ref, g_ref, sh_ref, sc_ref, o_ref):
    o_ref[...] = _norm_mod(x_ref[...], g_ref[...], sh_ref[0], sc_ref[0])


def _norm_mod_call(x, g, shift, scale, tm, rows_per_group):
    m, d = x.shape
    gs = _group_spec(tm, rows_per_group, d)
    return pl.pallas_call(
        _norm_mod_body,
        grid=(m // tm,),
        in_specs=[pl.BlockSpec((tm, d), lambda i: (i, 0)),
                  pl.BlockSpec((1, d), lambda i: (0, 0)),
                  gs(shift.shape[1]), gs(scale.shape[1])],
        out_specs=pl.BlockSpec((tm, d), lambda i: (i, 0)),
        out_shape=jax.ShapeDtypeStruct((m, d), _f32),
        compiler_params=_params("arbitrary"),
        name="norm_mod",
    )(x, g, shift, scale)


def _in_proj_body(x_ref, g_ref, sh_ref, sc_ref, wa_ref, wq_ref, wk_ref, wv_ref, wqi_ref, wkw_ref,
                  pa_ref, q_ref, k_ref, v_ref, qi_ref, ki_ref, wi_ref):
    h = _bf(_norm_mod(x_ref[...], g_ref[...], sh_ref[0], sc_ref[0]))
    pa_ref[...] = _dot(h, wa_ref[...])
    q_ref[...] = _bf(_dot(h, wq_ref[...]))
    k_ref[...] = _dot(h, wk_ref[...])
    v_ref[...] = _dot(h, wv_ref[...])
    qi_ref[...] = _bf(_dot(h, wqi_ref[...]))
    kw = _dot(h, wkw_ref[...])
    ki_ref[...] = kw[:, :D_IDX]
    wi_ref[...] = kw[:, D_IDX:D_IDX + H_IDX]


def _in_proj(x, g, shift, scale, w_in, tm, rows_per_group):
    m, d = x.shape
    d_a = d // 2
    d_b = d // 2
    shift_w = 3 * d_a + LORA_W + LORA_A + LORA_G
    c1 = shift_w + 3 * d_b
    c2 = c1 + H_IDX * D_IDX
    wa = _bf(w_in[:, :shift_w])
    wq = _bf(w_in[:, shift_w:shift_w + d_b])
    wk = _bf(w_in[:, shift_w + d_b:shift_w + 2 * d_b])
    wv = _bf(w_in[:, shift_w + 2 * d_b:c1])
    wqi = _bf(w_in[:, c1:c2])
    wkw = _bf(jnp.pad(w_in[:, c2:], ((0, 0), (0, LANES - D_IDX - H_IDX))))
    gs = _group_spec(tm, rows_per_group, d)
    full = lambda a: pl.BlockSpec(a.shape, lambda i: (0, 0))
    row = lambda w: pl.BlockSpec((tm, w), lambda i: (i, 0))
    widths = [(shift_w, _f32), (d_b, _bf16), (d_b, _f32), (d_b, _f32), (H_IDX * D_IDX, _bf16),
              (D_IDX, _f32), (H_IDX, _f32)]
    return pl.pallas_call(
        _in_proj_body,
        grid=(m // tm,),
        in_specs=[row(d), pl.BlockSpec((1, d), lambda i: (0, 0)), gs(shift.shape[1]), gs(scale.shape[1]),
                  full(wa), full(wq), full(wk), full(wv), full(wqi), full(wkw)],
        out_specs=[row(w) for w, _ in widths],
        out_shape=[jax.ShapeDtypeStruct((m, w), dt) for w, dt in widths],
        compiler_params=_params("arbitrary"),
        name="in_proj",
    )(x, g, shift, scale, wa, wq, wk, wv, wqi, wkw)


def _rwkv_prep_body(seq_is_one, rows_per_group, tm,
                    p_ref, pprev_ref, prev0_ref, mu_ref, w0_ref, a0_ref, kk_ref, ka_ref, rk_ref,
                    wwa_ref, g2_ref, ones_ref,
                    r_ref, w_ref, k_ref, v_ref, av_ref, bv_ref, g_ref, bonus_ref):
    d_a = r_ref.shape[1]
    p = p_ref[...]
    if seq_is_one:
        p_prev = prev0_ref[0]
    else:
        i = pl.program_id(0)
        first = (i * tm) % rows_per_group == 0
        prev_row = jnp.where(first, prev0_ref[0], pprev_ref[SUBLANES - 1:SUBLANES, :])
        rolled = pltpu.roll(p, 1, axis=0)
        row_id = lax.broadcasted_iota(jnp.int32, p.shape, 0)
        p_prev = jnp.where(row_id == 0, prev_row, rolled)
    ps = p + (p_prev - p) * mu_ref[...]
    r = ps[:, :d_a]
    k = ps[:, d_a:2 * d_a]
    v = ps[:, 2 * d_a:3 * d_a]
    xwa = ps[:, 3 * d_a:3 * d_a + LORA_W + LORA_A]
    xg = ps[:, 3 * d_a + LORA_W + LORA_A:]
    lane = lax.broadcasted_iota(jnp.int32, xwa.shape, 1)
    xwa = jnp.where(lane < LORA_W, jnp.tanh(xwa), xwa)
    lwa = _dot(_bf(xwa), wwa_ref[...])
    w_log = -_softplus(-(w0_ref[...] + lwa[:, :d_a])) - 0.5
    decay = jnp.exp(-jnp.exp(w_log))
    a = jax.nn.sigmoid(a0_ref[...] + lwa[:, d_a:])
    g_ref[...] = _dot(_bf(jax.nn.sigmoid(xg)), g2_ref[...])
    ones_bd = ones_ref[...]
    kk = k * kk_ref[...]
    kk = kk / jnp.maximum(jnp.sqrt(_segsum(kk * kk, ones_bd)), 1e-12)
    k_mod = k * (1.0 + (a - 1.0) * ka_ref[...])
    r_ref[...] = r
    w_ref[...] = decay
    k_ref[...] = k_mod
    v_ref[...] = v
    av_ref[...] = -kk
    bv_ref[...] = kk * a
    bonus_ref[...] = _segsum(r * k_mod * rk_ref[...], ones_bd) * v


def _ones_blockdiag(n, group):
    idx = jnp.arange(n, dtype=jnp.int32) // group
    return (idx[:, None] == idx[None, :]).astype(_bf16)


def _rwkv_prep(p_a, prev0, mu, w0, w2, a0, a2, g2, k_k, k_a, r_k, tm, rows_per_group):
    m, shift_w = p_a.shape
    d_a = w0.shape[-1]
    seq_is_one = rows_per_group == tm and prev0.shape[1] == tm
    wwa = jnp.zeros((LORA_W + LORA_A, 2 * d_a), _f32)
    wwa = _bf(wwa.at[:LORA_W, :d_a].set(w2).at[LORA_W:, d_a:].set(a2))
    ones_bd = _ones_blockdiag(d_a, HEAD_DIM)
    vec = lambda a: a.reshape(1, -1)
    gs = _group_spec(tm, rows_per_group, shift_w)
    full = lambda a: pl.BlockSpec(a.shape, lambda i: (0, 0))
    row = pl.BlockSpec((tm, d_a), lambda i: (i, 0))
    ins = [p_a, p_a, prev0, vec(mu), vec(w0), vec(a0), vec(k_k), vec(k_a), vec(r_k), wwa, _bf(g2), ones_bd]
    in_specs = [pl.BlockSpec((tm, shift_w), lambda i: (i, 0)),
                pl.BlockSpec((SUBLANES, shift_w), lambda i: (jnp.maximum(i * (tm // SUBLANES) - 1, 0), 0)),
                gs(prev0.shape[1])] + [full(a) for a in ins[3:]]
    return pl.pallas_call(
        functools.partial(_rwkv_prep_body, seq_is_one, rows_per_group, tm),
        grid=(m // tm,),
        in_specs=in_specs,
        out_specs=[row] * 8,
        out_shape=[jax.ShapeDtypeStruct((m, d_a), _f32)] * 8,
        compiler_params=_params("arbitrary"),
        name="rwkv_prep",
    )(*ins)


def _rwkv_scan_body(nb, n_pairs, tc,
                    r_ref, w_ref, k_ref, v_ref, av_ref, bv_ref, s0_ref, ones2_ref, ones3_ref, eye_ref,
                    o_ref, s_ref):
    t_blk = pl.program_id(1)

    @pl.when(t_blk == 0)
    def _():
        s_ref[...] = s0_ref[...]

    ones2 = ones2_ref[...]
    ones3 = ones3_ref[...]
    eye = eye_ref[...] > 0.5
    units = [(n, hp) for n in range(nb) for hp in range(n_pairs)]

    def rowsum(xs, ones, parts):
        lhs = jnp.concatenate([jnp.concatenate(_split_bf16(x, parts), axis=1) for x in xs], axis=0)
        res = _dot(lhs, ones)
        return [res[u * HEAD_DIM:(u + 1) * HEAD_DIM] for u in range(len(xs))]

    sub = min(SUBLANES, tc)
    row_id = lax.broadcasted_iota(jnp.int32, (sub, LANES), 0)

    def tile_steps(j, carry):
        t0 = pl.multiple_of(j * sub, sub)
        tiles = {}
        for (n, hp) in units:
            sl = pl.ds(hp * LANES, LANES)
            tiles[(n, hp)] = tuple(ref[n, pl.ds(t0, sub), sl] for ref in (r_ref, w_ref, k_ref, v_ref, av_ref, bv_ref))
        states = [s_ref[n, hp] for (n, hp) in units]
        o_tiles = [jnp.zeros((sub, LANES), _f32) for _ in units]
        for tt in range(sub):
            rows = {u: tuple(x[tt:tt + 1, :] for x in tiles[u]) for u in units}
            sa = rowsum([s * rows[u][4] for s, u in zip(states, units)], ones2, 2)
            vdiag = [jnp.where(eye, jnp.broadcast_to(rows[u][3], (HEAD_DIM, LANES)), 0.0) for u in units]
            vcol = rowsum(vdiag, ones3, 3)
            for idx, u in enumerate(units):
                r_t, w_t, k_t, v_t, av_t, bv_t = rows[u]
                states[idx] = states[idx] * w_t + sa[idx] * bv_t + vcol[idx] * k_t
            ob = rowsum([s * rows[u][0] for s, u in zip(states, units)], ones2, 2)
            for idx in range(len(units)):
                o_row = jnp.sum(jnp.where(eye, ob[idx], 0.0), axis=0, keepdims=True)
                o_tiles[idx] = jnp.where(row_id == tt, jnp.broadcast_to(o_row, (sub, LANES)), o_tiles[idx])
        for idx, (n, hp) in enumerate(units):
            s_ref[n, hp] = states[idx]
            o_ref[n, pl.ds(t0, sub), pl.ds(hp * LANES, LANES)] = o_tiles[idx]
        return carry

    lax.fori_loop(0, tc // sub, tile_steps, 0)


def _rwkv_scan(r, w, k, v, av, bv, s0, nb, tc):
    n, t, d_a = r.shape
    n_pairs = d_a // LANES
    half = (jnp.arange(LANES, dtype=jnp.int32) // HEAD_DIM)
    ones_blk = (half[:, None] == half[None, :]).astype(_bf16)
    ones2 = jnp.concatenate([ones_blk] * 2, axis=0)
    ones3 = jnp.concatenate([ones_blk] * 3, axis=0)
    eye = (jnp.arange(HEAD_DIM, dtype=jnp.int32)[:, None]
           == (jnp.arange(LANES, dtype=jnp.int32) % HEAD_DIM)[None, :]).astype(_f32)
    seq = pl.BlockSpec((nb, tc, d_a), lambda b, j: (b, j, 0))
    st = pl.BlockSpec((nb, n_pairs, HEAD_DIM, LANES), lambda b, j: (b, 0, 0, 0))
    full = lambda a: pl.BlockSpec(a.shape, lambda b, j: (0, 0))
    return pl.pallas_call(
        functools.partial(_rwkv_scan_body, nb, n_pairs, tc),
        grid=(n // nb, t // tc),
        in_specs=[seq] * 6 + [st, full(ones2), full(ones3), full(eye)],
        out_specs=[seq, st],
        out_shape=[jax.ShapeDtypeStruct((n, t, d_a), _f32),
                   jax.ShapeDtypeStruct(s0.shape, _f32)],
        compiler_params=_params("arbitrary", "arbitrary"),
        name="rwkv_scan",
    )(r, w, k, v, av, bv, s0, ones2, ones3, eye)


def _pack_state(s):
    n, h = s.shape[:2]
    return s.reshape(n, h // 2, 2, HEAD_DIM, HEAD_DIM).transpose(0, 1, 3, 2, 4).reshape(n, h // 2, HEAD_DIM, LANES)


def _unpack_state(s):
    n, hp = s.shape[:2]
    return s.reshape(n, hp, HEAD_DIM, 2, HEAD_DIM).transpose(0, 1, 3, 2, 4).reshape(n, hp * 2, HEAD_DIM, HEAD_DIM)


def _mix_out_body(o_ref, bonus_ref, g_ref, ob_ref, x_ref, gate_ref, lw_ref, lb_ref, ones_ref, wa_ref, wb_ref,
                  out_ref):
    ones_bd = ones_ref[...]
    o = o_ref[...]
    inv = 1.0 / HEAD_DIM
    mean = _segsum(o, ones_bd) * inv
    dlt = o - mean
    var = _segsum(dlt * dlt, ones_bd) * inv
    on = dlt * lax.rsqrt(var + LNX_EPS) * lw_ref[...] + lb_ref[...]
    oa = (on + bonus_ref[...]) * g_ref[...]
    y = _dot(_bf(oa), wa_ref[...]) + _dot(ob_ref[...], wb_ref[...])
    out_ref[...] = x_ref[...] + gate_ref[0] * y


def _mix_out(o, bonus, g, o_b, x, gate, lnx_w, lnx_b, w_out, tm, rows_per_group):
    m, d = x.shape
    d_a = o.shape[1]
    ones_bd = _ones_blockdiag(d_a, HEAD_DIM)
    wa = _bf(w_out[:d_a])
    wb = _bf(w_out[d_a:])
    gs = _group_spec(tm, rows_per_group, d)
    full = lambda a: pl.BlockSpec(a.shape, lambda i: (0, 0))
    rowa = pl.BlockSpec((tm, d_a), lambda i: (i, 0))
    rowd = pl.BlockSpec((tm, d), lambda i: (i, 0))
    lw = lnx_w.reshape(1, -1)
    lb = lnx_b.reshape(1, -1)
    return pl.pallas_call(
        _mix_out_body,
        grid=(m // tm,),
        in_specs=[rowa, rowa, rowa, pl.BlockSpec((tm, o_b.shape[1]), lambda i: (i, 0)), rowd, gs(gate.shape[1]),
                  full(lw), full(lb), full(ones_bd), full(wa), full(wb)],
        out_specs=rowd,
        out_shape=jax.ShapeDtypeStruct((m, d), _f32),
        compiler_params=_params("arbitrary"),
        name="mix_out",
    )(o, bonus, g, o_b, x, gate, lw, lb, ones_bd, wa, wb)


def _key_to_float(u):
    key = u ^ jnp.int32(-2147483648)
    bits = jnp.where(key >= 0, key, key ^ jnp.int32(0x7FFFFFFF))
    return lax.bitcast_convert_type(bits, _f32)


def _kth_largest(count_ge, k_row, shape):
    def body(it, u):
        bit = jnp.left_shift(jnp.int32(1), 31 - it)
        cand_u = u | bit
        ok = count_ge(_key_to_float(cand_u)) >= k_row
        return jnp.where(ok, cand_u, u)

    u = lax.fori_loop(0, 32, body, jnp.zeros(shape, jnp.int32))
    return _key_to_float(u)


def _tie_cut(count_eq_lt, budget, n_bits, shape):
    def body(it, c):
        cand = c | jnp.left_shift(jnp.int32(1), n_bits - 1 - it)
        ok = count_eq_lt(cand) <= budget
        return jnp.where(ok, cand, c)

    return lax.fori_loop(0, n_bits, body, jnp.zeros(shape, jnp.int32))


def _dsa_prompt_body(n_heads, k_sel, t_len,
                     far_ref, qt_ref, qit_ref, wit_ref, k_ref, vt_ref, ki_ref, near_ref,
                     ot_ref, sc_ref, cut_ref, qz_ref, m_ref, l_ref, acc_ref):
    qb = qt_ref.shape[2]
    i = pl.program_id(1)
    q0 = i * qb
    qpos = q0 + lax.broadcasted_iota(jnp.int32, (1, qb), 1)
    n_kb = (q0 + qb + KEY_BLOCK - 1) // KEY_BLOCK
    row_kb = lax.broadcasted_iota(jnp.int32, (KEY_BLOCK, qb), 0)

    @pl.when(i == 0)
    def _():
        sc_ref[...] = jnp.full(sc_ref.shape, -jnp.inf, _f32)

    w8 = (wit_ref[0] * (H_IDX ** -0.5)) * (D_IDX ** -0.5)

    qi_all = jnp.concatenate([qit_ref[0, h * D_IDX:(h + 1) * D_IDX, :] for h in range(H_IDX)], axis=1)

    def score_block(kb, carry):
        c0 = pl.multiple_of(kb * KEY_BLOCK, KEY_BLOCK)
        s_all = _dot(ki_ref[0, pl.ds(c0, KEY_BLOCK), :], qi_all)
        acc = w8[0:1, :] * jnp.maximum(s_all[:, :qb], 0.0)
        for h in range(1, H_IDX):
            acc = acc + w8[h:h + 1, :] * jnp.maximum(s_all[:, h * qb:(h + 1) * qb], 0.0)
        sc_ref[pl.ds(c0, KEY_BLOCK), :] = jnp.where(c0 + row_kb <= qpos, acc, -jnp.inf)
        return carry

    lax.fori_loop(0, n_kb, score_block, 0)

    cb = min(COUNT_BLOCK, t_len)
    n_cb = (q0 + qb + cb - 1) // cb
    row_cb = lax.broadcasted_iota(jnp.int32, (cb, qb), 0)

    def count(pred):
        def body(kb, acc):
            c0 = pl.multiple_of(kb * cb, cb)
            hit = jnp.where(pred(sc_ref[pl.ds(c0, cb), :], c0 + row_cb), 1.0, 0.0)
            for j in range(cb // COUNT_ROWS):
                acc = acc + hit[j * COUNT_ROWS:(j + 1) * COUNT_ROWS, :]
            return acc
        acc = lax.fori_loop(0, n_cb, body, jnp.zeros((COUNT_ROWS, qb), _f32))
        return jnp.sum(acc, axis=0, keepdims=True)

    k_row = jnp.minimum(k_sel, qpos + 1).astype(_f32)
    thr = _kth_largest(lambda cand: count(lambda s, pos: s >= cand), k_row, (1, qb))
    n_gt = count(lambda s, pos: s > thr)
    n_eq = count(lambda s, pos: s == thr)
    budget = k_row - n_gt
    cut_ref[...] = jnp.full((1, qb), 2 * t_len, jnp.int32)

    @pl.when(jnp.max(n_eq - budget) > 0.5)
    def _():
        cut_ref[...] = _tie_cut(lambda c: count(lambda s, pos: (s == thr) & (pos < c)), budget,
                                (2 * t_len).bit_length(), (1, qb))

    cut = cut_ref[...]

    m_ref[...] = jnp.full(m_ref.shape, M_INIT, _f32)
    l_ref[...] = jnp.zeros(l_ref.shape, _f32)
    acc_ref[...] = jnp.zeros(acc_ref.shape, _f32)
    pair_row = lax.broadcasted_iota(jnp.int32, (LANES, qb), 0) // HEAD_DIM
    for hp in range(n_heads // 2):
        qp = qt_ref[0, hp * LANES:(hp + 1) * LANES, :] * (HEAD_DIM ** -0.5)
        zero = jnp.zeros_like(qp)
        qz_ref[hp] = jnp.concatenate([jnp.where(pair_row == 0, qp, zero), jnp.where(pair_row == 1, qp, zero)], axis=1)

    def attend(c0, width, bias_of_head):
        scb = sc_ref[pl.ds(c0, width), :]
        pos = c0 + lax.broadcasted_iota(jnp.int32, (width, qb), 0)
        sel = (scb > thr) | ((scb == thr) & (pos < cut))
        m_old = [m_ref[h] for h in range(n_heads)]
        l_old = [l_ref[h] for h in range(n_heads)]
        a_old = [acc_ref[h] for h in range(n_heads)]
        s2 = [_dot(k_ref[0, pl.ds(c0, width), hp * LANES:(hp + 1) * LANES], qz_ref[hp]) for hp in range(n_heads // 2)]
        m_out, l_out, a_out = [], [], []
        for h in range(n_heads):
            s = s2[h // 2][:, (h % 2) * qb:(h % 2 + 1) * qb] + bias_of_head(h)
            s = jnp.where(sel, s, NEG)
            m_new = jnp.maximum(m_old[h], jnp.max(s, axis=0, keepdims=True))
            alpha = jnp.exp(m_old[h] - m_new)
            p = jnp.exp(s - m_new)
            l_out.append(alpha * l_old[h] + jnp.sum(p, axis=0, keepdims=True))
            pv = _dot(vt_ref[0, h * HEAD_DIM:(h + 1) * HEAD_DIM, pl.ds(c0, width)], _bf(p))
            a_out.append(alpha * a_old[h] + pv)
            m_out.append(m_new)
        for h in range(n_heads):
            m_ref[h] = m_out[h]
            l_ref[h] = l_out[h]
            acc_ref[h] = a_out[h]

    n_far = jnp.maximum(i - 1, 0) // (KEY_BLOCK // qb)

    def far_block(kb, carry):
        attend(pl.multiple_of(kb * KEY_BLOCK, KEY_BLOCK), KEY_BLOCK, lambda h: far_ref[h])
        return carry

    lax.fori_loop(0, n_far, far_block, 0)

    def near_block(j, carry):
        attend(pl.multiple_of(j * qb, qb), qb, lambda h: near_ref[i - j, h])
        return carry

    lax.fori_loop(n_far * (KEY_BLOCK // qb), i + 1, near_block, 0)

    for h in range(n_heads):
        ot_ref[0, h * HEAD_DIM:(h + 1) * HEAD_DIM, :] = _bf(acc_ref[h] / l_ref[h])


def _dsa_prompt(q, k, v, qi, ki, wi, rel_bias, qb):
    n, t, d_b = q.shape
    n_heads = d_b // HEAD_DIM
    k_sel = min(TOPK_MAX, t // 4)
    tr = lambda a: a.transpose(0, 2, 1)
    n_near = KEY_BLOCK // qb + 1
    qq = jnp.arange(qb, dtype=jnp.int32)
    dist = (jnp.arange(n_near, dtype=jnp.int32)[:, None, None] * qb + qq[None, None, :] - qq[None, :, None])
    onehot = (_rel_bucket(dist)[..., None] == jnp.arange(N_BUCKETS, dtype=jnp.int32)).astype(_f32)
    near = jnp.einsum('dkqb,bh->dhkq', onehot, rel_bias, precision=lax.Precision.HIGHEST)
    far = rel_bias[N_BUCKETS - 1]
    res = lambda shape: pl.BlockSpec(shape, lambda b, i, *_: (b, 0, 0))
    blk = lambda w: pl.BlockSpec((1, w, qb), lambda b, i, *_: (b, 0, i))
    grid_spec = pltpu.PrefetchScalarGridSpec(
        num_scalar_prefetch=0,
        grid=(n, t // qb),
        in_specs=[pl.BlockSpec(memory_space=pltpu.SMEM),
                  blk(d_b), blk(H_IDX * D_IDX), blk(H_IDX),
                  res((1, t, d_b)), res((1, d_b, t)), res((1, t, D_IDX)),
                  pl.BlockSpec(near.shape, lambda b, i, *_: (0, 0, 0, 0))],
        out_specs=blk(d_b),
        scratch_shapes=[pltpu.VMEM((t, qb), _f32), pltpu.VMEM((1, qb), jnp.int32),
                        pltpu.VMEM((n_heads // 2, LANES, 2 * qb), _bf16),
                        pltpu.VMEM((n_heads, 1, qb), _f32), pltpu.VMEM((n_heads, 1, qb), _f32),
                        pltpu.VMEM((n_heads, HEAD_DIM, qb), _f32)],
    )
    o_t = pl.pallas_call(
        functools.partial(_dsa_prompt_body, n_heads, k_sel, t),
        grid_spec=grid_spec,
        out_shape=jax.ShapeDtypeStruct((n, d_b, t), _bf16),
        compiler_params=_params("arbitrary", "arbitrary"),
        name="dsa_prompt",
    )(far, tr(q), tr(qi), tr(wi), _bf(k), tr(_bf(v)), _bf(ki), near)
    return tr(o_t)


def _dsa_sample_body(n_heads, k_sel, n_pages, page,
                     pt_ref, q_ref, qi_ref, wi_ref, kn_ref, vn_ref, kin_ref, ck_ref, cv_ref, cki_ref,
                     btab_ref, bnew_ref, o_ref, sc_ref, lg_ref, vs_ref):
    p = pl.program_id(1)
    past = n_pages * page
    d_b = n_heads * HEAD_DIM
    c0 = pl.multiple_of(p * page, page)
    w8 = wi_ref[0] * (H_IDX ** -0.5)
    qi8 = qi_ref[0]
    head_row = lax.broadcasted_iota(jnp.int32, (n_heads, d_b), 0)
    head_lane = lax.broadcasted_iota(jnp.int32, (n_heads, d_b), 1) // HEAD_DIM
    q_blk = jnp.where(head_row == head_lane, jnp.broadcast_to(q_ref[0].astype(_f32), (n_heads, d_b)), 0.0)

    s8 = _dot_nt(qi8, _bf(cki_ref[0])) * (D_IDX ** -0.5)
    sc_ref[:, pl.ds(c0, page)] = jnp.sum(w8 * jnp.maximum(s8, 0.0), axis=0, keepdims=True)
    lg = _dot_nt(_bf(q_blk), _bf(ck_ref[0])) * (HEAD_DIM ** -0.5)
    lg_ref[:, pl.ds(c0, page)] = lg + btab_ref[:, pl.ds(c0, page)]
    vs_ref[pl.ds(c0, page), :] = _bf(cv_ref[0])

    @pl.when(p == n_pages - 1)
    def _():
        sc = sc_ref[...]
        s_new8 = jnp.sum(qi8.astype(_f32) * _bf(kin_ref[0]).astype(_f32), axis=1, keepdims=True) * (D_IDX ** -0.5)
        s_new = jnp.sum(w8 * jnp.maximum(s_new8, 0.0), axis=0, keepdims=True)
        pos = lax.broadcasted_iota(jnp.int32, sc.shape, 1)

        def count(pred):
            hits = jnp.sum(jnp.where(pred(sc, pos), 1.0, 0.0), axis=1, keepdims=True)
            return hits + jnp.where(pred(s_new, past), 1.0, 0.0)

        k_row = jnp.full((1, 1), float(k_sel), _f32)
        thr = _kth_largest(lambda cand: count(lambda s, ps: s >= cand), k_row, (1, 1))
        budget = k_row - count(lambda s, ps: s > thr)
        cut = _tie_cut(lambda c: count(lambda s, ps: (s == thr) & (ps < c)), budget,
                       (2 * (past + 1)).bit_length(), (1, 1))
        sel = (sc > thr) | ((sc == thr) & (pos < cut))
        sel_new = (s_new > thr) | ((s_new == thr) & (past < cut))

        kn = _bf(kn_ref[0]).astype(_f32)
        lg_new = jnp.sum(_bf(q_blk).astype(_f32) * kn, axis=1, keepdims=True) * (HEAD_DIM ** -0.5) + bnew_ref[...]
        lg_new = jnp.where(sel_new, lg_new, NEG)
        s_all = jnp.where(sel, lg_ref[...], NEG)
        m = jnp.maximum(jnp.maximum(jnp.max(s_all, axis=1, keepdims=True), lg_new), M_INIT)
        pr = jnp.exp(s_all - m)
        pr_new = jnp.exp(lg_new - m)
        l = jnp.sum(pr, axis=1, keepdims=True) + pr_new
        out8 = (_dot(_bf(pr), vs_ref[...]) + pr_new * _bf(vn_ref[0]).astype(_f32)) / l
        o_ref[0] = _bf(jnp.sum(jnp.where(head_row == head_lane, out8, 0.0), axis=0, keepdims=True))


def _dsa_sample(q, k_new, v_new, qi, ki_new, wi, ck, cv, cki, page_table, rel_bias):
    b, _, d_b = q.shape
    n_heads = d_b // HEAD_DIM
    n_pages = page_table.shape[1]
    page = ck.shape[1]
    past = n_pages * page
    k_sel = min(TOPK_MAX, (past + 1) // 4)
    kpos = jnp.arange(past, dtype=jnp.int32)
    btab = rel_bias[_rel_bucket(past - kpos)].T
    bnew = rel_bias[_rel_bucket(jnp.zeros((1,), jnp.int32))].T
    qi8 = qi.reshape(b, H_IDX, D_IDX)
    wi8 = wi.reshape(b, H_IDX, 1)
    row = lambda w: pl.BlockSpec((1, 1, w), lambda i, p, pt: (i, 0, 0))
    paged = lambda w: pl.BlockSpec((1, page, w), lambda i, p, pt: (pt[i * n_pages + p], 0, 0))
    grid_spec = pltpu.PrefetchScalarGridSpec(
        num_scalar_prefetch=1,
        grid=(b, n_pages),
        in_specs=[row(d_b),
                  pl.BlockSpec((1, H_IDX, D_IDX), lambda i, p, pt: (i, 0, 0)),
                  pl.BlockSpec((1, H_IDX, 1), lambda i, p, pt: (i, 0, 0)),
                  row(d_b), row(d_b), row(D_IDX),
                  paged(d_b), paged(d_b), paged(D_IDX),
                  pl.BlockSpec(btab.shape, lambda i, p, pt: (0, 0)),
                  pl.BlockSpec(bnew.shape, lambda i, p, pt: (0, 0))],
        out_specs=row(d_b),
        scratch_shapes=[pltpu.VMEM((1, past), _f32), pltpu.VMEM((n_heads, past), _f32),
                        pltpu.VMEM((past, d_b), _bf16)],
    )
    return pl.pallas_call(
        functools.partial(_dsa_sample_body, n_heads, k_sel, n_pages, page),
        grid_spec=grid_spec,
        out_shape=jax.ShapeDtypeStruct((b, 1, d_b), _bf16),
        compiler_params=_params("arbitrary", "arbitrary"),
        name="dsa_sample",
    )(page_table.reshape(-1), q, qi8, wi8, k_new, v_new, ki_new, ck, cv, cki, btab, bnew)


def _ffn_body(routed, final_norm, n_exp,
              x_ref, g_ref, sh_ref, sc_ref, gate_ref, rw_ref, rb_ref, w1_ref, w3_ref, w2_ref, gf_ref,
              o_ref, h_ref, acc_ref, dg_ref):
    e = pl.program_id(1)
    f = pl.program_id(2)
    first = (e == 0) & (f == 0)
    last = (e == pl.num_programs(1) - 1) & (f == pl.num_programs(2) - 1)

    @pl.when(first)
    def _():
        h = _norm_mod(x_ref[...], g_ref[...], sh_ref[0], sc_ref[0])
        h_ref[...] = _bf(h)
        acc_ref[...] = jnp.zeros(acc_ref.shape, _f32)
        if routed:
            logits = _dot(_bf(h), rw_ref[...]) + rb_ref[...]
            lane = lax.broadcasted_iota(jnp.int32, logits.shape, 1).astype(_f32)
            logits = jnp.where(lane < n_exp, logits, -jnp.inf)
            v1 = jnp.max(logits, axis=1, keepdims=True)
            i1 = jnp.min(jnp.where(logits == v1, lane, float(LANES)), axis=1, keepdims=True)
            rest = jnp.where(lane == i1, -jnp.inf, logits)
            v2 = jnp.max(rest, axis=1, keepdims=True)
            i2 = jnp.min(jnp.where(rest == v2, lane, float(LANES)), axis=1, keepdims=True)
            e2 = jnp.exp(v2 - v1)
            den = 1.0 + e2
            dg_ref[...] = jnp.where(lane == i1, 1.0 / den, 0.0) + jnp.where(lane == i2, e2 / den, 0.0)

    h = h_ref[...]
    a = _silu(_dot(h, w1_ref[0])) * _dot(h, w3_ref[0])
    if routed:
        lane = lax.broadcasted_iota(jnp.int32, dg_ref.shape, 1)
        a = a * jnp.sum(jnp.where(lane == e, dg_ref[...], 0.0), axis=1, keepdims=True)
    acc_ref[...] += _dot(_bf(a), w2_ref[0])

    @pl.when(last)
    def _():
        y = x_ref[...] + gate_ref[0] * acc_ref[...]
        if final_norm:
            y = y * lax.rsqrt(jnp.mean(y * y, axis=-1, keepdims=True) + EPS) * gf_ref[...]
        o_ref[...] = y


def _ffn(x, g, shift, scale, gate, w13, w2, tm, tf, rows_per_group, router=None, final_g=None):
    m, d = x.shape
    n_exp, ff, _ = w2.shape
    routed = router is not None
    final_norm = final_g is not None
    if routed:
        rw, rb = router
        rw = _bf(jnp.pad(rw, ((0, 0), (0, LANES - n_exp))))
        rb = jnp.pad(rb.reshape(1, -1), ((0, 0), (0, LANES - n_exp)))
    else:
        rw = jnp.zeros((d, LANES), _bf16)
        rb = jnp.zeros((1, LANES), _f32)
    gf = final_g.reshape(1, -1) if final_norm else jnp.ones((1, d), _f32)
    nf = ff // tf
    gs = _group_spec(tm, rows_per_group, d)
    rowd = pl.BlockSpec((tm, d), lambda i, e, f: (i, 0))
    full = lambda a: pl.BlockSpec(a.shape, lambda i, e, f: (0, 0))
    return pl.pallas_call(
        functools.partial(_ffn_body, routed, final_norm, n_exp),
        grid=(m // tm, n_exp, nf),
        in_specs=[rowd, full(g), gs(shift.shape[1]), gs(scale.shape[1]), gs(gate.shape[1]), full(rw), full(rb),
                  pl.BlockSpec((1, d, tf), lambda i, e, f: (e, 0, f)),
                  pl.BlockSpec((1, d, tf), lambda i, e, f: (e, 0, f + nf)),
                  pl.BlockSpec((1, tf, d), lambda i, e, f: (e, f, 0)),
                  full(gf)],
        out_specs=rowd,
        out_shape=jax.ShapeDtypeStruct((m, d), _f32),
        scratch_shapes=[pltpu.VMEM((tm, d), _bf16), pltpu.VMEM((tm, d), _f32), pltpu.VMEM((tm, LANES), _f32)],
        compiler_params=_params("arbitrary", "arbitrary", "arbitrary"),
        name="moe_ffn" if routed else "dense_ffn",
    )(x, g, shift, scale, gate, rw, rb, w13, w13, w2, gf)


def _cmul(ar, ai, br, bi):
    return ar * br - ai * bi, ar * bi + ai * br


def _s5_io(u, b_ref, c_ref, d_ref, h_of_bb):
    half = b_ref.shape[2] // 2
    bb = _dot(_bf(u), b_ref[0])
    h_re, h_im = h_of_bb(bb[:, :half], bb[:, half:])
    y = _dot(_bf(jnp.concatenate([h_re, h_im], axis=1)), c_ref[0]) + d_ref[0] * u
    return _bf(_gelu_tanh(y)), h_re, h_im


def _s5_scan_body(tc, u_ref, b_ref, c_ref, d_ref, are_ref, aim_ref, h0r_ref, h0i_ref,
                  y_ref, hr_ref, hi_ref, xr_ref, xi_ref):
    t_blk = pl.program_id(2)

    @pl.when(t_blk == 0)
    def _():
        hr_ref[0, 0] = h0r_ref[0, 0]
        hi_ref[0, 0] = h0i_ref[0, 0]

    width = are_ref.shape[2]
    a1 = (jnp.broadcast_to(are_ref[0], (SUBLANES, width)), jnp.broadcast_to(aim_ref[0], (SUBLANES, width)))
    a2 = _cmul(*a1, *a1)
    a3 = _cmul(*a2, *a1)
    a4 = _cmul(*a2, *a2)
    a5 = _cmul(*a4, *a1)
    a6 = _cmul(*a4, *a2)
    a7 = _cmul(*a4, *a3)
    a8 = _cmul(*a4, *a4)
    row = lax.broadcasted_iota(jnp.int32, (SUBLANES, width), 0)
    lvl = [tuple(jnp.where(row >= s, c, 0.0) for c in a) for s, a in ((1, a1), (2, a2), (4, a4))]
    pw = []
    for comp in range(2):
        acc = a8[comp]
        for s, a in ((6, a7), (5, a6), (4, a5), (3, a4), (2, a3), (1, a2), (0, a1)):
            acc = jnp.where(row == s, a[comp], acc)
        pw.append(acc)

    def scan_chunk(bb_re, bb_im):
        xr_ref[...] = bb_re
        xi_ref[...] = bb_im

        def tile(j, carry):
            cr, ci = carry
            r0 = pl.multiple_of(j * SUBLANES, SUBLANES)
            xr = xr_ref[pl.ds(r0, SUBLANES), :]
            xi = xi_ref[pl.ds(r0, SUBLANES), :]
            for (s, (fr, fi)) in zip((1, 2, 4), lvl):
                sr = pltpu.roll(xr, s, axis=0)
                si = pltpu.roll(xi, s, axis=0)
                pr, pi = _cmul(fr, fi, sr, si)
                xr = xr + pr
                xi = xi + pi
            pr, pi = _cmul(pw[0], pw[1], jnp.broadcast_to(cr, xr.shape), jnp.broadcast_to(ci, xi.shape))
            xr = xr + pr
            xi = xi + pi
            xr_ref[pl.ds(r0, SUBLANES), :] = xr
            xi_ref[pl.ds(r0, SUBLANES), :] = xi
            return xr[SUBLANES - 1:SUBLANES], xi[SUBLANES - 1:SUBLANES]

        cr, ci = lax.fori_loop(0, tc // SUBLANES, tile, (hr_ref[0, 0], hi_ref[0, 0]))
        hr_ref[0, 0] = cr
        hi_ref[0, 0] = ci
        return xr_ref[...], xi_ref[...]

    y, _, _ = _s5_io(u_ref[0], b_ref, c_ref, d_ref, scan_chunk)
    y_ref[0] = y


def _s5_step_body(u_ref, b_ref, c_ref, d_ref, are_ref, aim_ref, h0r_ref, h0i_ref, y_ref, hr_ref, hi_ref):
    def one_step(bb_re, bb_im):
        pr, pi = _cmul(are_ref[0], aim_ref[0], h0r_ref[0], h0i_ref[0])
        return pr + bb_re, pi + bb_im

    y, h_re, h_im = _s5_io(u_ref[...], b_ref, c_ref, d_ref, one_step)
    y_ref[...] = y
    hr_ref[0] = h_re
    hi_ref[0] = h_im


def _s5_tables(a_re, a_im, log_step, b_re, b_im, c_re, c_im, d):
    g_c, p_c, ch = b_re.shape
    gs = LANES // ch
    ns = g_c // gs
    dt = jnp.exp(log_step.astype(_f32))[:, None]
    mag = jnp.exp(dt * a_re)
    ab_re = mag * jnp.cos(dt * a_im)
    ab_im = mag * jnp.sin(dt * a_im)
    den = a_re * a_re + a_im * a_im
    nr = ab_re - 1.0
    co_re = (nr * a_re + ab_im * a_im) / den
    co_im = (ab_im * a_re - nr * a_im) / den
    bf_re = co_re[..., None] * b_re - co_im[..., None] * b_im
    bf_im = co_re[..., None] * b_im + co_im[..., None] * b_re
    eye = jnp.eye(gs, dtype=_f32)

    def pack_b(b):
        b = b.reshape(ns, gs, p_c, ch)
        return jnp.einsum('sgpc,gh->sgchp', b, eye).reshape(ns, gs * ch, gs * p_c)

    def pack_c(c):
        c = c.reshape(ns, gs, ch, p_c)
        return jnp.einsum('sgcp,gh->sgphc', c, eye).reshape(ns, gs * p_c, gs * ch)

    b_pack = _bf(jnp.concatenate([pack_b(bf_re), pack_b(bf_im)], axis=2))
    c_pack = _bf(jnp.concatenate([pack_c(c_re), -pack_c(c_im)], axis=1))
    vec = lambda a: a.reshape(ns, 1, gs * p_c)
    return b_pack, c_pack, d.reshape(ns, 1, gs * ch), vec(ab_re), vec(ab_im)


def _s5_scan(u, h0_re, h0_im, tables, tc):
    n, t, d = u.shape
    b_pack, c_pack, dvec, are, aim = tables
    ns, _, w2 = b_pack.shape
    w = w2 // 2
    h0r = h0_re.reshape(n, ns, 1, w)
    h0i = h0_im.reshape(n, ns, 1, w)
    tab = lambda a: pl.BlockSpec((1,) + a.shape[1:], lambda b, s, j: (s, 0, 0))
    st = pl.BlockSpec((1, 1, 1, w), lambda b, s, j: (b, s, 0, 0))
    seq = pl.BlockSpec((1, tc, LANES), lambda b, s, j: (b, j, s))
    y, hr, hi = pl.pallas_call(
        functools.partial(_s5_scan_body, tc),
        grid=(n, ns, t // tc),
        in_specs=[seq, tab(b_pack), tab(c_pack), tab(dvec), tab(are), tab(aim), st, st],
        out_specs=[seq, st, st],
        out_shape=[jax.ShapeDtypeStruct((n, t, d), _bf16),
                   jax.ShapeDtypeStruct((n, ns, 1, w), _f32), jax.ShapeDtypeStruct((n, ns, 1, w), _f32)],
        scratch_shapes=[pltpu.VMEM((tc, w), _f32), pltpu.VMEM((tc, w), _f32)],
        compiler_params=_params("arbitrary", "arbitrary", "arbitrary"),
        name="s5_scan",
    )(u, b_pack, c_pack, dvec, are, aim, h0r, h0i)
    return y, hr.reshape(n, ns * w), hi.reshape(n, ns * w)


def _s5_step(u, h0_re, h0_im, tables):
    b, d = u.shape
    b_pack, c_pack, dvec, are, aim = tables
    ns, _, w2 = b_pack.shape
    w = w2 // 2
    h0r = h0_re.reshape(b, ns, w).transpose(1, 0, 2)
    h0i = h0_im.reshape(b, ns, w).transpose(1, 0, 2)
    tab = lambda a: pl.BlockSpec((1,) + a.shape[1:], lambda s: (s, 0, 0))
    st = pl.BlockSpec((1, b, w), lambda s: (s, 0, 0))
    col = pl.BlockSpec((b, LANES), lambda s: (0, s))
    y, hr, hi = pl.pallas_call(
        _s5_step_body,
        grid=(ns,),
        in_specs=[col, tab(b_pack), tab(c_pack), tab(dvec), tab(are), tab(aim), st, st],
        out_specs=[col, st, st],
        out_shape=[jax.ShapeDtypeStruct((b, d), _bf16),
                   jax.ShapeDtypeStruct((ns, b, w), _f32), jax.ShapeDtypeStruct((ns, b, w), _f32)],
        compiler_params=_params("arbitrary"),
        name="s5_step",
    )(u, b_pack, c_pack, dvec, are, aim, h0r, h0i)
    return y, hr.transpose(1, 0, 2).reshape(b, ns * w), hi.transpose(1, 0, 2).reshape(b, ns * w)


def _glu_out_body(y_ref, x_ref, gate_ref, wl_ref, wr_ref, o_ref):
    y = y_ref[...]
    z = _dot(y, wl_ref[...]) * jax.nn.sigmoid(_dot(y, wr_ref[...]))
    o_ref[...] = x_ref[...] + gate_ref[0] * z


def _glu_out(yg, x, gate, w_glu, tm, rows_per_group):
    m, d = x.shape
    wl = _bf(w_glu[:, :d])
    wr = _bf(w_glu[:, d:])
    gs = _group_spec(tm, rows_per_group, d)
    rowd = pl.BlockSpec((tm, d), lambda i: (i, 0))
    full = lambda a: pl.BlockSpec(a.shape, lambda i: (0, 0))
    return pl.pallas_call(
        _glu_out_body,
        grid=(m // tm,),
        in_specs=[rowd, rowd, gs(gate.shape[1]), full(wl), full(wr)],
        out_specs=rowd,
        out_shape=jax.ShapeDtypeStruct((m, d), _f32),
        compiler_params=_params("arbitrary"),
        name="glu_out",
    )(yg, x, gate, wl, wr)


def _tile(m, target):
    if m <= target:
        return m
    t = target
    while m % t:
        t -= SUBLANES
    return t


def _run_group(x, mods, is_prompt, st, wts):
    n, t, d = x.shape
    m = n * t
    depth = wts["norm_mix"].shape[0]
    d_a = d // 2
    d_b = d // 2
    n_heads = d_a // HEAD_DIM
    rows_per_group = t if is_prompt else m
    tm = _tile(t, 256) if is_prompt else m
    xf = x.reshape(m, d).astype(_f32)

    def mod_vecs(l):
        parts = jnp.split(mods[l], 6, axis=-1)
        if is_prompt:
            return [p.reshape(n, 1, d) for p in parts]
        return [p.reshape(1, m, d) for p in parts]

    outs = {k: [] for k in ("shift", "wkv", "k", "v", "kidx", "re", "im")}
    for l in range(depth):
        sh_m, sc_m, g_m, sh_f, sc_f, g_f = mod_vecs(l)
        i = l // 2
        last = l == depth - 1
        final_g = wts["norm_final"] if last else None
        norm_mix = wts["norm_mix"][l].reshape(1, d)
        norm_ffn = wts["norm_ffn"][l].reshape(1, d)
        if l % 2 == 0:
            p_a, q, k, v, qi, ki, wi = _in_proj(xf, norm_mix, sh_m, sc_m, wts["e_w_in"][i], tm, rows_per_group)
            if is_prompt:
                prev0 = st["shift"][i].reshape(n, 1, -1)
            else:
                prev0 = st["shift"][i].reshape(1, m, -1)
            r, w, k_mod, v_a, av, bv, g, bonus = _rwkv_prep(
                p_a, prev0.astype(_f32), wts["e_mu"][i], wts["e_w0"][i], wts["e_w2"][i], wts["e_a0"][i],
                wts["e_a2"][i], wts["e_g2"][i], wts["e_k_k"][i], wts["e_k_a"][i], wts["e_r_k"][i],
                tm, rows_per_group)
            seq = lambda a: a.reshape(n, t, d_a)
            nb = n if is_prompt else _tile(n, 4)
            tc = _tile(t, 256)
            o, s_fin = _rwkv_scan(seq(r), seq(w), seq(k_mod), seq(v_a), seq(av), seq(bv),
                                  _pack_state(st["wkv"][i].astype(_f32)), nb, tc)
            if is_prompt:
                o_b = _dsa_prompt(q.reshape(n, t, d_b), k.reshape(n, t, d_b), v.reshape(n, t, d_b),
                                  qi.reshape(n, t, -1), ki.reshape(n, t, -1), wi.reshape(n, t, -1),
                                  wts["rel_bias"], 128)
            else:
                n_phys, page = st["cache_k"].shape[1:3]
                o_b = _dsa_sample(q.reshape(n, t, d_b), k.reshape(n, t, d_b), v.reshape(n, t, d_b),
                                  qi.reshape(n, t, -1), ki.reshape(n, t, -1), wi.reshape(n, t, -1),
                                  st["cache_k"].reshape(-1, page, d_b),
                                  st["cache_v"].reshape(-1, page, d_b),
                                  st["cache_kidx"].reshape(-1, page, D_IDX),
                                  st["page_table"] + i * n_phys, wts["rel_bias"])
            xf = _mix_out(o.reshape(m, d_a), bonus, g, o_b.reshape(m, d_b), xf, g_m,
                          wts["e_lnx_w"][i], wts["e_lnx_b"][i], wts["e_w_out"][i], tm, rows_per_group)
            outs["shift"].append(p_a.reshape(n, t, -1)[:, -1].astype(x.dtype))
            outs["wkv"].append(_unpack_state(s_fin))
            outs["k"].append(k.reshape(n, t, n_heads, HEAD_DIM))
            outs["v"].append(v.reshape(n, t, n_heads, HEAD_DIM))
            outs["kidx"].append(ki.reshape(n, t, D_IDX))
            tm_f = _tile(t, 1024) if is_prompt else m
            ff = wts["ffn_w2"][i].shape[0]
            xf = _ffn(xf, norm_ffn, sh_f, sc_f, g_f, wts["ffn_w13"][i][None], wts["ffn_w2"][i][None],
                      tm_f, _tile(ff, 256) if ff % 256 == 0 else ff, rows_per_group, final_g=final_g)
        else:
            tables = _s5_tables(wts["o_a_re"][i], wts["o_a_im"][i], wts["o_log_step"][i], wts["o_b_re"][i],
                                wts["o_b_im"][i], wts["o_c_re"][i], wts["o_c_im"][i], wts["o_d"][i])
            u = _norm_mod_call(xf, norm_mix, sh_m, sc_m, tm, rows_per_group)
            h0r = st["ssm_re"][i].reshape(n, -1).astype(_f32)
            h0i = st["ssm_im"][i].reshape(n, -1).astype(_f32)
            if is_prompt:
                yg, hr, hi = _s5_scan(u.reshape(n, t, d), h0r, h0i, tables, _tile(t, 256))
            else:
                yg, hr, hi = _s5_step(u, h0r, h0i, tables)
            xf = _glu_out(yg.reshape(m, d), xf, g_m, wts["o_w_glu"][i], tm, rows_per_group)
            g_c = d // CH_G
            outs["re"].append(hr.reshape(n, g_c, P_C))
            outs["im"].append(hi.reshape(n, g_c, P_C))
            tm_f = _tile(t, 1024) if is_prompt else m
            ff = wts["moe_w2"][i].shape[1]
            xf = _ffn(xf, norm_ffn, sh_f, sc_f, g_f, wts["moe_w13"][i], wts["moe_w2"][i],
                      tm_f, _tile(ff, 512) if ff % 512 == 0 else ff, rows_per_group,
                      router=(wts["o_router_w"][i], wts["o_router_b"][i]), final_g=final_g)
    y = xf.reshape(n, t, d).astype(x.dtype)
    return (y, jnp.stack(outs["shift"]), jnp.stack(outs["wkv"]), jnp.stack(outs["k"]), jnp.stack(outs["v"]),
            jnp.stack(outs["kidx"]), jnp.stack(outs["re"]), jnp.stack(outs["im"]))


def kernel(x_prompt, x_sample, cache_k, cache_v, cache_kidx, state_shift, state_wkv, state_ssm_re, state_ssm_im, page_table, c_prompt, c_sample, norm_mix, norm_ffn, ada_w, ada_b, rel_bias, norm_final, e_w_in, e_mu, e_w0, e_w2, e_a0, e_a2, e_g2, e_k_k, e_k_a, e_r_k, e_lnx_w, e_lnx_b, e_w_out, e_ffn_w13, e_ffn_w2, o_a_re, o_a_im, o_log_step, o_b_re, o_b_im, o_c_re, o_c_im, o_d, o_w_glu, o_router_w, o_router_b, o_moe_w13, o_moe_w2):
    nb, _, d = x_prompt.shape
    n_dec = x_sample.shape[0]
    n_even = e_w_in.shape[0]
    n_odd = o_a_re.shape[0]
    d_a = d // 2
    n_heads = d_a // HEAD_DIM
    shift_w = state_shift.shape[-1]

    c_all = jnp.concatenate([c_prompt, c_sample], axis=0)
    pad = (-c_all.shape[0]) % SUBLANES
    mods = _ada(jnp.pad(c_all, ((0, pad), (0, 0))), ada_w, ada_b)
    mods_p = mods[:, :nb]
    mods_s = mods[:, nb:nb + n_dec]

    wts = dict(norm_mix=norm_mix, norm_ffn=norm_ffn, norm_final=norm_final, rel_bias=rel_bias,
               e_w_in=e_w_in, e_mu=e_mu, e_w0=e_w0, e_w2=e_w2, e_a0=e_a0, e_a2=e_a2, e_g2=e_g2, e_k_k=e_k_k,
               e_k_a=e_k_a, e_r_k=e_r_k, e_lnx_w=e_lnx_w, e_lnx_b=e_lnx_b, e_w_out=e_w_out,
               ffn_w13=_bf(e_ffn_w13), ffn_w2=_bf(e_ffn_w2),
               o_a_re=o_a_re, o_a_im=o_a_im, o_log_step=o_log_step, o_b_re=o_b_re, o_b_im=o_b_im,
               o_c_re=o_c_re, o_c_im=o_c_im, o_d=o_d, o_w_glu=o_w_glu, o_router_w=o_router_w,
               o_router_b=o_router_b, moe_w13=_bf(o_moe_w13), moe_w2=_bf(o_moe_w2))

    st_p = dict(shift=jnp.zeros((n_even, nb, shift_w), x_prompt.dtype),
                wkv=jnp.zeros((n_even, nb, n_heads, HEAD_DIM, HEAD_DIM), _f32),
                ssm_re=jnp.zeros((n_odd, nb, d // CH_G, P_C), _f32),
                ssm_im=jnp.zeros((n_odd, nb, d // CH_G, P_C), _f32))
    st_s = dict(shift=state_shift, wkv=state_wkv, ssm_re=state_ssm_re, ssm_im=state_ssm_im,
                cache_k=cache_k, cache_v=cache_v, cache_kidx=cache_kidx, page_table=page_table)
    out_p = _run_group(x_prompt, mods_p, True, st_p, wts)
    out_s = _run_group(x_sample, mods_s, False, st_s, wts)
    return (out_p[0], out_s[0]) + out_p[1:] + out_s[1:]
```

```python
import functools
import math

import jax
import jax.numpy as jnp
from jax import lax
from jax.experimental import pallas as pl
from jax.experimental.pallas import tpu as pltpu

HEAD_DIM = 64
LORA_W = 64
LORA_A = 64
LORA_G = 128
H_IDX = 8
D_IDX = 64
TOPK_MAX = 256
N_BUCKETS = 32
MAX_DIST = 128
CH_G = 16
P_C = 64
TOP_E = 2
EPS = 1e-6
LNX_EPS = 64e-5

LANES = 128
SUBLANES = 8
VMEM_LIMIT = 56 * 1024 * 1024
KEY_BLOCK = 256
COUNT_BLOCK = 1024
COUNT_ROWS = 64
NEG = -2.0e30
M_INIT = -1.0e30

_bf16 = jnp.bfloat16
_f32 = jnp.float32


def _bf(x):
    return x.astype(_bf16)


def _dot(a, b):
    return jnp.dot(a, b, preferred_element_type=_f32)


def _dot_nt(a, b):
    return lax.dot_general(a, b, (((1,), (1,)), ((), ())), preferred_element_type=_f32)


def _params(*sem):
    return pltpu.CompilerParams(dimension_semantics=sem, vmem_limit_bytes=VMEM_LIMIT)


def _split_bf16(x, parts):
    out = []
    for _ in range(parts - 1):
        hi = _bf(x)
        out.append(hi)
        x = x - hi.astype(_f32)
    out.append(_bf(x))
    return out


def _segsum(x, ones_bd):
    hi, mid, lo = _split_bf16(x, 3)
    return _dot(hi, ones_bd) + _dot(mid, ones_bd) + _dot(lo, ones_bd)


def _norm_mod(x, g, shift, scale):
    ms = jnp.mean(x * x, axis=-1, keepdims=True)
    return (x * lax.rsqrt(ms + EPS) * g) * (1.0 + scale) + shift


def _silu(x):
    return x * jax.nn.sigmoid(x)


def _gelu_tanh(x):
    return 0.5 * x * (1.0 + jnp.tanh(math.sqrt(2.0 / math.pi) * (x + 0.044715 * (x * x * x))))


def _softplus(x):
    return jnp.maximum(x, 0.0) + jnp.log(1.0 + jnp.exp(-jnp.abs(x)))


def _rel_bucket(dist):
    max_exact = N_BUCKETS // 2
    n = jnp.maximum(dist, 0)
    nf = jnp.maximum(n, 1).astype(_f32)
    large = max_exact + (jnp.log(nf / max_exact) / math.log(MAX_DIST / max_exact) * (N_BUCKETS - max_exact)).astype(jnp.int32)
    return jnp.where(n < max_exact, n, jnp.minimum(large, N_BUCKETS - 1))


def _group_spec(rows_per_block, rows_per_group, width):
    def spec(r):
        return pl.BlockSpec((1, r, width), lambda i, *_: ((i * rows_per_block) // rows_per_group, 0, 0))
    return spec


def _ada_body(c_ref, w_ref, b_ref, o_ref):
    o_ref[0] = _dot(_bf(_silu(c_ref[...])), _bf(w_ref[0])) + b_ref[0]


def _ada(c, ada_w, ada_b):
    depth, d, n6 = ada_w.shape
    rows = c.shape[0]
    tn = n6 // 4
    return pl.pallas_call(
        _ada_body,
        grid=(depth, n6 // tn),
        in_specs=[pl.BlockSpec((rows, d), lambda l, j: (0, 0)),
                  pl.BlockSpec((1, d, tn), lambda l, j: (l, 0, j)),
                  pl.BlockSpec((1, 1, tn), lambda l, j: (l, 0, j))],
        out_specs=pl.BlockSpec((1, rows, tn), lambda l, j: (l, 0, j)),
        out_shape=jax.ShapeDtypeStruct((depth, rows, n6), _f32),
        compiler_params=_params("arbitrary", "arbitrary"),
        name="ada_mod",
    )(c, ada_w, ada_b.reshape(depth, 1, n6))


def _norm_mod_body(x_ref, g_ref, sh_ref, sc_ref, o_ref):
    o_ref[...] = _norm_mod(x_ref[...], g_ref[...], sh_ref[0], sc_ref[0])


def _norm_mod_call(x, g, shift, scale, tm, rows_per_group):
    m, d = x.shape
    gs = _group_spec(tm, rows_per_group, d)
    return pl.pallas_call(
        _norm_mod_body,
        grid=(m // tm,),
        in_specs=[pl.BlockSpec((tm, d), lambda i: (i, 0)),
                  pl.BlockSpec((1, d), lambda i: (0, 0)),
                  gs(shift.shape[1]), gs(scale.shape[1])],
        out_specs=pl.BlockSpec((tm, d), lambda i: (i, 0)),
        out_shape=jax.ShapeDtypeStruct((m, d), _f32),
        compiler_params=_params("arbitrary"),
        name="norm_mod",
    )(x, g, shift, scale)


def _in_proj_body(x_ref, g_ref, sh_ref, sc_ref, wa_ref, wq_ref, wk_ref, wv_ref, wqi_ref, wkw_ref,
                  pa_ref, q_ref, k_ref, v_ref, qi_ref, ki_ref, wi_ref):
    h = _bf(_norm_mod(x_ref[...], g_ref[...], sh_ref[0], sc_ref[0]))
    pa_ref[...] = _dot(h, wa_ref[...])
    q_ref[...] = _bf(_dot(h, wq_ref[...]))
    k_ref[...] = _dot(h, wk_ref[...])
    v_ref[...] = _dot(h, wv_ref[...])
    qi_ref[...] = _bf(_dot(h, wqi_ref[...]))
    kw = _dot(h, wkw_ref[...])
    ki_ref[...] = kw[:, :D_IDX]
    wi_ref[...] = kw[:, D_IDX:D_IDX + H_IDX]


def _in_proj(x, g, shift, scale, w_in, tm, rows_per_group):
    m, d = x.shape
    d_a = d // 2
    d_b = d // 2
    shift_w = 3 * d_a + LORA_W + LORA_A + LORA_G
    c1 = shift_w + 3 * d_b
    c2 = c1 + H_IDX * D_IDX
    wa = _bf(w_in[:, :shift_w])
    wq = _bf(w_in[:, shift_w:shift_w + d_b])
    wk = _bf(w_in[:, shift_w + d_b:shift_w + 2 * d_b])
    wv = _bf(w_in[:, shift_w + 2 * d_b:c1])
    wqi = _bf(w_in[:, c1:c2])
    wkw = _bf(jnp.pad(w_in[:, c2:], ((0, 0), (0, LANES - D_IDX - H_IDX))))
    gs = _group_spec(tm, rows_per_group, d)
    full = lambda a: pl.BlockSpec(a.shape, lambda i: (0, 0))
    row = lambda w: pl.BlockSpec((tm, w), lambda i: (i, 0))
    widths = [(shift_w, _f32), (d_b, _bf16), (d_b, _f32), (d_b, _f32), (H_IDX * D_IDX, _bf16),
              (D_IDX, _f32), (H_IDX, _f32)]
    return pl.pallas_call(
        _in_proj_body,
        grid=(m // tm,),
        in_specs=[row(d), pl.BlockSpec((1, d), lambda i: (0, 0)), gs(shift.shape[1]), gs(scale.shape[1]),
                  full(wa), full(wq), full(wk), full(wv), full(wqi), full(wkw)],
        out_specs=[row(w) for w, _ in widths],
        out_shape=[jax.ShapeDtypeStruct((m, w), dt) for w, dt in widths],
        compiler_params=_params("arbitrary"),
        name="in_proj",
    )(x, g, shift, scale, wa, wq, wk, wv, wqi, wkw)


def _rwkv_prep_body(seq_is_one, rows_per_group, tm,
                    p_ref, pprev_ref, prev0_ref, mu_ref, w0_ref, a0_ref, kk_ref, ka_ref, rk_ref,
                    wwa_ref, g2_ref, ones_ref,
                    r_ref, w_ref, k_ref, v_ref, av_ref, bv_ref, g_ref, bonus_ref):
    d_a = r_ref.shape[1]
    p = p_ref[...]
    if seq_is_one:
        p_prev = prev0_ref[0]
    else:
        i = pl.program_id(0)
        first = (i * tm) % rows_per_group == 0
        prev_row = jnp.where(first, prev0_ref[0], pprev_ref[SUBLANES - 1:SUBLANES, :])
        rolled = pltpu.roll(p, 1, axis=0)
        row_id = lax.broadcasted_iota(jnp.int32, p.shape, 0)
        p_prev = jnp.where(row_id == 0, prev_row, rolled)
    ps = p + (p_prev - p) * mu_ref[...]
    r = ps[:, :d_a]
    k = ps[:, d_a:2 * d_a]
    v = ps[:, 2 * d_a:3 * d_a]
    xwa = ps[:, 3 * d_a:3 * d_a + LORA_W + LORA_A]
    xg = ps[:, 3 * d_a + LORA_W + LORA_A:]
    lane = lax.broadcasted_iota(jnp.int32, xwa.shape, 1)
    xwa = jnp.where(lane < LORA_W, jnp.tanh(xwa), xwa)
    lwa = _dot(_bf(xwa), wwa_ref[...])
    w_log = -_softplus(-(w0_ref[...] + lwa[:, :d_a])) - 0.5
    decay = jnp.exp(-jnp.exp(w_log))
    a = jax.nn.sigmoid(a0_ref[...] + lwa[:, d_a:])
    g_ref[...] = _dot(_bf(jax.nn.sigmoid(xg)), g2_ref[...])
    ones_bd = ones_ref[...]
    kk = k * kk_ref[...]
    kk = kk / jnp.maximum(jnp.sqrt(_segsum(kk * kk, ones_bd)), 1e-12)
    k_mod = k * (1.0 + (a - 1.0) * ka_ref[...])
    r_ref[...] = r
    w_ref[...] = decay
    k_ref[...] = k_mod
    v_ref[...] = v
    av_ref[...] = -kk
    bv_ref[...] = kk * a
    bonus_ref[...] = _segsum(r * k_mod * rk_ref[...], ones_bd) * v


def _ones_blockdiag(n, group):
    idx = jnp.arange(n, dtype=jnp.int32) // group
    return (idx[:, None] == idx[None, :]).astype(_bf16)


def _rwkv_prep(p_a, prev0, mu, w0, w2, a0, a2, g2, k_k, k_a, r_k, tm, rows_per_group):
    m, shift_w = p_a.shape
    d_a = w0.shape[-1]
    seq_is_one = rows_per_group == tm and prev0.shape[1] == tm
    wwa = jnp.zeros((LORA_W + LORA_A, 2 * d_a), _f32)
    wwa = _bf(wwa.at[:LORA_W, :d_a].set(w2).at[LORA_W:, d_a:].set(a2))
    ones_bd = _ones_blockdiag(d_a, HEAD_DIM)
    vec = lambda a: a.reshape(1, -1)
    gs = _group_spec(tm, rows_per_group, shift_w)
    full = lambda a: pl.BlockSpec(a.shape, lambda i: (0, 0))
    row = pl.BlockSpec((tm, d_a), lambda i: (i, 0))
    ins = [p_a, p_a, prev0, vec(mu), vec(w0), vec(a0), vec(k_k), vec(k_a), vec(r_k), wwa, _bf(g2), ones_bd]
    in_specs = [pl.BlockSpec((tm, shift_w), lambda i: (i, 0)),
                pl.BlockSpec((SUBLANES, shift_w), lambda i: (jnp.maximum(i * (tm // SUBLANES) - 1, 0), 0)),
                gs(prev0.shape[1])] + [full(a) for a in ins[3:]]
    return pl.pallas_call(
        functools.partial(_rwkv_prep_body, seq_is_one, rows_per_group, tm),
        grid=(m // tm,),
        in_specs=in_specs,
        out_specs=[row] * 8,
        out_shape=[jax.ShapeDtypeStruct((m, d_a), _f32)] * 8,
        compiler_params=_params("arbitrary"),
        name="rwkv_prep",
    )(*ins)


def _rwkv_scan_body(nb, n_pairs, tc,
                    r_ref, w_ref, k_ref, v_ref, av_ref, bv_ref, s0_ref, ones_ref, expand_ref, eye_ref,
                    o_ref, s_ref):
    t_blk = pl.program_id(1)

    @pl.when(t_blk == 0)
    def _():
        s_ref[...] = s0_ref[...]

    ones_k = ones_ref[...]
    parts = ones_k.shape[0] // LANES
    expand = expand_ref[...]
    eye = eye_ref[...] > 0.5
    units = [(n, hp) for n in range(nb) for hp in range(n_pairs)]

    def rowsum(xs, ones, parts):
        lhs = jnp.concatenate([jnp.concatenate(_split_bf16(x, parts), axis=1) for x in xs], axis=0)
        res = _dot(lhs, ones)
        return [res[u * HEAD_DIM:(u + 1) * HEAD_DIM] for u in range(len(xs))]

    sub = min(SUBLANES, tc)
    row_id = lax.broadcasted_iota(jnp.int32, (sub, LANES), 0)

    def tile_steps(j, carry):
        t0 = pl.multiple_of(j * sub, sub)
        tiles = {}
        for (n, hp) in units:
            sl = pl.ds(hp * LANES, LANES)
            tiles[(n, hp)] = tuple(ref[n, pl.ds(t0, sub), sl] for ref in (r_ref, w_ref, k_ref, v_ref, av_ref, bv_ref))
        states = [s_ref[n, hp] for (n, hp) in units]
        o_tiles = [jnp.zeros((sub, LANES), _f32) for _ in units]
        vcols = []
        for u in units:
            vt = tiles[u][3]
            by_head = jnp.concatenate([vt[:, :HEAD_DIM], vt[:, HEAD_DIM:]], axis=0)
            lhs = jnp.concatenate(_split_bf16(by_head, 3), axis=0)
            vcols.append(lax.dot_general(lhs, expand, (((0,), (0,)), ((), ())), preferred_element_type=_f32))
        for tt in range(sub):
            rows = {u: tuple(x[tt:tt + 1, :] for x in tiles[u]) for u in units}
            sa = rowsum([s * rows[u][4] for s, u in zip(states, units)], ones_k, parts)
            for idx, u in enumerate(units):
                r_t, w_t, k_t, v_t, av_t, bv_t = rows[u]
                vcol = vcols[idx][:, tt * LANES:(tt + 1) * LANES]
                states[idx] = states[idx] * w_t + sa[idx] * bv_t + vcol * k_t
            ob = rowsum([s * rows[u][0] for s, u in zip(states, units)], ones_k, parts)
            for idx in range(len(units)):
                o_row = jnp.sum(jnp.where(eye, ob[idx], 0.0), axis=0, keepdims=True)
                o_tiles[idx] = jnp.where(row_id == tt, jnp.broadcast_to(o_row, (sub, LANES)), o_tiles[idx])
        for idx, (n, hp) in enumerate(units):
            s_ref[n, hp] = states[idx]
            o_ref[n, pl.ds(t0, sub), pl.ds(hp * LANES, LANES)] = o_tiles[idx]
        return carry

    lax.fori_loop(0, tc // sub, tile_steps, 0)


def _rwkv_scan(r, w, k, v, av, bv, s0, nb, tc):
    n, t, d_a = r.shape
    n_pairs = d_a // LANES
    half = (jnp.arange(LANES, dtype=jnp.int32) // HEAD_DIM)
    ones_blk = (half[:, None] == half[None, :]).astype(_bf16)
    ones_k = jnp.concatenate([ones_blk] * (3 if t == 1 else 2), axis=0)
    sub = min(SUBLANES, tc)
    src_h = jnp.arange(2 * sub, dtype=jnp.int32) // sub
    src_t = jnp.arange(2 * sub, dtype=jnp.int32) % sub
    dst_t = jnp.arange(sub * LANES, dtype=jnp.int32) // LANES
    dst_h = (jnp.arange(sub * LANES, dtype=jnp.int32) % LANES) // HEAD_DIM
    expand = ((src_t[:, None] == dst_t[None, :]) & (src_h[:, None] == dst_h[None, :])).astype(_bf16)
    expand = jnp.concatenate([expand] * 3, axis=0)
    eye =(jnp.arange(HEAD_DIM, dtype=jnp.int32)[:, None]
           == (jnp.arange(LANES, dtype=jnp.int32) % HEAD_DIM)[None, :]).astype(_f32)
    seq = pl.BlockSpec((nb, tc, d_a), lambda b, j: (b, j, 0))
    st = pl.BlockSpec((nb, n_pairs, HEAD_DIM, LANES), lambda b, j: (b, 0, 0, 0))
    full = lambda a: pl.BlockSpec(a.shape, lambda b, j: (0, 0))
    return pl.pallas_call(
        functools.partial(_rwkv_scan_body, nb, n_pairs, tc),
        grid=(n // nb, t // tc),
        in_specs=[seq] * 6 + [st, full(ones_k), full(expand), full(eye)],
        out_specs=[seq, st],
        out_shape=[jax.ShapeDtypeStruct((n, t, d_a), _f32),
                   jax.ShapeDtypeStruct(s0.shape, _f32)],
        compiler_params=_params("arbitrary", "arbitrary"),
        name="rwkv_scan",
    )(r, w, k, v, av, bv, s0, ones_k, expand, eye)


def _pack_state(s):
    n, h = s.shape[:2]
    return s.reshape(n, h // 2, 2, HEAD_DIM, HEAD_DIM).transpose(0, 1, 3, 2, 4).reshape(n, h // 2, HEAD_DIM, LANES)


def _unpack_state(s):
    n, hp = s.shape[:2]
    return s.reshape(n, hp, HEAD_DIM, 2, HEAD_DIM).transpose(0, 1, 3, 2, 4).reshape(n, hp * 2, HEAD_DIM, HEAD_DIM)


def _mix_out_body(o_ref, bonus_ref, g_ref, ob_ref, x_ref, gate_ref, lw_ref, lb_ref, ones_ref, wa_ref, wb_ref,
                  out_ref):
    ones_bd = ones_ref[...]
    o = o_ref[...]
    inv = 1.0 / HEAD_DIM
    mean = _segsum(o, ones_bd) * inv
    dlt = o - mean
    var = _segsum(dlt * dlt, ones_bd) * inv
    on = dlt * lax.rsqrt(var + LNX_EPS) * lw_ref[...] + lb_ref[...]
    oa = (on + bonus_ref[...]) * g_ref[...]
    y = _dot(_bf(oa), wa_ref[...]) + _dot(ob_ref[...], wb_ref[...])
    out_ref[...] = x_ref[...] + gate_ref[0] * y


def _mix_out(o, bonus, g, o_b, x, gate, lnx_w, lnx_b, w_out, tm, rows_per_group):
    m, d = x.shape
    d_a = o.shape[1]
    ones_bd = _ones_blockdiag(d_a, HEAD_DIM)
    wa = _bf(w_out[:d_a])
    wb = _bf(w_out[d_a:])
    gs = _group_spec(tm, rows_per_group, d)
    full = lambda a: pl.BlockSpec(a.shape, lambda i: (0, 0))
    rowa = pl.BlockSpec((tm, d_a), lambda i: (i, 0))
    rowd = pl.BlockSpec((tm, d), lambda i: (i, 0))
    lw = lnx_w.reshape(1, -1)
    lb = lnx_b.reshape(1, -1)
    return pl.pallas_call(
        _mix_out_body,
        grid=(m // tm,),
        in_specs=[rowa, rowa, rowa, pl.BlockSpec((tm, o_b.shape[1]), lambda i: (i, 0)), rowd, gs(gate.shape[1]),
                  full(lw), full(lb), full(ones_bd), full(wa), full(wb)],
        out_specs=rowd,
        out_shape=jax.ShapeDtypeStruct((m, d), _f32),
        compiler_params=_params("arbitrary"),
        name="mix_out",
    )(o, bonus, g, o_b, x, gate, lw, lb, ones_bd, wa, wb)


def _key_to_float(u):
    key = u ^ jnp.int32(-2147483648)
    bits = jnp.where(key >= 0, key, key ^ jnp.int32(0x7FFFFFFF))
    return lax.bitcast_convert_type(bits, _f32)


def _kth_largest(count_ge, k_row, shape):
    def body(it, u):
        bit = jnp.left_shift(jnp.int32(1), 31 - it)
        cand_u = u | bit
        ok = count_ge(_key_to_float(cand_u)) >= k_row
        return jnp.where(ok, cand_u, u)

    u = lax.fori_loop(0, 32, body, jnp.zeros(shape, jnp.int32))
    return _key_to_float(u)


def _tie_cut(count_eq_lt, budget, n_bits, shape):
    def body(it, c):
        cand = c | jnp.left_shift(jnp.int32(1), n_bits - 1 - it)
        ok = count_eq_lt(cand) <= budget
        return jnp.where(ok, cand, c)

    return lax.fori_loop(0, n_bits, body, jnp.zeros(shape, jnp.int32))


def _dsa_prompt_body(n_heads, k_sel, t_len,
                     far_ref, qt_ref, qit_ref, wit_ref, k_ref, vt_ref, ki_ref, near_ref,
                     ot_ref, sc_ref, cut_ref, qz_ref, m_ref, l_ref, acc_ref):
    qb = qt_ref.shape[2]
    i = pl.program_id(1)
    q0 = i * qb
    qpos = q0 + lax.broadcasted_iota(jnp.int32, (1, qb), 1)
    n_kb = (q0 + qb + KEY_BLOCK - 1) // KEY_BLOCK
    row_kb = lax.broadcasted_iota(jnp.int32, (KEY_BLOCK, qb), 0)

    @pl.when(i == 0)
    def _():
        sc_ref[...] = jnp.full(sc_ref.shape, -jnp.inf, _f32)

    w8 = (wit_ref[0] * (H_IDX ** -0.5)) * (D_IDX ** -0.5)

    qi_all = jnp.concatenate([qit_ref[0, h * D_IDX:(h + 1) * D_IDX, :] for h in range(H_IDX)], axis=1)

    def score_block(kb, carry):
        c0 = pl.multiple_of(kb * KEY_BLOCK, KEY_BLOCK)
        s_all = _dot(ki_ref[0, pl.ds(c0, KEY_BLOCK), :], qi_all)
        acc = w8[0:1, :] * jnp.maximum(s_all[:, :qb], 0.0)
        for h in range(1, H_IDX):
            acc = acc + w8[h:h + 1, :] * jnp.maximum(s_all[:, h * qb:(h + 1) * qb], 0.0)
        sc_ref[pl.ds(c0, KEY_BLOCK), :] = jnp.where(c0 + row_kb <= qpos, acc, -jnp.inf)
        return carry

    lax.fori_loop(0, n_kb, score_block, 0)

    cb = min(COUNT_BLOCK, t_len)
    n_cb = (q0 + qb + cb - 1) // cb
    row_cb = lax.broadcasted_iota(jnp.int32, (cb, qb), 0)

    def count(pred):
        def body(kb, acc):
            c0 = pl.multiple_of(kb * cb, cb)
            hit = jnp.where(pred(sc_ref[pl.ds(c0, cb), :], c0 + row_cb), 1.0, 0.0)
            for j in range(cb // COUNT_ROWS):
                acc = acc + hit[j * COUNT_ROWS:(j + 1) * COUNT_ROWS, :]
            return acc
        acc = lax.fori_loop(0, n_cb, body, jnp.zeros((COUNT_ROWS, qb), _f32))
        return jnp.sum(acc, axis=0, keepdims=True)

    k_row = jnp.minimum(k_sel, qpos + 1).astype(_f32)
    thr = _kth_largest(lambda cand: count(lambda s, pos: s >= cand), k_row, (1, qb))
    n_gt = count(lambda s, pos: s > thr)
    n_eq = count(lambda s, pos: s == thr)
    budget = k_row - n_gt
    cut_ref[...] = jnp.full((1, qb), 2 * t_len, jnp.int32)

    @pl.when(jnp.max(n_eq - budget) > 0.5)
    def _():
        cut_ref[...] = _tie_cut(lambda c: count(lambda s, pos: (s == thr) & (pos < c)), budget,
                                (2 * t_len).bit_length(), (1, qb))

    cut = cut_ref[...]

    m_ref[...] = jnp.full(m_ref.shape, M_INIT, _f32)
    l_ref[...] = jnp.zeros(l_ref.shape, _f32)
    acc_ref[...] = jnp.zeros(acc_ref.shape, _f32)
    pair_row = lax.broadcasted_iota(jnp.int32, (LANES, qb), 0) // HEAD_DIM
    for hp in range(n_heads // 2):
        qp = qt_ref[0, hp * LANES:(hp + 1) * LANES, :] * (HEAD_DIM ** -0.5)
        zero = jnp.zeros_like(qp)
        qz_ref[hp] = jnp.concatenate([jnp.where(pair_row == 0, qp, zero), jnp.where(pair_row == 1, qp, zero)], axis=1)

    def attend(c0, width, bias_of_head):
        scb = sc_ref[pl.ds(c0, width), :]
        pos = c0 + lax.broadcasted_iota(jnp.int32, (width, qb), 0)
        sel = (scb > thr) | ((scb == thr) & (pos < cut))
        m_old = [m_ref[h] for h in range(n_heads)]
        l_old = [l_ref[h] for h in range(n_heads)]
        a_old = [acc_ref[h] for h in range(n_heads)]
        s2 = [_dot(k_ref[0, pl.ds(c0, width), hp * LANES:(hp + 1) * LANES], qz_ref[hp]) for hp in range(n_heads // 2)]
        m_out, l_out, a_out = [], [], []
        for h in range(n_heads):
            s = s2[h // 2][:, (h % 2) * qb:(h % 2 + 1) * qb] + bias_of_head(h)
            s = jnp.where(sel, s, NEG)
            m_new = jnp.maximum(m_old[h], jnp.max(s, axis=0, keepdims=True))
            alpha = jnp.exp(m_old[h] - m_new)
            p = jnp.exp(s - m_new)
            l_out.append(alpha * l_old[h] + jnp.sum(p, axis=0, keepdims=True))
            pv = _dot(vt_ref[0, h * HEAD_DIM:(h + 1) * HEAD_DIM, pl.ds(c0, width)], _bf(p))
            a_out.append(alpha * a_old[h] + pv)
            m_out.append(m_new)
        for h in range(n_heads):
            m_ref[h] = m_out[h]
            l_ref[h] = l_out[h]
            acc_ref[h] = a_out[h]

    n_far = jnp.maximum(i - 1, 0) // (KEY_BLOCK // qb)

    def far_block(kb, carry):
        attend(pl.multiple_of(kb * KEY_BLOCK, KEY_BLOCK), KEY_BLOCK, lambda h: far_ref[h])
        return carry

    lax.fori_loop(0, n_far, far_block, 0)

    def near_block(j, carry):
        attend(pl.multiple_of(j * qb, qb), qb, lambda h: near_ref[i - j, h])
        return carry

    lax.fori_loop(n_far * (KEY_BLOCK // qb), i + 1, near_block, 0)

    for h in range(n_heads):
        ot_ref[0, h * HEAD_DIM:(h + 1) * HEAD_DIM, :] = _bf(acc_ref[h] / l_ref[h])


def _dsa_prompt(q, k, v, qi, ki, wi, rel_bias, qb):
    n, t, d_b = q.shape
    n_heads = d_b // HEAD_DIM
    k_sel = min(TOPK_MAX, t // 4)
    tr = lambda a: a.transpose(0, 2, 1)
    n_near = KEY_BLOCK // qb + 1
    qq = jnp.arange(qb, dtype=jnp.int32)
    dist = (jnp.arange(n_near, dtype=jnp.int32)[:, None, None] * qb + qq[None, None, :] - qq[None, :, None])
    onehot = (_rel_bucket(dist)[..., None] == jnp.arange(N_BUCKETS, dtype=jnp.int32)).astype(_f32)
    near = jnp.einsum('dkqb,bh->dhkq', onehot, rel_bias, precision=lax.Precision.HIGHEST)
    far = rel_bias[N_BUCKETS - 1]
    res = lambda shape: pl.BlockSpec(shape, lambda b, i, *_: (b, 0, 0))
    blk = lambda w: pl.BlockSpec((1, w, qb), lambda b, i, *_: (b, 0, i))
    grid_spec = pltpu.PrefetchScalarGridSpec(
        num_scalar_prefetch=0,
        grid=(n, t // qb),
        in_specs=[pl.BlockSpec(memory_space=pltpu.SMEM),
                  blk(d_b), blk(H_IDX * D_IDX), blk(H_IDX),
                  res((1, t, d_b)), res((1, d_b, t)), res((1, t, D_IDX)),
                  pl.BlockSpec(near.shape, lambda b, i, *_: (0, 0, 0, 0))],
        out_specs=blk(d_b),
        scratch_shapes=[pltpu.VMEM((t, qb), _f32), pltpu.VMEM((1, qb), jnp.int32),
                        pltpu.VMEM((n_heads // 2, LANES, 2 * qb), _bf16),
                        pltpu.VMEM((n_heads, 1, qb), _f32), pltpu.VMEM((n_heads, 1, qb), _f32),
                        pltpu.VMEM((n_heads, HEAD_DIM, qb), _f32)],
    )
    o_t = pl.pallas_call(
        functools.partial(_dsa_prompt_body, n_heads, k_sel, t),
        grid_spec=grid_spec,
        out_shape=jax.ShapeDtypeStruct((n, d_b, t), _bf16),
        compiler_params=_params("arbitrary", "arbitrary"),
        name="dsa_prompt",
    )(far, tr(q), tr(qi), tr(wi), _bf(k), tr(_bf(v)), _bf(ki), near)
    return tr(o_t)


def _dsa_sample_scores_body(n_heads, k_sel, n_pages, page, pg, pt_ref, q_ref, qi_ref, wi_ref, kn_ref, kin_ref, *rest):
    cki_refs, ck_refs = rest[:pg], rest[pg:2 * pg]
    btab_ref, bnew_ref, p_ref, pnew_ref, sc_ref, lg_ref = rest[2 * pg:]
    step = pl.program_id(1)
    past = n_pages * page
    w8 = _bf(wi_ref[0] * (H_IDX ** -0.5)).astype(_f32)
    relu_bf = lambda s: _bf(jnp.maximum(s, 0.0)).astype(_f32)
    qi8 = qi_ref[0]
    q8 = q_ref[0]

    for j in range(pg):
        c0 = pl.multiple_of((step * pg + j) * page, page)
        s8 = _dot_nt(qi8, _bf(cki_refs[j][0, 0])) * (D_IDX ** -0.5)
        sc_ref[:, pl.ds(c0, page)] = jnp.sum(w8 * relu_bf(s8), axis=0, keepdims=True)
        rows = [_dot_nt(q8, _bf(ck_refs[j][0, 0, :, h, :]))[h:h + 1] for h in range(n_heads)]
        lg = jnp.concatenate(rows, axis=0) * (HEAD_DIM ** -0.5)
        lg_ref[:, pl.ds(c0, page)] = lg + btab_ref[:, pl.ds(c0, page)]

    @pl.when(step == n_pages // pg - 1)
    def _():
        sc = sc_ref[...]
        s_new8 = jnp.sum(qi8.astype(_f32) * _bf(kin_ref[0]).astype(_f32), axis=1, keepdims=True) * (D_IDX ** -0.5)
        s_new = jnp.sum(w8 * relu_bf(s_new8), axis=0, keepdims=True)
        pos = lax.broadcasted_iota(jnp.int32, sc.shape, 1)

        def count(pred):
            hits = jnp.sum(jnp.where(pred(sc, pos), 1.0, 0.0), axis=1, keepdims=True)
            return hits + jnp.where(pred(s_new, past), 1.0, 0.0)

        k_row = jnp.full((1, 1), float(k_sel), _f32)
        thr = _kth_largest(lambda cand: count(lambda s, ps: s >= cand), k_row, (1, 1))
        budget = k_row - count(lambda s, ps: s > thr)
        cut = _tie_cut(lambda c: count(lambda s, ps: (s == thr) & (ps < c)), budget,
                       (2 * (past + 1)).bit_length(), (1, 1))
        sel = (sc > thr) | ((sc == thr) & (pos < cut))
        sel_new = (s_new > thr) | ((s_new == thr) & (past < cut))

        kn = _bf(kn_ref[0]).astype(_f32)
        lg_new = jnp.sum(q8.astype(_f32) * kn, axis=1, keepdims=True) * (HEAD_DIM ** -0.5) + bnew_ref[...]
        lg_new = jnp.where(sel_new, lg_new, NEG)
        s_all = jnp.where(sel, lg_ref[...], NEG)
        m = jnp.maximum(jnp.maximum(jnp.max(s_all, axis=1, keepdims=True), lg_new), M_INIT)
        pr = jnp.exp(s_all - m)
        pr_new = jnp.exp(lg_new - m)
        l = jnp.sum(pr, axis=1, keepdims=True) + pr_new
        p_ref[0] = pr
        col = lax.broadcasted_iota(jnp.int32, (n_heads, 2), 1)
        pnew_ref[0] = jnp.where(col == 0, pr_new, l)


def _dsa_sample_pv_body(n_heads, n_pages, page, pg, pt_ref, p_ref, pnew_ref, vn_ref, *rest):
    cv_refs, (o_ref, acc_ref) = rest[:pg], rest[pg:]
    step = pl.program_id(1)

    @pl.when(step == 0)
    def _():
        acc_ref[...] = pnew_ref[0][:, 0:1] * _bf(vn_ref[0]).astype(_f32)

    acc = acc_ref[...]
    for j in range(pg):
        c0 = pl.multiple_of((step * pg + j) * page, page)
        pb = _bf(p_ref[0, :, pl.ds(c0, page)])
        rows = [_dot(pb, _bf(cv_refs[j][0, 0, :, h, :]))[h:h + 1] for h in range(n_heads)]
        acc = acc + jnp.concatenate(rows, axis=0)
    acc_ref[...] = acc

    @pl.when(step == n_pages // pg - 1)
    def _():
        o_ref[0] = _bf(acc / pnew_ref[0][:, 1:2])


def _dsa_sample(q, k_new, v_new, qi, ki_new, wi, ck, cv, cki, layer, page_table, rel_bias):
    b, _, d_b = q.shape
    n_heads = d_b // HEAD_DIM
    n_pages = page_table.shape[1]
    page = ck.shape[2]
    past = n_pages * page
    k_sel = min(TOPK_MAX, (past + 1) // 4)
    pg = max(g for g in (1, 2, 4) if n_pages % g == 0)
    kpos = jnp.arange(past, dtype=jnp.int32)
    btab = rel_bias[_rel_bucket(past - kpos)].T
    bnew = rel_bias[_rel_bucket(jnp.zeros((1,), jnp.int32))].T
    heads = lambda a: a.reshape(b, n_heads, HEAD_DIM)
    pt = page_table.reshape(-1)
    per_b = lambda shape: pl.BlockSpec((1,) + shape, lambda i, s, pt: (i,) + (0,) * len(shape))
    const = lambda a: pl.BlockSpec(a.shape, lambda i, s, pt: (0,) * a.ndim)

    def paged(tail, j):
        return pl.BlockSpec((1, 1, page) + tail,
                            lambda i, s, pt: (layer, pt[i * n_pages + s * pg + j], 0) + (0,) * len(tail))

    grid = (b, n_pages // pg)
    probs, p_new = pl.pallas_call(
        functools.partial(_dsa_sample_scores_body, n_heads, k_sel, n_pages, page, pg),
        grid_spec=pltpu.PrefetchScalarGridSpec(
            num_scalar_prefetch=1, grid=grid,
            in_specs=[per_b((n_heads, HEAD_DIM)), per_b((H_IDX, D_IDX)), per_b((H_IDX, 1)),
                      per_b((n_heads, HEAD_DIM)), per_b((1, D_IDX))]
                     + [paged((D_IDX,), j) for j in range(pg)]
                     + [paged((n_heads, HEAD_DIM), j) for j in range(pg)]
                     + [const(btab), const(bnew)],
            out_specs=[per_b((n_heads, past)), per_b((n_heads, 2))],
            scratch_shapes=[pltpu.VMEM((1, past), _f32), pltpu.VMEM((n_heads, past), _f32)]),
        out_shape=[jax.ShapeDtypeStruct((b, n_heads, past), _f32), jax.ShapeDtypeStruct((b, n_heads, 2), _f32)],
        compiler_params=_params("arbitrary", "arbitrary"),
        name="dsa_sample_scores",
    )(pt, heads(q), qi.reshape(b, H_IDX, D_IDX), wi.reshape(b, H_IDX, 1), heads(k_new), ki_new,
      *([cki] * pg), *([ck] * pg), btab, bnew)
    out = pl.pallas_call(
        functools.partial(_dsa_sample_pv_body, n_heads, n_pages, page, pg),
        grid_spec=pltpu.PrefetchScalarGridSpec(
            num_scalar_prefetch=1, grid=grid,
            in_specs=[per_b((n_heads, past)), per_b((n_heads, 2)), per_b((n_heads, HEAD_DIM))]
                     + [paged((n_heads, HEAD_DIM), j) for j in range(pg)],
            out_specs=per_b((n_heads, HEAD_DIM)),
            scratch_shapes=[pltpu.VMEM((n_heads, HEAD_DIM), _f32)]),
        out_shape=jax.ShapeDtypeStruct((b, n_heads, HEAD_DIM), _bf16),
        compiler_params=_params("arbitrary", "arbitrary"),
        name="dsa_sample_pv",
    )(pt, probs, p_new, heads(v_new), *([cv] * pg))
    return out.reshape(b, 1, d_b)


def _ffn_body(routed, final_norm, n_exp,
              x_ref, g_ref, sh_ref, sc_ref, gate_ref, rw_ref, rb_ref, w1_ref, w3_ref, w2_ref, gf_ref,
              o_ref, h_ref, acc_ref, dg_ref):
    e = pl.program_id(1)
    f = pl.program_id(2)
    first = (e == 0) & (f == 0)
    last = (e == pl.num_programs(1) - 1) & (f == pl.num_programs(2) - 1)

    @pl.when(first)
    def _():
        h = _norm_mod(x_ref[...], g_ref[...], sh_ref[0], sc_ref[0])
        h_ref[...] = _bf(h)
        acc_ref[...] = jnp.zeros(acc_ref.shape, _f32)
        if routed:
            logits = _dot(_bf(h), rw_ref[...]) + rb_ref[...]
            lane = lax.broadcasted_iota(jnp.int32, logits.shape, 1).astype(_f32)
            logits = jnp.where(lane < n_exp, logits, -jnp.inf)
            v1 = jnp.max(logits, axis=1, keepdims=True)
            i1 = jnp.min(jnp.where(logits == v1, lane, float(LANES)), axis=1, keepdims=True)
            rest = jnp.where(lane == i1, -jnp.inf, logits)
            v2 = jnp.max(rest, axis=1, keepdims=True)
            i2 = jnp.min(jnp.where(rest == v2, lane, float(LANES)), axis=1, keepdims=True)
            e2 = jnp.exp(v2 - v1)
            den = 1.0 + e2
            dg_ref[...] = jnp.where(lane == i1, 1.0 / den, 0.0) + jnp.where(lane == i2, e2 / den, 0.0)

    h = h_ref[...]
    a = _silu(_dot(h, w1_ref[0])) * _dot(h, w3_ref[0])
    y = _dot(_bf(a), w2_ref[0])
    if routed:
        lane = lax.broadcasted_iota(jnp.int32, dg_ref.shape, 1)
        y = y * jnp.sum(jnp.where(lane == e, dg_ref[...], 0.0), axis=1, keepdims=True)
    acc_ref[...] += y

    @pl.when(last)
    def _():
        y = x_ref[...] + gate_ref[0] * acc_ref[...]
        if final_norm:
            y = y * lax.rsqrt(jnp.mean(y * y, axis=-1, keepdims=True) + EPS) * gf_ref[...]
        o_ref[...] = y


def _ffn(x, g, shift, scale, gate, w13, w2, tm, tf, rows_per_group, router=None, final_g=None):
    m, d = x.shape
    n_exp, ff, _ = w2.shape
    routed = router is not None
    final_norm = final_g is not None
    if routed:
        rw, rb = router
        rw = _bf(jnp.pad(rw, ((0, 0), (0, LANES - n_exp))))
        rb = jnp.pad(rb.reshape(1, -1), ((0, 0), (0, LANES - n_exp)))
    else:
        rw = jnp.zeros((d, LANES), _bf16)
        rb = jnp.zeros((1, LANES), _f32)
    gf = final_g.reshape(1, -1) if final_norm else jnp.ones((1, d), _f32)
    nf = ff // tf
    gs = _group_spec(tm, rows_per_group, d)
    rowd = pl.BlockSpec((tm, d), lambda i, e, f: (i, 0))
    full = lambda a: pl.BlockSpec(a.shape, lambda i, e, f: (0, 0))
    return pl.pallas_call(
        functools.partial(_ffn_body, routed, final_norm, n_exp),
        grid=(m // tm, n_exp, nf),
        in_specs=[rowd, full(g), gs(shift.shape[1]), gs(scale.shape[1]), gs(gate.shape[1]), full(rw), full(rb),
                  pl.BlockSpec((1, d, tf), lambda i, e, f: (e, 0, f)),
                  pl.BlockSpec((1, d, tf), lambda i, e, f: (e, 0, f + nf)),
                  pl.BlockSpec((1, tf, d), lambda i, e, f: (e, f, 0)),
                  full(gf)],
        out_specs=rowd,
        out_shape=jax.ShapeDtypeStruct((m, d), _f32),
        scratch_shapes=[pltpu.VMEM((tm, d), _bf16), pltpu.VMEM((tm, d), _f32), pltpu.VMEM((tm, LANES), _f32)],
        compiler_params=_params("arbitrary", "arbitrary", "arbitrary"),
        name="moe_ffn" if routed else "dense_ffn",
    )(x, g, shift, scale, gate, rw, rb, w13, w13, w2, gf)


def _cmul(ar, ai, br, bi):
    return ar * br - ai * bi, ar * bi + ai * br


def _s5_io(u, b_ref, c_ref, d_ref, are_ref, aim_ref, h_of_bb):
    half = b_ref.shape[2] // 2
    bu = _dot(_bf(u), b_ref[0])
    bb_re, bb_im = _cmul(are_ref[0, 1:2], aim_ref[0, 1:2], bu[:, :half], bu[:, half:])
    h_re, h_im = h_of_bb(bb_re, bb_im)
    y = _dot(_bf(jnp.concatenate([h_re, h_im], axis=1)), c_ref[0]) + d_ref[0] * u
    return _bf(_gelu_tanh(y)), h_re, h_im


def _s5_scan_body(tc, u_ref, b_ref, c_ref, d_ref, are_ref, aim_ref, h0r_ref, h0i_ref,
                  y_ref, hr_ref, hi_ref, xr_ref, xi_ref):
    t_blk = pl.program_id(2)

    @pl.when(t_blk == 0)
    def _():
        hr_ref[0, 0] = h0r_ref[0, 0]
        hi_ref[0, 0] = h0i_ref[0, 0]

    width = are_ref.shape[2]
    a1 = (jnp.broadcast_to(are_ref[0, 0:1], (SUBLANES, width)), jnp.broadcast_to(aim_ref[0, 0:1], (SUBLANES, width)))
    a2 = _cmul(*a1, *a1)
    a3 = _cmul(*a2, *a1)
    a4 = _cmul(*a2, *a2)
    a5 = _cmul(*a4, *a1)
    a6 = _cmul(*a4, *a2)
    a7 = _cmul(*a4, *a3)
    a8 = _cmul(*a4, *a4)
    row = lax.broadcasted_iota(jnp.int32, (SUBLANES, width), 0)
    lvl = [tuple(jnp.where(row >= s, c, 0.0) for c in a) for s, a in ((1, a1), (2, a2), (4, a4))]
    pw = []
    for comp in range(2):
        acc = a8[comp]
        for s, a in ((6, a7), (5, a6), (4, a5), (3, a4), (2, a3), (1, a2), (0, a1)):
            acc = jnp.where(row == s, a[comp], acc)
        pw.append(acc)

    def scan_chunk(bb_re, bb_im):
        xr_ref[...] = bb_re
        xi_ref[...] = bb_im

        def tile(j, carry):
            cr, ci = carry
            r0 = pl.multiple_of(j * SUBLANES, SUBLANES)
            xr = xr_ref[pl.ds(r0, SUBLANES), :]
            xi = xi_ref[pl.ds(r0, SUBLANES), :]
            for (s, (fr, fi)) in zip((1, 2, 4), lvl):
                sr = pltpu.roll(xr, s, axis=0)
                si = pltpu.roll(xi, s, axis=0)
                pr, pi = _cmul(fr, fi, sr, si)
                xr = xr + pr
                xi = xi + pi
            pr, pi = _cmul(pw[0], pw[1], jnp.broadcast_to(cr, xr.shape), jnp.broadcast_to(ci, xi.shape))
            xr = xr + pr
            xi = xi + pi
            xr_ref[pl.ds(r0, SUBLANES), :] = xr
            xi_ref[pl.ds(r0, SUBLANES), :] = xi
            return xr[SUBLANES - 1:SUBLANES], xi[SUBLANES - 1:SUBLANES]

        cr, ci = lax.fori_loop(0, tc // SUBLANES, tile, (hr_ref[0, 0], hi_ref[0, 0]))
        hr_ref[0, 0] = cr
        hi_ref[0, 0] = ci
        return xr_ref[...], xi_ref[...]

    y, _, _ = _s5_io(u_ref[0], b_ref, c_ref, d_ref, are_ref, aim_ref, scan_chunk)
    y_ref[0] = y


def _s5_step_body(u_ref, b_ref, c_ref, d_ref, are_ref, aim_ref, h0r_ref, h0i_ref, y_ref, hr_ref, hi_ref):
    def one_step(bb_re, bb_im):
        pr, pi = _cmul(are_ref[0, 0:1], aim_ref[0, 0:1], h0r_ref[0], h0i_ref[0])
        return pr + bb_re, pi + bb_im

    y, h_re, h_im = _s5_io(u_ref[...], b_ref, c_ref, d_ref, are_ref, aim_ref, one_step)
    y_ref[...] = y
    hr_ref[0] = h_re
    hi_ref[0] = h_im


def _s5_tables(a_re, a_im, log_step, b_re, b_im, c_re, c_im, d):
    g_c, p_c, ch = b_re.shape
    gs = LANES // ch
    ns = g_c // gs
    dt = jnp.exp(log_step.astype(_f32))[:, None]
    mag = jnp.exp(dt * a_re)
    ab_re = mag * jnp.cos(dt * a_im)
    ab_im = mag * jnp.sin(dt * a_im)
    den = a_re * a_re + a_im * a_im
    nr = ab_re - 1.0
    co_re = (nr * a_re + ab_im * a_im) / den
    co_im = (ab_im * a_re - nr * a_im) / den
    eye = jnp.eye(gs, dtype=_f32)

    def pack_b(b):
        b = b.reshape(ns, gs, p_c, ch)
        return jnp.einsum('sgpc,gh->sgchp', b, eye).reshape(ns, gs * ch, gs * p_c)

    def pack_c(c):
        c = c.reshape(ns, gs, ch, p_c)
        return jnp.einsum('sgcp,gh->sgphc', c, eye).reshape(ns, gs * p_c, gs * ch)

    b_pack = _bf(jnp.concatenate([pack_b(b_re), pack_b(b_im)], axis=2))
    c_pack = _bf(jnp.concatenate([pack_c(c_re), -pack_c(c_im)], axis=1))
    vec = lambda a, c: jnp.stack([a.reshape(ns, gs * p_c), c.reshape(ns, gs * p_c)], axis=1)
    return b_pack, c_pack, d.reshape(ns, 1, gs * ch), vec(ab_re, co_re), vec(ab_im, co_im)


def _s5_scan(u, h0_re, h0_im, tables, tc):
    n, t, d = u.shape
    b_pack, c_pack, dvec, are, aim = tables
    ns, _, w2 = b_pack.shape
    w = w2 // 2
    h0r = h0_re.reshape(n, ns, 1, w)
    h0i = h0_im.reshape(n, ns, 1, w)
    tab = lambda a: pl.BlockSpec((1,) + a.shape[1:], lambda b, s, j: (s, 0, 0))
    st = pl.BlockSpec((1, 1, 1, w), lambda b, s, j: (b, s, 0, 0))
    seq = pl.BlockSpec((1, tc, LANES), lambda b, s, j: (b, j, s))
    y, hr, hi = pl.pallas_call(
        functools.partial(_s5_scan_body, tc),
        grid=(n, ns, t // tc),
        in_specs=[seq, tab(b_pack), tab(c_pack), tab(dvec), tab(are), tab(aim), st, st],
        out_specs=[seq, st, st],
        out_shape=[jax.ShapeDtypeStruct((n, t, d), _bf16),
                   jax.ShapeDtypeStruct((n, ns, 1, w), _f32), jax.ShapeDtypeStruct((n, ns, 1, w), _f32)],
        scratch_shapes=[pltpu.VMEM((tc, w), _f32), pltpu.VMEM((tc, w), _f32)],
        compiler_params=_params("arbitrary", "arbitrary", "arbitrary"),
        name="s5_scan",
    )(u, b_pack, c_pack, dvec, are, aim, h0r, h0i)
    return y, hr.reshape(n, ns * w), hi.reshape(n, ns * w)


def _s5_step(u, h0_re, h0_im, tables):
    b, d = u.shape
    b_pack, c_pack, dvec, are, aim = tables
    ns, _, w2 = b_pack.shape
    w = w2 // 2
    h0r = h0_re.reshape(b, ns, w).transpose(1, 0, 2)
    h0i = h0_im.reshape(b, ns, w).transpose(1, 0, 2)
    tab = lambda a: pl.BlockSpec((1,) + a.shape[1:], lambda s: (s, 0, 0))
    st = pl.BlockSpec((1, b, w), lambda s: (s, 0, 0))
    col = pl.BlockSpec((b, LANES), lambda s: (0, s))
    y, hr, hi = pl.pallas_call(
        _s5_step_body,
        grid=(ns,),
        in_specs=[col, tab(b_pack), tab(c_pack), tab(dvec), tab(are), tab(aim), st, st],
        out_specs=[col, st, st],
        out_shape=[jax.ShapeDtypeStruct((b, d), _bf16),
                   jax.ShapeDtypeStruct((ns, b, w), _f32), jax.ShapeDtypeStruct((ns, b, w), _f32)],
        compiler_params=_params("arbitrary"),
        name="s5_step",
    )(u, b_pack, c_pack, dvec, are, aim, h0r, h0i)
    return y, hr.transpose(1, 0, 2).reshape(b, ns * w), hi.transpose(1, 0, 2).reshape(b, ns * w)


def _glu_out_body(y_ref, x_ref, gate_ref, wl_ref, wr_ref, o_ref):
    y = y_ref[...]
    z = _dot(y, wl_ref[...]) * jax.nn.sigmoid(_dot(y, wr_ref[...]))
    o_ref[...] = x_ref[...] + gate_ref[0] * z


def _glu_out(yg, x, gate, w_glu, tm, rows_per_group):
    m, d = x.shape
    wl = _bf(w_glu[:, :d])
    wr = _bf(w_glu[:, d:])
    gs = _group_spec(tm, rows_per_group, d)
    rowd = pl.BlockSpec((tm, d), lambda i: (i, 0))
    full = lambda a: pl.BlockSpec(a.shape, lambda i: (0, 0))
    return pl.pallas_call(
        _glu_out_body,
        grid=(m // tm,),
        in_specs=[rowd, rowd, gs(gate.shape[1]), full(wl), full(wr)],
        out_specs=rowd,
        out_shape=jax.ShapeDtypeStruct((m, d), _f32),
        compiler_params=_params("arbitrary"),
        name="glu_out",
    )(yg, x, gate, wl, wr)


def _tile(m, target):
    if m <= target:
        return m
    t = target
    while m % t:
        t -= SUBLANES
    return t


def _run_group(x, mods, is_prompt, st, wts):
    n, t, d = x.shape
    m = n * t
    depth = wts["norm_mix"].shape[0]
    d_a = d // 2
    d_b = d // 2
    n_heads = d_a // HEAD_DIM
    rows_per_group = t if is_prompt else m
    tm = _tile(t, 256) if is_prompt else m
    xf = x.reshape(m, d).astype(_f32)

    def mod_vecs(l):
        parts = jnp.split(mods[l], 6, axis=-1)
        if is_prompt:
            return [p.reshape(n, 1, d) for p in parts]
        return [p.reshape(1, m, d) for p in parts]

    outs = {k: [] for k in ("shift", "wkv", "k", "v", "kidx", "re", "im")}
    for l in range(depth):
        sh_m, sc_m, g_m, sh_f, sc_f, g_f = mod_vecs(l)
        i = l // 2
        last = l == depth - 1
        final_g = wts["norm_final"] if last else None
        norm_mix = wts["norm_mix"][l].reshape(1, d)
        norm_ffn = wts["norm_ffn"][l].reshape(1, d)
        if l % 2 == 0:
            p_a, q, k, v, qi, ki, wi = _in_proj(xf, norm_mix, sh_m, sc_m, wts["e_w_in"][i], tm, rows_per_group)
            if is_prompt:
                prev0 = st["shift"][i].reshape(n, 1, -1)
            else:
                prev0 = st["shift"][i].reshape(1, m, -1)
            r, w, k_mod, v_a, av, bv, g, bonus = _rwkv_prep(
                p_a, prev0.astype(_f32), wts["e_mu"][i], wts["e_w0"][i], wts["e_w2"][i], wts["e_a0"][i],
                wts["e_a2"][i], wts["e_g2"][i], wts["e_k_k"][i], wts["e_k_a"][i], wts["e_r_k"][i],
                tm, rows_per_group)
            seq = lambda a: a.reshape(n, t, d_a)
            nb = n if is_prompt else _tile(n, 4)
            tc = _tile(t, 256)
            o, s_fin = _rwkv_scan(seq(r), seq(w), seq(k_mod), seq(v_a), seq(av), seq(bv),
                                  _pack_state(st["wkv"][i].astype(_f32)), nb, tc)
            if is_prompt:
                o_b = _dsa_prompt(q.reshape(n, t, d_b), k.reshape(n, t, d_b), v.reshape(n, t, d_b),
                                  qi.reshape(n, t, -1), ki.reshape(n, t, -1), wi.reshape(n, t, -1),
                                  wts["rel_bias"], 128)
            else:
                o_b = _dsa_sample(q.reshape(n, t, d_b), k.reshape(n, t, d_b), v.reshape(n, t, d_b),
                                  qi.reshape(n, t, -1), ki.reshape(n, t, -1), wi.reshape(n, t, -1),
                                  st["cache_k"], st["cache_v"], st["cache_kidx"], i,
                                  st["page_table"], wts["rel_bias"])
            xf = _mix_out(o.reshape(m, d_a), bonus, g, o_b.reshape(m, d_b), xf, g_m,
                          wts["e_lnx_w"][i], wts["e_lnx_b"][i], wts["e_w_out"][i], tm, rows_per_group)
            outs["shift"].append(p_a.reshape(n, t, -1)[:, -1].astype(x.dtype))
            outs["wkv"].append(_unpack_state(s_fin))
            outs["k"].append(k.reshape(n, t, n_heads, HEAD_DIM))
            outs["v"].append(v.reshape(n, t, n_heads, HEAD_DIM))
            outs["kidx"].append(ki.reshape(n, t, D_IDX))
            tm_f = _tile(t, 1024) if is_prompt else m
            ff = wts["ffn_w2"][i].shape[0]
            xf = _ffn(xf, norm_ffn, sh_f, sc_f, g_f, wts["ffn_w13"][i][None], wts["ffn_w2"][i][None],
                      tm_f, _tile(ff, 256) if ff % 256 == 0 else ff, rows_per_group, final_g=final_g)
        else:
            tables = _s5_tables(wts["o_a_re"][i], wts["o_a_im"][i], wts["o_log_step"][i], wts["o_b_re"][i],
                                wts["o_b_im"][i], wts["o_c_re"][i], wts["o_c_im"][i], wts["o_d"][i])
            u = _norm_mod_call(xf, norm_mix, sh_m, sc_m, tm, rows_per_group)
            h0r = st["ssm_re"][i].reshape(n, -1).astype(_f32)
            h0i = st["ssm_im"][i].reshape(n, -1).astype(_f32)
            if is_prompt:
                yg, hr, hi = _s5_scan(u.reshape(n, t, d), h0r, h0i, tables, _tile(t, 256))
            else:
                yg, hr, hi = _s5_step(u, h0r, h0i, tables)
            xf = _glu_out(yg.reshape(m, d), xf, g_m, wts["o_w_glu"][i], tm, rows_per_group)
            g_c = d // CH_G
            outs["re"].append(hr.reshape(n, g_c, P_C))
            outs["im"].append(hi.reshape(n, g_c, P_C))
            tm_f = _tile(t, 1024) if is_prompt else m
            ff = wts["moe_w2"][i].shape[1]
            xf = _ffn(xf, norm_ffn, sh_f, sc_f, g_f, wts["moe_w13"][i], wts["moe_w2"][i],
                      tm_f, _tile(ff, 512) if ff % 512 == 0 else ff, rows_per_group,
                      router=(wts["o_router_w"][i], wts["o_router_b"][i]), final_g=final_g)
    y = xf.reshape(n, t, d).astype(x.dtype)
    return (y, jnp.stack(outs["shift"]), jnp.stack(outs["wkv"]), jnp.stack(outs["k"]), jnp.stack(outs["v"]),
            jnp.stack(outs["kidx"]), jnp.stack(outs["re"]), jnp.stack(outs["im"]))


def kernel(x_prompt, x_sample, cache_k, cache_v, cache_kidx, state_shift, state_wkv, state_ssm_re, state_ssm_im, page_table, c_prompt, c_sample, norm_mix, norm_ffn, ada_w, ada_b, rel_bias, norm_final, e_w_in, e_mu, e_w0, e_w2, e_a0, e_a2, e_g2, e_k_k, e_k_a, e_r_k, e_lnx_w, e_lnx_b, e_w_out, e_ffn_w13, e_ffn_w2, o_a_re, o_a_im, o_log_step, o_b_re, o_b_im, o_c_re, o_c_im, o_d, o_w_glu, o_router_w, o_router_b, o_moe_w13, o_moe_w2):
    nb, _, d = x_prompt.shape
    n_dec = x_sample.shape[0]
    n_even = e_w_in.shape[0]
    n_odd = o_a_re.shape[0]
    d_a = d // 2
    n_heads = d_a // HEAD_DIM
    shift_w = state_shift.shape[-1]

    c_all = jnp.concatenate([c_prompt, c_sample], axis=0)
    pad = (-c_all.shape[0]) % SUBLANES
    mods = _ada(jnp.pad(c_all, ((0, pad), (0, 0))), ada_w, ada_b)
    mods_p = mods[:, :nb]
    mods_s = mods[:, nb:nb + n_dec]

    wts = dict(norm_mix=norm_mix, norm_ffn=norm_ffn, norm_final=norm_final, rel_bias=rel_bias,
               e_w_in=e_w_in, e_mu=e_mu, e_w0=e_w0, e_w2=e_w2, e_a0=e_a0, e_a2=e_a2, e_g2=e_g2, e_k_k=e_k_k,
               e_k_a=e_k_a, e_r_k=e_r_k, e_lnx_w=e_lnx_w, e_lnx_b=e_lnx_b, e_w_out=e_w_out,
               ffn_w13=_bf(e_ffn_w13), ffn_w2=_bf(e_ffn_w2),
               o_a_re=o_a_re, o_a_im=o_a_im, o_log_step=o_log_step, o_b_re=o_b_re, o_b_im=o_b_im,
               o_c_re=o_c_re, o_c_im=o_c_im, o_d=o_d, o_w_glu=o_w_glu, o_router_w=o_router_w,
               o_router_b=o_router_b, moe_w13=_bf(o_moe_w13), moe_w2=_bf(o_moe_w2))

    st_p = dict(shift=jnp.zeros((n_even, nb, shift_w), x_prompt.dtype),
                wkv=jnp.zeros((n_even, nb, n_heads, HEAD_DIM, HEAD_DIM), _f32),
                ssm_re=jnp.zeros((n_odd, nb, d // CH_G, P_C), _f32),
                ssm_im=jnp.zeros((n_odd, nb, d // CH_G, P_C), _f32))
    st_s = dict(shift=state_shift, wkv=state_wkv, ssm_re=state_ssm_re, ssm_im=state_ssm_im,
                cache_k=cache_k, cache_v=cache_v, cache_kidx=cache_kidx, page_table=page_table)
    out_p = _run_group(x_prompt, mods_p, True, st_p, wts)
    out_s = _run_group(x_sample, mods_s, False, st_s, wts)
    return (out_p[0], out_s[0]) + out_p[1:] + out_s[1:]
```

```python
import functools
import math

import jax
import jax.numpy as jnp
from jax import lax
from jax.experimental import pallas as pl
from jax.experimental.pallas import tpu as pltpu

HEAD_DIM = 64
LORA_W = 64
LORA_A = 64
LORA_G = 128
H_IDX = 8
D_IDX = 64
TOPK_MAX = 256
N_BUCKETS = 32
MAX_DIST = 128
CH_G = 16
P_C = 64
TOP_E = 2
EPS = 1e-6
LNX_EPS = 64e-5

LANES = 128
SUBLANES = 8
VMEM_LIMIT = 56 * 1024 * 1024
KEY_BLOCK = 256
COUNT_BLOCK = 1024
COUNT_ROWS = 64
NEG = -2.0e30
M_INIT = -1.0e30

_bf16 = jnp.bfloat16
_f32 = jnp.float32


def _bf(x):
    return x.astype(_bf16)


def _dot(a, b):
    return jnp.dot(a, b, preferred_element_type=_f32)


def _dot_nt(a, b):
    return lax.dot_general(a, b, (((1,), (1,)), ((), ())), preferred_element_type=_f32)


def _params(*sem):
    return pltpu.CompilerParams(dimension_semantics=sem, vmem_limit_bytes=VMEM_LIMIT)


def _split_bf16(x, parts):
    out = []
    for _ in range(parts - 1):
        hi = _bf(x)
        out.append(hi)
        x = x - hi.astype(_f32)
    out.append(_bf(x))
    return out


def _segsum(x, ones_bd):
    hi, mid, lo = _split_bf16(x, 3)
    return _dot(hi, ones_bd) + _dot(mid, ones_bd) + _dot(lo, ones_bd)


def _norm_mod(x, g, shift, scale):
    ms = jnp.mean(x * x, axis=-1, keepdims=True)
    return (x * lax.rsqrt(ms + EPS) * g) * (1.0 + scale) + shift


def _silu(x):
    return x * jax.nn.sigmoid(x)


def _gelu_tanh(x):
    return 0.5 * x * (1.0 + jnp.tanh(math.sqrt(2.0 / math.pi) * (x + 0.044715 * (x * x * x))))


def _softplus(x):
    return jnp.maximum(x, 0.0) + jnp.log(1.0 + jnp.exp(-jnp.abs(x)))


def _rel_bucket(dist):
    max_exact = N_BUCKETS // 2
    n = jnp.maximum(dist, 0)
    nf = jnp.maximum(n, 1).astype(_f32)
    large = max_exact + (jnp.log(nf / max_exact) / math.log(MAX_DIST / max_exact) * (N_BUCKETS - max_exact)).astype(jnp.int32)
    return jnp.where(n < max_exact, n, jnp.minimum(large, N_BUCKETS - 1))


def _group_spec(rows_per_block, rows_per_group, width):
    def spec(r):
        return pl.BlockSpec((1, r, width), lambda i, *_: ((i * rows_per_block) // rows_per_group, 0, 0))
    return spec


def _ada_body(c_ref, w_ref, b_ref, o_ref):
    o_ref[0] = _dot(_bf(_silu(c_ref[...])), _bf(w_ref[0])) + b_ref[0]


def _ada(c, ada_w, ada_b):
    depth, d, n6 = ada_w.shape
    rows = c.shape[0]
    tn = n6 // 4
    return pl.pallas_call(
        _ada_body,
        grid=(depth, n6 // tn),
        in_specs=[pl.BlockSpec((rows, d), lambda l, j: (0, 0)),
                  pl.BlockSpec((1, d, tn), lambda l, j: (l, 0, j)),
                  pl.BlockSpec((1, 1, tn), lambda l, j: (l, 0, j))],
        out_specs=pl.BlockSpec((1, rows, tn), lambda l, j: (l, 0, j)),
        out_shape=jax.ShapeDtypeStruct((depth, rows, n6), _f32),
        compiler_params=_params("arbitrary", "arbitrary"),
        name="ada_mod",
    )(c, ada_w, ada_b.reshape(depth, 1, n6))


def _norm_mod_body(x_ref, g_ref, sh_ref, sc_ref, o_ref):
    o_ref[...] = _norm_mod(x_ref[...], g_ref[...], sh_ref[0], sc_ref[0])


def _norm_mod_call(x, g, shift, scale, tm, rows_per_group):
    m, d = x.shape
    gs = _group_spec(tm, rows_per_group, d)
    return pl.pallas_call(
        _norm_mod_body,
        grid=(m // tm,),
        in_specs=[pl.BlockSpec((tm, d), lambda i: (i, 0)),
                  pl.BlockSpec((1, d), lambda i: (0, 0)),
                  gs(shift.shape[1]), gs(scale.shape[1])],
        out_specs=pl.BlockSpec((tm, d), lambda i: (i, 0)),
        out_shape=jax.ShapeDtypeStruct((m, d), _f32),
        compiler_params=_params("arbitrary"),
        name="norm_mod",
    )(x, g, shift, scale)


def _in_proj_body(x_ref, g_ref, sh_ref, sc_ref, wa_ref, wq_ref, wk_ref, wv_ref, wqi_ref, wkw_ref,
                  pa_ref, q_ref, k_ref, v_ref, qi_ref, ki_ref, wi_ref):
    h = _bf(_norm_mod(x_ref[...], g_ref[...], sh_ref[0], sc_ref[0]))
    pa_ref[...] = _dot(h, wa_ref[...])
    q_ref[...] = _bf(_dot(h, wq_ref[...]))
    k_ref[...] = _dot(h, wk_ref[...])
    v_ref[...] = _dot(h, wv_ref[...])
    qi_ref[...] = _bf(_dot(h, wqi_ref[...]))
    kw = _dot(h, wkw_ref[...])
    ki_ref[...] = kw[:, :D_IDX]
    wi_ref[...] = kw[:, D_IDX:D_IDX + H_IDX]


def _in_proj(x, g, shift, scale, w_in, tm, rows_per_group):
    m, d = x.shape
    d_a = d // 2
    d_b = d // 2
    shift_w = 3 * d_a + LORA_W + LORA_A + LORA_G
    c1 = shift_w + 3 * d_b
    c2 = c1 + H_IDX * D_IDX
    wa = _bf(w_in[:, :shift_w])
    wq = _bf(w_in[:, shift_w:shift_w + d_b])
    wk = _bf(w_in[:, shift_w + d_b:shift_w + 2 * d_b])
    wv = _bf(w_in[:, shift_w + 2 * d_b:c1])
    wqi = _bf(w_in[:, c1:c2])
    wkw = _bf(jnp.pad(w_in[:, c2:], ((0, 0), (0, LANES - D_IDX - H_IDX))))
    gs = _group_spec(tm, rows_per_group, d)
    full = lambda a: pl.BlockSpec(a.shape, lambda i: (0, 0))
    row = lambda w: pl.BlockSpec((tm, w), lambda i: (i, 0))
    widths = [(shift_w, _f32), (d_b, _bf16), (d_b, _f32), (d_b, _f32), (H_IDX * D_IDX, _bf16),
              (D_IDX, _f32), (H_IDX, _f32)]
    return pl.pallas_call(
        _in_proj_body,
        grid=(m // tm,),
        in_specs=[row(d), pl.BlockSpec((1, d), lambda i: (0, 0)), gs(shift.shape[1]), gs(scale.shape[1]),
                  full(wa), full(wq), full(wk), full(wv), full(wqi), full(wkw)],
        out_specs=[row(w) for w, _ in widths],
        out_shape=[jax.ShapeDtypeStruct((m, w), dt) for w, dt in widths],
        compiler_params=_params("arbitrary"),
        name="in_proj",
    )(x, g, shift, scale, wa, wq, wk, wv, wqi, wkw)


def _rwkv_prep_body(seq_is_one, rows_per_group, tm,
                    p_ref, pprev_ref, prev0_ref, mu_ref, w0_ref, a0_ref, kk_ref, ka_ref, rk_ref,
                    wwa_ref, g2_ref, ones_ref,
                    r_ref, w_ref, k_ref, v_ref, av_ref, bv_ref, g_ref, bonus_ref):
    d_a = r_ref.shape[1]
    p = p_ref[...]
    if seq_is_one:
        p_prev = prev0_ref[0]
    else:
        i = pl.program_id(0)
        first = (i * tm) % rows_per_group == 0
        prev_row = jnp.where(first, prev0_ref[0], pprev_ref[SUBLANES - 1:SUBLANES, :])
        rolled = pltpu.roll(p, 1, axis=0)
        row_id = lax.broadcasted_iota(jnp.int32, p.shape, 0)
        p_prev = jnp.where(row_id == 0, prev_row, rolled)
    ps = p + (p_prev - p) * mu_ref[...]
    r = ps[:, :d_a]
    k = ps[:, d_a:2 * d_a]
    v = ps[:, 2 * d_a:3 * d_a]
    xwa = ps[:, 3 * d_a:3 * d_a + LORA_W + LORA_A]
    xg = ps[:, 3 * d_a + LORA_W + LORA_A:]
    lane = lax.broadcasted_iota(jnp.int32, xwa.shape, 1)
    xwa = jnp.where(lane < LORA_W, jnp.tanh(xwa), xwa)
    lwa = _dot(_bf(xwa), wwa_ref[...])
    w_log = -_softplus(-(w0_ref[...] + lwa[:, :d_a])) - 0.5
    decay = jnp.exp(-jnp.exp(w_log))
    a = jax.nn.sigmoid(a0_ref[...] + lwa[:, d_a:])
    g_ref[...] = _dot(_bf(jax.nn.sigmoid(xg)), g2_ref[...])
    ones_bd = ones_ref[...]
    kk = k * kk_ref[...]
    kk = kk / jnp.maximum(jnp.sqrt(_segsum(kk * kk, ones_bd)), 1e-12)
    k_mod = k * (1.0 + (a - 1.0) * ka_ref[...])
    r_ref[...] = r
    w_ref[...] = decay
    k_ref[...] = k_mod
    v_ref[...] = v
    av_ref[...] = -kk
    bv_ref[...] = kk * a
    bonus_ref[...] = _segsum(r * k_mod * rk_ref[...], ones_bd) * v


def _ones_blockdiag(n, group):
    idx = jnp.arange(n, dtype=jnp.int32) // group
    return (idx[:, None] == idx[None, :]).astype(_bf16)


def _rwkv_prep(p_a, prev0, mu, w0, w2, a0, a2, g2, k_k, k_a, r_k, tm, rows_per_group):
    m, shift_w = p_a.shape
    d_a = w0.shape[-1]
    seq_is_one = rows_per_group == tm and prev0.shape[1] == tm
    wwa = jnp.zeros((LORA_W + LORA_A, 2 * d_a), _f32)
    wwa = _bf(wwa.at[:LORA_W, :d_a].set(w2).at[LORA_W:, d_a:].set(a2))
    ones_bd = _ones_blockdiag(d_a, HEAD_DIM)
    vec = lambda a: a.reshape(1, -1)
    gs = _group_spec(tm, rows_per_group, shift_w)
    full = lambda a: pl.BlockSpec(a.shape, lambda i: (0, 0))
    row = pl.BlockSpec((tm, d_a), lambda i: (i, 0))
    ins = [p_a, p_a, prev0, vec(mu), vec(w0), vec(a0), vec(k_k), vec(k_a), vec(r_k), wwa, _bf(g2), ones_bd]
    in_specs = [pl.BlockSpec((tm, shift_w), lambda i: (i, 0)),
                pl.BlockSpec((SUBLANES, shift_w), lambda i: (jnp.maximum(i * (tm // SUBLANES) - 1, 0), 0)),
                gs(prev0.shape[1])] + [full(a) for a in ins[3:]]
    return pl.pallas_call(
        functools.partial(_rwkv_prep_body, seq_is_one, rows_per_group, tm),
        grid=(m // tm,),
        in_specs=in_specs,
        out_specs=[row] * 8,
        out_shape=[jax.ShapeDtypeStruct((m, d_a), _f32)] * 8,
        compiler_params=_params("arbitrary"),
        name="rwkv_prep",
    )(*ins)


def _rwkv_scan_body(nb, n_pairs, tc,
                    r_ref, w_ref, k_ref, v_ref, av_ref, bv_ref, s0_ref, ones_ref, expand_ref, eye_ref,
                    o_ref, s_ref):
    t_blk = pl.program_id(1)

    @pl.when(t_blk == 0)
    def _():
        s_ref[...] = s0_ref[...]

    ones_k = ones_ref[...]
    parts = ones_k.shape[0] // LANES
    expand = expand_ref[...]
    eye = eye_ref[...] > 0.5
    units = [(n, hp) for n in range(nb) for hp in range(n_pairs)]

    def rowsum(xs, ones, parts):
        lhs = jnp.concatenate([jnp.concatenate(_split_bf16(x, parts), axis=1) for x in xs], axis=0)
        res = _dot(lhs, ones)
        return [res[u * HEAD_DIM:(u + 1) * HEAD_DIM] for u in range(len(xs))]

    sub = min(SUBLANES, tc)
    row_id = lax.broadcasted_iota(jnp.int32, (sub, LANES), 0)

    def tile_steps(j, carry):
        t0 = pl.multiple_of(j * sub, sub)
        tiles = {}
        for (n, hp) in units:
            sl = pl.ds(hp * LANES, LANES)
            tiles[(n, hp)] = tuple(ref[n, pl.ds(t0, sub), sl] for ref in (r_ref, w_ref, k_ref, v_ref, av_ref, bv_ref))
        states = [s_ref[n, hp] for (n, hp) in units]
        o_tiles = [jnp.zeros((sub, LANES), _f32) for _ in units]
        vcols = []
        for u in units:
            vt = tiles[u][3]
            by_head = jnp.concatenate([vt[:, :HEAD_DIM], vt[:, HEAD_DIM:]], axis=0)
            lhs = jnp.concatenate(_split_bf16(by_head, 3), axis=0)
            vcols.append(lax.dot_general(lhs, expand, (((0,), (0,)), ((), ())), preferred_element_type=_f32))
        for tt in range(sub):
            rows = {u: tuple(x[tt:tt + 1, :] for x in tiles[u]) for u in units}
            sa = rowsum([s * rows[u][4] for s, u in zip(states, units)], ones_k, parts)
            for idx, u in enumerate(units):
                r_t, w_t, k_t, v_t, av_t, bv_t = rows[u]
                vcol = vcols[idx][:, tt * LANES:(tt + 1) * LANES]
                states[idx] = states[idx] * w_t + sa[idx] * bv_t + vcol * k_t
            ob = rowsum([s * rows[u][0] for s, u in zip(states, units)], ones_k, parts)
            for idx in range(len(units)):
                o_row = jnp.sum(jnp.where(eye, ob[idx], 0.0), axis=0, keepdims=True)
                o_tiles[idx] = jnp.where(row_id == tt, jnp.broadcast_to(o_row, (sub, LANES)), o_tiles[idx])
        for idx, (n, hp) in enumerate(units):
            s_ref[n, hp] = states[idx]
            o_ref[n, pl.ds(t0, sub), pl.ds(hp * LANES, LANES)] = o_tiles[idx]
        return carry

    lax.fori_loop(0, tc // sub, tile_steps, 0)


def _rwkv_scan(r, w, k, v, av, bv, s0, nb, tc):
    n, t, d_a = r.shape
    n_pairs = d_a // LANES
    half = (jnp.arange(LANES, dtype=jnp.int32) // HEAD_DIM)
    ones_blk = (half[:, None] == half[None, :]).astype(_bf16)
    ones_k = jnp.concatenate([ones_blk] * (3 if t == 1 else 2), axis=0)
    sub = min(SUBLANES, tc)
    src_h = jnp.arange(2 * sub, dtype=jnp.int32) // sub
    src_t = jnp.arange(2 * sub, dtype=jnp.int32) % sub
    dst_t = jnp.arange(sub * LANES, dtype=jnp.int32) // LANES
    dst_h = (jnp.arange(sub * LANES, dtype=jnp.int32) % LANES) // HEAD_DIM
    expand = ((src_t[:, None] == dst_t[None, :]) & (src_h[:, None] == dst_h[None, :])).astype(_bf16)
    expand = jnp.concatenate([expand] * 3, axis=0)
    eye =(jnp.arange(HEAD_DIM, dtype=jnp.int32)[:, None]
           == (jnp.arange(LANES, dtype=jnp.int32) % HEAD_DIM)[None, :]).astype(_f32)
    seq = pl.BlockSpec((nb, tc, d_a), lambda b, j: (b, j, 0))
    st = pl.BlockSpec((nb, n_pairs, HEAD_DIM, LANES), lambda b, j: (b, 0, 0, 0))
    full = lambda a: pl.BlockSpec(a.shape, lambda b, j: (0, 0))
    return pl.pallas_call(
        functools.partial(_rwkv_scan_body, nb, n_pairs, tc),
        grid=(n // nb, t // tc),
        in_specs=[seq] * 6 + [st, full(ones_k), full(expand), full(eye)],
        out_specs=[seq, st],
        out_shape=[jax.ShapeDtypeStruct((n, t, d_a), _f32),
                   jax.ShapeDtypeStruct(s0.shape, _f32)],
        compiler_params=_params("arbitrary", "arbitrary"),
        name="rwkv_scan",
    )(r, w, k, v, av, bv, s0, ones_k, expand, eye)


def _pack_state(s):
    n, h = s.shape[:2]
    return s.reshape(n, h // 2, 2, HEAD_DIM, HEAD_DIM).transpose(0, 1, 3, 2, 4).reshape(n, h // 2, HEAD_DIM, LANES)


def _unpack_state(s):
    n, hp = s.shape[:2]
    return s.reshape(n, hp, HEAD_DIM, 2, HEAD_DIM).transpose(0, 1, 3, 2, 4).reshape(n, hp * 2, HEAD_DIM, HEAD_DIM)


def _mix_out_body(o_ref, bonus_ref, g_ref, ob_ref, x_ref, gate_ref, lw_ref, lb_ref, ones_ref, wa_ref, wb_ref,
                  out_ref):
    ones_bd = ones_ref[...]
    o = o_ref[...]
    inv = 1.0 / HEAD_DIM
    mean = _segsum(o, ones_bd) * inv
    dlt = o - mean
    var = _segsum(dlt * dlt, ones_bd) * inv
    on = dlt * lax.rsqrt(var + LNX_EPS) * lw_ref[...] + lb_ref[...]
    oa = (on + bonus_ref[...]) * g_ref[...]
    y = _dot(_bf(oa), wa_ref[...]) + _dot(ob_ref[...], wb_ref[...])
    out_ref[...] = x_ref[...] + gate_ref[0] * y


def _mix_out(o, bonus, g, o_b, x, gate, lnx_w, lnx_b, w_out, tm, rows_per_group):
    m, d = x.shape
    d_a = o.shape[1]
    ones_bd = _ones_blockdiag(d_a, HEAD_DIM)
    wa = _bf(w_out[:d_a])
    wb = _bf(w_out[d_a:])
    gs = _group_spec(tm, rows_per_group, d)
    full = lambda a: pl.BlockSpec(a.shape, lambda i: (0, 0))
    rowa = pl.BlockSpec((tm, d_a), lambda i: (i, 0))
    rowd = pl.BlockSpec((tm, d), lambda i: (i, 0))
    lw = lnx_w.reshape(1, -1)
    lb = lnx_b.reshape(1, -1)
    return pl.pallas_call(
        _mix_out_body,
        grid=(m // tm,),
        in_specs=[rowa, rowa, rowa, pl.BlockSpec((tm, o_b.shape[1]), lambda i: (i, 0)), rowd, gs(gate.shape[1]),
                  full(lw), full(lb), full(ones_bd), full(wa), full(wb)],
        out_specs=rowd,
        out_shape=jax.ShapeDtypeStruct((m, d), _f32),
        compiler_params=_params("arbitrary"),
        name="mix_out",
    )(o, bonus, g, o_b, x, gate, lw, lb, ones_bd, wa, wb)


def _key_to_float(u):
    key = u ^ jnp.int32(-2147483648)
    bits = jnp.where(key >= 0, key, key ^ jnp.int32(0x7FFFFFFF))
    return lax.bitcast_convert_type(bits, _f32)


def _kth_largest(count_ge, k_row, shape):
    def body(it, u):
        bit = jnp.left_shift(jnp.int32(1), 31 - it)
        cand_u = u | bit
        ok = count_ge(_key_to_float(cand_u)) >= k_row
        return jnp.where(ok, cand_u, u)

    u = lax.fori_loop(0, 32, body, jnp.zeros(shape, jnp.int32))
    return _key_to_float(u)


def _tie_cut(count_eq_lt, budget, n_bits, shape):
    def body(it, c):
        cand = c | jnp.left_shift(jnp.int32(1), n_bits - 1 - it)
        ok = count_eq_lt(cand) <= budget
        return jnp.where(ok, cand, c)

    return lax.fori_loop(0, n_bits, body, jnp.zeros(shape, jnp.int32))


def _dsa_prompt_body(n_heads, k_sel, t_len,
                     far_ref, qt_ref, qit_ref, wit_ref, k_ref, vt_ref, ki_ref, near_ref,
                     ot_ref, sc_ref, cut_ref, qz_ref, m_ref, l_ref, acc_ref):
    qb = qt_ref.shape[2]
    i = pl.program_id(1)
    q0 = i * qb
    qpos = q0 + lax.broadcasted_iota(jnp.int32, (1, qb), 1)
    n_kb = (q0 + qb + KEY_BLOCK - 1) // KEY_BLOCK
    row_kb = lax.broadcasted_iota(jnp.int32, (KEY_BLOCK, qb), 0)

    @pl.when(i == 0)
    def _():
        sc_ref[...] = jnp.full(sc_ref.shape, -jnp.inf, _f32)

    w8 = (wit_ref[0] * (H_IDX ** -0.5)) * (D_IDX ** -0.5)

    qi_all = jnp.concatenate([qit_ref[0, h * D_IDX:(h + 1) * D_IDX, :] for h in range(H_IDX)], axis=1)

    def score_block(kb, carry):
        c0 = pl.multiple_of(kb * KEY_BLOCK, KEY_BLOCK)
        s_all = _dot(ki_ref[0, pl.ds(c0, KEY_BLOCK), :], qi_all)
        acc = w8[0:1, :] * jnp.maximum(s_all[:, :qb], 0.0)
        for h in range(1, H_IDX):
            acc = acc + w8[h:h + 1, :] * jnp.maximum(s_all[:, h * qb:(h + 1) * qb], 0.0)
        sc_ref[pl.ds(c0, KEY_BLOCK), :] = jnp.where(c0 + row_kb <= qpos, acc, -jnp.inf)
        return carry

    lax.fori_loop(0, n_kb, score_block, 0)

    cb = min(COUNT_BLOCK, t_len)
    n_cb = (q0 + qb + cb - 1) // cb
    row_cb = lax.broadcasted_iota(jnp.int32, (cb, qb), 0)

    def count(pred):
        def body(kb, acc):
            c0 = pl.multiple_of(kb * cb, cb)
            hit = jnp.where(pred(sc_ref[pl.ds(c0, cb), :], c0 + row_cb), 1.0, 0.0)
            for j in range(cb // COUNT_ROWS):
                acc = acc + hit[j * COUNT_ROWS:(j + 1) * COUNT_ROWS, :]
            return acc
        acc = lax.fori_loop(0, n_cb, body, jnp.zeros((COUNT_ROWS, qb), _f32))
        return jnp.sum(acc, axis=0, keepdims=True)

    k_row = jnp.minimum(k_sel, qpos + 1).astype(_f32)
    thr = _kth_largest(lambda cand: count(lambda s, pos: s >= cand), k_row, (1, qb))
    n_gt = count(lambda s, pos: s > thr)
    n_eq = count(lambda s, pos: s == thr)
    budget = k_row - n_gt
    cut_ref[...] = jnp.full((1, qb), 2 * t_len, jnp.int32)

    @pl.when(jnp.max(n_eq - budget) > 0.5)
    def _():
        cut_ref[...] = _tie_cut(lambda c: count(lambda s, pos: (s == thr) & (pos < c)), budget,
                                (2 * t_len).bit_length(), (1, qb))

    cut = cut_ref[...]

    m_ref[...] = jnp.full(m_ref.shape, M_INIT, _f32)
    l_ref[...] = jnp.zeros(l_ref.shape, _f32)
    acc_ref[...] = jnp.zeros(acc_ref.shape, _f32)
    pair_row = lax.broadcasted_iota(jnp.int32, (LANES, qb), 0) // HEAD_DIM
    for hp in range(n_heads // 2):
        qp = qt_ref[0, hp * LANES:(hp + 1) * LANES, :] * (HEAD_DIM ** -0.5)
        zero = jnp.zeros_like(qp)
        qz_ref[hp] = jnp.concatenate([jnp.where(pair_row == 0, qp, zero), jnp.where(pair_row == 1, qp, zero)], axis=1)

    def attend(c0, width, bias_of_head):
        scb = sc_ref[pl.ds(c0, width), :]
        pos = c0 + lax.broadcasted_iota(jnp.int32, (width, qb), 0)
        sel = (scb > thr) | ((scb == thr) & (pos < cut))
        m_old = [m_ref[h] for h in range(n_heads)]
        l_old = [l_ref[h] for h in range(n_heads)]
        a_old = [acc_ref[h] for h in range(n_heads)]
        s2 = [_dot(k_ref[0, pl.ds(c0, width), hp * LANES:(hp + 1) * LANES], qz_ref[hp]) for hp in range(n_heads // 2)]
        m_out, l_out, a_out = [], [], []
        for h in range(n_heads):
            s = s2[h // 2][:, (h % 2) * qb:(h % 2 + 1) * qb] + bias_of_head(h)
            s = jnp.where(sel, s, NEG)
            m_new = jnp.maximum(m_old[h], jnp.max(s, axis=0, keepdims=True))
            alpha = jnp.exp(m_old[h] - m_new)
            p = jnp.exp(s - m_new)
            l_out.append(alpha * l_old[h] + jnp.sum(p, axis=0, keepdims=True))
            pv = _dot(vt_ref[0, h * HEAD_DIM:(h + 1) * HEAD_DIM, pl.ds(c0, width)], _bf(p))
            a_out.append(alpha * a_old[h] + pv)
            m_out.append(m_new)
        for h in range(n_heads):
            m_ref[h] = m_out[h]
            l_ref[h] = l_out[h]
            acc_ref[h] = a_out[h]

    n_far = jnp.maximum(i - 1, 0) // (KEY_BLOCK // qb)

    def far_block(kb, carry):
        attend(pl.multiple_of(kb * KEY_BLOCK, KEY_BLOCK), KEY_BLOCK, lambda h: far_ref[h])
        return carry

    lax.fori_loop(0, n_far, far_block, 0)

    def near_block(j, carry):
        attend(pl.multiple_of(j * qb, qb), qb, lambda h: near_ref[i - j, h])
        return carry

    lax.fori_loop(n_far * (KEY_BLOCK // qb), i + 1, near_block, 0)

    for h in range(n_heads):
        ot_ref[0, h * HEAD_DIM:(h + 1) * HEAD_DIM, :] = _bf(acc_ref[h] / l_ref[h])


def _dsa_prompt(q, k, v, qi, ki, wi, rel_bias, qb):
    n, t, d_b = q.shape
    n_heads = d_b // HEAD_DIM
    k_sel = min(TOPK_MAX, t // 4)
    tr = lambda a: a.transpose(0, 2, 1)
    n_near = KEY_BLOCK // qb + 1
    qq = jnp.arange(qb, dtype=jnp.int32)
    dist = (jnp.arange(n_near, dtype=jnp.int32)[:, None, None] * qb + qq[None, None, :] - qq[None, :, None])
    onehot = (_rel_bucket(dist)[..., None] == jnp.arange(N_BUCKETS, dtype=jnp.int32)).astype(_f32)
    near = jnp.einsum('dkqb,bh->dhkq', onehot, rel_bias, precision=lax.Precision.HIGHEST)
    far = rel_bias[N_BUCKETS - 1]
    res = lambda shape: pl.BlockSpec(shape, lambda b, i, *_: (b, 0, 0))
    blk = lambda w: pl.BlockSpec((1, w, qb), lambda b, i, *_: (b, 0, i))
    grid_spec = pltpu.PrefetchScalarGridSpec(
        num_scalar_prefetch=0,
        grid=(n, t // qb),
        in_specs=[pl.BlockSpec(memory_space=pltpu.SMEM),
                  blk(d_b), blk(H_IDX * D_IDX), blk(H_IDX),
                  res((1, t, d_b)), res((1, d_b, t)), res((1, t, D_IDX)),
                  pl.BlockSpec(near.shape, lambda b, i, *_: (0, 0, 0, 0))],
        out_specs=blk(d_b),
        scratch_shapes=[pltpu.VMEM((t, qb), _f32), pltpu.VMEM((1, qb), jnp.int32),
                        pltpu.VMEM((n_heads // 2, LANES, 2 * qb), _bf16),
                        pltpu.VMEM((n_heads, 1, qb), _f32), pltpu.VMEM((n_heads, 1, qb), _f32),
                        pltpu.VMEM((n_heads, HEAD_DIM, qb), _f32)],
    )
    o_t = pl.pallas_call(
        functools.partial(_dsa_prompt_body, n_heads, k_sel, t),
        grid_spec=grid_spec,
        out_shape=jax.ShapeDtypeStruct((n, d_b, t), _bf16),
        compiler_params=_params("arbitrary", "arbitrary"),
        name="dsa_prompt",
    )(far, tr(q), tr(qi), tr(wi), _bf(k), tr(_bf(v)), _bf(ki), near)
    return tr(o_t)


def _dsa_sample_scores_body(n_heads, k_sel, n_pages, page, pg, pt_ref, q_ref, qi_ref, wi_ref, kn_ref, kin_ref, *rest):
    cki_refs, ck_refs = rest[:pg], rest[pg:2 * pg]
    btab_ref, bnew_ref, p_ref, pnew_ref, sc_ref, lg_ref = rest[2 * pg:]
    step = pl.program_id(1)
    past = n_pages * page
    w8 = _bf(wi_ref[0] * (H_IDX ** -0.5)).astype(_f32)
    relu_bf = lambda s: _bf(jnp.maximum(s, 0.0)).astype(_f32)
    qi8 = qi_ref[0]
    q8 = q_ref[0]

    for j in range(pg):
        c0 = pl.multiple_of((step * pg + j) * page, page)
        s8 = _dot(qi8, _bf(cki_refs[j][0, 0])) * (D_IDX ** -0.5)
        sc_ref[:, pl.ds(c0, page)] = jnp.sum(w8 * relu_bf(s8), axis=0, keepdims=True)
        rows = [_dot(q8, _bf(ck_refs[j][0, 0, h]))[h:h + 1] for h in range(n_heads)]
        lg = jnp.concatenate(rows, axis=0) * (HEAD_DIM ** -0.5)
        lg_ref[:, pl.ds(c0, page)] = lg + btab_ref[:, pl.ds(c0, page)]

    @pl.when(step == n_pages // pg - 1)
    def _():
        sc = sc_ref[...]
        s_new8 = jnp.sum(qi8.astype(_f32) * _bf(kin_ref[0]).astype(_f32), axis=1, keepdims=True) * (D_IDX ** -0.5)
        s_new = jnp.sum(w8 * relu_bf(s_new8), axis=0, keepdims=True)
        pos = lax.broadcasted_iota(jnp.int32, sc.shape, 1)

        def count(pred):
            hits = jnp.sum(jnp.where(pred(sc, pos), 1.0, 0.0), axis=1, keepdims=True)
            return hits + jnp.where(pred(s_new, past), 1.0, 0.0)

        k_row = jnp.full((1, 1), float(k_sel), _f32)
        thr = _kth_largest(lambda cand: count(lambda s, ps: s >= cand), k_row, (1, 1))
        budget = k_row - count(lambda s, ps: s > thr)
        cut = _tie_cut(lambda c: count(lambda s, ps: (s == thr) & (ps < c)), budget,
                       (2 * (past + 1)).bit_length(), (1, 1))
        sel = (sc > thr) | ((sc == thr) & (pos < cut))
        sel_new = (s_new > thr) | ((s_new == thr) & (past < cut))

        kn = _bf(kn_ref[0]).astype(_f32)
        lg_new = jnp.sum(q8.astype(_f32) * kn, axis=1, keepdims=True) * (HEAD_DIM ** -0.5) + bnew_ref[...]
        lg_new = jnp.where(sel_new, lg_new, NEG)
        s_all = jnp.where(sel, lg_ref[...], NEG)
        m = jnp.maximum(jnp.maximum(jnp.max(s_all, axis=1, keepdims=True), lg_new), M_INIT)
        pr = jnp.exp(s_all - m)
        pr_new = jnp.exp(lg_new - m)
        l = jnp.sum(pr, axis=1, keepdims=True) + pr_new
        p_ref[0] = pr
        col = lax.broadcasted_iota(jnp.int32, (n_heads, 2), 1)
        pnew_ref[0] = jnp.where(col == 0, pr_new, l)


def _dsa_sample_pv_body(n_heads, n_pages, page, pg, pt_ref, p_ref, pnew_ref, vn_ref, *rest):
    cv_refs, (o_ref, acc_ref) = rest[:pg], rest[pg:]
    step = pl.program_id(1)

    @pl.when(step == 0)
    def _():
        acc_ref[...] = pnew_ref[0][:, 0:1] * _bf(vn_ref[0]).astype(_f32)

    acc = acc_ref[...]
    for j in range(pg):
        c0 = pl.multiple_of((step * pg + j) * page, page)
        pb = _bf(p_ref[0, :, pl.ds(c0, page)])
        rows = [_dot_nt(pb, _bf(cv_refs[j][0, 0, h]))[h:h + 1] for h in range(n_heads)]
        acc = acc + jnp.concatenate(rows, axis=0)
    acc_ref[...] = acc

    @pl.when(step == n_pages // pg - 1)
    def _():
        o_ref[0] = _bf(acc / pnew_ref[0][:, 1:2])


def _dsa_sample(q, k_new, v_new, qi, ki_new, wi, ck, cv, cki, layer, page_table, rel_bias):
    b, _, d_b = q.shape
    n_heads = d_b // HEAD_DIM
    n_pages = page_table.shape[1]
    page = ck.shape[2]
    ck, cv, cki = (jnp.moveaxis(a, 2, -1) for a in (ck, cv, cki))
    past = n_pages * page
    k_sel = min(TOPK_MAX, (past + 1) // 4)
    pg = max(g for g in (1, 2, 4, 8) if n_pages % g == 0)
    kpos = jnp.arange(past, dtype=jnp.int32)
    btab = rel_bias[_rel_bucket(past - kpos)].T
    bnew = rel_bias[_rel_bucket(jnp.zeros((1,), jnp.int32))].T
    heads = lambda a: a.reshape(b, n_heads, HEAD_DIM)
    pt = page_table.reshape(-1)
    per_b = lambda shape: pl.BlockSpec((1,) + shape, lambda i, s, pt: (i,) + (0,) * len(shape))
    const = lambda a: pl.BlockSpec(a.shape, lambda i, s, pt: (0,) * a.ndim)

    def paged(tail, j):
        return pl.BlockSpec((1, 1) + tail + (page,),
                            lambda i, s, pt: (layer, pt[i * n_pages + s * pg + j]) + (0,) * (len(tail) + 1))

    grid = (b, n_pages // pg)
    probs, p_new = pl.pallas_call(
        functools.partial(_dsa_sample_scores_body, n_heads, k_sel, n_pages, page, pg),
        grid_spec=pltpu.PrefetchScalarGridSpec(
            num_scalar_prefetch=1, grid=grid,
            in_specs=[per_b((n_heads, HEAD_DIM)), per_b((H_IDX, D_IDX)), per_b((H_IDX, 1)),
                      per_b((n_heads, HEAD_DIM)), per_b((1, D_IDX))]
                     + [paged((D_IDX,), j) for j in range(pg)]
                     + [paged((n_heads, HEAD_DIM), j) for j in range(pg)]
                     + [const(btab), const(bnew)],
            out_specs=[per_b((n_heads, past)), per_b((n_heads, 2))],
            scratch_shapes=[pltpu.VMEM((1, past), _f32), pltpu.VMEM((n_heads, past), _f32)]),
        out_shape=[jax.ShapeDtypeStruct((b, n_heads, past), _f32), jax.ShapeDtypeStruct((b, n_heads, 2), _f32)],
        compiler_params=_params("arbitrary", "arbitrary"),
        name="dsa_sample_scores",
    )(pt, heads(q), qi.reshape(b, H_IDX, D_IDX), wi.reshape(b, H_IDX, 1), heads(k_new), ki_new,
      *([cki] * pg), *([ck] * pg), btab, bnew)
    out = pl.pallas_call(
        functools.partial(_dsa_sample_pv_body, n_heads, n_pages, page, pg),
        grid_spec=pltpu.PrefetchScalarGridSpec(
            num_scalar_prefetch=1, grid=grid,
            in_specs=[per_b((n_heads, past)), per_b((n_heads, 2)), per_b((n_heads, HEAD_DIM))]
                     + [paged((n_heads, HEAD_DIM), j) for j in range(pg)],
            out_specs=per_b((n_heads, HEAD_DIM)),
            scratch_shapes=[pltpu.VMEM((n_heads, HEAD_DIM), _f32)]),
        out_shape=jax.ShapeDtypeStruct((b, n_heads, HEAD_DIM), _bf16),
        compiler_params=_params("arbitrary", "arbitrary"),
        name="dsa_sample_pv",
    )(pt, probs, p_new, heads(v_new), *([cv] * pg))
    return out.reshape(b, 1, d_b)


def _ffn_body(routed, final_norm, n_exp,
              x_ref, g_ref, sh_ref, sc_ref, gate_ref, rw_ref, rb_ref, w1_ref, w3_ref, w2_ref, gf_ref,
              o_ref, h_ref, acc_ref, dg_ref):
    e = pl.program_id(1)
    f = pl.program_id(2)
    first = (e == 0) & (f == 0)
    last = (e == pl.num_programs(1) - 1) & (f == pl.num_programs(2) - 1)

    @pl.when(first)
    def _():
        h = _norm_mod(x_ref[...], g_ref[...], sh_ref[0], sc_ref[0])
        h_ref[...] = _bf(h)
        acc_ref[...] = jnp.zeros(acc_ref.shape, _f32)
        if routed:
            logits = _dot(_bf(h), rw_ref[...]) + rb_ref[...]
            lane = lax.broadcasted_iota(jnp.int32, logits.shape, 1).astype(_f32)
            logits = jnp.where(lane < n_exp, logits, -jnp.inf)
            v1 = jnp.max(logits, axis=1, keepdims=True)
            i1 = jnp.min(jnp.where(logits == v1, lane, float(LANES)), axis=1, keepdims=True)
            rest = jnp.where(lane == i1, -jnp.inf, logits)
            v2 = jnp.max(rest, axis=1, keepdims=True)
            i2 = jnp.min(jnp.where(rest == v2, lane, float(LANES)), axis=1, keepdims=True)
            e2 = jnp.exp(v2 - v1)
            den = 1.0 + e2
            dg_ref[...] = jnp.where(lane == i1, 1.0 / den, 0.0) + jnp.where(lane == i2, e2 / den, 0.0)

    h = h_ref[...]
    a = _silu(_dot(h, w1_ref[0])) * _dot(h, w3_ref[0])
    y = _dot(_bf(a), w2_ref[0])
    if routed:
        lane = lax.broadcasted_iota(jnp.int32, dg_ref.shape, 1)
        y = y * jnp.sum(jnp.where(lane == e, dg_ref[...], 0.0), axis=1, keepdims=True)
    acc_ref[...] += y

    @pl.when(last)
    def _():
        y = x_ref[...] + gate_ref[0] * acc_ref[...]
        if final_norm:
            y = y * lax.rsqrt(jnp.mean(y * y, axis=-1, keepdims=True) + EPS) * gf_ref[...]
        o_ref[...] = y


def _ffn(x, g, shift, scale, gate, w13, w2, tm, tf, rows_per_group, router=None, final_g=None):
    m, d = x.shape
    n_exp, ff, _ = w2.shape
    routed = router is not None
    final_norm = final_g is not None
    if routed:
        rw, rb = router
        rw = _bf(jnp.pad(rw, ((0, 0), (0, LANES - n_exp))))
        rb = jnp.pad(rb.reshape(1, -1), ((0, 0), (0, LANES - n_exp)))
    else:
        rw = jnp.zeros((d, LANES), _bf16)
        rb = jnp.zeros((1, LANES), _f32)
    gf = final_g.reshape(1, -1) if final_norm else jnp.ones((1, d), _f32)
    nf = ff // tf
    gs = _group_spec(tm, rows_per_group, d)
    rowd = pl.BlockSpec((tm, d), lambda i, e, f: (i, 0))
    full = lambda a: pl.BlockSpec(a.shape, lambda i, e, f: (0, 0))
    return pl.pallas_call(
        functools.partial(_ffn_body, routed, final_norm, n_exp),
        grid=(m // tm, n_exp, nf),
        in_specs=[rowd, full(g), gs(shift.shape[1]), gs(scale.shape[1]), gs(gate.shape[1]), full(rw), full(rb),
                  pl.BlockSpec((1, d, tf), lambda i, e, f: (e, 0, f)),
                  pl.BlockSpec((1, d, tf), lambda i, e, f: (e, 0, f + nf)),
                  pl.BlockSpec((1, tf, d), lambda i, e, f: (e, f, 0)),
                  full(gf)],
        out_specs=rowd,
        out_shape=jax.ShapeDtypeStruct((m, d), _f32),
        scratch_shapes=[pltpu.VMEM((tm, d), _bf16), pltpu.VMEM((tm, d), _f32), pltpu.VMEM((tm, LANES), _f32)],
        compiler_params=_params("arbitrary", "arbitrary", "arbitrary"),
        name="moe_ffn" if routed else "dense_ffn",
    )(x, g, shift, scale, gate, rw, rb, w13, w13, w2, gf)


def _cmul(ar, ai, br, bi):
    return ar * br - ai * bi, ar * bi + ai * br


def _s5_io(u, b_ref, c_ref, d_ref, are_ref, aim_ref, h_of_bb):
    half = b_ref.shape[2] // 2
    bu = _dot(_bf(u), b_ref[0])
    bb_re, bb_im = _cmul(are_ref[0, 1:2], aim_ref[0, 1:2], bu[:, :half], bu[:, half:])
    h_re, h_im = h_of_bb(bb_re, bb_im)
    y = _dot(_bf(jnp.concatenate([h_re, h_im], axis=1)), c_ref[0]) + d_ref[0] * u
    return _bf(_gelu_tanh(y)), h_re, h_im


def _s5_scan_body(tc, u_ref, b_ref, c_ref, d_ref, are_ref, aim_ref, h0r_ref, h0i_ref,
                  y_ref, hr_ref, hi_ref, xr_ref, xi_ref):
    t_blk = pl.program_id(2)

    @pl.when(t_blk == 0)
    def _():
        hr_ref[0, 0] = h0r_ref[0, 0]
        hi_ref[0, 0] = h0i_ref[0, 0]

    width = are_ref.shape[2]
    a1 = (jnp.broadcast_to(are_ref[0, 0:1], (SUBLANES, width)), jnp.broadcast_to(aim_ref[0, 0:1], (SUBLANES, width)))
    a2 = _cmul(*a1, *a1)
    a3 = _cmul(*a2, *a1)
    a4 = _cmul(*a2, *a2)
    a5 = _cmul(*a4, *a1)
    a6 = _cmul(*a4, *a2)
    a7 = _cmul(*a4, *a3)
    a8 = _cmul(*a4, *a4)
    row = lax.broadcasted_iota(jnp.int32, (SUBLANES, width), 0)
    lvl = [tuple(jnp.where(row >= s, c, 0.0) for c in a) for s, a in ((1, a1), (2, a2), (4, a4))]
    pw = []
    for comp in range(2):
        acc = a8[comp]
        for s, a in ((6, a7), (5, a6), (4, a5), (3, a4), (2, a3), (1, a2), (0, a1)):
            acc = jnp.where(row == s, a[comp], acc)
        pw.append(acc)

    def scan_chunk(bb_re, bb_im):
        xr_ref[...] = bb_re
        xi_ref[...] = bb_im

        def tile(j, carry):
            cr, ci = carry
            r0 = pl.multiple_of(j * SUBLANES, SUBLANES)
            xr = xr_ref[pl.ds(r0, SUBLANES), :]
            xi = xi_ref[pl.ds(r0, SUBLANES), :]
            for (s, (fr, fi)) in zip((1, 2, 4), lvl):
                sr = pltpu.roll(xr, s, axis=0)
                si = pltpu.roll(xi, s, axis=0)
                pr, pi = _cmul(fr, fi, sr, si)
                xr = xr + pr
                xi = xi + pi
            pr, pi = _cmul(pw[0], pw[1], jnp.broadcast_to(cr, xr.shape), jnp.broadcast_to(ci, xi.shape))
            xr = xr + pr
            xi = xi + pi
            xr_ref[pl.ds(r0, SUBLANES), :] = xr
            xi_ref[pl.ds(r0, SUBLANES), :] = xi
            return xr[SUBLANES - 1:SUBLANES], xi[SUBLANES - 1:SUBLANES]

        cr, ci = lax.fori_loop(0, tc // SUBLANES, tile, (hr_ref[0, 0], hi_ref[0, 0]))
        hr_ref[0, 0] = cr
        hi_ref[0, 0] = ci
        return xr_ref[...], xi_ref[...]

    y, _, _ = _s5_io(u_ref[0], b_ref, c_ref, d_ref, are_ref, aim_ref, scan_chunk)
    y_ref[0] = y


def _s5_step_body(u_ref, b_ref, c_ref, d_ref, are_ref, aim_ref, h0r_ref, h0i_ref, y_ref, hr_ref, hi_ref):
    def one_step(bb_re, bb_im):
        pr, pi = _cmul(are_ref[0, 0:1], aim_ref[0, 0:1], h0r_ref[0], h0i_ref[0])
        return pr + bb_re, pi + bb_im

    y, h_re, h_im = _s5_io(u_ref[...], b_ref, c_ref, d_ref, are_ref, aim_ref, one_step)
    y_ref[...] = y
    hr_ref[0] = h_re
    hi_ref[0] = h_im


def _s5_tables(a_re, a_im, log_step, b_re, b_im, c_re, c_im, d):
    g_c, p_c, ch = b_re.shape
    gs = LANES // ch
    ns = g_c // gs
    dt = jnp.exp(log_step.astype(_f32))[:, None]
    mag = jnp.exp(dt * a_re)
    ab_re = mag * jnp.cos(dt * a_im)
    ab_im = mag * jnp.sin(dt * a_im)
    den = a_re * a_re + a_im * a_im
    nr = ab_re - 1.0
    co_re = (nr * a_re + ab_im * a_im) / den
    co_im = (ab_im * a_re - nr * a_im) / den
    eye = jnp.eye(gs, dtype=_f32)

    def pack_b(b):
        b = b.reshape(ns, gs, p_c, ch)
        return jnp.einsum('sgpc,gh->sgchp', b, eye).reshape(ns, gs * ch, gs * p_c)

    def pack_c(c):
        c = c.reshape(ns, gs, ch, p_c)
        return jnp.einsum('sgcp,gh->sgphc', c, eye).reshape(ns, gs * p_c, gs * ch)

    b_pack = _bf(jnp.concatenate([pack_b(b_re), pack_b(b_im)], axis=2))
    c_pack = _bf(jnp.concatenate([pack_c(c_re), -pack_c(c_im)], axis=1))
    vec = lambda a, c: jnp.stack([a.reshape(ns, gs * p_c), c.reshape(ns, gs * p_c)], axis=1)
    return b_pack, c_pack, d.reshape(ns, 1, gs * ch), vec(ab_re, co_re), vec(ab_im, co_im)


def _s5_scan(u, h0_re, h0_im, tables, tc):
    n, t, d = u.shape
    b_pack, c_pack, dvec, are, aim = tables
    ns, _, w2 = b_pack.shape
    w = w2 // 2
    h0r = h0_re.reshape(n, ns, 1, w)
    h0i = h0_im.reshape(n, ns, 1, w)
    tab = lambda a: pl.BlockSpec((1,) + a.shape[1:], lambda b, s, j: (s, 0, 0))
    st = pl.BlockSpec((1, 1, 1, w), lambda b, s, j: (b, s, 0, 0))
    seq = pl.BlockSpec((1, tc, LANES), lambda b, s, j: (b, j, s))
    y, hr, hi = pl.pallas_call(
        functools.partial(_s5_scan_body, tc),
        grid=(n, ns, t // tc),
        in_specs=[seq, tab(b_pack), tab(c_pack), tab(dvec), tab(are), tab(aim), st, st],
        out_specs=[seq, st, st],
        out_shape=[jax.ShapeDtypeStruct((n, t, d), _bf16),
                   jax.ShapeDtypeStruct((n, ns, 1, w), _f32), jax.ShapeDtypeStruct((n, ns, 1, w), _f32)],
        scratch_shapes=[pltpu.VMEM((tc, w), _f32), pltpu.VMEM((tc, w), _f32)],
        compiler_params=_params("arbitrary", "arbitrary", "arbitrary"),
        name="s5_scan",
    )(u, b_pack, c_pack, dvec, are, aim, h0r, h0i)
    return y, hr.reshape(n, ns * w), hi.reshape(n, ns * w)


def _s5_step(u, h0_re, h0_im, tables):
    b, d = u.shape
    b_pack, c_pack, dvec, are, aim = tables
    ns, _, w2 = b_pack.shape
    w = w2 // 2
    h0r = h0_re.reshape(b, ns, w).transpose(1, 0, 2)
    h0i = h0_im.reshape(b, ns, w).transpose(1, 0, 2)
    tab = lambda a: pl.BlockSpec((1,) + a.shape[1:], lambda s: (s, 0, 0))
    st = pl.BlockSpec((1, b, w), lambda s: (s, 0, 0))
    col = pl.BlockSpec((b, LANES), lambda s: (0, s))
    y, hr, hi = pl.pallas_call(
        _s5_step_body,
        grid=(ns,),
        in_specs=[col, tab(b_pack), tab(c_pack), tab(dvec), tab(are), tab(aim), st, st],
        out_specs=[col, st, st],
        out_shape=[jax.ShapeDtypeStruct((b, d), _bf16),
                   jax.ShapeDtypeStruct((ns, b, w), _f32), jax.ShapeDtypeStruct((ns, b, w), _f32)],
        compiler_params=_params("arbitrary"),
        name="s5_step",
    )(u, b_pack, c_pack, dvec, are, aim, h0r, h0i)
    return y, hr.transpose(1, 0, 2).reshape(b, ns * w), hi.transpose(1, 0, 2).reshape(b, ns * w)


def _glu_out_body(y_ref, x_ref, gate_ref, wl_ref, wr_ref, o_ref):
    y = y_ref[...]
    z = _dot(y, wl_ref[...]) * jax.nn.sigmoid(_dot(y, wr_ref[...]))
    o_ref[...] = x_ref[...] + gate_ref[0] * z


def _glu_out(yg, x, gate, w_glu, tm, rows_per_group):
    m, d = x.shape
    wl = _bf(w_glu[:, :d])
    wr = _bf(w_glu[:, d:])
    gs = _group_spec(tm, rows_per_group, d)
    rowd = pl.BlockSpec((tm, d), lambda i: (i, 0))
    full = lambda a: pl.BlockSpec(a.shape, lambda i: (0, 0))
    return pl.pallas_call(
        _glu_out_body,
        grid=(m // tm,),
        in_specs=[rowd, rowd, gs(gate.shape[1]), full(wl), full(wr)],
        out_specs=rowd,
        out_shape=jax.ShapeDtypeStruct((m, d), _f32),
        compiler_params=_params("arbitrary"),
        name="glu_out",
    )(yg, x, gate, wl, wr)


def _tile(m, target):
    if m <= target:
        return m
    t = target
    while m % t:
        t -= SUBLANES
    return t


def _run_group(x, mods, is_prompt, st, wts):
    n, t, d = x.shape
    m = n * t
    depth = wts["norm_mix"].shape[0]
    d_a = d // 2
    d_b = d // 2
    n_heads = d_a // HEAD_DIM
    rows_per_group = t if is_prompt else m
    tm = _tile(t, 256) if is_prompt else m
    xf = x.reshape(m, d).astype(_f32)

    def mod_vecs(l):
        parts = jnp.split(mods[l], 6, axis=-1)
        if is_prompt:
            return [p.reshape(n, 1, d) for p in parts]
        return [p.reshape(1, m, d) for p in parts]

    outs = {k: [] for k in ("shift", "wkv", "k", "v", "kidx", "re", "im")}
    for l in range(depth):
        sh_m, sc_m, g_m, sh_f, sc_f, g_f = mod_vecs(l)
        i = l // 2
        last = l == depth - 1
        final_g = wts["norm_final"] if last else None
        norm_mix = wts["norm_mix"][l].reshape(1, d)
        norm_ffn = wts["norm_ffn"][l].reshape(1, d)
        if l % 2 == 0:
            p_a, q, k, v, qi, ki, wi = _in_proj(xf, norm_mix, sh_m, sc_m, wts["e_w_in"][i], tm, rows_per_group)
            if is_prompt:
                prev0 = st["shift"][i].reshape(n, 1, -1)
            else:
                prev0 = st["shift"][i].reshape(1, m, -1)
            r, w, k_mod, v_a, av, bv, g, bonus = _rwkv_prep(
                p_a, prev0.astype(_f32), wts["e_mu"][i], wts["e_w0"][i], wts["e_w2"][i], wts["e_a0"][i],
                wts["e_a2"][i], wts["e_g2"][i], wts["e_k_k"][i], wts["e_k_a"][i], wts["e_r_k"][i],
                tm, rows_per_group)
            seq = lambda a: a.reshape(n, t, d_a)
            nb = n if is_prompt else _tile(n, 4)
            tc = _tile(t, 256)
            o, s_fin = _rwkv_scan(seq(r), seq(w), seq(k_mod), seq(v_a), seq(av), seq(bv),
                                  _pack_state(st["wkv"][i].astype(_f32)), nb, tc)
            if is_prompt:
                o_b = _dsa_prompt(q.reshape(n, t, d_b), k.reshape(n, t, d_b), v.reshape(n, t, d_b),
                                  qi.reshape(n, t, -1), ki.reshape(n, t, -1), wi.reshape(n, t, -1),
                                  wts["rel_bias"], 128)
            else:
                o_b = _dsa_sample(q.reshape(n, t, d_b), k.reshape(n, t, d_b), v.reshape(n, t, d_b),
                                  qi.reshape(n, t, -1), ki.reshape(n, t, -1), wi.reshape(n, t, -1),
                                  st["cache_k"], st["cache_v"], st["cache_kidx"], i,
                                  st["page_table"], wts["rel_bias"])
            xf = _mix_out(o.reshape(m, d_a), bonus, g, o_b.reshape(m, d_b), xf, g_m,
                          wts["e_lnx_w"][i], wts["e_lnx_b"][i], wts["e_w_out"][i], tm, rows_per_group)
            outs["shift"].append(p_a.reshape(n, t, -1)[:, -1].astype(x.dtype))
            outs["wkv"].append(_unpack_state(s_fin))
            outs["k"].append(k.reshape(n, t, n_heads, HEAD_DIM))
            outs["v"].append(v.reshape(n, t, n_heads, HEAD_DIM))
            outs["kidx"].append(ki.reshape(n, t, D_IDX))
            tm_f = _tile(t, 1024) if is_prompt else m
            ff = wts["ffn_w2"][i].shape[0]
            xf = _ffn(xf, norm_ffn, sh_f, sc_f, g_f, wts["ffn_w13"][i][None], wts["ffn_w2"][i][None],
                      tm_f, _tile(ff, 256) if ff % 256 == 0 else ff, rows_per_group, final_g=final_g)
        else:
            tables = _s5_tables(wts["o_a_re"][i], wts["o_a_im"][i], wts["o_log_step"][i], wts["o_b_re"][i],
                                wts["o_b_im"][i], wts["o_c_re"][i], wts["o_c_im"][i], wts["o_d"][i])
            u = _norm_mod_call(xf, norm_mix, sh_m, sc_m, tm, rows_per_group)
            h0r = st["ssm_re"][i].reshape(n, -1).astype(_f32)
            h0i = st["ssm_im"][i].reshape(n, -1).astype(_f32)
            if is_prompt:
                yg, hr, hi = _s5_scan(u.reshape(n, t, d), h0r, h0i, tables, _tile(t, 256))
            else:
                yg, hr, hi = _s5_step(u, h0r, h0i, tables)
            xf = _glu_out(yg.reshape(m, d), xf, g_m, wts["o_w_glu"][i], tm, rows_per_group)
            g_c = d // CH_G
            outs["re"].append(hr.reshape(n, g_c, P_C))
            outs["im"].append(hi.reshape(n, g_c, P_C))
            tm_f = _tile(t, 1024) if is_prompt else m
            ff = wts["moe_w2"][i].shape[1]
            xf = _ffn(xf, norm_ffn, sh_f, sc_f, g_f, wts["moe_w13"][i], wts["moe_w2"][i],
                      tm_f, _tile(ff, 512) if ff % 512 == 0 else ff, rows_per_group,
                      router=(wts["o_router_w"][i], wts["o_router_b"][i]), final_g=final_g)
    y = xf.reshape(n, t, d).astype(x.dtype)
    return (y, jnp.stack(outs["shift"]), jnp.stack(outs["wkv"]), jnp.stack(outs["k"]), jnp.stack(outs["v"]),
            jnp.stack(outs["kidx"]), jnp.stack(outs["re"]), jnp.stack(outs["im"]))


def kernel(x_prompt, x_sample, cache_k, cache_v, cache_kidx, state_shift, state_wkv, state_ssm_re, state_ssm_im, page_table, c_prompt, c_sample, norm_mix, norm_ffn, ada_w, ada_b, rel_bias, norm_final, e_w_in, e_mu, e_w0, e_w2, e_a0, e_a2, e_g2, e_k_k, e_k_a, e_r_k, e_lnx_w, e_lnx_b, e_w_out, e_ffn_w13, e_ffn_w2, o_a_re, o_a_im, o_log_step, o_b_re, o_b_im, o_c_re, o_c_im, o_d, o_w_glu, o_router_w, o_router_b, o_moe_w13, o_moe_w2):
    nb, _, d = x_prompt.shape
    n_dec = x_sample.shape[0]
    n_even = e_w_in.shape[0]
    n_odd = o_a_re.shape[0]
    d_a = d // 2
    n_heads = d_a // HEAD_DIM
    shift_w = state_shift.shape[-1]

    c_all = jnp.concatenate([c_prompt, c_sample], axis=0)
    pad = (-c_all.shape[0]) % SUBLANES
    mods = _ada(jnp.pad(c_all, ((0, pad), (0, 0))), ada_w, ada_b)
    mods_p = mods[:, :nb]
    mods_s = mods[:, nb:nb + n_dec]

    wts = dict(norm_mix=norm_mix, norm_ffn=norm_ffn, norm_final=norm_final, rel_bias=rel_bias,
               e_w_in=e_w_in, e_mu=e_mu, e_w0=e_w0, e_w2=e_w2, e_a0=e_a0, e_a2=e_a2, e_g2=e_g2, e_k_k=e_k_k,
               e_k_a=e_k_a, e_r_k=e_r_k, e_lnx_w=e_lnx_w, e_lnx_b=e_lnx_b, e_w_out=e_w_out,
               ffn_w13=_bf(e_ffn_w13), ffn_w2=_bf(e_ffn_w2),
               o_a_re=o_a_re, o_a_im=o_a_im, o_log_step=o_log_step, o_b_re=o_b_re, o_b_im=o_b_im,
               o_c_re=o_c_re, o_c_im=o_c_im, o_d=o_d, o_w_glu=o_w_glu, o_router_w=o_router_w,
               o_router_b=o_router_b, moe_w13=_bf(o_moe_w13), moe_w2=_bf(o_moe_w2))

    st_p = dict(shift=jnp.zeros((n_even, nb, shift_w), x_prompt.dtype),
                wkv=jnp.zeros((n_even, nb, n_heads, HEAD_DIM, HEAD_DIM), _f32),
                ssm_re=jnp.zeros((n_odd, nb, d // CH_G, P_C), _f32),
                ssm_im=jnp.zeros((n_odd, nb, d // CH_G, P_C), _f32))
    st_s = dict(shift=state_shift, wkv=state_wkv, ssm_re=state_ssm_re, ssm_im=state_ssm_im,
                cache_k=cache_k, cache_v=cache_v, cache_kidx=cache_kidx, page_table=page_table)
    out_p = _run_group(x_prompt, mods_p, True, st_p, wts)
    out_s = _run_group(x_sample, mods_s, False, st_s, wts)
    return (out_p[0], out_s[0]) + out_p[1:] + out_s[1:]
```

```python
import functools
import math

import jax
import jax.numpy as jnp
from jax import lax
from jax.experimental import pallas as pl
from jax.experimental.pallas import tpu as pltpu

HEAD_DIM = 64
LORA_W = 64
LORA_A = 64
LORA_G = 128
H_IDX = 8
D_IDX = 64
TOPK_MAX = 256
N_BUCKETS = 32
MAX_DIST = 128
CH_G = 16
P_C = 64
TOP_E = 2
EPS = 1e-6
LNX_EPS = 64e-5

LANES = 128
SUBLANES = 8
VMEM_LIMIT = 56 * 1024 * 1024
KEY_BLOCK = 256
COUNT_BLOCK = 1024
COUNT_ROWS = 64
NEG = -2.0e30
M_INIT = -1.0e30

_bf16 = jnp.bfloat16
_f32 = jnp.float32


def _bf(x):
    return x.astype(_bf16)


def _dot(a, b):
    return jnp.dot(a, b, preferred_element_type=_f32)


def _dot_nt(a, b):
    return lax.dot_general(a, b, (((1,), (1,)), ((), ())), preferred_element_type=_f32)


def _params(*sem):
    return pltpu.CompilerParams(dimension_semantics=sem, vmem_limit_bytes=VMEM_LIMIT)


def _split_bf16(x, parts):
    out = []
    for _ in range(parts - 1):
        hi = _bf(x)
        out.append(hi)
        x = x - hi.astype(_f32)
    out.append(_bf(x))
    return out


def _segsum(x, ones_bd):
    hi, mid, lo = _split_bf16(x, 3)
    return _dot(hi, ones_bd) + _dot(mid, ones_bd) + _dot(lo, ones_bd)


def _norm_mod(x, g, shift, scale):
    ms = jnp.mean(x * x, axis=-1, keepdims=True)
    return (x * lax.rsqrt(ms + EPS) * g) * (1.0 + scale) + shift


def _silu(x):
    return x * jax.nn.sigmoid(x)


def _gelu_tanh(x):
    return 0.5 * x * (1.0 + jnp.tanh(math.sqrt(2.0 / math.pi) * (x + 0.044715 * (x * x * x))))


def _softplus(x):
    return jnp.maximum(x, 0.0) + jnp.log(1.0 + jnp.exp(-jnp.abs(x)))


def _rel_bucket(dist):
    max_exact = N_BUCKETS // 2
    n = jnp.maximum(dist, 0)
    nf = jnp.maximum(n, 1).astype(_f32)
    large = max_exact + (jnp.log(nf / max_exact) / math.log(MAX_DIST / max_exact) * (N_BUCKETS - max_exact)).astype(jnp.int32)
    return jnp.where(n < max_exact, n, jnp.minimum(large, N_BUCKETS - 1))


def _group_spec(rows_per_block, rows_per_group, width):
    def spec(r):
        return pl.BlockSpec((1, r, width), lambda i, *_: ((i * rows_per_block) // rows_per_group, 0, 0))
    return spec


def _ada_body(c_ref, w_ref, b_ref, o_ref):
    o_ref[0] = _dot(_bf(_silu(c_ref[...])), _bf(w_ref[0])) + b_ref[0]


def _ada(c, ada_w, ada_b):
    depth, d, n6 = ada_w.shape
    rows = c.shape[0]
    tn = n6 // 4
    return pl.pallas_call(
        _ada_body,
        grid=(depth, n6 // tn),
        in_specs=[pl.BlockSpec((rows, d), lambda l, j: (0, 0)),
                  pl.BlockSpec((1, d, tn), lambda l, j: (l, 0, j)),
                  pl.BlockSpec((1, 1, tn), lambda l, j: (l, 0, j))],
        out_specs=pl.BlockSpec((1, rows, tn), lambda l, j: (l, 0, j)),
        out_shape=jax.ShapeDtypeStruct((depth, rows, n6), _f32),
        compiler_params=_params("arbitrary", "arbitrary"),
        name="ada_mod",
    )(c, ada_w, ada_b.reshape(depth, 1, n6))


def _norm_mod_body(x_ref, g_ref, sh_ref, sc_ref, o_ref):
    o_ref[...] = _norm_mod(x_ref[...], g_ref[...], sh_ref[0], sc_ref[0])


def _norm_mod_call(x, g, shift, scale, tm, rows_per_group):
    m, d = x.shape
    gs = _group_spec(tm, rows_per_group, d)
    return pl.pallas_call(
        _norm_mod_body,
        grid=(m // tm,),
        in_specs=[pl.BlockSpec((tm, d), lambda i: (i, 0)),
                  pl.BlockSpec((1, d), lambda i: (0, 0)),
                  gs(shift.shape[1]), gs(scale.shape[1])],
        out_specs=pl.BlockSpec((tm, d), lambda i: (i, 0)),
        out_shape=jax.ShapeDtypeStruct((m, d), _f32),
        compiler_params=_params("arbitrary"),
        name="norm_mod",
    )(x, g, shift, scale)


def _in_proj_body(x_ref, g_ref, sh_ref, sc_ref, wa_ref, wq_ref, wk_ref, wv_ref, wqi_ref, wkw_ref,
                  pa_ref, q_ref, k_ref, v_ref, qi_ref, ki_ref, wi_ref):
    h = _bf(_norm_mod(x_ref[...], g_ref[...], sh_ref[0], sc_ref[0]))
    pa_ref[...] = _dot(h, wa_ref[...])
    q_ref[...] = _bf(_dot(h, wq_ref[...]))
    k_ref[...] = _dot(h, wk_ref[...])
    v_ref[...] = _dot(h, wv_ref[...])
    qi_ref[...] = _bf(_dot(h, wqi_ref[...]))
    kw = _dot(h, wkw_ref[...])
    ki_ref[...] = kw[:, :D_IDX]
    wi_ref[...] = kw[:, D_IDX:D_IDX + H_IDX]


def _in_proj(x, g, shift, scale, w_in, tm, rows_per_group):
    m, d = x.shape
    d_a = d // 2
    d_b = d // 2
    shift_w = 3 * d_a + LORA_W + LORA_A + LORA_G
    c1 = shift_w + 3 * d_b
    c2 = c1 + H_IDX * D_IDX
    wa = _bf(w_in[:, :shift_w])
    wq = _bf(w_in[:, shift_w:shift_w + d_b])
    wk = _bf(w_in[:, shift_w + d_b:shift_w + 2 * d_b])
    wv = _bf(w_in[:, shift_w + 2 * d_b:c1])
    wqi = _bf(w_in[:, c1:c2])
    wkw = _bf(jnp.pad(w_in[:, c2:], ((0, 0), (0, LANES - D_IDX - H_IDX))))
    gs = _group_spec(tm, rows_per_group, d)
    full = lambda a: pl.BlockSpec(a.shape, lambda i: (0, 0))
    row = lambda w: pl.BlockSpec((tm, w), lambda i: (i, 0))
    widths = [(shift_w, _f32), (d_b, _bf16), (d_b, _f32), (d_b, _f32), (H_IDX * D_IDX, _bf16),
              (D_IDX, _f32), (H_IDX, _f32)]
    return pl.pallas_call(
        _in_proj_body,
        grid=(m // tm,),
        in_specs=[row(d), pl.BlockSpec((1, d), lambda i: (0, 0)), gs(shift.shape[1]), gs(scale.shape[1]),
                  full(wa), full(wq), full(wk), full(wv), full(wqi), full(wkw)],
        out_specs=[row(w) for w, _ in widths],
        out_shape=[jax.ShapeDtypeStruct((m, w), dt) for w, dt in widths],
        compiler_params=_params("arbitrary"),
        name="in_proj",
    )(x, g, shift, scale, wa, wq, wk, wv, wqi, wkw)


def _rwkv_prep_body(seq_is_one, rows_per_group, tm,
                    p_ref, pprev_ref, prev0_ref, mu_ref, w0_ref, a0_ref, kk_ref, ka_ref, rk_ref,
                    wwa_ref, g2_ref, ones_ref,
                    r_ref, w_ref, k_ref, v_ref, av_ref, bv_ref, g_ref, bonus_ref):
    d_a = r_ref.shape[1]
    p = p_ref[...]
    if seq_is_one:
        p_prev = prev0_ref[0]
    else:
        i = pl.program_id(0)
        first = (i * tm) % rows_per_group == 0
        prev_row = jnp.where(first, prev0_ref[0], pprev_ref[SUBLANES - 1:SUBLANES, :])
        rolled = pltpu.roll(p, 1, axis=0)
        row_id = lax.broadcasted_iota(jnp.int32, p.shape, 0)
        p_prev = jnp.where(row_id == 0, prev_row, rolled)
    ps = p + (p_prev - p) * mu_ref[...]
    r = ps[:, :d_a]
    k = ps[:, d_a:2 * d_a]
    v = ps[:, 2 * d_a:3 * d_a]
    xwa = ps[:, 3 * d_a:3 * d_a + LORA_W + LORA_A]
    xg = ps[:, 3 * d_a + LORA_W + LORA_A:]
    lane = lax.broadcasted_iota(jnp.int32, xwa.shape, 1)
    xwa = jnp.where(lane < LORA_W, jnp.tanh(xwa), xwa)
    lwa = _dot(_bf(xwa), wwa_ref[...])
    w_log = -_softplus(-(w0_ref[...] + lwa[:, :d_a])) - 0.5
    decay = jnp.exp(-jnp.exp(w_log))
    a = jax.nn.sigmoid(a0_ref[...] + lwa[:, d_a:])
    g_ref[...] = _dot(_bf(jax.nn.sigmoid(xg)), g2_ref[...])
    ones_bd = ones_ref[...]
    kk = k * kk_ref[...]
    kk = kk / jnp.maximum(jnp.sqrt(_segsum(kk * kk, ones_bd)), 1e-12)
    k_mod = k * (1.0 + (a - 1.0) * ka_ref[...])
    r_ref[...] = r
    w_ref[...] = decay
    k_ref[...] = k_mod
    v_ref[...] = v
    av_ref[...] = -kk
    bv_ref[...] = kk * a
    bonus_ref[...] = _segsum(r * k_mod * rk_ref[...], ones_bd) * v


def _ones_blockdiag(n, group):
    idx = jnp.arange(n, dtype=jnp.int32) // group
    return (idx[:, None] == idx[None, :]).astype(_bf16)


def _rwkv_prep(p_a, prev0, mu, w0, w2, a0, a2, g2, k_k, k_a, r_k, tm, rows_per_group):
    m, shift_w = p_a.shape
    d_a = w0.shape[-1]
    seq_is_one = rows_per_group == tm and prev0.shape[1] == tm
    wwa = jnp.zeros((LORA_W + LORA_A, 2 * d_a), _f32)
    wwa = _bf(wwa.at[:LORA_W, :d_a].set(w2).at[LORA_W:, d_a:].set(a2))
    ones_bd = _ones_blockdiag(d_a, HEAD_DIM)
    vec = lambda a: a.reshape(1, -1)
    gs = _group_spec(tm, rows_per_group, shift_w)
    full = lambda a: pl.BlockSpec(a.shape, lambda i: (0, 0))
    row = pl.BlockSpec((tm, d_a), lambda i: (i, 0))
    ins = [p_a, p_a, prev0, vec(mu), vec(w0), vec(a0), vec(k_k), vec(k_a), vec(r_k), wwa, _bf(g2), ones_bd]
    in_specs = [pl.BlockSpec((tm, shift_w), lambda i: (i, 0)),
                pl.BlockSpec((SUBLANES, shift_w), lambda i: (jnp.maximum(i * (tm // SUBLANES) - 1, 0), 0)),
                gs(prev0.shape[1])] + [full(a) for a in ins[3:]]
    return pl.pallas_call(
        functools.partial(_rwkv_prep_body, seq_is_one, rows_per_group, tm),
        grid=(m // tm,),
        in_specs=in_specs,
        out_specs=[row] * 8,
        out_shape=[jax.ShapeDtypeStruct((m, d_a), _f32)] * 8,
        compiler_params=_params("arbitrary"),
        name="rwkv_prep",
    )(*ins)


def _rwkv_scan_body(nb, n_pairs, tc,
                    r_ref, w_ref, k_ref, v_ref, av_ref, bv_ref, s0_ref, ones_ref, expand_ref, eye_ref,
                    o_ref, s_ref):
    t_blk = pl.program_id(1)

    @pl.when(t_blk == 0)
    def _():
        s_ref[...] = s0_ref[...]

    ones_k = ones_ref[...]
    parts = ones_k.shape[0] // LANES
    expand = expand_ref[...]
    eye = eye_ref[...] > 0.5
    units = [(n, hp) for n in range(nb) for hp in range(n_pairs)]

    def rowsum(xs, ones, parts):
        lhs = jnp.concatenate([jnp.concatenate(_split_bf16(x, parts), axis=1) for x in xs], axis=0)
        res = _dot(lhs, ones)
        return [res[u * HEAD_DIM:(u + 1) * HEAD_DIM] for u in range(len(xs))]

    sub = min(SUBLANES, tc)
    row_id = lax.broadcasted_iota(jnp.int32, (sub, LANES), 0)

    def tile_steps(j, carry):
        t0 = pl.multiple_of(j * sub, sub)
        tiles = {}
        for (n, hp) in units:
            sl = pl.ds(hp * LANES, LANES)
            tiles[(n, hp)] = tuple(ref[n, pl.ds(t0, sub), sl] for ref in (r_ref, w_ref, k_ref, v_ref, av_ref, bv_ref))
        states = [s_ref[n, hp] for (n, hp) in units]
        o_tiles = [jnp.zeros((sub, LANES), _f32) for _ in units]
        vcols = []
        for u in units:
            vt = tiles[u][3]
            by_head = jnp.concatenate([vt[:, :HEAD_DIM], vt[:, HEAD_DIM:]], axis=0)
            lhs = jnp.concatenate(_split_bf16(by_head, 3), axis=0)
            vcols.append(lax.dot_general(lhs, expand, (((0,), (0,)), ((), ())), preferred_element_type=_f32))
        for tt in range(sub):
            rows = {u: tuple(x[tt:tt + 1, :] for x in tiles[u]) for u in units}
            sa = rowsum([s * rows[u][4] for s, u in zip(states, units)], ones_k, parts)
            for idx, u in enumerate(units):
                r_t, w_t, k_t, v_t, av_t, bv_t = rows[u]
                vcol = vcols[idx][:, tt * LANES:(tt + 1) * LANES]
                states[idx] = states[idx] * w_t + sa[idx] * bv_t + vcol * k_t
            ob = rowsum([s * rows[u][0] for s, u in zip(states, units)], ones_k, parts)
            for idx in range(len(units)):
                o_row = jnp.sum(jnp.where(eye, ob[idx], 0.0), axis=0, keepdims=True)
                o_tiles[idx] = jnp.where(row_id == tt, jnp.broadcast_to(o_row, (sub, LANES)), o_tiles[idx])
        for idx, (n, hp) in enumerate(units):
            s_ref[n, hp] = states[idx]
            o_ref[n, pl.ds(t0, sub), pl.ds(hp * LANES, LANES)] = o_tiles[idx]
        return carry

    lax.fori_loop(0, tc // sub, tile_steps, 0)


def _rwkv_scan(r, w, k, v, av, bv, s0, nb, tc):
    n, t, d_a = r.shape
    n_pairs = d_a // LANES
    half = (jnp.arange(LANES, dtype=jnp.int32) // HEAD_DIM)
    ones_blk = (half[:, None] == half[None, :]).astype(_bf16)
    ones_k = jnp.concatenate([ones_blk] * (3 if t == 1 else 2), axis=0)
    sub = min(SUBLANES, tc)
    src_h = jnp.arange(2 * sub, dtype=jnp.int32) // sub
    src_t = jnp.arange(2 * sub, dtype=jnp.int32) % sub
    dst_t = jnp.arange(sub * LANES, dtype=jnp.int32) // LANES
    dst_h = (jnp.arange(sub * LANES, dtype=jnp.int32) % LANES) // HEAD_DIM
    expand = ((src_t[:, None] == dst_t[None, :]) & (src_h[:, None] == dst_h[None, :])).astype(_bf16)
    expand = jnp.concatenate([expand] * 3, axis=0)
    eye =(jnp.arange(HEAD_DIM, dtype=jnp.int32)[:, None]
           == (jnp.arange(LANES, dtype=jnp.int32) % HEAD_DIM)[None, :]).astype(_f32)
    seq = pl.BlockSpec((nb, tc, d_a), lambda b, j: (b, j, 0))
    st = pl.BlockSpec((nb, n_pairs, HEAD_DIM, LANES), lambda b, j: (b, 0, 0, 0))
    full = lambda a: pl.BlockSpec(a.shape, lambda b, j: (0, 0))
    return pl.pallas_call(
        functools.partial(_rwkv_scan_body, nb, n_pairs, tc),
        grid=(n // nb, t // tc),
        in_specs=[seq] * 6 + [st, full(ones_k), full(expand), full(eye)],
        out_specs=[seq, st],
        out_shape=[jax.ShapeDtypeStruct((n, t, d_a), _f32),
                   jax.ShapeDtypeStruct(s0.shape, _f32)],
        compiler_params=_params("arbitrary", "arbitrary"),
        name="rwkv_scan",
    )(r, w, k, v, av, bv, s0, ones_k, expand, eye)


def _pack_state(s):
    n, h = s.shape[:2]
    return s.reshape(n, h // 2, 2, HEAD_DIM, HEAD_DIM).transpose(0, 1, 3, 2, 4).reshape(n, h // 2, HEAD_DIM, LANES)


def _unpack_state(s):
    n, hp = s.shape[:2]
    return s.reshape(n, hp, HEAD_DIM, 2, HEAD_DIM).transpose(0, 1, 3, 2, 4).reshape(n, hp * 2, HEAD_DIM, HEAD_DIM)


def _mix_out_body(o_ref, bonus_ref, g_ref, ob_ref, x_ref, gate_ref, lw_ref, lb_ref, ones_ref, wa_ref, wb_ref,
                  out_ref):
    ones_bd = ones_ref[...]
    o = o_ref[...]
    inv = 1.0 / HEAD_DIM
    mean = _segsum(o, ones_bd) * inv
    dlt = o - mean
    var = _segsum(dlt * dlt, ones_bd) * inv
    on = dlt * lax.rsqrt(var + LNX_EPS) * lw_ref[...] + lb_ref[...]
    oa = (on + bonus_ref[...]) * g_ref[...]
    y = _dot(_bf(oa), wa_ref[...]) + _dot(ob_ref[...], wb_ref[...])
    out_ref[...] = x_ref[...] + gate_ref[0] * y


def _mix_out(o, bonus, g, o_b, x, gate, lnx_w, lnx_b, w_out, tm, rows_per_group):
    m, d = x.shape
    d_a = o.shape[1]
    ones_bd = _ones_blockdiag(d_a, HEAD_DIM)
    wa = _bf(w_out[:d_a])
    wb = _bf(w_out[d_a:])
    gs = _group_spec(tm, rows_per_group, d)
    full = lambda a: pl.BlockSpec(a.shape, lambda i: (0, 0))
    rowa = pl.BlockSpec((tm, d_a), lambda i: (i, 0))
    rowd = pl.BlockSpec((tm, d), lambda i: (i, 0))
    lw = lnx_w.reshape(1, -1)
    lb = lnx_b.reshape(1, -1)
    return pl.pallas_call(
        _mix_out_body,
        grid=(m // tm,),
        in_specs=[rowa, rowa, rowa, pl.BlockSpec((tm, o_b.shape[1]), lambda i: (i, 0)), rowd, gs(gate.shape[1]),
                  full(lw), full(lb), full(ones_bd), full(wa), full(wb)],
        out_specs=rowd,
        out_shape=jax.ShapeDtypeStruct((m, d), _f32),
        compiler_params=_params("arbitrary"),
        name="mix_out",
    )(o, bonus, g, o_b, x, gate, lw, lb, ones_bd, wa, wb)


def _key_to_float(u):
    key = u ^ jnp.int32(-2147483648)
    bits = jnp.where(key >= 0, key, key ^ jnp.int32(0x7FFFFFFF))
    return lax.bitcast_convert_type(bits, _f32)


def _kth_largest(count_ge, k_row, shape):
    def body(it, u):
        bit = jnp.left_shift(jnp.int32(1), 31 - it)
        cand_u = u | bit
        ok = count_ge(_key_to_float(cand_u)) >= k_row
        return jnp.where(ok, cand_u, u)

    u = lax.fori_loop(0, 32, body, jnp.zeros(shape, jnp.int32))
    return _key_to_float(u)


def _tie_cut(count_eq_lt, budget, n_bits, shape):
    def body(it, c):
        cand = c | jnp.left_shift(jnp.int32(1), n_bits - 1 - it)
        ok = count_eq_lt(cand) <= budget
        return jnp.where(ok, cand, c)

    return lax.fori_loop(0, n_bits, body, jnp.zeros(shape, jnp.int32))


def _dsa_prompt_body(n_heads, k_sel, t_len,
                     far_ref, qt_ref, qit_ref, wit_ref, k_ref, vt_ref, ki_ref, near_ref,
                     ot_ref, sc_ref, cut_ref, qz_ref, m_ref, l_ref, acc_ref):
    qb = qt_ref.shape[2]
    i = pl.program_id(1)
    q0 = i * qb
    qpos = q0 + lax.broadcasted_iota(jnp.int32, (1, qb), 1)
    n_kb = (q0 + qb + KEY_BLOCK - 1) // KEY_BLOCK
    row_kb = lax.broadcasted_iota(jnp.int32, (KEY_BLOCK, qb), 0)

    @pl.when(i == 0)
    def _():
        sc_ref[...] = jnp.full(sc_ref.shape, -jnp.inf, _f32)

    w8 = (wit_ref[0] * (H_IDX ** -0.5)) * (D_IDX ** -0.5)

    qi_all = jnp.concatenate([qit_ref[0, h * D_IDX:(h + 1) * D_IDX, :] for h in range(H_IDX)], axis=1)

    def score_block(kb, carry):
        c0 = pl.multiple_of(kb * KEY_BLOCK, KEY_BLOCK)
        s_all = _dot(ki_ref[0, pl.ds(c0, KEY_BLOCK), :], qi_all)
        acc = w8[0:1, :] * jnp.maximum(s_all[:, :qb], 0.0)
        for h in range(1, H_IDX):
            acc = acc + w8[h:h + 1, :] * jnp.maximum(s_all[:, h * qb:(h + 1) * qb], 0.0)
        sc_ref[pl.ds(c0, KEY_BLOCK), :] = jnp.where(c0 + row_kb <= qpos, acc, -jnp.inf)
        return carry

    lax.fori_loop(0, n_kb, score_block, 0)

    cb = min(COUNT_BLOCK, t_len)
    n_cb = (q0 + qb + cb - 1) // cb
    row_cb = lax.broadcasted_iota(jnp.int32, (cb, qb), 0)

    def count(pred):
        def body(kb, acc):
            c0 = pl.multiple_of(kb * cb, cb)
            hit = jnp.where(pred(sc_ref[pl.ds(c0, cb), :], c0 + row_cb), 1.0, 0.0)
            for j in range(cb // COUNT_ROWS):
                acc = acc + hit[j * COUNT_ROWS:(j + 1) * COUNT_ROWS, :]
            return acc
        acc = lax.fori_loop(0, n_cb, body, jnp.zeros((COUNT_ROWS, qb), _f32))
        return jnp.sum(acc, axis=0, keepdims=True)

    k_row = jnp.minimum(k_sel, qpos + 1).astype(_f32)
    thr = _kth_largest(lambda cand: count(lambda s, pos: s >= cand), k_row, (1, qb))
    n_gt = count(lambda s, pos: s > thr)
    n_eq = count(lambda s, pos: s == thr)
    budget = k_row - n_gt
    cut_ref[...] = jnp.full((1, qb), 2 * t_len, jnp.int32)

    @pl.when(jnp.max(n_eq - budget) > 0.5)
    def _():
        cut_ref[...] = _tie_cut(lambda c: count(lambda s, pos: (s == thr) & (pos < c)), budget,
                                (2 * t_len).bit_length(), (1, qb))

    cut = cut_ref[...]

    m_ref[...] = jnp.full(m_ref.shape, M_INIT, _f32)
    l_ref[...] = jnp.zeros(l_ref.shape, _f32)
    acc_ref[...] = jnp.zeros(acc_ref.shape, _f32)
    pair_row = lax.broadcasted_iota(jnp.int32, (LANES, qb), 0) // HEAD_DIM
    for hp in range(n_heads // 2):
        qp = qt_ref[0, hp * LANES:(hp + 1) * LANES, :] * (HEAD_DIM ** -0.5)
        zero = jnp.zeros_like(qp)
        qz_ref[hp] = jnp.concatenate([jnp.where(pair_row == 0, qp, zero), jnp.where(pair_row == 1, qp, zero)], axis=1)

    def attend(c0, width, bias_of_head):
        scb = sc_ref[pl.ds(c0, width), :]
        pos = c0 + lax.broadcasted_iota(jnp.int32, (width, qb), 0)
        sel = (scb > thr) | ((scb == thr) & (pos < cut))
        m_old = [m_ref[h] for h in range(n_heads)]
        l_old = [l_ref[h] for h in range(n_heads)]
        a_old = [acc_ref[h] for h in range(n_heads)]
        s2 = [_dot(k_ref[0, pl.ds(c0, width), hp * LANES:(hp + 1) * LANES], qz_ref[hp]) for hp in range(n_heads // 2)]
        m_out, l_out, a_out = [], [], []
        for h in range(n_heads):
            s = s2[h // 2][:, (h % 2) * qb:(h % 2 + 1) * qb] + bias_of_head(h)
            s = jnp.where(sel, s, NEG)
            m_new = jnp.maximum(m_old[h], jnp.max(s, axis=0, keepdims=True))
            alpha = jnp.exp(m_old[h] - m_new)
            p = jnp.exp(s - m_new)
            l_out.append(alpha * l_old[h] + jnp.sum(p, axis=0, keepdims=True))
            pv = _dot(vt_ref[0, h * HEAD_DIM:(h + 1) * HEAD_DIM, pl.ds(c0, width)], _bf(p))
            a_out.append(alpha * a_old[h] + pv)
            m_out.append(m_new)
        for h in range(n_heads):
            m_ref[h] = m_out[h]
            l_ref[h] = l_out[h]
            acc_ref[h] = a_out[h]

    n_far = jnp.maximum(i - 1, 0) // (KEY_BLOCK // qb)

    def far_block(kb, carry):
        attend(pl.multiple_of(kb * KEY_BLOCK, KEY_BLOCK), KEY_BLOCK, lambda h: far_ref[h])
        return carry

    lax.fori_loop(0, n_far, far_block, 0)

    def near_block(j, carry):
        attend(pl.multiple_of(j * qb, qb), qb, lambda h: near_ref[i - j, h])
        return carry

    lax.fori_loop(n_far * (KEY_BLOCK // qb), i + 1, near_block, 0)

    for h in range(n_heads):
        ot_ref[0, h * HEAD_DIM:(h + 1) * HEAD_DIM, :] = _bf(acc_ref[h] / l_ref[h])


def _dsa_prompt(q, k, v, qi, ki, wi, rel_bias, qb):
    n, t, d_b = q.shape
    n_heads = d_b // HEAD_DIM
    k_sel = min(TOPK_MAX, t // 4)
    tr = lambda a: a.transpose(0, 2, 1)
    n_near = KEY_BLOCK // qb + 1
    qq = jnp.arange(qb, dtype=jnp.int32)
    dist = (jnp.arange(n_near, dtype=jnp.int32)[:, None, None] * qb + qq[None, None, :] - qq[None, :, None])
    onehot = (_rel_bucket(dist)[..., None] == jnp.arange(N_BUCKETS, dtype=jnp.int32)).astype(_f32)
    near = jnp.einsum('dkqb,bh->dhkq', onehot, rel_bias, precision=lax.Precision.HIGHEST)
    far = rel_bias[N_BUCKETS - 1]
    res = lambda shape: pl.BlockSpec(shape, lambda b, i, *_: (b, 0, 0))
    blk = lambda w: pl.BlockSpec((1, w, qb), lambda b, i, *_: (b, 0, i))
    grid_spec = pltpu.PrefetchScalarGridSpec(
        num_scalar_prefetch=0,
        grid=(n, t // qb),
        in_specs=[pl.BlockSpec(memory_space=pltpu.SMEM),
                  blk(d_b), blk(H_IDX * D_IDX), blk(H_IDX),
                  res((1, t, d_b)), res((1, d_b, t)), res((1, t, D_IDX)),
                  pl.BlockSpec(near.shape, lambda b, i, *_: (0, 0, 0, 0))],
        out_specs=blk(d_b),
        scratch_shapes=[pltpu.VMEM((t, qb), _f32), pltpu.VMEM((1, qb), jnp.int32),
                        pltpu.VMEM((n_heads // 2, LANES, 2 * qb), _bf16),
                        pltpu.VMEM((n_heads, 1, qb), _f32), pltpu.VMEM((n_heads, 1, qb), _f32),
                        pltpu.VMEM((n_heads, HEAD_DIM, qb), _f32)],
    )
    o_t = pl.pallas_call(
        functools.partial(_dsa_prompt_body, n_heads, k_sel, t),
        grid_spec=grid_spec,
        out_shape=jax.ShapeDtypeStruct((n, d_b, t), _bf16),
        compiler_params=_params("arbitrary", "arbitrary"),
        name="dsa_prompt",
    )(far, tr(q), tr(qi), tr(wi), _bf(k), tr(_bf(v)), _bf(ki), near)
    return tr(o_t)


def _dsa_sample_scores_body(n_heads, k_sel, n_pages, page, pg, pt_ref, q_ref, qi_ref, wi_ref, kn_ref, kin_ref, *rest):
    cki_refs, ck_refs = rest[:pg], rest[pg:2 * pg]
    btab_ref, bnew_ref, p_ref, pnew_ref, sc_ref, lg_ref = rest[2 * pg:]
    step = pl.program_id(1)
    past = n_pages * page
    w8 = _bf(wi_ref[0] * (H_IDX ** -0.5)).astype(_f32)
    relu_bf = lambda s: _bf(jnp.maximum(s, 0.0)).astype(_f32)
    qi8 = qi_ref[0]
    q8 = q_ref[0]

    for j in range(pg):
        c0 = pl.multiple_of((step * pg + j) * page, page)
        s8 = _dot(qi8, _bf(cki_refs[j][0, 0])) * (D_IDX ** -0.5)
        sc_ref[:, pl.ds(c0, page)] = jnp.sum(w8 * relu_bf(s8), axis=0, keepdims=True)
        rows = [_dot(q8, _bf(ck_refs[j][0, 0, h]))[h:h + 1] for h in range(n_heads)]
        lg = jnp.concatenate(rows, axis=0) * (HEAD_DIM ** -0.5)
        lg_ref[:, pl.ds(c0, page)] = lg + btab_ref[:, pl.ds(c0, page)]

    @pl.when(step == n_pages // pg - 1)
    def _():
        sc = sc_ref[...]
        s_new8 = jnp.sum(qi8.astype(_f32) * _bf(kin_ref[0]).astype(_f32), axis=1, keepdims=True) * (D_IDX ** -0.5)
        s_new = jnp.sum(w8 * relu_bf(s_new8), axis=0, keepdims=True)
        pos = lax.broadcasted_iota(jnp.int32, sc.shape, 1)

        def count(pred):
            hits = jnp.sum(jnp.where(pred(sc, pos), 1.0, 0.0), axis=1, keepdims=True)
            return hits + jnp.where(pred(s_new, past), 1.0, 0.0)

        k_row = jnp.full((1, 1), float(k_sel), _f32)
        thr = _kth_largest(lambda cand: count(lambda s, ps: s >= cand), k_row, (1, 1))
        budget = k_row - count(lambda s, ps: s > thr)
        cut = _tie_cut(lambda c: count(lambda s, ps: (s == thr) & (ps < c)), budget,
                       (2 * (past + 1)).bit_length(), (1, 1))
        sel = (sc > thr) | ((sc == thr) & (pos < cut))
        sel_new = (s_new > thr) | ((s_new == thr) & (past < cut))

        kn = _bf(kn_ref[0]).astype(_f32)
        lg_new = jnp.sum(q8.astype(_f32) * kn, axis=1, keepdims=True) * (HEAD_DIM ** -0.5) + bnew_ref[...]
        lg_new = jnp.where(sel_new, lg_new, NEG)
        s_all = jnp.where(sel, lg_ref[...], NEG)
        m = jnp.maximum(jnp.maximum(jnp.max(s_all, axis=1, keepdims=True), lg_new), M_INIT)
        pr = jnp.exp(s_all - m)
        pr_new = jnp.exp(lg_new - m)
        l = jnp.sum(pr, axis=1, keepdims=True) + pr_new
        p_ref[0] = pr
        col = lax.broadcasted_iota(jnp.int32, (n_heads, 2), 1)
        pnew_ref[0] = jnp.where(col == 0, pr_new, l)


def _dsa_sample_pv_body(n_heads, n_pages, page, pg, pt_ref, p_ref, pnew_ref, vn_ref, *rest):
    cv_refs, (o_ref, acc_ref) = rest[:pg], rest[pg:]
    step = pl.program_id(1)

    @pl.when(step == 0)
    def _():
        acc_ref[...] = pnew_ref[0][:, 0:1] * _bf(vn_ref[0]).astype(_f32)

    acc = acc_ref[...]
    for j in range(pg):
        c0 = pl.multiple_of((step * pg + j) * page, page)
        pb = _bf(p_ref[0, :, pl.ds(c0, page)])
        rows = [_dot_nt(pb, _bf(cv_refs[j][0, 0, h]))[h:h + 1] for h in range(n_heads)]
        acc = acc + jnp.concatenate(rows, axis=0)
    acc_ref[...] = acc

    @pl.when(step == n_pages // pg - 1)
    def _():
        o_ref[0] = _bf(acc / pnew_ref[0][:, 1:2])


def _dsa_sample(q, k_new, v_new, qi, ki_new, wi, ck, cv, cki, layer, page_table, rel_bias):
    b, _, d_b = q.shape
    n_heads = d_b // HEAD_DIM
    n_pages = page_table.shape[1]
    page = ck.shape[2]
    ck, cv, cki = (jnp.moveaxis(a, 2, -1) for a in (ck, cv, cki))
    past = n_pages * page
    k_sel = min(TOPK_MAX, (past + 1) // 4)
    pg = max(g for g in (1, 2, 4, 8) if n_pages % g == 0)
    kpos = jnp.arange(past, dtype=jnp.int32)
    btab = rel_bias[_rel_bucket(past - kpos)].T
    bnew = rel_bias[_rel_bucket(jnp.zeros((1,), jnp.int32))].T
    heads = lambda a: a.reshape(b, n_heads, HEAD_DIM)
    pt = page_table.reshape(-1)
    per_b = lambda shape: pl.BlockSpec((1,) + shape, lambda i, s, pt: (i,) + (0,) * len(shape))
    const = lambda a: pl.BlockSpec(a.shape, lambda i, s, pt: (0,) * a.ndim)

    def paged(tail, j):
        return pl.BlockSpec((1, 1) + tail + (page,),
                            lambda i, s, pt: (layer, pt[i * n_pages + s * pg + j]) + (0,) * (len(tail) + 1))

    grid = (b, n_pages // pg)
    probs, p_new = pl.pallas_call(
        functools.partial(_dsa_sample_scores_body, n_heads, k_sel, n_pages, page, pg),
        grid_spec=pltpu.PrefetchScalarGridSpec(
            num_scalar_prefetch=1, grid=grid,
            in_specs=[per_b((n_heads, HEAD_DIM)), per_b((H_IDX, D_IDX)), per_b((H_IDX, 1)),
                      per_b((n_heads, HEAD_DIM)), per_b((1, D_IDX))]
                     + [paged((D_IDX,), j) for j in range(pg)]
                     + [paged((n_heads, HEAD_DIM), j) for j in range(pg)]
                     + [const(btab), const(bnew)],
            out_specs=[per_b((n_heads, past)), per_b((n_heads, 2))],
            scratch_shapes=[pltpu.VMEM((1, past), _f32), pltpu.VMEM((n_heads, past), _f32)]),
        out_shape=[jax.ShapeDtypeStruct((b, n_heads, past), _f32), jax.ShapeDtypeStruct((b, n_heads, 2), _f32)],
        compiler_params=_params("arbitrary", "arbitrary"),
        name="dsa_sample_scores",
    )(pt, heads(q), qi.reshape(b, H_IDX, D_IDX), wi.reshape(b, H_IDX, 1), heads(k_new), ki_new,
      *([cki] * pg), *([ck] * pg), btab, bnew)
    out = pl.pallas_call(
        functools.partial(_dsa_sample_pv_body, n_heads, n_pages, page, pg),
        grid_spec=pltpu.PrefetchScalarGridSpec(
            num_scalar_prefetch=1, grid=grid,
            in_specs=[per_b((n_heads, past)), per_b((n_heads, 2)), per_b((n_heads, HEAD_DIM))]
                     + [paged((n_heads, HEAD_DIM), j) for j in range(pg)],
            out_specs=per_b((n_heads, HEAD_DIM)),
            scratch_shapes=[pltpu.VMEM((n_heads, HEAD_DIM), _f32)]),
        out_shape=jax.ShapeDtypeStruct((b, n_heads, HEAD_DIM), _bf16),
        compiler_params=_params("arbitrary", "arbitrary"),
        name="dsa_sample_pv",
    )(pt, probs, p_new, heads(v_new), *([cv] * pg))
    return out.reshape(b, 1, d_b)


def _ffn_body(routed, final_norm, n_exp,
              x_ref, g_ref, sh_ref, sc_ref, gate_ref, rw_ref, rb_ref, w1_ref, w3_ref, w2_ref, gf_ref,
              o_ref, h_ref, acc_ref, dg_ref):
    e = pl.program_id(1)
    f = pl.program_id(2)
    first = (e == 0) & (f == 0)
    last = (e == pl.num_programs(1) - 1) & (f == pl.num_programs(2) - 1)

    @pl.when(first)
    def _():
        h = _norm_mod(x_ref[...], g_ref[...], sh_ref[0], sc_ref[0])
        h_ref[...] = _bf(h)
        acc_ref[...] = jnp.zeros(acc_ref.shape, _f32)
        if routed:
            logits = _dot(_bf(h), rw_ref[...]) + rb_ref[...]
            lane = lax.broadcasted_iota(jnp.int32, logits.shape, 1).astype(_f32)
            logits = jnp.where(lane < n_exp, logits, -jnp.inf)
            v1 = jnp.max(logits, axis=1, keepdims=True)
            i1 = jnp.min(jnp.where(logits == v1, lane, float(LANES)), axis=1, keepdims=True)
            rest = jnp.where(lane == i1, -jnp.inf, logits)
            v2 = jnp.max(rest, axis=1, keepdims=True)
            i2 = jnp.min(jnp.where(rest == v2, lane, float(LANES)), axis=1, keepdims=True)
            e2 = jnp.exp(v2 - v1)
            den = 1.0 + e2
            dg_ref[...] = jnp.where(lane == i1, 1.0 / den, 0.0) + jnp.where(lane == i2, e2 / den, 0.0)

    h = h_ref[...]
    a = _silu(_dot(h, w1_ref[0])) * _dot(h, w3_ref[0])
    y = _dot(_bf(a), w2_ref[0])
    if routed:
        lane = lax.broadcasted_iota(jnp.int32, dg_ref.shape, 1)
        y = y * jnp.sum(jnp.where(lane == e, dg_ref[...], 0.0), axis=1, keepdims=True)
    acc_ref[...] += y

    @pl.when(last)
    def _():
        y = x_ref[...] + gate_ref[0] * acc_ref[...]
        if final_norm:
            y = y * lax.rsqrt(jnp.mean(y * y, axis=-1, keepdims=True) + EPS) * gf_ref[...]
        o_ref[...] = y


def _ffn(x, g, shift, scale, gate, w13, w2, tm, tf, rows_per_group, router=None, final_g=None):
    m, d = x.shape
    n_exp, ff, _ = w2.shape
    routed = router is not None
    final_norm = final_g is not None
    if routed:
        rw, rb = router
        rw = _bf(jnp.pad(rw, ((0, 0), (0, LANES - n_exp))))
        rb = jnp.pad(rb.reshape(1, -1), ((0, 0), (0, LANES - n_exp)))
    else:
        rw = jnp.zeros((d, LANES), _bf16)
        rb = jnp.zeros((1, LANES), _f32)
    gf = final_g.reshape(1, -1) if final_norm else jnp.ones((1, d), _f32)
    nf = ff // tf
    gs = _group_spec(tm, rows_per_group, d)
    rowd = pl.BlockSpec((tm, d), lambda i, e, f: (i, 0))
    full = lambda a: pl.BlockSpec(a.shape, lambda i, e, f: (0, 0))
    return pl.pallas_call(
        functools.partial(_ffn_body, routed, final_norm, n_exp),
        grid=(m // tm, n_exp, nf),
        in_specs=[rowd, full(g), gs(shift.shape[1]), gs(scale.shape[1]), gs(gate.shape[1]), full(rw), full(rb),
                  pl.BlockSpec((1, d, tf), lambda i, e, f: (e, 0, f)),
                  pl.BlockSpec((1, d, tf), lambda i, e, f: (e, 0, f + nf)),
                  pl.BlockSpec((1, tf, d), lambda i, e, f: (e, f, 0)),
                  full(gf)],
        out_specs=rowd,
        out_shape=jax.ShapeDtypeStruct((m, d), _f32),
        scratch_shapes=[pltpu.VMEM((tm, d), _bf16), pltpu.VMEM((tm, d), _f32), pltpu.VMEM((tm, LANES), _f32)],
        compiler_params=_params("arbitrary", "arbitrary", "arbitrary"),
        name="moe_ffn" if routed else "dense_ffn",
    )(x, g, shift, scale, gate, rw, rb, w13, w13, w2, gf)


MOE_CAP = 384


def _top2_gates(logits, n_exp):
    lane = lax.broadcasted_iota(jnp.int32, logits.shape, 1).astype(_f32)
    logits = jnp.where(lane < n_exp, logits, -jnp.inf)
    v1 = jnp.max(logits, axis=1, keepdims=True)
    i1 = jnp.min(jnp.where(logits == v1, lane, float(LANES)), axis=1, keepdims=True)
    rest = jnp.where(lane == i1, -jnp.inf, logits)
    v2 = jnp.max(rest, axis=1, keepdims=True)
    i2 = jnp.min(jnp.where(rest == v2, lane, float(LANES)), axis=1, keepdims=True)
    e2 = jnp.exp(v2 - v1)
    den = 1.0 + e2
    gates = jnp.where(lane == i1, 1.0 / den, 0.0) + jnp.where(lane == i2, e2 / den, 0.0)
    routed = jnp.where((lane == i1) | (lane == i2), 1.0, 0.0)
    return gates, routed


def _moe_route_body(n_exp, x_ref, g_ref, sh_ref, sc_ref, rw_ref, rb_ref, ltri_ref,
                    xs_ref, dg_ref, slot_ref, flag_ref):
    tm = x_ref.shape[0]
    hb = _bf(_norm_mod(x_ref[...], g_ref[...], sh_ref[0], sc_ref[0]))
    gates, routed = _top2_gates(_dot(hb, rw_ref[...]) + rb_ref[...], n_exp)
    dg_ref[...] = gates
    rank = _dot(ltri_ref[...], _bf(routed))
    slot = jnp.where(routed > 0.5, rank, -1.0)
    slot_ref[...] = slot
    slot_t = slot.T
    cap = xs_ref.shape[2]
    want = lax.broadcasted_iota(jnp.int32, (LANES, tm), 0).astype(_f32)
    worst = jnp.zeros((1, 1), _f32)
    for e in range(n_exp):
        s_row = slot_t[e:e + 1, :]
        for c in range(cap // LANES):
            onehot = jnp.where(s_row == want + float(c * LANES), 1.0, 0.0)
            xs_ref[e, 0, c * LANES:(c + 1) * LANES, :] = _bf(_dot(_bf(onehot), hb))
        worst = jnp.maximum(worst, jnp.max(s_row, axis=1, keepdims=True))
    flag_ref[0] = jnp.broadcast_to(jnp.where(worst >= float(cap), 1.0, 0.0), (1, LANES))


def _moe_expert_body(x_ref, w1_ref, w3_ref, w2_ref, o_ref, acc_ref):
    f = pl.program_id(2)

    @pl.when(f == 0)
    def _():
        acc_ref[...] = jnp.zeros(acc_ref.shape, _f32)

    x = x_ref[0]
    a = _silu(_dot(x, w1_ref[0])) * _dot(x, w3_ref[0])
    acc_ref[...] += _dot(_bf(a), w2_ref[0])

    @pl.when(f == pl.num_programs(2) - 1)
    def _():
        o_ref[0] = _bf(acc_ref[...])


def _moe_combine_body(n_exp, final_norm, x_ref, gate_ref, dg_ref, slot_ref, ys_ref, gf_ref, o_ref):
    tm = x_ref.shape[0]
    cap = ys_ref.shape[2]
    lane = lax.broadcasted_iota(jnp.int32, (tm, LANES), 1)
    col = lax.broadcasted_iota(jnp.int32, (tm, cap), 1).astype(_f32)
    dg = dg_ref[...]
    slot = slot_ref[...]
    acc = jnp.zeros(x_ref.shape, _f32)
    for e in range(n_exp):
        slot_e = jnp.sum(jnp.where(lane == e, slot, 0.0), axis=1, keepdims=True)
        gate_e = jnp.sum(jnp.where(lane == e, dg, 0.0), axis=1, keepdims=True)
        scatter = _bf(jnp.where(slot_e == col, 1.0, 0.0))
        acc = acc + gate_e * _dot(scatter, ys_ref[e, 0])
    y = x_ref[...] + gate_ref[0] * acc
    if final_norm:
        y = y * lax.rsqrt(jnp.mean(y * y, axis=-1, keepdims=True) + EPS) * gf_ref[...]
    o_ref[...] = y


def _moe_sparse(x, g, shift, scale, gate, w13, w2, tm, tf, rows_per_group, router, final_g):
    m, d = x.shape
    n_exp, ff, _ = w2.shape
    rw, rb = router
    rw = _bf(jnp.pad(rw, ((0, 0), (0, LANES - n_exp))))
    rb = jnp.pad(rb.reshape(1, -1), ((0, 0), (0, LANES - n_exp)))
    final_norm = final_g is not None
    gf = final_g.reshape(1, -1) if final_norm else jnp.ones((1, d), _f32)
    tiles = m // tm
    cap = MOE_CAP
    ltri = (jnp.arange(tm, dtype=jnp.int32)[:, None] > jnp.arange(tm, dtype=jnp.int32)[None, :]).astype(_bf16)
    gs = _group_spec(tm, rows_per_group, d)
    rowd = pl.BlockSpec((tm, d), lambda i: (i, 0))
    rowl = pl.BlockSpec((tm, LANES), lambda i: (i, 0))
    full = lambda a: pl.BlockSpec(a.shape, lambda i: (0,) * a.ndim)
    xs_spec = pl.BlockSpec((n_exp, 1, cap, d), lambda i: (0, i, 0, 0))
    xs, dg, slot, flags = pl.pallas_call(
        functools.partial(_moe_route_body, n_exp),
        grid=(tiles,),
        in_specs=[rowd, full(g), gs(shift.shape[1]), gs(scale.shape[1]), full(rw), full(rb), full(ltri)],
        out_specs=[xs_spec, rowl, rowl, pl.BlockSpec((1, 1, LANES), lambda i: (i, 0, 0))],
        out_shape=[jax.ShapeDtypeStruct((n_exp, tiles, cap, d), _bf16), jax.ShapeDtypeStruct((m, LANES), _f32),
                   jax.ShapeDtypeStruct((m, LANES), _f32), jax.ShapeDtypeStruct((tiles, 1, LANES), _f32)],
        compiler_params=_params("arbitrary"),
        name="moe_route",
    )(x, g, shift, scale, rw, rb, ltri)

    rows = tiles * cap
    tr = _tile(rows, 1536)
    nf = ff // tf

    def dense_path():
        return _ffn(x, g, shift, scale, gate, w13, w2, tm, tf, rows_per_group, router=router, final_g=final_g)

    def sparse_path():
        ys = pl.pallas_call(
            _moe_expert_body,
            grid=(n_exp, rows // tr, nf),
            in_specs=[pl.BlockSpec((1, tr, d), lambda e, r, f: (e, r, 0)),
                      pl.BlockSpec((1, d, tf), lambda e, r, f: (e, 0, f)),
                      pl.BlockSpec((1, d, tf), lambda e, r, f: (e, 0, f + nf)),
                      pl.BlockSpec((1, tf, d), lambda e, r, f: (e, f, 0))],
            out_specs=pl.BlockSpec((1, tr, d), lambda e, r, f: (e, r, 0)),
            out_shape=jax.ShapeDtypeStruct((n_exp, rows, d), _bf16),
            scratch_shapes=[pltpu.VMEM((tr, d), _f32)],
            compiler_params=_params("arbitrary", "arbitrary", "arbitrary"),
            name="moe_experts",
        )(xs.reshape(n_exp, rows, d), w13, w13, w2)
        return pl.pallas_call(
            functools.partial(_moe_combine_body, n_exp, final_norm),
            grid=(tiles,),
            in_specs=[rowd, gs(gate.shape[1]), rowl, rowl, xs_spec, full(gf)],
            out_specs=rowd,
            out_shape=jax.ShapeDtypeStruct((m, d), _f32),
            compiler_params=_params("arbitrary"),
            name="moe_combine",
        )(x, gate, dg, slot, ys.reshape(n_exp, tiles, cap, d), gf)

    return lax.cond(jnp.max(flags) > 0.5, dense_path, sparse_path)


def _cmul(ar, ai, br, bi):
    return ar * br - ai * bi, ar * bi + ai * br


def _s5_io(u, b_ref, c_ref, d_ref, are_ref, aim_ref, h_of_bb):
    half = b_ref.shape[2] // 2
    bu = _dot(_bf(u), b_ref[0])
    bb_re, bb_im = _cmul(are_ref[0, 1:2], aim_ref[0, 1:2], bu[:, :half], bu[:, half:])
    h_re, h_im = h_of_bb(bb_re, bb_im)
    y = _dot(_bf(jnp.concatenate([h_re, h_im], axis=1)), c_ref[0]) + d_ref[0] * u
    return _bf(_gelu_tanh(y)), h_re, h_im


def _s5_scan_body(tc, u_ref, b_ref, c_ref, d_ref, are_ref, aim_ref, h0r_ref, h0i_ref,
                  y_ref, hr_ref, hi_ref, xr_ref, xi_ref):
    t_blk = pl.program_id(2)

    @pl.when(t_blk == 0)
    def _():
        hr_ref[0, 0] = h0r_ref[0, 0]
        hi_ref[0, 0] = h0i_ref[0, 0]

    width = are_ref.shape[2]
    a1 = (jnp.broadcast_to(are_ref[0, 0:1], (SUBLANES, width)), jnp.broadcast_to(aim_ref[0, 0:1], (SUBLANES, width)))
    a2 = _cmul(*a1, *a1)
    a3 = _cmul(*a2, *a1)
    a4 = _cmul(*a2, *a2)
    a5 = _cmul(*a4, *a1)
    a6 = _cmul(*a4, *a2)
    a7 = _cmul(*a4, *a3)
    a8 = _cmul(*a4, *a4)
    row = lax.broadcasted_iota(jnp.int32, (SUBLANES, width), 0)
    lvl = [tuple(jnp.where(row >= s, c, 0.0) for c in a) for s, a in ((1, a1), (2, a2), (4, a4))]
    pw = []
    for comp in range(2):
        acc = a8[comp]
        for s, a in ((6, a7), (5, a6), (4, a5), (3, a4), (2, a3), (1, a2), (0, a1)):
            acc = jnp.where(row == s, a[comp], acc)
        pw.append(acc)

    def scan_chunk(bb_re, bb_im):
        xr_ref[...] = bb_re
        xi_ref[...] = bb_im

        def tile(j, carry):
            cr, ci = carry
            r0 = pl.multiple_of(j * SUBLANES, SUBLANES)
            xr = xr_ref[pl.ds(r0, SUBLANES), :]
            xi = xi_ref[pl.ds(r0, SUBLANES), :]
            for (s, (fr, fi)) in zip((1, 2, 4), lvl):
                sr = pltpu.roll(xr, s, axis=0)
                si = pltpu.roll(xi, s, axis=0)
                pr, pi = _cmul(fr, fi, sr, si)
                xr = xr + pr
                xi = xi + pi
            pr, pi = _cmul(pw[0], pw[1], jnp.broadcast_to(cr, xr.shape), jnp.broadcast_to(ci, xi.shape))
            xr = xr + pr
            xi = xi + pi
            xr_ref[pl.ds(r0, SUBLANES), :] = xr
            xi_ref[pl.ds(r0, SUBLANES), :] = xi
            return xr[SUBLANES - 1:SUBLANES], xi[SUBLANES - 1:SUBLANES]

        cr, ci = lax.fori_loop(0, tc // SUBLANES, tile, (hr_ref[0, 0], hi_ref[0, 0]))
        hr_ref[0, 0] = cr
        hi_ref[0, 0] = ci
        return xr_ref[...], xi_ref[...]

    y, _, _ = _s5_io(u_ref[0], b_ref, c_ref, d_ref, are_ref, aim_ref, scan_chunk)
    y_ref[0] = y


def _s5_step_body(u_ref, b_ref, c_ref, d_ref, are_ref, aim_ref, h0r_ref, h0i_ref, y_ref, hr_ref, hi_ref):
    def one_step(bb_re, bb_im):
        pr, pi = _cmul(are_ref[0, 0:1], aim_ref[0, 0:1], h0r_ref[0], h0i_ref[0])
        return pr + bb_re, pi + bb_im

    y, h_re, h_im = _s5_io(u_ref[...], b_ref, c_ref, d_ref, are_ref, aim_ref, one_step)
    y_ref[...] = y
    hr_ref[0] = h_re
    hi_ref[0] = h_im


def _s5_tables(a_re, a_im, log_step, b_re, b_im, c_re, c_im, d):
    g_c, p_c, ch = b_re.shape
    gs = LANES // ch
    ns = g_c // gs
    dt = jnp.exp(log_step.astype(_f32))[:, None]
    mag = jnp.exp(dt * a_re)
    ab_re = mag * jnp.cos(dt * a_im)
    ab_im = mag * jnp.sin(dt * a_im)
    den = a_re * a_re + a_im * a_im
    nr = ab_re - 1.0
    co_re = (nr * a_re + ab_im * a_im) / den
    co_im = (ab_im * a_re - nr * a_im) / den
    eye = jnp.eye(gs, dtype=_f32)

    def pack_b(b):
        b = b.reshape(ns, gs, p_c, ch)
        return jnp.einsum('sgpc,gh->sgchp', b, eye).reshape(ns, gs * ch, gs * p_c)

    def pack_c(c):
        c = c.reshape(ns, gs, ch, p_c)
        return jnp.einsum('sgcp,gh->sgphc', c, eye).reshape(ns, gs * p_c, gs * ch)

    b_pack = _bf(jnp.concatenate([pack_b(b_re), pack_b(b_im)], axis=2))
    c_pack = _bf(jnp.concatenate([pack_c(c_re), -pack_c(c_im)], axis=1))
    vec = lambda a, c: jnp.stack([a.reshape(ns, gs * p_c), c.reshape(ns, gs * p_c)], axis=1)
    return b_pack, c_pack, d.reshape(ns, 1, gs * ch), vec(ab_re, co_re), vec(ab_im, co_im)


def _s5_scan(u, h0_re, h0_im, tables, tc):
    n, t, d = u.shape
    b_pack, c_pack, dvec, are, aim = tables
    ns, _, w2 = b_pack.shape
    w = w2 // 2
    h0r = h0_re.reshape(n, ns, 1, w)
    h0i = h0_im.reshape(n, ns, 1, w)
    tab = lambda a: pl.BlockSpec((1,) + a.shape[1:], lambda b, s, j: (s, 0, 0))
    st = pl.BlockSpec((1, 1, 1, w), lambda b, s, j: (b, s, 0, 0))
    seq = pl.BlockSpec((1, tc, LANES), lambda b, s, j: (b, j, s))
    y, hr, hi = pl.pallas_call(
        functools.partial(_s5_scan_body, tc),
        grid=(n, ns, t // tc),
        in_specs=[seq, tab(b_pack), tab(c_pack), tab(dvec), tab(are), tab(aim), st, st],
        out_specs=[seq, st, st],
        out_shape=[jax.ShapeDtypeStruct((n, t, d), _bf16),
                   jax.ShapeDtypeStruct((n, ns, 1, w), _f32), jax.ShapeDtypeStruct((n, ns, 1, w), _f32)],
        scratch_shapes=[pltpu.VMEM((tc, w), _f32), pltpu.VMEM((tc, w), _f32)],
        compiler_params=_params("arbitrary", "arbitrary", "arbitrary"),
        name="s5_scan",
    )(u, b_pack, c_pack, dvec, are, aim, h0r, h0i)
    return y, hr.reshape(n, ns * w), hi.reshape(n, ns * w)


def _s5_step(u, h0_re, h0_im, tables):
    b, d = u.shape
    b_pack, c_pack, dvec, are, aim = tables
    ns, _, w2 = b_pack.shape
    w = w2 // 2
    h0r = h0_re.reshape(b, ns, w).transpose(1, 0, 2)
    h0i = h0_im.reshape(b, ns, w).transpose(1, 0, 2)
    tab = lambda a: pl.BlockSpec((1,) + a.shape[1:], lambda s: (s, 0, 0))
    st = pl.BlockSpec((1, b, w), lambda s: (s, 0, 0))
    col = pl.BlockSpec((b, LANES), lambda s: (0, s))
    y, hr, hi = pl.pallas_call(
        _s5_step_body,
        grid=(ns,),
        in_specs=[col, tab(b_pack), tab(c_pack), tab(dvec), tab(are), tab(aim), st, st],
        out_specs=[col, st, st],
        out_shape=[jax.ShapeDtypeStruct((b, d), _bf16),
                   jax.ShapeDtypeStruct((ns, b, w), _f32), jax.ShapeDtypeStruct((ns, b, w), _f32)],
        compiler_params=_params("arbitrary"),
        name="s5_step",
    )(u, b_pack, c_pack, dvec, are, aim, h0r, h0i)
    return y, hr.transpose(1, 0, 2).reshape(b, ns * w), hi.transpose(1, 0, 2).reshape(b, ns * w)


def _glu_out_body(y_ref, x_ref, gate_ref, wl_ref, wr_ref, o_ref):
    y = y_ref[...]
    z = _dot(y, wl_ref[...]) * jax.nn.sigmoid(_dot(y, wr_ref[...]))
    o_ref[...] = x_ref[...] + gate_ref[0] * z


def _glu_out(yg, x, gate, w_glu, tm, rows_per_group):
    m, d = x.shape
    wl = _bf(w_glu[:, :d])
    wr = _bf(w_glu[:, d:])
    gs = _group_spec(tm, rows_per_group, d)
    rowd = pl.BlockSpec((tm, d), lambda i: (i, 0))
    full = lambda a: pl.BlockSpec(a.shape, lambda i: (0, 0))
    return pl.pallas_call(
        _glu_out_body,
        grid=(m // tm,),
        in_specs=[rowd, rowd, gs(gate.shape[1]), full(wl), full(wr)],
        out_specs=rowd,
        out_shape=jax.ShapeDtypeStruct((m, d), _f32),
        compiler_params=_params("arbitrary"),
        name="glu_out",
    )(yg, x, gate, wl, wr)


def _tile(m, target):
    if m <= target:
        return m
    t = target
    while m % t:
        t -= SUBLANES
    return t


def _run_group(x, mods, is_prompt, st, wts):
    n, t, d = x.shape
    m = n * t
    depth = wts["norm_mix"].shape[0]
    d_a = d // 2
    d_b = d // 2
    n_heads = d_a // HEAD_DIM
    rows_per_group = t if is_prompt else m
    tm = _tile(t, 256) if is_prompt else m
    xf = x.reshape(m, d).astype(_f32)

    def mod_vecs(l):
        parts = jnp.split(mods[l], 6, axis=-1)
        if is_prompt:
            return [p.reshape(n, 1, d) for p in parts]
        return [p.reshape(1, m, d) for p in parts]

    outs = {k: [] for k in ("shift", "wkv", "k", "v", "kidx", "re", "im")}
    for l in range(depth):
        sh_m, sc_m, g_m, sh_f, sc_f, g_f = mod_vecs(l)
        i = l // 2
        last = l == depth - 1
        final_g = wts["norm_final"] if last else None
        norm_mix = wts["norm_mix"][l].reshape(1, d)
        norm_ffn = wts["norm_ffn"][l].reshape(1, d)
        if l % 2 == 0:
            p_a, q, k, v, qi, ki, wi = _in_proj(xf, norm_mix, sh_m, sc_m, wts["e_w_in"][i], tm, rows_per_group)
            if is_prompt:
                prev0 = st["shift"][i].reshape(n, 1, -1)
            else:
                prev0 = st["shift"][i].reshape(1, m, -1)
            r, w, k_mod, v_a, av, bv, g, bonus = _rwkv_prep(
                p_a, prev0.astype(_f32), wts["e_mu"][i], wts["e_w0"][i], wts["e_w2"][i], wts["e_a0"][i],
                wts["e_a2"][i], wts["e_g2"][i], wts["e_k_k"][i], wts["e_k_a"][i], wts["e_r_k"][i],
                tm, rows_per_group)
            seq = lambda a: a.reshape(n, t, d_a)
            nb = n if is_prompt else _tile(n, 4)
            tc = _tile(t, 256)
            o, s_fin = _rwkv_scan(seq(r), seq(w), seq(k_mod), seq(v_a), seq(av), seq(bv),
                                  _pack_state(st["wkv"][i].astype(_f32)), nb, tc)
            if is_prompt:
                o_b = _dsa_prompt(q.reshape(n, t, d_b), k.reshape(n, t, d_b), v.reshape(n, t, d_b),
                                  qi.reshape(n, t, -1), ki.reshape(n, t, -1), wi.reshape(n, t, -1),
                                  wts["rel_bias"], 128)
            else:
                o_b = _dsa_sample(q.reshape(n, t, d_b), k.reshape(n, t, d_b), v.reshape(n, t, d_b),
                                  qi.reshape(n, t, -1), ki.reshape(n, t, -1), wi.reshape(n, t, -1),
                                  st["cache_k"], st["cache_v"], st["cache_kidx"], i,
                                  st["page_table"], wts["rel_bias"])
            xf = _mix_out(o.reshape(m, d_a), bonus, g, o_b.reshape(m, d_b), xf, g_m,
                          wts["e_lnx_w"][i], wts["e_lnx_b"][i], wts["e_w_out"][i], tm, rows_per_group)
            outs["shift"].append(p_a.reshape(n, t, -1)[:, -1].astype(x.dtype))
            outs["wkv"].append(_unpack_state(s_fin))
            outs["k"].append(k.reshape(n, t, n_heads, HEAD_DIM))
            outs["v"].append(v.reshape(n, t, n_heads, HEAD_DIM))
            outs["kidx"].append(ki.reshape(n, t, D_IDX))
            tm_f = _tile(t, 1024) if is_prompt else m
            ff = wts["ffn_w2"][i].shape[0]
            xf = _ffn(xf, norm_ffn, sh_f, sc_f, g_f, wts["ffn_w13"][i][None], wts["ffn_w2"][i][None],
                      tm_f, _tile(ff, 256) if ff % 256 == 0 else ff, rows_per_group, final_g=final_g)
        else:
            tables = _s5_tables(wts["o_a_re"][i], wts["o_a_im"][i], wts["o_log_step"][i], wts["o_b_re"][i],
                                wts["o_b_im"][i], wts["o_c_re"][i], wts["o_c_im"][i], wts["o_d"][i])
            u = _norm_mod_call(xf, norm_mix, sh_m, sc_m, tm, rows_per_group)
            h0r = st["ssm_re"][i].reshape(n, -1).astype(_f32)
            h0i = st["ssm_im"][i].reshape(n, -1).astype(_f32)
            if is_prompt:
                yg, hr, hi = _s5_scan(u.reshape(n, t, d), h0r, h0i, tables, _tile(t, 256))
            else:
                yg, hr, hi = _s5_step(u, h0r, h0i, tables)
            xf = _glu_out(yg.reshape(m, d), xf, g_m, wts["o_w_glu"][i], tm, rows_per_group)
            g_c = d // CH_G
            outs["re"].append(hr.reshape(n, g_c, P_C))
            outs["im"].append(hi.reshape(n, g_c, P_C))
            tm_f = _tile(t, 1024) if is_prompt else m
            ff = wts["moe_w2"][i].shape[1]
            moe = _moe_sparse if (is_prompt and tm_f == 1024) else _ffn
            xf = moe(xf, norm_ffn, sh_f, sc_f, g_f, wts["moe_w13"][i], wts["moe_w2"][i],
                     tm_f, _tile(ff, 512) if ff % 512 == 0 else ff, rows_per_group,
                     router=(wts["o_router_w"][i], wts["o_router_b"][i]), final_g=final_g)
    y = xf.reshape(n, t, d).astype(x.dtype)
    return (y, jnp.stack(outs["shift"]), jnp.stack(outs["wkv"]), jnp.stack(outs["k"]), jnp.stack(outs["v"]),
            jnp.stack(outs["kidx"]), jnp.stack(outs["re"]), jnp.stack(outs["im"]))


def kernel(x_prompt, x_sample, cache_k, cache_v, cache_kidx, state_shift, state_wkv, state_ssm_re, state_ssm_im, page_table, c_prompt, c_sample, norm_mix, norm_ffn, ada_w, ada_b, rel_bias, norm_final, e_w_in, e_mu, e_w0, e_w2, e_a0, e_a2, e_g2, e_k_k, e_k_a, e_r_k, e_lnx_w, e_lnx_b, e_w_out, e_ffn_w13, e_ffn_w2, o_a_re, o_a_im, o_log_step, o_b_re, o_b_im, o_c_re, o_c_im, o_d, o_w_glu, o_router_w, o_router_b, o_moe_w13, o_moe_w2):
    nb, _, d = x_prompt.shape
    n_dec = x_sample.shape[0]
    n_even = e_w_in.shape[0]
    n_odd = o_a_re.shape[0]
    d_a = d // 2
    n_heads = d_a // HEAD_DIM
    shift_w = state_shift.shape[-1]

    c_all = jnp.concatenate([c_prompt, c_sample], axis=0)
    pad = (-c_all.shape[0]) % SUBLANES
    mods = _ada(jnp.pad(c_all, ((0, pad), (0, 0))), ada_w, ada_b)
    mods_p = mods[:, :nb]
    mods_s = mods[:, nb:nb + n_dec]

    wts = dict(norm_mix=norm_mix, norm_ffn=norm_ffn, norm_final=norm_final, rel_bias=rel_bias,
               e_w_in=e_w_in, e_mu=e_mu, e_w0=e_w0, e_w2=e_w2, e_a0=e_a0, e_a2=e_a2, e_g2=e_g2, e_k_k=e_k_k,
               e_k_a=e_k_a, e_r_k=e_r_k, e_lnx_w=e_lnx_w, e_lnx_b=e_lnx_b, e_w_out=e_w_out,
               ffn_w13=_bf(e_ffn_w13), ffn_w2=_bf(e_ffn_w2),
               o_a_re=o_a_re, o_a_im=o_a_im, o_log_step=o_log_step, o_b_re=o_b_re, o_b_im=o_b_im,
               o_c_re=o_c_re, o_c_im=o_c_im, o_d=o_d, o_w_glu=o_w_glu, o_router_w=o_router_w,
               o_router_b=o_router_b, moe_w13=_bf(o_moe_w13), moe_w2=_bf(o_moe_w2))

    st_p = dict(shift=jnp.zeros((n_even, nb, shift_w), x_prompt.dtype),
                wkv=jnp.zeros((n_even, nb, n_heads, HEAD_DIM, HEAD_DIM), _f32),
                ssm_re=jnp.zeros((n_odd, nb, d // CH_G, P_C), _f32),
                ssm_im=jnp.zeros((n_odd, nb, d // CH_G, P_C), _f32))
    st_s = dict(shift=state_shift, wkv=state_wkv, ssm_re=state_ssm_re, ssm_im=state_ssm_im,
                cache_k=cache_k, cache_v=cache_v, cache_kidx=cache_kidx, page_table=page_table)
    out_p = _run_group(x_prompt, mods_p, True, st_p, wts)
    out_s = _run_group(x_sample, mods_s, False, st_s, wts)
    return (out_p[0], out_s[0]) + out_p[1:] + out_s[1:]
```

```python
import functools
import math

import jax
import jax.numpy as jnp
from jax import lax
from jax.experimental import pallas as pl
from jax.experimental.pallas import tpu as pltpu

HEAD_DIM = 64
LORA_W = 64
LORA_A = 64
LORA_G = 128
H_IDX = 8
D_IDX = 64
TOPK_MAX = 256
N_BUCKETS = 32
MAX_DIST = 128
CH_G = 16
P_C = 64
TOP_E = 2
EPS = 1e-6
LNX_EPS = 64e-5

LANES = 128
SUBLANES = 8
VMEM_LIMIT = 56 * 1024 * 1024
KEY_BLOCK = 256
COUNT_BLOCK = 1024
COUNT_ROWS = 64
NEG = -2.0e30
M_INIT = -1.0e30

_bf16 = jnp.bfloat16
_f32 = jnp.float32


def _bf(x):
    return x.astype(_bf16)


def _dot(a, b):
    return jnp.dot(a, b, preferred_element_type=_f32)


def _dot_nt(a, b):
    return lax.dot_general(a, b, (((1,), (1,)), ((), ())), preferred_element_type=_f32)


def _params(*sem):
    return pltpu.CompilerParams(dimension_semantics=sem, vmem_limit_bytes=VMEM_LIMIT)


def _split_bf16(x, parts):
    out = []
    for _ in range(parts - 1):
        hi = _bf(x)
        out.append(hi)
        x = x - hi.astype(_f32)
    out.append(_bf(x))
    return out


def _segsum(x, ones_bd):
    hi, mid, lo = _split_bf16(x, 3)
    return _dot(hi, ones_bd) + _dot(mid, ones_bd) + _dot(lo, ones_bd)


def _norm_mod(x, g, shift, scale):
    ms = jnp.mean(x * x, axis=-1, keepdims=True)
    return (x * lax.rsqrt(ms + EPS) * g) * (1.0 + scale) + shift


def _silu(x):
    return x * jax.nn.sigmoid(x)


def _gelu_tanh(x):
    return 0.5 * x * (1.0 + jnp.tanh(math.sqrt(2.0 / math.pi) * (x + 0.044715 * (x * x * x))))


def _softplus(x):
    return jnp.maximum(x, 0.0) + jnp.log(1.0 + jnp.exp(-jnp.abs(x)))


def _rel_bucket(dist):
    max_exact = N_BUCKETS // 2
    n = jnp.maximum(dist, 0)
    nf = jnp.maximum(n, 1).astype(_f32)
    large = max_exact + (jnp.log(nf / max_exact) / math.log(MAX_DIST / max_exact) * (N_BUCKETS - max_exact)).astype(jnp.int32)
    return jnp.where(n < max_exact, n, jnp.minimum(large, N_BUCKETS - 1))


def _group_spec(rows_per_block, rows_per_group, width):
    def spec(r):
        return pl.BlockSpec((1, r, width), lambda i, *_: ((i * rows_per_block) // rows_per_group, 0, 0))
    return spec


def _ada_body(c_ref, w_ref, b_ref, o_ref):
    o_ref[0] = _dot(_bf(_silu(c_ref[...])), _bf(w_ref[0])) + b_ref[0]


def _ada(c, ada_w, ada_b):
    depth, d, n6 = ada_w.shape
    rows = c.shape[0]
    tn = n6 // 4
    return pl.pallas_call(
        _ada_body,
        grid=(depth, n6 // tn),
        in_specs=[pl.BlockSpec((rows, d), lambda l, j: (0, 0)),
                  pl.BlockSpec((1, d, tn), lambda l, j: (l, 0, j)),
                  pl.BlockSpec((1, 1, tn), lambda l, j: (l, 0, j))],
        out_specs=pl.BlockSpec((1, rows, tn), lambda l, j: (l, 0, j)),
        out_shape=jax.ShapeDtypeStruct((depth, rows, n6), _f32),
        compiler_params=_params("arbitrary", "arbitrary"),
        name="ada_mod",
    )(c, ada_w, ada_b.reshape(depth, 1, n6))


def _norm_mod_body(x_ref, g_ref, sh_ref, sc_ref, o_ref):
    o_ref[...] = _norm_mod(x_ref[...], g_ref[...], sh_ref[0], sc_ref[0])


def _norm_mod_call(x, g, shift, scale, tm, rows_per_group):
    m, d = x.shape
    gs = _group_spec(tm, rows_per_group, d)
    return pl.pallas_call(
        _norm_mod_body,
        grid=(m // tm,),
        in_specs=[pl.BlockSpec((tm, d), lambda i: (i, 0)),
                  pl.BlockSpec((1, d), lambda i: (0, 0)),
                  gs(shift.shape[1]), gs(scale.shape[1])],
        out_specs=pl.BlockSpec((tm, d), lambda i: (i, 0)),
        out_shape=jax.ShapeDtypeStruct((m, d), _f32),
        compiler_params=_params("arbitrary"),
        name="norm_mod",
    )(x, g, shift, scale)


def _in_proj_body(x_ref, g_ref, sh_ref, sc_ref, wa_ref, wq_ref, wk_ref, wv_ref, wqi_ref, wkw_ref,
                  pa_ref, q_ref, k_ref, v_ref, qi_ref, ki_ref, wi_ref):
    h = _bf(_norm_mod(x_ref[...], g_ref[...], sh_ref[0], sc_ref[0]))
    pa_ref[...] = _dot(h, wa_ref[...])
    q_ref[...] = _bf(_dot(h, wq_ref[...]))
    k_ref[...] = _dot(h, wk_ref[...])
    v_ref[...] = _dot(h, wv_ref[...])
    qi_ref[...] = _bf(_dot(h, wqi_ref[...]))
    kw = _dot(h, wkw_ref[...])
    ki_ref[...] = kw[:, :D_IDX]
    wi_ref[...] = kw[:, D_IDX:D_IDX + H_IDX]


def _in_proj(x, g, shift, scale, w_in, tm, rows_per_group):
    m, d = x.shape
    d_a = d // 2
    d_b = d // 2
    shift_w = 3 * d_a + LORA_W + LORA_A + LORA_G
    c1 = shift_w + 3 * d_b
    c2 = c1 + H_IDX * D_IDX
    wa = _bf(w_in[:, :shift_w])
    wq = _bf(w_in[:, shift_w:shift_w + d_b])
    wk = _bf(w_in[:, shift_w + d_b:shift_w + 2 * d_b])
    wv = _bf(w_in[:, shift_w + 2 * d_b:c1])
    wqi = _bf(w_in[:, c1:c2])
    wkw = _bf(jnp.pad(w_in[:, c2:], ((0, 0), (0, LANES - D_IDX - H_IDX))))
    gs = _group_spec(tm, rows_per_group, d)
    full = lambda a: pl.BlockSpec(a.shape, lambda i: (0, 0))
    row = lambda w: pl.BlockSpec((tm, w), lambda i: (i, 0))
    widths = [(shift_w, _f32), (d_b, _bf16), (d_b, _f32), (d_b, _f32), (H_IDX * D_IDX, _bf16),
              (D_IDX, _f32), (H_IDX, _f32)]
    return pl.pallas_call(
        _in_proj_body,
        grid=(m // tm,),
        in_specs=[row(d), pl.BlockSpec((1, d), lambda i: (0, 0)), gs(shift.shape[1]), gs(scale.shape[1]),
                  full(wa), full(wq), full(wk), full(wv), full(wqi), full(wkw)],
        out_specs=[row(w) for w, _ in widths],
        out_shape=[jax.ShapeDtypeStruct((m, w), dt) for w, dt in widths],
        compiler_params=_params("arbitrary"),
        name="in_proj",
    )(x, g, shift, scale, wa, wq, wk, wv, wqi, wkw)


def _rwkv_prep_body(seq_is_one, rows_per_group, tm,
                    p_ref, pprev_ref, prev0_ref, mu_ref, w0_ref, a0_ref, kk_ref, ka_ref, rk_ref,
                    wwa_ref, g2_ref, ones_ref,
                    r_ref, w_ref, k_ref, v_ref, av_ref, bv_ref, g_ref, bonus_ref):
    d_a = r_ref.shape[1]
    p = p_ref[...]
    if seq_is_one:
        p_prev = prev0_ref[0]
    else:
        i = pl.program_id(0)
        first = (i * tm) % rows_per_group == 0
        prev_row = jnp.where(first, prev0_ref[0], pprev_ref[SUBLANES - 1:SUBLANES, :])
        rolled = pltpu.roll(p, 1, axis=0)
        row_id = lax.broadcasted_iota(jnp.int32, p.shape, 0)
        p_prev = jnp.where(row_id == 0, prev_row, rolled)
    ps = p + (p_prev - p) * mu_ref[...]
    r = ps[:, :d_a]
    k = ps[:, d_a:2 * d_a]
    v = ps[:, 2 * d_a:3 * d_a]
    xwa = ps[:, 3 * d_a:3 * d_a + LORA_W + LORA_A]
    xg = ps[:, 3 * d_a + LORA_W + LORA_A:]
    lane = lax.broadcasted_iota(jnp.int32, xwa.shape, 1)
    xwa = jnp.where(lane < LORA_W, jnp.tanh(xwa), xwa)
    lwa = _dot(_bf(xwa), wwa_ref[...])
    w_log = -_softplus(-(w0_ref[...] + lwa[:, :d_a])) - 0.5
    decay = jnp.exp(-jnp.exp(w_log))
    a = jax.nn.sigmoid(a0_ref[...] + lwa[:, d_a:])
    g_ref[...] = _dot(_bf(jax.nn.sigmoid(xg)), g2_ref[...])
    ones_bd = ones_ref[...]
    kk = k * kk_ref[...]
    kk = kk / jnp.maximum(jnp.sqrt(_segsum(kk * kk, ones_bd)), 1e-12)
    k_mod = k * (1.0 + (a - 1.0) * ka_ref[...])
    r_ref[...] = r
    w_ref[...] = decay
    k_ref[...] = k_mod
    v_ref[...] = v
    av_ref[...] = -kk
    bv_ref[...] = kk * a
    bonus_ref[...] = _segsum(r * k_mod * rk_ref[...], ones_bd) * v


def _ones_blockdiag(n, group):
    idx = jnp.arange(n, dtype=jnp.int32) // group
    return (idx[:, None] == idx[None, :]).astype(_bf16)


def _rwkv_prep(p_a, prev0, mu, w0, w2, a0, a2, g2, k_k, k_a, r_k, tm, rows_per_group):
    m, shift_w = p_a.shape
    d_a = w0.shape[-1]
    seq_is_one = rows_per_group == tm and prev0.shape[1] == tm
    wwa = jnp.zeros((LORA_W + LORA_A, 2 * d_a), _f32)
    wwa = _bf(wwa.at[:LORA_W, :d_a].set(w2).at[LORA_W:, d_a:].set(a2))
    ones_bd = _ones_blockdiag(d_a, HEAD_DIM)
    vec = lambda a: a.reshape(1, -1)
    gs = _group_spec(tm, rows_per_group, shift_w)
    full = lambda a: pl.BlockSpec(a.shape, lambda i: (0, 0))
    row = pl.BlockSpec((tm, d_a), lambda i: (i, 0))
    ins = [p_a, p_a, prev0, vec(mu), vec(w0), vec(a0), vec(k_k), vec(k_a), vec(r_k), wwa, _bf(g2), ones_bd]
    in_specs = [pl.BlockSpec((tm, shift_w), lambda i: (i, 0)),
                pl.BlockSpec((SUBLANES, shift_w), lambda i: (jnp.maximum(i * (tm // SUBLANES) - 1, 0), 0)),
                gs(prev0.shape[1])] + [full(a) for a in ins[3:]]
    return pl.pallas_call(
        functools.partial(_rwkv_prep_body, seq_is_one, rows_per_group, tm),
        grid=(m // tm,),
        in_specs=in_specs,
        out_specs=[row] * 8,
        out_shape=[jax.ShapeDtypeStruct((m, d_a), _f32)] * 8,
        compiler_params=_params("arbitrary"),
        name="rwkv_prep",
    )(*ins)


def _rwkv_scan_body(nb, n_pairs, tc,
                    r_ref, w_ref, k_ref, v_ref, av_ref, bv_ref, s0_ref, ones_ref, expand_ref, eye_ref,
                    o_ref, s_ref):
    t_blk = pl.program_id(1)

    @pl.when(t_blk == 0)
    def _():
        s_ref[...] = s0_ref[...]

    ones_k = ones_ref[...]
    parts = ones_k.shape[0] // LANES
    expand = expand_ref[...]
    eye = eye_ref[...] > 0.5
    units = [(n, hp) for n in range(nb) for hp in range(n_pairs)]

    def rowsum(xs, ones, parts):
        lhs = jnp.concatenate([jnp.concatenate(_split_bf16(x, parts), axis=1) for x in xs], axis=0)
        res = _dot(lhs, ones)
        return [res[u * HEAD_DIM:(u + 1) * HEAD_DIM] for u in range(len(xs))]

    sub = min(SUBLANES, tc)
    row_id = lax.broadcasted_iota(jnp.int32, (sub, LANES), 0)

    def tile_steps(j, carry):
        t0 = pl.multiple_of(j * sub, sub)
        tiles = {}
        for (n, hp) in units:
            sl = pl.ds(hp * LANES, LANES)
            tiles[(n, hp)] = tuple(ref[n, pl.ds(t0, sub), sl] for ref in (r_ref, w_ref, k_ref, v_ref, av_ref, bv_ref))
        states = [s_ref[n, hp] for (n, hp) in units]
        o_tiles = [jnp.zeros((sub, LANES), _f32) for _ in units]
        vcols = []
        for u in units:
            vt = tiles[u][3]
            by_head = jnp.concatenate([vt[:, :HEAD_DIM], vt[:, HEAD_DIM:]], axis=0)
            lhs = jnp.concatenate(_split_bf16(by_head, 3), axis=0)
            vcols.append(lax.dot_general(lhs, expand, (((0,), (0,)), ((), ())), preferred_element_type=_f32))
        for tt in range(sub):
            rows = {u: tuple(x[tt:tt + 1, :] for x in tiles[u]) for u in units}
            sa = rowsum([s * rows[u][4] for s, u in zip(states, units)], ones_k, parts)
            for idx, u in enumerate(units):
                r_t, w_t, k_t, v_t, av_t, bv_t = rows[u]
                vcol = vcols[idx][:, tt * LANES:(tt + 1) * LANES]
                states[idx] = states[idx] * w_t + sa[idx] * bv_t + vcol * k_t
            ob = rowsum([s * rows[u][0] for s, u in zip(states, units)], ones_k, parts)
            for idx in range(len(units)):
                o_row = jnp.sum(jnp.where(eye, ob[idx], 0.0), axis=0, keepdims=True)
                o_tiles[idx] = jnp.where(row_id == tt, jnp.broadcast_to(o_row, (sub, LANES)), o_tiles[idx])
        for idx, (n, hp) in enumerate(units):
            s_ref[n, hp] = states[idx]
            o_ref[n, pl.ds(t0, sub), pl.ds(hp * LANES, LANES)] = o_tiles[idx]
        return carry

    lax.fori_loop(0, tc // sub, tile_steps, 0)


def _rwkv_scan(r, w, k, v, av, bv, s0, nb, tc):
    n, t, d_a = r.shape
    n_pairs = d_a // LANES
    half = (jnp.arange(LANES, dtype=jnp.int32) // HEAD_DIM)
    ones_blk = (half[:, None] == half[None, :]).astype(_bf16)
    ones_k = jnp.concatenate([ones_blk] * (3 if t == 1 else 2), axis=0)
    sub = min(SUBLANES, tc)
    src_h = jnp.arange(2 * sub, dtype=jnp.int32) // sub
    src_t = jnp.arange(2 * sub, dtype=jnp.int32) % sub
    dst_t = jnp.arange(sub * LANES, dtype=jnp.int32) // LANES
    dst_h = (jnp.arange(sub * LANES, dtype=jnp.int32) % LANES) // HEAD_DIM
    expand = ((src_t[:, None] == dst_t[None, :]) & (src_h[:, None] == dst_h[None, :])).astype(_bf16)
    expand = jnp.concatenate([expand] * 3, axis=0)
    eye =(jnp.arange(HEAD_DIM, dtype=jnp.int32)[:, None]
           == (jnp.arange(LANES, dtype=jnp.int32) % HEAD_DIM)[None, :]).astype(_f32)
    seq = pl.BlockSpec((nb, tc, d_a), lambda b, j: (b, j, 0))
    st = pl.BlockSpec((nb, n_pairs, HEAD_DIM, LANES), lambda b, j: (b, 0, 0, 0))
    full = lambda a: pl.BlockSpec(a.shape, lambda b, j: (0, 0))
    return pl.pallas_call(
        functools.partial(_rwkv_scan_body, nb, n_pairs, tc),
        grid=(n // nb, t // tc),
        in_specs=[seq] * 6 + [st, full(ones_k), full(expand), full(eye)],
        out_specs=[seq, st],
        out_shape=[jax.ShapeDtypeStruct((n, t, d_a), _f32),
                   jax.ShapeDtypeStruct(s0.shape, _f32)],
        compiler_params=_params("arbitrary", "arbitrary"),
        name="rwkv_scan",
    )(r, w, k, v, av, bv, s0, ones_k, expand, eye)


def _pack_state(s):
    n, h = s.shape[:2]
    return s.reshape(n, h // 2, 2, HEAD_DIM, HEAD_DIM).transpose(0, 1, 3, 2, 4).reshape(n, h // 2, HEAD_DIM, LANES)


def _unpack_state(s):
    n, hp = s.shape[:2]
    return s.reshape(n, hp, HEAD_DIM, 2, HEAD_DIM).transpose(0, 1, 3, 2, 4).reshape(n, hp * 2, HEAD_DIM, HEAD_DIM)


def _mix_out_body(o_ref, bonus_ref, g_ref, ob_ref, x_ref, gate_ref, lw_ref, lb_ref, ones_ref, wa_ref, wb_ref,
                  out_ref):
    ones_bd = ones_ref[...]
    o = o_ref[...]
    inv = 1.0 / HEAD_DIM
    mean = _segsum(o, ones_bd) * inv
    dlt = o - mean
    var = _segsum(dlt * dlt, ones_bd) * inv
    on = dlt * lax.rsqrt(var + LNX_EPS) * lw_ref[...] + lb_ref[...]
    oa = (on + bonus_ref[...]) * g_ref[...]
    y = _dot(_bf(oa), wa_ref[...]) + _dot(ob_ref[...], wb_ref[...])
    out_ref[...] = x_ref[...] + gate_ref[0] * y


def _mix_out(o, bonus, g, o_b, x, gate, lnx_w, lnx_b, w_out, tm, rows_per_group):
    m, d = x.shape
    d_a = o.shape[1]
    ones_bd = _ones_blockdiag(d_a, HEAD_DIM)
    wa = _bf(w_out[:d_a])
    wb = _bf(w_out[d_a:])
    gs = _group_spec(tm, rows_per_group, d)
    full = lambda a: pl.BlockSpec(a.shape, lambda i: (0, 0))
    rowa = pl.BlockSpec((tm, d_a), lambda i: (i, 0))
    rowd = pl.BlockSpec((tm, d), lambda i: (i, 0))
    lw = lnx_w.reshape(1, -1)
    lb = lnx_b.reshape(1, -1)
    return pl.pallas_call(
        _mix_out_body,
        grid=(m // tm,),
        in_specs=[rowa, rowa, rowa, pl.BlockSpec((tm, o_b.shape[1]), lambda i: (i, 0)), rowd, gs(gate.shape[1]),
                  full(lw), full(lb), full(ones_bd), full(wa), full(wb)],
        out_specs=rowd,
        out_shape=jax.ShapeDtypeStruct((m, d), _f32),
        compiler_params=_params("arbitrary"),
        name="mix_out",
    )(o, bonus, g, o_b, x, gate, lw, lb, ones_bd, wa, wb)


def _key_to_float(u):
    key = u ^ jnp.int32(-2147483648)
    bits = jnp.where(key >= 0, key, key ^ jnp.int32(0x7FFFFFFF))
    return lax.bitcast_convert_type(bits, _f32)


def _kth_largest(count_ge, k_row, shape):
    def body(it, u):
        bit = jnp.left_shift(jnp.int32(1), 31 - it)
        cand_u = u | bit
        ok = count_ge(_key_to_float(cand_u)) >= k_row
        return jnp.where(ok, cand_u, u)

    u = lax.fori_loop(0, 32, body, jnp.zeros(shape, jnp.int32))
    return _key_to_float(u)


def _tie_cut(count_eq_lt, budget, n_bits, shape):
    def body(it, c):
        cand = c | jnp.left_shift(jnp.int32(1), n_bits - 1 - it)
        ok = count_eq_lt(cand) <= budget
        return jnp.where(ok, cand, c)

    return lax.fori_loop(0, n_bits, body, jnp.zeros(shape, jnp.int32))


def _dsa_prompt_body(n_heads, k_sel, t_len,
                     far_ref, qt_ref, qit_ref, wit_ref, k_ref, vt_ref, ki_ref, near_ref,
                     ot_ref, sc_ref, cut_ref, qz_ref, m_ref, l_ref, acc_ref):
    qb = qt_ref.shape[2]
    i = pl.program_id(1)
    q0 = i * qb
    qpos = q0 + lax.broadcasted_iota(jnp.int32, (1, qb), 1)
    n_kb = (q0 + qb + KEY_BLOCK - 1) // KEY_BLOCK
    row_kb = lax.broadcasted_iota(jnp.int32, (KEY_BLOCK, qb), 0)

    @pl.when(i == 0)
    def _():
        sc_ref[...] = jnp.full(sc_ref.shape, -jnp.inf, _f32)

    w8 = (wit_ref[0] * (H_IDX ** -0.5)) * (D_IDX ** -0.5)

    qi_all = jnp.concatenate([qit_ref[0, h * D_IDX:(h + 1) * D_IDX, :] for h in range(H_IDX)], axis=1)

    def score_block(kb, carry):
        c0 = pl.multiple_of(kb * KEY_BLOCK, KEY_BLOCK)
        s_all = _dot(ki_ref[0, pl.ds(c0, KEY_BLOCK), :], qi_all)
        acc = w8[0:1, :] * jnp.maximum(s_all[:, :qb], 0.0)
        for h in range(1, H_IDX):
            acc = acc + w8[h:h + 1, :] * jnp.maximum(s_all[:, h * qb:(h + 1) * qb], 0.0)
        sc_ref[pl.ds(c0, KEY_BLOCK), :] = jnp.where(c0 + row_kb <= qpos, acc, -jnp.inf)
        return carry

    lax.fori_loop(0, n_kb, score_block, 0)

    cb = min(COUNT_BLOCK, t_len)
    n_cb = (q0 + qb + cb - 1) // cb
    row_cb = lax.broadcasted_iota(jnp.int32, (cb, qb), 0)

    def count(pred):
        def body(kb, acc):
            c0 = pl.multiple_of(kb * cb, cb)
            hit = jnp.where(pred(sc_ref[pl.ds(c0, cb), :], c0 + row_cb), 1.0, 0.0)
            for j in range(cb // COUNT_ROWS):
                acc = acc + hit[j * COUNT_ROWS:(j + 1) * COUNT_ROWS, :]
            return acc
        acc = lax.fori_loop(0, n_cb, body, jnp.zeros((COUNT_ROWS, qb), _f32))
        return jnp.sum(acc, axis=0, keepdims=True)

    k_row = jnp.minimum(k_sel, qpos + 1).astype(_f32)
    thr = _kth_largest(lambda cand: count(lambda s, pos: s >= cand), k_row, (1, qb))
    n_gt = count(lambda s, pos: s > thr)
    n_eq = count(lambda s, pos: s == thr)
    budget = k_row - n_gt
    cut_ref[...] = jnp.full((1, qb), 2 * t_len, jnp.int32)

    @pl.when(jnp.max(n_eq - budget) > 0.5)
    def _():
        cut_ref[...] = _tie_cut(lambda c: count(lambda s, pos: (s == thr) & (pos < c)), budget,
                                (2 * t_len).bit_length(), (1, qb))

    cut = cut_ref[...]

    m_ref[...] = jnp.full(m_ref.shape, M_INIT, _f32)
    l_ref[...] = jnp.zeros(l_ref.shape, _f32)
    acc_ref[...] = jnp.zeros(acc_ref.shape, _f32)
    pair_row = lax.broadcasted_iota(jnp.int32, (LANES, qb), 0) // HEAD_DIM
    for hp in range(n_heads // 2):
        qp = qt_ref[0, hp * LANES:(hp + 1) * LANES, :] * (HEAD_DIM ** -0.5)
        zero = jnp.zeros_like(qp)
        qz_ref[hp] = jnp.concatenate([jnp.where(pair_row == 0, qp, zero), jnp.where(pair_row == 1, qp, zero)], axis=1)

    def attend(c0, width, bias_of_head):
        scb = sc_ref[pl.ds(c0, width), :]
        pos = c0 + lax.broadcasted_iota(jnp.int32, (width, qb), 0)
        sel = (scb > thr) | ((scb == thr) & (pos < cut))
        m_old = [m_ref[h] for h in range(n_heads)]
        l_old = [l_ref[h] for h in range(n_heads)]
        a_old = [acc_ref[h] for h in range(n_heads)]
        s2 = [_dot(k_ref[0, pl.ds(c0, width), hp * LANES:(hp + 1) * LANES], qz_ref[hp]) for hp in range(n_heads // 2)]
        m_out, l_out, a_out = [], [], []
        for h in range(n_heads):
            s = s2[h // 2][:, (h % 2) * qb:(h % 2 + 1) * qb] + bias_of_head(h)
            s = jnp.where(sel, s, NEG)
            m_new = jnp.maximum(m_old[h], jnp.max(s, axis=0, keepdims=True))
            alpha = jnp.exp(m_old[h] - m_new)
            p = jnp.exp(s - m_new)
            l_out.append(alpha * l_old[h] + jnp.sum(p, axis=0, keepdims=True))
            pv = _dot(vt_ref[0, h * HEAD_DIM:(h + 1) * HEAD_DIM, pl.ds(c0, width)], _bf(p))
            a_out.append(alpha * a_old[h] + pv)
            m_out.append(m_new)
        for h in range(n_heads):
            m_ref[h] = m_out[h]
            l_ref[h] = l_out[h]
            acc_ref[h] = a_out[h]

    n_far = jnp.maximum(i - 1, 0) // (KEY_BLOCK // qb)

    def far_block(kb, carry):
        attend(pl.multiple_of(kb * KEY_BLOCK, KEY_BLOCK), KEY_BLOCK, lambda h: far_ref[h])
        return carry

    lax.fori_loop(0, n_far, far_block, 0)

    def near_block(j, carry):
        attend(pl.multiple_of(j * qb, qb), qb, lambda h: near_ref[i - j, h])
        return carry

    lax.fori_loop(n_far * (KEY_BLOCK // qb), i + 1, near_block, 0)

    for h in range(n_heads):
        ot_ref[0, h * HEAD_DIM:(h + 1) * HEAD_DIM, :] = _bf(acc_ref[h] / l_ref[h])


def _dsa_prompt(q, k, v, qi, ki, wi, rel_bias, qb):
    n, t, d_b = q.shape
    n_heads = d_b // HEAD_DIM
    k_sel = min(TOPK_MAX, t // 4)
    tr = lambda a: a.transpose(0, 2, 1)
    n_near = KEY_BLOCK // qb + 1
    qq = jnp.arange(qb, dtype=jnp.int32)
    dist = (jnp.arange(n_near, dtype=jnp.int32)[:, None, None] * qb + qq[None, None, :] - qq[None, :, None])
    onehot = (_rel_bucket(dist)[..., None] == jnp.arange(N_BUCKETS, dtype=jnp.int32)).astype(_f32)
    near = jnp.einsum('dkqb,bh->dhkq', onehot, rel_bias, precision=lax.Precision.HIGHEST)
    far = rel_bias[N_BUCKETS - 1]
    res = lambda shape: pl.BlockSpec(shape, lambda b, i, *_: (b, 0, 0))
    blk = lambda w: pl.BlockSpec((1, w, qb), lambda b, i, *_: (b, 0, i))
    grid_spec = pltpu.PrefetchScalarGridSpec(
        num_scalar_prefetch=0,
        grid=(n, t // qb),
        in_specs=[pl.BlockSpec(memory_space=pltpu.SMEM),
                  blk(d_b), blk(H_IDX * D_IDX), blk(H_IDX),
                  res((1, t, d_b)), res((1, d_b, t)), res((1, t, D_IDX)),
                  pl.BlockSpec(near.shape, lambda b, i, *_: (0, 0, 0, 0))],
        out_specs=blk(d_b),
        scratch_shapes=[pltpu.VMEM((t, qb), _f32), pltpu.VMEM((1, qb), jnp.int32),
                        pltpu.VMEM((n_heads // 2, LANES, 2 * qb), _bf16),
                        pltpu.VMEM((n_heads, 1, qb), _f32), pltpu.VMEM((n_heads, 1, qb), _f32),
                        pltpu.VMEM((n_heads, HEAD_DIM, qb), _f32)],
    )
    o_t = pl.pallas_call(
        functools.partial(_dsa_prompt_body, n_heads, k_sel, t),
        grid_spec=grid_spec,
        out_shape=jax.ShapeDtypeStruct((n, d_b, t), _bf16),
        compiler_params=_params("arbitrary", "arbitrary"),
        name="dsa_prompt",
    )(far, tr(q), tr(qi), tr(wi), _bf(k), tr(_bf(v)), _bf(ki), near)
    return tr(o_t)


def _dsa_sample_index_body(n_pages, page, pg, pt_ref, qi_ref, wi_ref, kin_ref, *rest):
    cki_refs, (sc_ref, snew_ref) = rest[:pg], rest[pg:]
    step = pl.program_id(1)
    w8 = _bf(wi_ref[0] * (H_IDX ** -0.5)).astype(_f32)
    relu_bf = lambda s: _bf(jnp.maximum(s, 0.0)).astype(_f32)
    qi8 = qi_ref[0]
    for j in range(pg):
        c0 = pl.multiple_of((step * pg + j) * page, page)
        s8 = _dot(qi8, _bf(cki_refs[j][0, 0])) * (D_IDX ** -0.5)
        sc_ref[0, :, pl.ds(c0, page)] = jnp.sum(w8 * relu_bf(s8), axis=0, keepdims=True)

    @pl.when(step == n_pages // pg - 1)
    def _():
        s_new8 = jnp.sum(qi8.astype(_f32) * _bf(kin_ref[0]).astype(_f32), axis=1, keepdims=True) * (D_IDX ** -0.5)
        snew_ref[0] = jnp.sum(w8 * relu_bf(s_new8), axis=0, keepdims=True)


def _dsa_sample_select_body(k_sel, sc_ref, snew_ref, thr_ref, cut_ref):
    sc = sc_ref[...]
    s_new = snew_ref[...]
    rows, past = sc.shape
    pos = lax.broadcasted_iota(jnp.int32, sc.shape, 1)

    def count(pred):
        hits = jnp.sum(jnp.where(pred(sc, pos), 1.0, 0.0), axis=1, keepdims=True)
        return hits + jnp.where(pred(s_new, past), 1.0, 0.0)

    k_row = jnp.full((rows, 1), float(k_sel), _f32)
    thr = _kth_largest(lambda cand: count(lambda s, ps: s >= cand), k_row, (rows, 1))
    budget = k_row - count(lambda s, ps: s > thr)
    thr_ref[...] = thr
    cut_ref[...] = _tie_cut(lambda c: count(lambda s, ps: (s == thr) & (ps < c)), budget,
                            (2 * (past + 1)).bit_length(), (rows, 1))


def _dsa_sample_probs_body(n_heads, n_pages, page, pg, pt_ref, q_ref, kn_ref, sc_ref, snew_ref, thr_ref, cut_ref, *rest):
    ck_refs = rest[:pg]
    btab_ref, bnew_ref, p_ref, pnew_ref, lg_ref = rest[pg:]
    step = pl.program_id(1)
    past = n_pages * page
    q8 = q_ref[0]

    for j in range(pg):
        c0 = pl.multiple_of((step * pg + j) * page, page)
        rows = [_dot(q8, _bf(ck_refs[j][0, 0, h]))[h:h + 1] for h in range(n_heads)]
        lg = jnp.concatenate(rows, axis=0) * (HEAD_DIM ** -0.5)
        lg_ref[:, pl.ds(c0, page)] = lg + btab_ref[:, pl.ds(c0, page)]

    @pl.when(step == n_pages // pg - 1)
    def _():
        sc = sc_ref[0]
        s_new = snew_ref[0]
        thr = thr_ref[0]
        cut = cut_ref[0]
        pos = lax.broadcasted_iota(jnp.int32, sc.shape, 1)
        sel = (sc > thr) | ((sc == thr) & (pos < cut))
        sel_new = (s_new > thr) | ((s_new == thr) & (past < cut))

        kn = _bf(kn_ref[0]).astype(_f32)
        lg_new = jnp.sum(q8.astype(_f32) * kn, axis=1, keepdims=True) * (HEAD_DIM ** -0.5) + bnew_ref[...]
        lg_new = jnp.where(sel_new, lg_new, NEG)
        s_all = jnp.where(sel, lg_ref[...], NEG)
        m = jnp.maximum(jnp.maximum(jnp.max(s_all, axis=1, keepdims=True), lg_new), M_INIT)
        pr = jnp.exp(s_all - m)
        pr_new = jnp.exp(lg_new - m)
        l = jnp.sum(pr, axis=1, keepdims=True) + pr_new
        p_ref[0] = pr
        col = lax.broadcasted_iota(jnp.int32, (n_heads, 2), 1)
        pnew_ref[0] = jnp.where(col == 0, pr_new, l)


def _dsa_sample_pv_body(n_heads, n_pages, page, pg, pt_ref, p_ref, pnew_ref, vn_ref, *rest):
    cv_refs, (o_ref, acc_ref) = rest[:pg], rest[pg:]
    step = pl.program_id(1)

    @pl.when(step == 0)
    def _():
        acc_ref[...] = pnew_ref[0][:, 0:1] * _bf(vn_ref[0]).astype(_f32)

    acc = acc_ref[...]
    for j in range(pg):
        c0 = pl.multiple_of((step * pg + j) * page, page)
        pb = _bf(p_ref[0, :, pl.ds(c0, page)])
        rows = [_dot_nt(pb, _bf(cv_refs[j][0, 0, h]))[h:h + 1] for h in range(n_heads)]
        acc = acc + jnp.concatenate(rows, axis=0)
    acc_ref[...] = acc

    @pl.when(step == n_pages // pg - 1)
    def _():
        o_ref[0] = _bf(acc / pnew_ref[0][:, 1:2])


def _dsa_sample(q, k_new, v_new, qi, ki_new, wi, ck, cv, cki, layer, page_table, rel_bias):
    b, _, d_b = q.shape
    n_heads = d_b // HEAD_DIM
    n_pages = page_table.shape[1]
    page = ck.shape[2]
    ck, cv, cki = (jnp.moveaxis(a, 2, -1) for a in (ck, cv, cki))
    past = n_pages * page
    k_sel = min(TOPK_MAX, (past + 1) // 4)
    pg = max(g for g in (1, 2, 4, 8) if n_pages % g == 0)
    kpos = jnp.arange(past, dtype=jnp.int32)
    btab = rel_bias[_rel_bucket(past - kpos)].T
    bnew = rel_bias[_rel_bucket(jnp.zeros((1,), jnp.int32))].T
    heads = lambda a: a.reshape(b, n_heads, HEAD_DIM)
    pt = page_table.reshape(-1)
    per_b = lambda shape: pl.BlockSpec((1,) + shape, lambda i, s, pt: (i,) + (0,) * len(shape))
    const = lambda a: pl.BlockSpec(a.shape, lambda i, s, pt: (0,) * a.ndim)

    def paged(tail, j):
        return pl.BlockSpec((1, 1) + tail + (page,),
                            lambda i, s, pt: (layer, pt[i * n_pages + s * pg + j]) + (0,) * (len(tail) + 1))

    grid = (b, n_pages // pg)
    scores, s_new = pl.pallas_call(
        functools.partial(_dsa_sample_index_body, n_pages, page, pg),
        grid_spec=pltpu.PrefetchScalarGridSpec(
            num_scalar_prefetch=1, grid=grid,
            in_specs=[per_b((H_IDX, D_IDX)), per_b((H_IDX, 1)), per_b((1, D_IDX))]
                     + [paged((D_IDX,), j) for j in range(pg)],
            out_specs=[per_b((1, past)), per_b((1, 1))]),
        out_shape=[jax.ShapeDtypeStruct((b, 1, past), _f32), jax.ShapeDtypeStruct((b, 1, 1), _f32)],
        compiler_params=_params("arbitrary", "arbitrary"),
        name="dsa_sample_index",
    )(pt, qi.reshape(b, H_IDX, D_IDX), wi.reshape(b, H_IDX, 1), ki_new, *([cki] * pg))
    thr, cut = pl.pallas_call(
        functools.partial(_dsa_sample_select_body, k_sel),
        out_shape=[jax.ShapeDtypeStruct((b, 1), _f32), jax.ShapeDtypeStruct((b, 1), jnp.int32)],
        compiler_params=pltpu.CompilerParams(vmem_limit_bytes=VMEM_LIMIT),
        name="dsa_sample_select",
    )(scores.reshape(b, past), s_new.reshape(b, 1))
    probs, p_new = pl.pallas_call(
        functools.partial(_dsa_sample_probs_body, n_heads, n_pages, page, pg),
        grid_spec=pltpu.PrefetchScalarGridSpec(
            num_scalar_prefetch=1, grid=grid,
            in_specs=[per_b((n_heads, HEAD_DIM)), per_b((n_heads, HEAD_DIM)), per_b((1, past)), per_b((1, 1)),
                      per_b((1, 1)), per_b((1, 1))]
                     + [paged((n_heads, HEAD_DIM), j) for j in range(pg)]
                     + [const(btab), const(bnew)],
            out_specs=[per_b((n_heads, past)), per_b((n_heads, 2))],
            scratch_shapes=[pltpu.VMEM((n_heads, past), _f32)]),
        out_shape=[jax.ShapeDtypeStruct((b, n_heads, past), _f32), jax.ShapeDtypeStruct((b, n_heads, 2), _f32)],
        compiler_params=_params("arbitrary", "arbitrary"),
        name="dsa_sample_probs",
    )(pt, heads(q), heads(k_new), scores, s_new, thr.reshape(b, 1, 1), cut.reshape(b, 1, 1),
      *([ck] * pg), btab, bnew)
    out = pl.pallas_call(
        functools.partial(_dsa_sample_pv_body, n_heads, n_pages, page, pg),
        grid_spec=pltpu.PrefetchScalarGridSpec(
            num_scalar_prefetch=1, grid=grid,
            in_specs=[per_b((n_heads, past)), per_b((n_heads, 2)), per_b((n_heads, HEAD_DIM))]
                     + [paged((n_heads, HEAD_DIM), j) for j in range(pg)],
            out_specs=per_b((n_heads, HEAD_DIM)),
            scratch_shapes=[pltpu.VMEM((n_heads, HEAD_DIM), _f32)]),
        out_shape=jax.ShapeDtypeStruct((b, n_heads, HEAD_DIM), _bf16),
        compiler_params=_params("arbitrary", "arbitrary"),
        name="dsa_sample_pv",
    )(pt, probs, p_new, heads(v_new), *([cv] * pg))
    return out.reshape(b, 1, d_b)


def _ffn_body(routed, final_norm, n_exp,
              x_ref, g_ref, sh_ref, sc_ref, gate_ref, rw_ref, rb_ref, w1_ref, w3_ref, w2_ref, gf_ref,
              o_ref, h_ref, acc_ref, dg_ref):
    e = pl.program_id(1)
    f = pl.program_id(2)
    first = (e == 0) & (f == 0)
    last = (e == pl.num_programs(1) - 1) & (f == pl.num_programs(2) - 1)

    @pl.when(first)
    def _():
        h = _norm_mod(x_ref[...], g_ref[...], sh_ref[0], sc_ref[0])
        h_ref[...] = _bf(h)
        acc_ref[...] = jnp.zeros(acc_ref.shape, _f32)
        if routed:
            logits = _dot(_bf(h), rw_ref[...]) + rb_ref[...]
            lane = lax.broadcasted_iota(jnp.int32, logits.shape, 1).astype(_f32)
            logits = jnp.where(lane < n_exp, logits, -jnp.inf)
            v1 = jnp.max(logits, axis=1, keepdims=True)
            i1 = jnp.min(jnp.where(logits == v1, lane, float(LANES)), axis=1, keepdims=True)
            rest = jnp.where(lane == i1, -jnp.inf, logits)
            v2 = jnp.max(rest, axis=1, keepdims=True)
            i2 = jnp.min(jnp.where(rest == v2, lane, float(LANES)), axis=1, keepdims=True)
            e2 = jnp.exp(v2 - v1)
            den = 1.0 + e2
            dg_ref[...] = jnp.where(lane == i1, 1.0 / den, 0.0) + jnp.where(lane == i2, e2 / den, 0.0)

    h = h_ref[...]
    a = _silu(_dot(h, w1_ref[0])) * _dot(h, w3_ref[0])
    y = _dot(_bf(a), w2_ref[0])
    if routed:
        lane = lax.broadcasted_iota(jnp.int32, dg_ref.shape, 1)
        y = y * jnp.sum(jnp.where(lane == e, dg_ref[...], 0.0), axis=1, keepdims=True)
    acc_ref[...] += y

    @pl.when(last)
    def _():
        y = x_ref[...] + gate_ref[0] * acc_ref[...]
        if final_norm:
            y = y * lax.rsqrt(jnp.mean(y * y, axis=-1, keepdims=True) + EPS) * gf_ref[...]
        o_ref[...] = y


def _ffn(x, g, shift, scale, gate, w13, w2, tm, tf, rows_per_group, router=None, final_g=None):
    m, d = x.shape
    n_exp, ff, _ = w2.shape
    routed = router is not None
    final_norm = final_g is not None
    if routed:
        rw, rb = router
        rw = _bf(jnp.pad(rw, ((0, 0), (0, LANES - n_exp))))
        rb = jnp.pad(rb.reshape(1, -1), ((0, 0), (0, LANES - n_exp)))
    else:
        rw = jnp.zeros((d, LANES), _bf16)
        rb = jnp.zeros((1, LANES), _f32)
    gf = final_g.reshape(1, -1) if final_norm else jnp.ones((1, d), _f32)
    nf = ff // tf
    gs = _group_spec(tm, rows_per_group, d)
    rowd = pl.BlockSpec((tm, d), lambda i, e, f: (i, 0))
    full = lambda a: pl.BlockSpec(a.shape, lambda i, e, f: (0, 0))
    return pl.pallas_call(
        functools.partial(_ffn_body, routed, final_norm, n_exp),
        grid=(m // tm, n_exp, nf),
        in_specs=[rowd, full(g), gs(shift.shape[1]), gs(scale.shape[1]), gs(gate.shape[1]), full(rw), full(rb),
                  pl.BlockSpec((1, d, tf), lambda i, e, f: (e, 0, f)),
                  pl.BlockSpec((1, d, tf), lambda i, e, f: (e, 0, f + nf)),
                  pl.BlockSpec((1, tf, d), lambda i, e, f: (e, f, 0)),
                  full(gf)],
        out_specs=rowd,
        out_shape=jax.ShapeDtypeStruct((m, d), _f32),
        scratch_shapes=[pltpu.VMEM((tm, d), _bf16), pltpu.VMEM((tm, d), _f32), pltpu.VMEM((tm, LANES), _f32)],
        compiler_params=_params("arbitrary", "arbitrary", "arbitrary"),
        name="moe_ffn" if routed else "dense_ffn",
    )(x, g, shift, scale, gate, rw, rb, w13, w13, w2, gf)


MOE_CAP = 512


def _top2_gates(logits, n_exp):
    lane = lax.broadcasted_iota(jnp.int32, logits.shape, 1).astype(_f32)
    logits = jnp.where(lane < n_exp, logits, -jnp.inf)
    v1 = jnp.max(logits, axis=1, keepdims=True)
    i1 = jnp.min(jnp.where(logits == v1, lane, float(LANES)), axis=1, keepdims=True)
    rest = jnp.where(lane == i1, -jnp.inf, logits)
    v2 = jnp.max(rest, axis=1, keepdims=True)
    i2 = jnp.min(jnp.where(rest == v2, lane, float(LANES)), axis=1, keepdims=True)
    e2 = jnp.exp(v2 - v1)
    den = 1.0 + e2
    gates = jnp.where(lane == i1, 1.0 / den, 0.0) + jnp.where(lane == i2, e2 / den, 0.0)
    routed = jnp.where((lane == i1) | (lane == i2), 1.0, 0.0)
    return gates, routed


def _moe_route_body(n_exp, x_ref, g_ref, sh_ref, sc_ref, rw_ref, rb_ref, ltri_ref,
                    xs_ref, dg_ref, slot_ref, flag_ref):
    tm = x_ref.shape[0]
    hb = _bf(_norm_mod(x_ref[...], g_ref[...], sh_ref[0], sc_ref[0]))
    gates, routed = _top2_gates(_dot(hb, rw_ref[...]) + rb_ref[...], n_exp)
    dg_ref[...] = gates
    rank = _dot(ltri_ref[...], _bf(routed))
    slot = jnp.where(routed > 0.5, rank, -1.0)
    slot_ref[...] = slot
    slot_t = slot.T
    cap = xs_ref.shape[2]
    want = lax.broadcasted_iota(jnp.int32, (LANES, tm), 0).astype(_f32)
    worst = jnp.zeros((1, 1), _f32)
    for e in range(n_exp):
        s_row = slot_t[e:e + 1, :]
        for c in range(cap // LANES):
            onehot = jnp.where(s_row == want + float(c * LANES), 1.0, 0.0)
            xs_ref[e, 0, c * LANES:(c + 1) * LANES, :] = _bf(_dot(_bf(onehot), hb))
        worst = jnp.maximum(worst, jnp.max(s_row, axis=1, keepdims=True))
    flag_ref[0] = jnp.broadcast_to(jnp.where(worst >= float(cap), 1.0, 0.0), (1, LANES))


def _moe_expert_body(x_ref, w1_ref, w3_ref, w2_ref, o_ref, acc_ref):
    f = pl.program_id(2)

    @pl.when(f == 0)
    def _():
        acc_ref[...] = jnp.zeros(acc_ref.shape, _f32)

    x = x_ref[0]
    a = _silu(_dot(x, w1_ref[0])) * _dot(x, w3_ref[0])
    acc_ref[...] += _dot(_bf(a), w2_ref[0])

    @pl.when(f == pl.num_programs(2) - 1)
    def _():
        o_ref[0] = _bf(acc_ref[...])


def _moe_combine_body(n_exp, final_norm, x_ref, gate_ref, dg_ref, slot_ref, ys_ref, gf_ref, o_ref):
    tm = x_ref.shape[0]
    cap = ys_ref.shape[2]
    lane = lax.broadcasted_iota(jnp.int32, (tm, LANES), 1)
    col = lax.broadcasted_iota(jnp.int32, (tm, cap), 1).astype(_f32)
    dg = dg_ref[...]
    slot = slot_ref[...]
    acc = jnp.zeros(x_ref.shape, _f32)
    for e in range(n_exp):
        slot_e = jnp.sum(jnp.where(lane == e, slot, 0.0), axis=1, keepdims=True)
        gate_e = jnp.sum(jnp.where(lane == e, dg, 0.0), axis=1, keepdims=True)
        scatter = _bf(jnp.where(slot_e == col, 1.0, 0.0))
        acc = acc + gate_e * _dot(scatter, ys_ref[e, 0])
    y = x_ref[...] + gate_ref[0] * acc
    if final_norm:
        y = y * lax.rsqrt(jnp.mean(y * y, axis=-1, keepdims=True) + EPS) * gf_ref[...]
    o_ref[...] = y


def _moe_sparse(x, g, shift, scale, gate, w13, w2, tm, tf, rows_per_group, router, final_g):
    m, d = x.shape
    n_exp, ff, _ = w2.shape
    rw, rb = router
    rw = _bf(jnp.pad(rw, ((0, 0), (0, LANES - n_exp))))
    rb = jnp.pad(rb.reshape(1, -1), ((0, 0), (0, LANES - n_exp)))
    final_norm = final_g is not None
    gf = final_g.reshape(1, -1) if final_norm else jnp.ones((1, d), _f32)
    tiles = m // tm
    cap = MOE_CAP
    ltri = (jnp.arange(tm, dtype=jnp.int32)[:, None] > jnp.arange(tm, dtype=jnp.int32)[None, :]).astype(_bf16)
    gs = _group_spec(tm, rows_per_group, d)
    rowd = pl.BlockSpec((tm, d), lambda i: (i, 0))
    rowl = pl.BlockSpec((tm, LANES), lambda i: (i, 0))
    full = lambda a: pl.BlockSpec(a.shape, lambda i: (0,) * a.ndim)
    xs_spec = pl.BlockSpec((n_exp, 1, cap, d), lambda i: (0, i, 0, 0))
    xs, dg, slot, flags = pl.pallas_call(
        functools.partial(_moe_route_body, n_exp),
        grid=(tiles,),
        in_specs=[rowd, full(g), gs(shift.shape[1]), gs(scale.shape[1]), full(rw), full(rb), full(ltri)],
        out_specs=[xs_spec, rowl, rowl, pl.BlockSpec((1, 1, LANES), lambda i: (i, 0, 0))],
        out_shape=[jax.ShapeDtypeStruct((n_exp, tiles, cap, d), _bf16), jax.ShapeDtypeStruct((m, LANES), _f32),
                   jax.ShapeDtypeStruct((m, LANES), _f32), jax.ShapeDtypeStruct((tiles, 1, LANES), _f32)],
        compiler_params=_params("arbitrary"),
        name="moe_route",
    )(x, g, shift, scale, rw, rb, ltri)

    rows = tiles * cap
    tr = _tile(rows, 1536)
    nf = ff // tf

    def dense_path():
        return _ffn(x, g, shift, scale, gate, w13, w2, tm, tf, rows_per_group, router=router, final_g=final_g)

    def sparse_path():
        ys = pl.pallas_call(
            _moe_expert_body,
            grid=(n_exp, rows // tr, nf),
            in_specs=[pl.BlockSpec((1, tr, d), lambda e, r, f: (e, r, 0)),
                      pl.BlockSpec((1, d, tf), lambda e, r, f: (e, 0, f)),
                      pl.BlockSpec((1, d, tf), lambda e, r, f: (e, 0, f + nf)),
                      pl.BlockSpec((1, tf, d), lambda e, r, f: (e, f, 0))],
            out_specs=pl.BlockSpec((1, tr, d), lambda e, r, f: (e, r, 0)),
            out_shape=jax.ShapeDtypeStruct((n_exp, rows, d), _bf16),
            scratch_shapes=[pltpu.VMEM((tr, d), _f32)],
            compiler_params=_params("arbitrary", "arbitrary", "arbitrary"),
            name="moe_experts",
        )(xs.reshape(n_exp, rows, d), w13, w13, w2)
        return pl.pallas_call(
            functools.partial(_moe_combine_body, n_exp, final_norm),
            grid=(tiles,),
            in_specs=[rowd, gs(gate.shape[1]), rowl, rowl, xs_spec, full(gf)],
            out_specs=rowd,
            out_shape=jax.ShapeDtypeStruct((m, d), _f32),
            compiler_params=_params("arbitrary"),
            name="moe_combine",
        )(x, gate, dg, slot, ys.reshape(n_exp, tiles, cap, d), gf)

    return lax.cond(jnp.max(flags) > 0.5, dense_path, sparse_path)


def _cmul(ar, ai, br, bi):
    return ar * br - ai * bi, ar * bi + ai * br


def _s5_io(u, b_ref, c_ref, d_ref, are_ref, aim_ref, h_of_bb):
    half = b_ref.shape[2] // 2
    bu = _dot(_bf(u), b_ref[0])
    bb_re, bb_im = _cmul(are_ref[0, 1:2], aim_ref[0, 1:2], bu[:, :half], bu[:, half:])
    h_re, h_im = h_of_bb(bb_re, bb_im)
    y = _dot(_bf(jnp.concatenate([h_re, h_im], axis=1)), c_ref[0]) + d_ref[0] * u
    return _bf(_gelu_tanh(y)), h_re, h_im


def _s5_scan_body(tc, u_ref, b_ref, c_ref, d_ref, are_ref, aim_ref, h0r_ref, h0i_ref,
                  y_ref, hr_ref, hi_ref, xr_ref, xi_ref):
    t_blk = pl.program_id(2)

    @pl.when(t_blk == 0)
    def _():
        hr_ref[0, 0] = h0r_ref[0, 0]
        hi_ref[0, 0] = h0i_ref[0, 0]

    width = are_ref.shape[2]
    a1 = (jnp.broadcast_to(are_ref[0, 0:1], (SUBLANES, width)), jnp.broadcast_to(aim_ref[0, 0:1], (SUBLANES, width)))
    a2 = _cmul(*a1, *a1)
    a3 = _cmul(*a2, *a1)
    a4 = _cmul(*a2, *a2)
    a5 = _cmul(*a4, *a1)
    a6 = _cmul(*a4, *a2)
    a7 = _cmul(*a4, *a3)
    a8 = _cmul(*a4, *a4)
    row = lax.broadcasted_iota(jnp.int32, (SUBLANES, width), 0)
    lvl = [tuple(jnp.where(row >= s, c, 0.0) for c in a) for s, a in ((1, a1), (2, a2), (4, a4))]
    pw = []
    for comp in range(2):
        acc = a8[comp]
        for s, a in ((6, a7), (5, a6), (4, a5), (3, a4), (2, a3), (1, a2), (0, a1)):
            acc = jnp.where(row == s, a[comp], acc)
        pw.append(acc)

    def scan_chunk(bb_re, bb_im):
        xr_ref[...] = bb_re
        xi_ref[...] = bb_im

        def tile(j, carry):
            cr, ci = carry
            r0 = pl.multiple_of(j * SUBLANES, SUBLANES)
            xr = xr_ref[pl.ds(r0, SUBLANES), :]
            xi = xi_ref[pl.ds(r0, SUBLANES), :]
            for (s, (fr, fi)) in zip((1, 2, 4), lvl):
                sr = pltpu.roll(xr, s, axis=0)
                si = pltpu.roll(xi, s, axis=0)
                pr, pi = _cmul(fr, fi, sr, si)
                xr = xr + pr
                xi = xi + pi
            pr, pi = _cmul(pw[0], pw[1], jnp.broadcast_to(cr, xr.shape), jnp.broadcast_to(ci, xi.shape))
            xr = xr + pr
            xi = xi + pi
            xr_ref[pl.ds(r0, SUBLANES), :] = xr
            xi_ref[pl.ds(r0, SUBLANES), :] = xi
            return xr[SUBLANES - 1:SUBLANES], xi[SUBLANES - 1:SUBLANES]

        cr, ci = lax.fori_loop(0, tc // SUBLANES, tile, (hr_ref[0, 0], hi_ref[0, 0]))
        hr_ref[0, 0] = cr
        hi_ref[0, 0] = ci
        return xr_ref[...], xi_ref[...]

    y, _, _ = _s5_io(u_ref[0], b_ref, c_ref, d_ref, are_ref, aim_ref, scan_chunk)
    y_ref[0] = y


def _s5_step_body(u_ref, b_ref, c_ref, d_ref, are_ref, aim_ref, h0r_ref, h0i_ref, y_ref, hr_ref, hi_ref):
    def one_step(bb_re, bb_im):
        pr, pi = _cmul(are_ref[0, 0:1], aim_ref[0, 0:1], h0r_ref[0], h0i_ref[0])
        return pr + bb_re, pi + bb_im

    y, h_re, h_im = _s5_io(u_ref[...], b_ref, c_ref, d_ref, are_ref, aim_ref, one_step)
    y_ref[...] = y
    hr_ref[0] = h_re
    hi_ref[0] = h_im


def _s5_tables(a_re, a_im, log_step, b_re, b_im, c_re, c_im, d):
    g_c, p_c, ch = b_re.shape
    gs = LANES // ch
    ns = g_c // gs
    dt = jnp.exp(log_step.astype(_f32))[:, None]
    mag = jnp.exp(dt * a_re)
    ab_re = mag * jnp.cos(dt * a_im)
    ab_im = mag * jnp.sin(dt * a_im)
    den = a_re * a_re + a_im * a_im
    nr = ab_re - 1.0
    co_re = (nr * a_re + ab_im * a_im) / den
    co_im = (ab_im * a_re - nr * a_im) / den
    eye = jnp.eye(gs, dtype=_f32)

    def pack_b(b):
        b = b.reshape(ns, gs, p_c, ch)
        return jnp.einsum('sgpc,gh->sgchp', b, eye).reshape(ns, gs * ch, gs * p_c)

    def pack_c(c):
        c = c.reshape(ns, gs, ch, p_c)
        return jnp.einsum('sgcp,gh->sgphc', c, eye).reshape(ns, gs * p_c, gs * ch)

    b_pack = _bf(jnp.concatenate([pack_b(b_re), pack_b(b_im)], axis=2))
    c_pack = _bf(jnp.concatenate([pack_c(c_re), -pack_c(c_im)], axis=1))
    vec = lambda a, c: jnp.stack([a.reshape(ns, gs * p_c), c.reshape(ns, gs * p_c)], axis=1)
    return b_pack, c_pack, d.reshape(ns, 1, gs * ch), vec(ab_re, co_re), vec(ab_im, co_im)


def _s5_scan(u, h0_re, h0_im, tables, tc):
    n, t, d = u.shape
    b_pack, c_pack, dvec, are, aim = tables
    ns, _, w2 = b_pack.shape
    w = w2 // 2
    h0r = h0_re.reshape(n, ns, 1, w)
    h0i = h0_im.reshape(n, ns, 1, w)
    tab = lambda a: pl.BlockSpec((1,) + a.shape[1:], lambda b, s, j: (s, 0, 0))
    st = pl.BlockSpec((1, 1, 1, w), lambda b, s, j: (b, s, 0, 0))
    seq = pl.BlockSpec((1, tc, LANES), lambda b, s, j: (b, j, s))
    y, hr, hi = pl.pallas_call(
        functools.partial(_s5_scan_body, tc),
        grid=(n, ns, t // tc),
        in_specs=[seq, tab(b_pack), tab(c_pack), tab(dvec), tab(are), tab(aim), st, st],
        out_specs=[seq, st, st],
        out_shape=[jax.ShapeDtypeStruct((n, t, d), _bf16),
                   jax.ShapeDtypeStruct((n, ns, 1, w), _f32), jax.ShapeDtypeStruct((n, ns, 1, w), _f32)],
        scratch_shapes=[pltpu.VMEM((tc, w), _f32), pltpu.VMEM((tc, w), _f32)],
        compiler_params=_params("arbitrary", "arbitrary", "arbitrary"),
        name="s5_scan",
    )(u, b_pack, c_pack, dvec, are, aim, h0r, h0i)
    return y, hr.reshape(n, ns * w), hi.reshape(n, ns * w)


def _s5_step(u, h0_re, h0_im, tables):
    b, d = u.shape
    b_pack, c_pack, dvec, are, aim = tables
    ns, _, w2 = b_pack.shape
    w = w2 // 2
    h0r = h0_re.reshape(b, ns, w).transpose(1, 0, 2)
    h0i = h0_im.reshape(b, ns, w).transpose(1, 0, 2)
    tab = lambda a: pl.BlockSpec((1,) + a.shape[1:], lambda s: (s, 0, 0))
    st = pl.BlockSpec((1, b, w), lambda s: (s, 0, 0))
    col = pl.BlockSpec((b, LANES), lambda s: (0, s))
    y, hr, hi = pl.pallas_call(
        _s5_step_body,
        grid=(ns,),
        in_specs=[col, tab(b_pack), tab(c_pack), tab(dvec), tab(are), tab(aim), st, st],
        out_specs=[col, st, st],
        out_shape=[jax.ShapeDtypeStruct((b, d), _bf16),
                   jax.ShapeDtypeStruct((ns, b, w), _f32), jax.ShapeDtypeStruct((ns, b, w), _f32)],
        compiler_params=_params("arbitrary"),
        name="s5_step",
    )(u, b_pack, c_pack, dvec, are, aim, h0r, h0i)
    return y, hr.transpose(1, 0, 2).reshape(b, ns * w), hi.transpose(1, 0, 2).reshape(b, ns * w)


def _glu_out_body(y_ref, x_ref, gate_ref, wl_ref, wr_ref, o_ref):
    y = y_ref[...]
    z = _dot(y, wl_ref[...]) * jax.nn.sigmoid(_dot(y, wr_ref[...]))
    o_ref[...] = x_ref[...] + gate_ref[0] * z


def _glu_out(yg, x, gate, w_glu, tm, rows_per_group):
    m, d = x.shape
    wl = _bf(w_glu[:, :d])
    wr = _bf(w_glu[:, d:])
    gs = _group_spec(tm, rows_per_group, d)
    rowd = pl.BlockSpec((tm, d), lambda i: (i, 0))
    full = lambda a: pl.BlockSpec(a.shape, lambda i: (0, 0))
    return pl.pallas_call(
        _glu_out_body,
        grid=(m // tm,),
        in_specs=[rowd, rowd, gs(gate.shape[1]), full(wl), full(wr)],
        out_specs=rowd,
        out_shape=jax.ShapeDtypeStruct((m, d), _f32),
        compiler_params=_params("arbitrary"),
        name="glu_out",
    )(yg, x, gate, wl, wr)


def _tile(m, target):
    if m <= target:
        return m
    t = target
    while m % t:
        t -= SUBLANES
    return t


def _run_group(x, mods, is_prompt, st, wts):
    n, t, d = x.shape
    m = n * t
    depth = wts["norm_mix"].shape[0]
    d_a = d // 2
    d_b = d // 2
    n_heads = d_a // HEAD_DIM
    rows_per_group = t if is_prompt else m
    tm = _tile(t, 256) if is_prompt else m
    xf = x.reshape(m, d).astype(_f32)

    def mod_vecs(l):
        parts = jnp.split(mods[l], 6, axis=-1)
        if is_prompt:
            return [p.reshape(n, 1, d) for p in parts]
        return [p.reshape(1, m, d) for p in parts]

    outs = {k: [] for k in ("shift", "wkv", "k", "v", "kidx", "re", "im")}
    for l in range(depth):
        sh_m, sc_m, g_m, sh_f, sc_f, g_f = mod_vecs(l)
        i = l // 2
        last = l == depth - 1
        final_g = wts["norm_final"] if last else None
        norm_mix = wts["norm_mix"][l].reshape(1, d)
        norm_ffn = wts["norm_ffn"][l].reshape(1, d)
        if l % 2 == 0:
            p_a, q, k, v, qi, ki, wi = _in_proj(xf, norm_mix, sh_m, sc_m, wts["e_w_in"][i], tm, rows_per_group)
            if is_prompt:
                prev0 = st["shift"][i].reshape(n, 1, -1)
            else:
                prev0 = st["shift"][i].reshape(1, m, -1)
            r, w, k_mod, v_a, av, bv, g, bonus = _rwkv_prep(
                p_a, prev0.astype(_f32), wts["e_mu"][i], wts["e_w0"][i], wts["e_w2"][i], wts["e_a0"][i],
                wts["e_a2"][i], wts["e_g2"][i], wts["e_k_k"][i], wts["e_k_a"][i], wts["e_r_k"][i],
                tm, rows_per_group)
            seq = lambda a: a.reshape(n, t, d_a)
            nb = n if is_prompt else _tile(n, 4)
            tc = _tile(t, 256)
            o, s_fin = _rwkv_scan(seq(r), seq(w), seq(k_mod), seq(v_a), seq(av), seq(bv),
                                  _pack_state(st["wkv"][i].astype(_f32)), nb, tc)
            if is_prompt:
                o_b = _dsa_prompt(q.reshape(n, t, d_b), k.reshape(n, t, d_b), v.reshape(n, t, d_b),
                                  qi.reshape(n, t, -1), ki.reshape(n, t, -1), wi.reshape(n, t, -1),
                                  wts["rel_bias"], 128)
            else:
                o_b = _dsa_sample(q.reshape(n, t, d_b), k.reshape(n, t, d_b), v.reshape(n, t, d_b),
                                  qi.reshape(n, t, -1), ki.reshape(n, t, -1), wi.reshape(n, t, -1),
                                  st["cache_k"], st["cache_v"], st["cache_kidx"], i,
                                  st["page_table"], wts["rel_bias"])
            xf = _mix_out(o.reshape(m, d_a), bonus, g, o_b.reshape(m, d_b), xf, g_m,
                          wts["e_lnx_w"][i], wts["e_lnx_b"][i], wts["e_w_out"][i], tm, rows_per_group)
            outs["shift"].append(p_a.reshape(n, t, -1)[:, -1].astype(x.dtype))
            outs["wkv"].append(_unpack_state(s_fin))
            outs["k"].append(k.reshape(n, t, n_heads, HEAD_DIM))
            outs["v"].append(v.reshape(n, t, n_heads, HEAD_DIM))
            outs["kidx"].append(ki.reshape(n, t, D_IDX))
            tm_f = _tile(t, 1024) if is_prompt else m
            ff = wts["ffn_w2"][i].shape[0]
            xf = _ffn(xf, norm_ffn, sh_f, sc_f, g_f, wts["ffn_w13"][i][None], wts["ffn_w2"][i][None],
                      tm_f, _tile(ff, 256) if ff % 256 == 0 else ff, rows_per_group, final_g=final_g)
        else:
            tables = _s5_tables(wts["o_a_re"][i], wts["o_a_im"][i], wts["o_log_step"][i], wts["o_b_re"][i],
                                wts["o_b_im"][i], wts["o_c_re"][i], wts["o_c_im"][i], wts["o_d"][i])
            u = _norm_mod_call(xf, norm_mix, sh_m, sc_m, tm, rows_per_group)
            h0r = st["ssm_re"][i].reshape(n, -1).astype(_f32)
            h0i = st["ssm_im"][i].reshape(n, -1).astype(_f32)
            if is_prompt:
                yg, hr, hi = _s5_scan(u.reshape(n, t, d), h0r, h0i, tables, _tile(t, 256))
            else:
                yg, hr, hi = _s5_step(u, h0r, h0i, tables)
            xf = _glu_out(yg.reshape(m, d), xf, g_m, wts["o_w_glu"][i], tm, rows_per_group)
            g_c = d // CH_G
            outs["re"].append(hr.reshape(n, g_c, P_C))
            outs["im"].append(hi.reshape(n, g_c, P_C))
            tm_f = _tile(t, 1024) if is_prompt else m
            ff = wts["moe_w2"][i].shape[1]
            moe = _moe_sparse if (is_prompt and tm_f == 1024) else _ffn
            xf = moe(xf, norm_ffn, sh_f, sc_f, g_f, wts["moe_w13"][i], wts["moe_w2"][i],
                     tm_f, _tile(ff, 512) if ff % 512 == 0 else ff, rows_per_group,
                     router=(wts["o_router_w"][i], wts["o_router_b"][i]), final_g=final_g)
    y = xf.reshape(n, t, d).astype(x.dtype)
    return (y, jnp.stack(outs["shift"]), jnp.stack(outs["wkv"]), jnp.stack(outs["k"]), jnp.stack(outs["v"]),
            jnp.stack(outs["kidx"]), jnp.stack(outs["re"]), jnp.stack(outs["im"]))


def kernel(x_prompt, x_sample, cache_k, cache_v, cache_kidx, state_shift, state_wkv, state_ssm_re, state_ssm_im, page_table, c_prompt, c_sample, norm_mix, norm_ffn, ada_w, ada_b, rel_bias, norm_final, e_w_in, e_mu, e_w0, e_w2, e_a0, e_a2, e_g2, e_k_k, e_k_a, e_r_k, e_lnx_w, e_lnx_b, e_w_out, e_ffn_w13, e_ffn_w2, o_a_re, o_a_im, o_log_step, o_b_re, o_b_im, o_c_re, o_c_im, o_d, o_w_glu, o_router_w, o_router_b, o_moe_w13, o_moe_w2):
    nb, _, d = x_prompt.shape
    n_dec = x_sample.shape[0]
    n_even = e_w_in.shape[0]
    n_odd = o_a_re.shape[0]
    d_a = d // 2
    n_heads = d_a // HEAD_DIM
    shift_w = state_shift.shape[-1]

    c_all = jnp.concatenate([c_prompt, c_sample], axis=0)
    pad = (-c_all.shape[0]) % SUBLANES
    mods = _ada(jnp.pad(c_all, ((0, pad), (0, 0))), ada_w, ada_b)
    mods_p = mods[:, :nb]
    mods_s = mods[:, nb:nb + n_dec]

    wts = dict(norm_mix=norm_mix, norm_ffn=norm_ffn, norm_final=norm_final, rel_bias=rel_bias,
               e_w_in=e_w_in, e_mu=e_mu, e_w0=e_w0, e_w2=e_w2, e_a0=e_a0, e_a2=e_a2, e_g2=e_g2, e_k_k=e_k_k,
               e_k_a=e_k_a, e_r_k=e_r_k, e_lnx_w=e_lnx_w, e_lnx_b=e_lnx_b, e_w_out=e_w_out,
               ffn_w13=_bf(e_ffn_w13), ffn_w2=_bf(e_ffn_w2),
               o_a_re=o_a_re, o_a_im=o_a_im, o_log_step=o_log_step, o_b_re=o_b_re, o_b_im=o_b_im,
               o_c_re=o_c_re, o_c_im=o_c_im, o_d=o_d, o_w_glu=o_w_glu, o_router_w=o_router_w,
               o_router_b=o_router_b, moe_w13=_bf(o_moe_w13), moe_w2=_bf(o_moe_w2))

    st_p = dict(shift=jnp.zeros((n_even, nb, shift_w), x_prompt.dtype),
                wkv=jnp.zeros((n_even, nb, n_heads, HEAD_DIM, HEAD_DIM), _f32),
                ssm_re=jnp.zeros((n_odd, nb, d // CH_G, P_C), _f32),
                ssm_im=jnp.zeros((n_odd, nb, d // CH_G, P_C), _f32))
    st_s = dict(shift=state_shift, wkv=state_wkv, ssm_re=state_ssm_re, ssm_im=state_ssm_im,
                cache_k=cache_k, cache_v=cache_v, cache_kidx=cache_kidx, page_table=page_table)
    out_p = _run_group(x_prompt, mods_p, True, st_p, wts)
    out_s = _run_group(x_sample, mods_s, False, st_s, wts)
    return (out_p[0], out_s[0]) + out_p[1:] + out_s[1:]
```

```python
import functools
import math

import jax
import jax.numpy as jnp
from jax import lax
from jax.experimental import pallas as pl
from jax.experimental.pallas import tpu as pltpu

HEAD_DIM = 64
LORA_W = 64
LORA_A = 64
LORA_G = 128
H_IDX = 8
D_IDX = 64
TOPK_MAX = 256
N_BUCKETS = 32
MAX_DIST = 128
CH_G = 16
P_C = 64
TOP_E = 2
EPS = 1e-6
LNX_EPS = 64e-5

LANES = 128
SUBLANES = 8
VMEM_LIMIT = 56 * 1024 * 1024
KEY_BLOCK = 256
COUNT_BLOCK = 1024
COUNT_ROWS = 64
NEG = -2.0e30
M_INIT = -1.0e30

_bf16 = jnp.bfloat16
_f32 = jnp.float32


def _bf(x):
    return x.astype(_bf16)


def _dot(a, b):
    return jnp.dot(a, b, preferred_element_type=_f32)


def _dot_nt(a, b):
    return lax.dot_general(a, b, (((1,), (1,)), ((), ())), preferred_element_type=_f32)


def _params(*sem):
    return pltpu.CompilerParams(dimension_semantics=sem, vmem_limit_bytes=VMEM_LIMIT)


def _split_bf16(x, parts):
    out = []
    for _ in range(parts - 1):
        hi = _bf(x)
        out.append(hi)
        x = x - hi.astype(_f32)
    out.append(_bf(x))
    return out


def _segsum(x, ones_bd):
    hi, mid, lo = _split_bf16(x, 3)
    return _dot(hi, ones_bd) + _dot(mid, ones_bd) + _dot(lo, ones_bd)


def _norm_mod(x, g, shift, scale):
    ms = jnp.mean(x * x, axis=-1, keepdims=True)
    return (x * lax.rsqrt(ms + EPS) * g) * (1.0 + scale) + shift


def _silu(x):
    return x * jax.nn.sigmoid(x)


def _gelu_tanh(x):
    return 0.5 * x * (1.0 + jnp.tanh(math.sqrt(2.0 / math.pi) * (x + 0.044715 * (x * x * x))))


def _softplus(x):
    return jnp.maximum(x, 0.0) + jnp.log(1.0 + jnp.exp(-jnp.abs(x)))


def _rel_bucket(dist):
    max_exact = N_BUCKETS // 2
    n = jnp.maximum(dist, 0)
    nf = jnp.maximum(n, 1).astype(_f32)
    large = max_exact + (jnp.log(nf / max_exact) / math.log(MAX_DIST / max_exact) * (N_BUCKETS - max_exact)).astype(jnp.int32)
    return jnp.where(n < max_exact, n, jnp.minimum(large, N_BUCKETS - 1))


def _group_spec(rows_per_block, rows_per_group, width):
    def spec(r):
        return pl.BlockSpec((1, r, width), lambda i, *_: ((i * rows_per_block) // rows_per_group, 0, 0))
    return spec


def _ada_body(c_ref, w_ref, b_ref, o_ref):
    o_ref[0] = _dot(_bf(_silu(c_ref[...])), _bf(w_ref[0])) + b_ref[0]


def _ada(c, ada_w, ada_b):
    depth, d, n6 = ada_w.shape
    rows = c.shape[0]
    tn = n6 // 4
    return pl.pallas_call(
        _ada_body,
        grid=(depth, n6 // tn),
        in_specs=[pl.BlockSpec((rows, d), lambda l, j: (0, 0)),
                  pl.BlockSpec((1, d, tn), lambda l, j: (l, 0, j)),
                  pl.BlockSpec((1, 1, tn), lambda l, j: (l, 0, j))],
        out_specs=pl.BlockSpec((1, rows, tn), lambda l, j: (l, 0, j)),
        out_shape=jax.ShapeDtypeStruct((depth, rows, n6), _f32),
        compiler_params=_params("arbitrary", "arbitrary"),
        name="ada_mod",
    )(c, ada_w, ada_b.reshape(depth, 1, n6))


def _norm_mod_body(x_ref, g_ref, sh_ref, sc_ref, o_ref):
    o_ref[...] = _norm_mod(x_ref[...], g_ref[...], sh_ref[0], sc_ref[0])


def _norm_mod_call(x, g, shift, scale, tm, rows_per_group):
    m, d = x.shape
    gs = _group_spec(tm, rows_per_group, d)
    return pl.pallas_call(
        _norm_mod_body,
        grid=(m // tm,),
        in_specs=[pl.BlockSpec((tm, d), lambda i: (i, 0)),
                  pl.BlockSpec((1, d), lambda i: (0, 0)),
                  gs(shift.shape[1]), gs(scale.shape[1])],
        out_specs=pl.BlockSpec((tm, d), lambda i: (i, 0)),
        out_shape=jax.ShapeDtypeStruct((m, d), _f32),
        compiler_params=_params("arbitrary"),
        name="norm_mod",
    )(x, g, shift, scale)


def _in_proj_body(x_ref, g_ref, sh_ref, sc_ref, wa_ref, wq_ref, wk_ref, wv_ref, wqi_ref, wkw_ref,
                  pa_ref, q_ref, k_ref, v_ref, qi_ref, ki_ref, wi_ref):
    h = _bf(_norm_mod(x_ref[...], g_ref[...], sh_ref[0], sc_ref[0]))
    pa_ref[...] = _dot(h, wa_ref[...])
    q_ref[...] = _bf(_dot(h, wq_ref[...]))
    k_ref[...] = _dot(h, wk_ref[...])
    v_ref[...] = _dot(h, wv_ref[...])
    qi_ref[...] = _bf(_dot(h, wqi_ref[...]))
    kw = _dot(h, wkw_ref[...])
    ki_ref[...] = kw[:, :D_IDX]
    wi_ref[...] = kw[:, D_IDX:D_IDX + H_IDX]


def _in_proj(x, g, shift, scale, w_in, tm, rows_per_group):
    m, d = x.shape
    d_a = d // 2
    d_b = d // 2
    shift_w = 3 * d_a + LORA_W + LORA_A + LORA_G
    c1 = shift_w + 3 * d_b
    c2 = c1 + H_IDX * D_IDX
    wa = _bf(w_in[:, :shift_w])
    wq = _bf(w_in[:, shift_w:shift_w + d_b])
    wk = _bf(w_in[:, shift_w + d_b:shift_w + 2 * d_b])
    wv = _bf(w_in[:, shift_w + 2 * d_b:c1])
    wqi = _bf(w_in[:, c1:c2])
    wkw = _bf(jnp.pad(w_in[:, c2:], ((0, 0), (0, LANES - D_IDX - H_IDX))))
    gs = _group_spec(tm, rows_per_group, d)
    full = lambda a: pl.BlockSpec(a.shape, lambda i: (0, 0))
    row = lambda w: pl.BlockSpec((tm, w), lambda i: (i, 0))
    widths = [(shift_w, _f32), (d_b, _bf16), (d_b, _f32), (d_b, _f32), (H_IDX * D_IDX, _bf16),
              (D_IDX, _f32), (H_IDX, _f32)]
    return pl.pallas_call(
        _in_proj_body,
        grid=(m // tm,),
        in_specs=[row(d), pl.BlockSpec((1, d), lambda i: (0, 0)), gs(shift.shape[1]), gs(scale.shape[1]),
                  full(wa), full(wq), full(wk), full(wv), full(wqi), full(wkw)],
        out_specs=[row(w) for w, _ in widths],
        out_shape=[jax.ShapeDtypeStruct((m, w), dt) for w, dt in widths],
        compiler_params=_params("arbitrary"),
        name="in_proj",
    )(x, g, shift, scale, wa, wq, wk, wv, wqi, wkw)


def _rwkv_prep_body(seq_is_one, rows_per_group, tm,
                    p_ref, pprev_ref, prev0_ref, mu_ref, w0_ref, a0_ref, kk_ref, ka_ref, rk_ref,
                    wwa_ref, g2_ref, ones_ref,
                    r_ref, w_ref, k_ref, v_ref, av_ref, bv_ref, g_ref, bonus_ref):
    d_a = r_ref.shape[1]
    p = p_ref[...]
    if seq_is_one:
        p_prev = prev0_ref[0]
    else:
        i = pl.program_id(0)
        first = (i * tm) % rows_per_group == 0
        prev_row = jnp.where(first, prev0_ref[0], pprev_ref[SUBLANES - 1:SUBLANES, :])
        rolled = pltpu.roll(p, 1, axis=0)
        row_id = lax.broadcasted_iota(jnp.int32, p.shape, 0)
        p_prev = jnp.where(row_id == 0, prev_row, rolled)
    ps = p + (p_prev - p) * mu_ref[...]
    r = ps[:, :d_a]
    k = ps[:, d_a:2 * d_a]
    v = ps[:, 2 * d_a:3 * d_a]
    xwa = ps[:, 3 * d_a:3 * d_a + LORA_W + LORA_A]
    xg = ps[:, 3 * d_a + LORA_W + LORA_A:]
    lane = lax.broadcasted_iota(jnp.int32, xwa.shape, 1)
    xwa = jnp.where(lane < LORA_W, jnp.tanh(xwa), xwa)
    lwa = _dot(_bf(xwa), wwa_ref[...])
    w_log = -_softplus(-(w0_ref[...] + lwa[:, :d_a])) - 0.5
    decay = jnp.exp(-jnp.exp(w_log))
    a = jax.nn.sigmoid(a0_ref[...] + lwa[:, d_a:])
    g_ref[...] = _dot(_bf(jax.nn.sigmoid(xg)), g2_ref[...])
    ones_bd = ones_ref[...]
    kk = k * kk_ref[...]
    kk = kk / jnp.maximum(jnp.sqrt(_segsum(kk * kk, ones_bd)), 1e-12)
    k_mod = k * (1.0 + (a - 1.0) * ka_ref[...])
    r_ref[...] = r
    w_ref[...] = decay
    k_ref[...] = k_mod
    v_ref[...] = v
    av_ref[...] = -kk
    bv_ref[...] = kk * a
    bonus_ref[...] = _segsum(r * k_mod * rk_ref[...], ones_bd) * v


def _ones_blockdiag(n, group):
    idx = jnp.arange(n, dtype=jnp.int32) // group
    return (idx[:, None] == idx[None, :]).astype(_bf16)


def _rwkv_prep(p_a, prev0, mu, w0, w2, a0, a2, g2, k_k, k_a, r_k, tm, rows_per_group):
    m, shift_w = p_a.shape
    d_a = w0.shape[-1]
    seq_is_one = rows_per_group == tm and prev0.shape[1] == tm
    wwa = jnp.zeros((LORA_W + LORA_A, 2 * d_a), _f32)
    wwa = _bf(wwa.at[:LORA_W, :d_a].set(w2).at[LORA_W:, d_a:].set(a2))
    ones_bd = _ones_blockdiag(d_a, HEAD_DIM)
    vec = lambda a: a.reshape(1, -1)
    gs = _group_spec(tm, rows_per_group, shift_w)
    full = lambda a: pl.BlockSpec(a.shape, lambda i: (0, 0))
    row = pl.BlockSpec((tm, d_a), lambda i: (i, 0))
    ins = [p_a, p_a, prev0, vec(mu), vec(w0), vec(a0), vec(k_k), vec(k_a), vec(r_k), wwa, _bf(g2), ones_bd]
    in_specs = [pl.BlockSpec((tm, shift_w), lambda i: (i, 0)),
                pl.BlockSpec((SUBLANES, shift_w), lambda i: (jnp.maximum(i * (tm // SUBLANES) - 1, 0), 0)),
                gs(prev0.shape[1])] + [full(a) for a in ins[3:]]
    return pl.pallas_call(
        functools.partial(_rwkv_prep_body, seq_is_one, rows_per_group, tm),
        grid=(m // tm,),
        in_specs=in_specs,
        out_specs=[row] * 8,
        out_shape=[jax.ShapeDtypeStruct((m, d_a), _f32)] * 8,
        compiler_params=_params("arbitrary"),
        name="rwkv_prep",
    )(*ins)


def _rwkv_scan_body(nb, n_pairs, tc,
                    r_ref, w_ref, k_ref, v_ref, av_ref, bv_ref, s0_ref, ones_ref, expand_ref, eye_ref,
                    o_ref, s_ref):
    t_blk = pl.program_id(1)

    @pl.when(t_blk == 0)
    def _():
        s_ref[...] = s0_ref[...]

    ones_k = ones_ref[...]
    parts = ones_k.shape[0] // LANES
    expand = expand_ref[...]
    eye = eye_ref[...] > 0.5
    units = [(n, hp) for n in range(nb) for hp in range(n_pairs)]

    def rowsum(xs, ones, parts):
        lhs = jnp.concatenate([jnp.concatenate(_split_bf16(x, parts), axis=1) for x in xs], axis=0)
        res = _dot(lhs, ones)
        return [res[u * HEAD_DIM:(u + 1) * HEAD_DIM] for u in range(len(xs))]

    pair_out = parts == 2 and len(units) % 2 == 0
    if pair_out:
        blk = ones_k[:LANES]
        zero = jnp.zeros_like(blk)
        ones_pair = jnp.concatenate([jnp.concatenate([blk, zero], axis=1), jnp.concatenate([zero, blk], axis=1)], axis=0)

    def rowsum_out(xs):
        if not pair_out:
            return rowsum(xs, ones_k, parts)
        lhs = jnp.concatenate([jnp.concatenate([_bf(xs[2 * i]), _bf(xs[2 * i + 1])], axis=1)
                               for i in range(len(xs) // 2)], axis=0)
        res = _dot(lhs, ones_pair)
        out = []
        for i in range(len(xs) // 2):
            both = res[i * HEAD_DIM:(i + 1) * HEAD_DIM]
            out += [both[:, :LANES], both[:, LANES:]]
        return out

    sub = min(SUBLANES, tc)
    row_id = lax.broadcasted_iota(jnp.int32, (sub, LANES), 0)

    def tile_steps(j, carry):
        t0 = pl.multiple_of(j * sub, sub)
        tiles = {}
        for (n, hp) in units:
            sl = pl.ds(hp * LANES, LANES)
            tiles[(n, hp)] = tuple(ref[n, pl.ds(t0, sub), sl] for ref in (r_ref, w_ref, k_ref, v_ref, av_ref, bv_ref))
        states = [s_ref[n, hp] for (n, hp) in units]
        o_tiles = [jnp.zeros((sub, LANES), _f32) for _ in units]
        vcols = []
        for u in units:
            vt = tiles[u][3]
            by_head = jnp.concatenate([vt[:, :HEAD_DIM], vt[:, HEAD_DIM:]], axis=0)
            lhs = jnp.concatenate(_split_bf16(by_head, 3), axis=0)
            vcols.append(lax.dot_general(lhs, expand, (((0,), (0,)), ((), ())), preferred_element_type=_f32))
        for tt in range(sub):
            rows = {u: tuple(x[tt:tt + 1, :] for x in tiles[u]) for u in units}
            sa = rowsum([s * rows[u][4] for s, u in zip(states, units)], ones_k, parts)
            for idx, u in enumerate(units):
                r_t, w_t, k_t, v_t, av_t, bv_t = rows[u]
                vcol = vcols[idx][:, tt * LANES:(tt + 1) * LANES]
                states[idx] = states[idx] * w_t + sa[idx] * bv_t + vcol * k_t
            ob = rowsum_out([s * rows[u][0] for s, u in zip(states, units)])
            for idx in range(len(units)):
                o_row = jnp.sum(jnp.where(eye, ob[idx], 0.0), axis=0, keepdims=True)
                o_tiles[idx] = jnp.where(row_id == tt, jnp.broadcast_to(o_row, (sub, LANES)), o_tiles[idx])
        for idx, (n, hp) in enumerate(units):
            s_ref[n, hp] = states[idx]
            o_ref[n, pl.ds(t0, sub), pl.ds(hp * LANES, LANES)] = o_tiles[idx]
        return carry

    lax.fori_loop(0, tc // sub, tile_steps, 0)


def _rwkv_scan(r, w, k, v, av, bv, s0, nb, tc):
    n, t, d_a = r.shape
    n_pairs = d_a // LANES
    half = (jnp.arange(LANES, dtype=jnp.int32) // HEAD_DIM)
    ones_blk = (half[:, None] == half[None, :]).astype(_bf16)
    ones_k = jnp.concatenate([ones_blk] * (3 if t == 1 else 2), axis=0)
    sub = min(SUBLANES, tc)
    src_h = jnp.arange(2 * sub, dtype=jnp.int32) // sub
    src_t = jnp.arange(2 * sub, dtype=jnp.int32) % sub
    dst_t = jnp.arange(sub * LANES, dtype=jnp.int32) // LANES
    dst_h = (jnp.arange(sub * LANES, dtype=jnp.int32) % LANES) // HEAD_DIM
    expand = ((src_t[:, None] == dst_t[None, :]) & (src_h[:, None] == dst_h[None, :])).astype(_bf16)
    expand = jnp.concatenate([expand] * 3, axis=0)
    eye =(jnp.arange(HEAD_DIM, dtype=jnp.int32)[:, None]
           == (jnp.arange(LANES, dtype=jnp.int32) % HEAD_DIM)[None, :]).astype(_f32)
    seq = pl.BlockSpec((nb, tc, d_a), lambda b, j: (b, j, 0))
    st = pl.BlockSpec((nb, n_pairs, HEAD_DIM, LANES), lambda b, j: (b, 0, 0, 0))
    full = lambda a: pl.BlockSpec(a.shape, lambda b, j: (0, 0))
    return pl.pallas_call(
        functools.partial(_rwkv_scan_body, nb, n_pairs, tc),
        grid=(n // nb, t // tc),
        in_specs=[seq] * 6 + [st, full(ones_k), full(expand), full(eye)],
        out_specs=[seq, st],
        out_shape=[jax.ShapeDtypeStruct((n, t, d_a), _f32),
                   jax.ShapeDtypeStruct(s0.shape, _f32)],
        compiler_params=_params("arbitrary", "arbitrary"),
        name="rwkv_scan",
    )(r, w, k, v, av, bv, s0, ones_k, expand, eye)


def _pack_state(s):
    n, h = s.shape[:2]
    return s.reshape(n, h // 2, 2, HEAD_DIM, HEAD_DIM).transpose(0, 1, 3, 2, 4).reshape(n, h // 2, HEAD_DIM, LANES)


def _unpack_state(s):
    n, hp = s.shape[:2]
    return s.reshape(n, hp, HEAD_DIM, 2, HEAD_DIM).transpose(0, 1, 3, 2, 4).reshape(n, hp * 2, HEAD_DIM, HEAD_DIM)


def _mix_out_body(o_ref, bonus_ref, g_ref, ob_ref, x_ref, gate_ref, lw_ref, lb_ref, ones_ref, wa_ref, wb_ref,
                  out_ref):
    ones_bd = ones_ref[...]
    o = o_ref[...]
    inv = 1.0 / HEAD_DIM
    mean = _segsum(o, ones_bd) * inv
    dlt = o - mean
    var = _segsum(dlt * dlt, ones_bd) * inv
    on = dlt * lax.rsqrt(var + LNX_EPS) * lw_ref[...] + lb_ref[...]
    oa = (on + bonus_ref[...]) * g_ref[...]
    y = _dot(_bf(oa), wa_ref[...]) + _dot(ob_ref[...], wb_ref[...])
    out_ref[...] = x_ref[...] + gate_ref[0] * y


def _mix_out(o, bonus, g, o_b, x, gate, lnx_w, lnx_b, w_out, tm, rows_per_group):
    m, d = x.shape
    d_a = o.shape[1]
    ones_bd = _ones_blockdiag(d_a, HEAD_DIM)
    wa = _bf(w_out[:d_a])
    wb = _bf(w_out[d_a:])
    gs = _group_spec(tm, rows_per_group, d)
    full = lambda a: pl.BlockSpec(a.shape, lambda i: (0, 0))
    rowa = pl.BlockSpec((tm, d_a), lambda i: (i, 0))
    rowd = pl.BlockSpec((tm, d), lambda i: (i, 0))
    lw = lnx_w.reshape(1, -1)
    lb = lnx_b.reshape(1, -1)
    return pl.pallas_call(
        _mix_out_body,
        grid=(m // tm,),
        in_specs=[rowa, rowa, rowa, pl.BlockSpec((tm, o_b.shape[1]), lambda i: (i, 0)), rowd, gs(gate.shape[1]),
                  full(lw), full(lb), full(ones_bd), full(wa), full(wb)],
        out_specs=rowd,
        out_shape=jax.ShapeDtypeStruct((m, d), _f32),
        compiler_params=_params("arbitrary"),
        name="mix_out",
    )(o, bonus, g, o_b, x, gate, lw, lb, ones_bd, wa, wb)


def _key_to_float(u):
    key = u ^ jnp.int32(-2147483648)
    bits = jnp.where(key >= 0, key, key ^ jnp.int32(0x7FFFFFFF))
    return lax.bitcast_convert_type(bits, _f32)


def _kth_largest(count_ge, k_row, shape):
    def body(it, u):
        bit = jnp.left_shift(jnp.int32(1), 31 - it)
        cand_u = u | bit
        ok = count_ge(_key_to_float(cand_u)) >= k_row
        return jnp.where(ok, cand_u, u)

    u = lax.fori_loop(0, 32, body, jnp.zeros(shape, jnp.int32))
    return _key_to_float(u)


def _tie_cut(count_eq_lt, budget, n_bits, shape):
    def body(it, c):
        cand = c | jnp.left_shift(jnp.int32(1), n_bits - 1 - it)
        ok = count_eq_lt(cand) <= budget
        return jnp.where(ok, cand, c)

    return lax.fori_loop(0, n_bits, body, jnp.zeros(shape, jnp.int32))


def _dsa_prompt_body(n_heads, k_sel, t_len,
                     far_ref, qt_ref, qit_ref, wit_ref, k_ref, vt_ref, ki_ref, near_ref,
                     ot_ref, sc_ref, cut_ref, qz_ref, m_ref, l_ref, acc_ref):
    qb = qt_ref.shape[2]
    i = pl.program_id(1)
    q0 = i * qb
    qpos = q0 + lax.broadcasted_iota(jnp.int32, (1, qb), 1)
    n_kb = (q0 + qb + KEY_BLOCK - 1) // KEY_BLOCK
    row_kb = lax.broadcasted_iota(jnp.int32, (KEY_BLOCK, qb), 0)

    @pl.when(i == 0)
    def _():
        sc_ref[...] = jnp.full(sc_ref.shape, -jnp.inf, _f32)

    w8 = (wit_ref[0] * (H_IDX ** -0.5)) * (D_IDX ** -0.5)

    qi_all = jnp.concatenate([qit_ref[0, h * D_IDX:(h + 1) * D_IDX, :] for h in range(H_IDX)], axis=1)

    def score_block(kb, carry):
        c0 = pl.multiple_of(kb * KEY_BLOCK, KEY_BLOCK)
        s_all = _dot(ki_ref[0, pl.ds(c0, KEY_BLOCK), :], qi_all)
        acc = w8[0:1, :] * jnp.maximum(s_all[:, :qb], 0.0)
        for h in range(1, H_IDX):
            acc = acc + w8[h:h + 1, :] * jnp.maximum(s_all[:, h * qb:(h + 1) * qb], 0.0)
        sc_ref[pl.ds(c0, KEY_BLOCK), :] = jnp.where(c0 + row_kb <= qpos, acc, -jnp.inf)
        return carry

    lax.fori_loop(0, n_kb, score_block, 0)

    cb = min(COUNT_BLOCK, t_len)
    n_cb = (q0 + qb + cb - 1) // cb
    row_cb = lax.broadcasted_iota(jnp.int32, (cb, qb), 0)

    def count(pred):
        def body(kb, acc):
            c0 = pl.multiple_of(kb * cb, cb)
            hit = jnp.where(pred(sc_ref[pl.ds(c0, cb), :], c0 + row_cb), 1.0, 0.0)
            for j in range(cb // COUNT_ROWS):
                acc = acc + hit[j * COUNT_ROWS:(j + 1) * COUNT_ROWS, :]
            return acc
        acc = lax.fori_loop(0, n_cb, body, jnp.zeros((COUNT_ROWS, qb), _f32))
        return jnp.sum(acc, axis=0, keepdims=True)

    k_row = jnp.minimum(k_sel, qpos + 1).astype(_f32)
    thr = _kth_largest(lambda cand: count(lambda s, pos: s >= cand), k_row, (1, qb))
    n_gt = count(lambda s, pos: s > thr)
    n_eq = count(lambda s, pos: s == thr)
    budget = k_row - n_gt
    cut_ref[...] = jnp.full((1, qb), 2 * t_len, jnp.int32)

    @pl.when(jnp.max(n_eq - budget) > 0.5)
    def _():
        cut_ref[...] = _tie_cut(lambda c: count(lambda s, pos: (s == thr) & (pos < c)), budget,
                                (2 * t_len).bit_length(), (1, qb))

    cut = cut_ref[...]

    m_ref[...] = jnp.full(m_ref.shape, M_INIT, _f32)
    l_ref[...] = jnp.zeros(l_ref.shape, _f32)
    acc_ref[...] = jnp.zeros(acc_ref.shape, _f32)
    pair_row = lax.broadcasted_iota(jnp.int32, (LANES, qb), 0) // HEAD_DIM
    for hp in range(n_heads // 2):
        qp = qt_ref[0, hp * LANES:(hp + 1) * LANES, :] * (HEAD_DIM ** -0.5)
        zero = jnp.zeros_like(qp)
        qz_ref[hp] = jnp.concatenate([jnp.where(pair_row == 0, qp, zero), jnp.where(pair_row == 1, qp, zero)], axis=1)

    def attend(c0, width, bias_of_head):
        scb = sc_ref[pl.ds(c0, width), :]
        pos = c0 + lax.broadcasted_iota(jnp.int32, (width, qb), 0)
        sel = (scb > thr) | ((scb == thr) & (pos < cut))
        m_old = [m_ref[h] for h in range(n_heads)]
        l_old = [l_ref[h] for h in range(n_heads)]
        a_old = [acc_ref[h] for h in range(n_heads)]
        s2 = [_dot(k_ref[0, pl.ds(c0, width), hp * LANES:(hp + 1) * LANES], qz_ref[hp]) for hp in range(n_heads // 2)]
        m_out, l_out, a_out = [], [], []
        for h in range(n_heads):
            s = s2[h // 2][:, (h % 2) * qb:(h % 2 + 1) * qb] + bias_of_head(h)
            s = jnp.where(sel, s, NEG)
            m_new = jnp.maximum(m_old[h], jnp.max(s, axis=0, keepdims=True))
            alpha = jnp.exp(m_old[h] - m_new)
            p = jnp.exp(s - m_new)
            l_out.append(alpha * l_old[h] + jnp.sum(p, axis=0, keepdims=True))
            pv = _dot(vt_ref[0, h * HEAD_DIM:(h + 1) * HEAD_DIM, pl.ds(c0, width)], _bf(p))
            a_out.append(alpha * a_old[h] + pv)
            m_out.append(m_new)
        for h in range(n_heads):
            m_ref[h] = m_out[h]
            l_ref[h] = l_out[h]
            acc_ref[h] = a_out[h]

    n_far = jnp.maximum(i - 1, 0) // (KEY_BLOCK // qb)

    def far_block(kb, carry):
        attend(pl.multiple_of(kb * KEY_BLOCK, KEY_BLOCK), KEY_BLOCK, lambda h: far_ref[h])
        return carry

    lax.fori_loop(0, n_far, far_block, 0)

    def near_block(j, carry):
        attend(pl.multiple_of(j * qb, qb), qb, lambda h: near_ref[i - j, h])
        return carry

    lax.fori_loop(n_far * (KEY_BLOCK // qb), i + 1, near_block, 0)

    for h in range(n_heads):
        ot_ref[0, h * HEAD_DIM:(h + 1) * HEAD_DIM, :] = _bf(acc_ref[h] / l_ref[h])


def _dsa_prompt(q, k, v, qi, ki, wi, rel_bias, qb):
    n, t, d_b = q.shape
    n_heads = d_b // HEAD_DIM
    k_sel = min(TOPK_MAX, t // 4)
    tr = lambda a: a.transpose(0, 2, 1)
    n_near = KEY_BLOCK // qb + 1
    qq = jnp.arange(qb, dtype=jnp.int32)
    dist = (jnp.arange(n_near, dtype=jnp.int32)[:, None, None] * qb + qq[None, None, :] - qq[None, :, None])
    onehot = (_rel_bucket(dist)[..., None] == jnp.arange(N_BUCKETS, dtype=jnp.int32)).astype(_f32)
    near = jnp.einsum('dkqb,bh->dhkq', onehot, rel_bias, precision=lax.Precision.HIGHEST)
    far = rel_bias[N_BUCKETS - 1]
    res = lambda shape: pl.BlockSpec(shape, lambda b, i, *_: (b, 0, 0))
    blk = lambda w: pl.BlockSpec((1, w, qb), lambda b, i, *_: (b, 0, i))
    grid_spec = pltpu.PrefetchScalarGridSpec(
        num_scalar_prefetch=0,
        grid=(n, t // qb),
        in_specs=[pl.BlockSpec(memory_space=pltpu.SMEM),
                  blk(d_b), blk(H_IDX * D_IDX), blk(H_IDX),
                  res((1, t, d_b)), res((1, d_b, t)), res((1, t, D_IDX)),
                  pl.BlockSpec(near.shape, lambda b, i, *_: (0, 0, 0, 0))],
        out_specs=blk(d_b),
        scratch_shapes=[pltpu.VMEM((t, qb), _f32), pltpu.VMEM((1, qb), jnp.int32),
                        pltpu.VMEM((n_heads // 2, LANES, 2 * qb), _bf16),
                        pltpu.VMEM((n_heads, 1, qb), _f32), pltpu.VMEM((n_heads, 1, qb), _f32),
                        pltpu.VMEM((n_heads, HEAD_DIM, qb), _f32)],
    )
    o_t = pl.pallas_call(
        functools.partial(_dsa_prompt_body, n_heads, k_sel, t),
        grid_spec=grid_spec,
        out_shape=jax.ShapeDtypeStruct((n, d_b, t), _bf16),
        compiler_params=_params("arbitrary", "arbitrary"),
        name="dsa_prompt",
    )(far, tr(q), tr(qi), tr(wi), _bf(k), tr(_bf(v)), _bf(ki), near)
    return tr(o_t)


def _dsa_sample_index_body(n_pages, page, pg, pt_ref, qi_ref, wi_ref, kin_ref, *rest):
    cki_refs, (sc_ref, snew_ref) = rest[:pg], rest[pg:]
    step = pl.program_id(1)
    w8 = _bf(wi_ref[0] * (H_IDX ** -0.5)).astype(_f32)
    relu_bf = lambda s: _bf(jnp.maximum(s, 0.0)).astype(_f32)
    qi8 = qi_ref[0]
    for j in range(pg):
        c0 = pl.multiple_of((step * pg + j) * page, page)
        s8 = _dot(qi8, _bf(cki_refs[j][0, 0])) * (D_IDX ** -0.5)
        sc_ref[0, :, pl.ds(c0, page)] = jnp.sum(w8 * relu_bf(s8), axis=0, keepdims=True)

    @pl.when(step == n_pages // pg - 1)
    def _():
        s_new8 = jnp.sum(qi8.astype(_f32) * _bf(kin_ref[0]).astype(_f32), axis=1, keepdims=True) * (D_IDX ** -0.5)
        snew_ref[0] = jnp.sum(w8 * relu_bf(s_new8), axis=0, keepdims=True)


def _dsa_sample_select_body(k_sel, sc_ref, snew_ref, thr_ref, cut_ref):
    sc = sc_ref[...]
    s_new = snew_ref[...]
    rows, past = sc.shape
    pos = lax.broadcasted_iota(jnp.int32, sc.shape, 1)

    def count(pred):
        hits = jnp.sum(jnp.where(pred(sc, pos), 1.0, 0.0), axis=1, keepdims=True)
        return hits + jnp.where(pred(s_new, past), 1.0, 0.0)

    k_row = jnp.full((rows, 1), float(k_sel), _f32)
    thr = _kth_largest(lambda cand: count(lambda s, ps: s >= cand), k_row, (rows, 1))
    budget = k_row - count(lambda s, ps: s > thr)
    thr_ref[...] = thr
    cut_ref[...] = _tie_cut(lambda c: count(lambda s, ps: (s == thr) & (ps < c)), budget,
                            (2 * (past + 1)).bit_length(), (rows, 1))


def _dsa_sample_probs_body(n_heads, n_pages, page, pg, pt_ref, q_ref, kn_ref, sc_ref, snew_ref, thr_ref, cut_ref, *rest):
    ck_refs = rest[:pg]
    btab_ref, bnew_ref, p_ref, pnew_ref, lg_ref = rest[pg:]
    step = pl.program_id(1)
    past = n_pages * page
    q8 = q_ref[0]

    for j in range(pg):
        c0 = pl.multiple_of((step * pg + j) * page, page)
        rows = [_dot(q8, _bf(ck_refs[j][0, 0, h]))[h:h + 1] for h in range(n_heads)]
        lg = jnp.concatenate(rows, axis=0) * (HEAD_DIM ** -0.5)
        lg_ref[:, pl.ds(c0, page)] = lg + btab_ref[:, pl.ds(c0, page)]

    @pl.when(step == n_pages // pg - 1)
    def _():
        sc = sc_ref[0]
        s_new = snew_ref[0]
        thr = thr_ref[0]
        cut = cut_ref[0]
        pos = lax.broadcasted_iota(jnp.int32, sc.shape, 1)
        sel = (sc > thr) | ((sc == thr) & (pos < cut))
        sel_new = (s_new > thr) | ((s_new == thr) & (past < cut))

        kn = _bf(kn_ref[0]).astype(_f32)
        lg_new = jnp.sum(q8.astype(_f32) * kn, axis=1, keepdims=True) * (HEAD_DIM ** -0.5) + bnew_ref[...]
        lg_new = jnp.where(sel_new, lg_new, NEG)
        s_all = jnp.where(sel, lg_ref[...], NEG)
        m = jnp.maximum(jnp.maximum(jnp.max(s_all, axis=1, keepdims=True), lg_new), M_INIT)
        pr = jnp.exp(s_all - m)
        pr_new = jnp.exp(lg_new - m)
        l = jnp.sum(pr, axis=1, keepdims=True) + pr_new
        p_ref[0] = pr
        col = lax.broadcasted_iota(jnp.int32, (n_heads, 2), 1)
        pnew_ref[0] = jnp.where(col == 0, pr_new, l)


def _dsa_sample_pv_body(n_heads, n_pages, page, pg, pt_ref, p_ref, pnew_ref, vn_ref, *rest):
    cv_refs, (o_ref, acc_ref) = rest[:pg], rest[pg:]
    step = pl.program_id(1)

    @pl.when(step == 0)
    def _():
        acc_ref[...] = pnew_ref[0][:, 0:1] * _bf(vn_ref[0]).astype(_f32)

    acc = acc_ref[...]
    for j in range(pg):
        c0 = pl.multiple_of((step * pg + j) * page, page)
        pb = _bf(p_ref[0, :, pl.ds(c0, page)])
        rows = [_dot_nt(pb, _bf(cv_refs[j][0, 0, h]))[h:h + 1] for h in range(n_heads)]
        acc = acc + jnp.concatenate(rows, axis=0)
    acc_ref[...] = acc

    @pl.when(step == n_pages // pg - 1)
    def _():
        o_ref[0] = _bf(acc / pnew_ref[0][:, 1:2])


def _dsa_sample(q, k_new, v_new, qi, ki_new, wi, ck, cv, cki, layer, page_table, rel_bias):
    b, _, d_b = q.shape
    n_heads = d_b // HEAD_DIM
    n_pages = page_table.shape[1]
    page = ck.shape[2]
    ck, cv, cki = (jnp.moveaxis(a, 2, -1) for a in (ck, cv, cki))
    past = n_pages * page
    k_sel = min(TOPK_MAX, (past + 1) // 4)
    pg = max(g for g in (1, 2, 4, 8) if n_pages % g == 0)
    kpos = jnp.arange(past, dtype=jnp.int32)
    btab = rel_bias[_rel_bucket(past - kpos)].T
    bnew = rel_bias[_rel_bucket(jnp.zeros((1,), jnp.int32))].T
    heads = lambda a: a.reshape(b, n_heads, HEAD_DIM)
    pt = page_table.reshape(-1)
    per_b = lambda shape: pl.BlockSpec((1,) + shape, lambda i, s, pt: (i,) + (0,) * len(shape))
    const = lambda a: pl.BlockSpec(a.shape, lambda i, s, pt: (0,) * a.ndim)

    def paged(tail, j):
        return pl.BlockSpec((1, 1) + tail + (page,),
                            lambda i, s, pt: (layer, pt[i * n_pages + s * pg + j]) + (0,) * (len(tail) + 1))

    grid = (b, n_pages // pg)
    scores, s_new = pl.pallas_call(
        functools.partial(_dsa_sample_index_body, n_pages, page, pg),
        grid_spec=pltpu.PrefetchScalarGridSpec(
            num_scalar_prefetch=1, grid=grid,
            in_specs=[per_b((H_IDX, D_IDX)), per_b((H_IDX, 1)), per_b((1, D_IDX))]
                     + [paged((D_IDX,), j) for j in range(pg)],
            out_specs=[per_b((1, past)), per_b((1, 1))]),
        out_shape=[jax.ShapeDtypeStruct((b, 1, past), _f32), jax.ShapeDtypeStruct((b, 1, 1), _f32)],
        compiler_params=_params("arbitrary", "arbitrary"),
        name="dsa_sample_index",
    )(pt, qi.reshape(b, H_IDX, D_IDX), wi.reshape(b, H_IDX, 1), ki_new, *([cki] * pg))
    thr, cut = pl.pallas_call(
        functools.partial(_dsa_sample_select_body, k_sel),
        out_shape=[jax.ShapeDtypeStruct((b, 1), _f32), jax.ShapeDtypeStruct((b, 1), jnp.int32)],
        compiler_params=pltpu.CompilerParams(vmem_limit_bytes=VMEM_LIMIT),
        name="dsa_sample_select",
    )(scores.reshape(b, past), s_new.reshape(b, 1))
    probs, p_new = pl.pallas_call(
        functools.partial(_dsa_sample_probs_body, n_heads, n_pages, page, pg),
        grid_spec=pltpu.PrefetchScalarGridSpec(
            num_scalar_prefetch=1, grid=grid,
            in_specs=[per_b((n_heads, HEAD_DIM)), per_b((n_heads, HEAD_DIM)), per_b((1, past)), per_b((1, 1)),
                      per_b((1, 1)), per_b((1, 1))]
                     + [paged((n_heads, HEAD_DIM), j) for j in range(pg)]
                     + [const(btab), const(bnew)],
            out_specs=[per_b((n_heads, past)), per_b((n_heads, 2))],
            scratch_shapes=[pltpu.VMEM((n_heads, past), _f32)]),
        out_shape=[jax.ShapeDtypeStruct((b, n_heads, past), _f32), jax.ShapeDtypeStruct((b, n_heads, 2), _f32)],
        compiler_params=_params("arbitrary", "arbitrary"),
        name="dsa_sample_probs",
    )(pt, heads(q), heads(k_new), scores, s_new, thr.reshape(b, 1, 1), cut.reshape(b, 1, 1),
      *([ck] * pg), btab, bnew)
    out = pl.pallas_call(
        functools.partial(_dsa_sample_pv_body, n_heads, n_pages, page, pg),
        grid_spec=pltpu.PrefetchScalarGridSpec(
            num_scalar_prefetch=1, grid=grid,
            in_specs=[per_b((n_heads, past)), per_b((n_heads, 2)), per_b((n_heads, HEAD_DIM))]
                     + [paged((n_heads, HEAD_DIM), j) for j in range(pg)],
            out_specs=per_b((n_heads, HEAD_DIM)),
            scratch_shapes=[pltpu.VMEM((n_heads, HEAD_DIM), _f32)]),
        out_shape=jax.ShapeDtypeStruct((b, n_heads, HEAD_DIM), _bf16),
        compiler_params=_params("arbitrary", "arbitrary"),
        name="dsa_sample_pv",
    )(pt, probs, p_new, heads(v_new), *([cv] * pg))
    return out.reshape(b, 1, d_b)


def _ffn_body(routed, final_norm, n_exp,
              x_ref, g_ref, sh_ref, sc_ref, gate_ref, rw_ref, rb_ref, w1_ref, w3_ref, w2_ref, gf_ref,
              o_ref, h_ref, acc_ref, dg_ref):
    e = pl.program_id(1)
    f = pl.program_id(2)
    first = (e == 0) & (f == 0)
    last = (e == pl.num_programs(1) - 1) & (f == pl.num_programs(2) - 1)

    @pl.when(first)
    def _():
        h = _norm_mod(x_ref[...], g_ref[...], sh_ref[0], sc_ref[0])
        h_ref[...] = _bf(h)
        acc_ref[...] = jnp.zeros(acc_ref.shape, _f32)
        if routed:
            logits = _dot(_bf(h), rw_ref[...]) + rb_ref[...]
            lane = lax.broadcasted_iota(jnp.int32, logits.shape, 1).astype(_f32)
            logits = jnp.where(lane < n_exp, logits, -jnp.inf)
            v1 = jnp.max(logits, axis=1, keepdims=True)
            i1 = jnp.min(jnp.where(logits == v1, lane, float(LANES)), axis=1, keepdims=True)
            rest = jnp.where(lane == i1, -jnp.inf, logits)
            v2 = jnp.max(rest, axis=1, keepdims=True)
            i2 = jnp.min(jnp.where(rest == v2, lane, float(LANES)), axis=1, keepdims=True)
            e2 = jnp.exp(v2 - v1)
            den = 1.0 + e2
            dg_ref[...] = jnp.where(lane == i1, 1.0 / den, 0.0) + jnp.where(lane == i2, e2 / den, 0.0)

    h = h_ref[...]
    a = _silu(_dot(h, w1_ref[0])) * _dot(h, w3_ref[0])
    y = _dot(_bf(a), w2_ref[0])
    if routed:
        lane = lax.broadcasted_iota(jnp.int32, dg_ref.shape, 1)
        y = y * jnp.sum(jnp.where(lane == e, dg_ref[...], 0.0), axis=1, keepdims=True)
    acc_ref[...] += y

    @pl.when(last)
    def _():
        y = x_ref[...] + gate_ref[0] * acc_ref[...]
        if final_norm:
            y = y * lax.rsqrt(jnp.mean(y * y, axis=-1, keepdims=True) + EPS) * gf_ref[...]
        o_ref[...] = y


def _ffn(x, g, shift, scale, gate, w13, w2, tm, tf, rows_per_group, router=None, final_g=None):
    m, d = x.shape
    n_exp, ff, _ = w2.shape
    routed = router is not None
    final_norm = final_g is not None
    if routed:
        rw, rb = router
        rw = _bf(jnp.pad(rw, ((0, 0), (0, LANES - n_exp))))
        rb = jnp.pad(rb.reshape(1, -1), ((0, 0), (0, LANES - n_exp)))
    else:
        rw = jnp.zeros((d, LANES), _bf16)
        rb = jnp.zeros((1, LANES), _f32)
    gf = final_g.reshape(1, -1) if final_norm else jnp.ones((1, d), _f32)
    nf = ff // tf
    gs = _group_spec(tm, rows_per_group, d)
    rowd = pl.BlockSpec((tm, d), lambda i, e, f: (i, 0))
    full = lambda a: pl.BlockSpec(a.shape, lambda i, e, f: (0, 0))
    return pl.pallas_call(
        functools.partial(_ffn_body, routed, final_norm, n_exp),
        grid=(m // tm, n_exp, nf),
        in_specs=[rowd, full(g), gs(shift.shape[1]), gs(scale.shape[1]), gs(gate.shape[1]), full(rw), full(rb),
                  pl.BlockSpec((1, d, tf), lambda i, e, f: (e, 0, f)),
                  pl.BlockSpec((1, d, tf), lambda i, e, f: (e, 0, f + nf)),
                  pl.BlockSpec((1, tf, d), lambda i, e, f: (e, f, 0)),
                  full(gf)],
        out_specs=rowd,
        out_shape=jax.ShapeDtypeStruct((m, d), _f32),
        scratch_shapes=[pltpu.VMEM((tm, d), _bf16), pltpu.VMEM((tm, d), _f32), pltpu.VMEM((tm, LANES), _f32)],
        compiler_params=_params("arbitrary", "arbitrary", "arbitrary"),
        name="moe_ffn" if routed else "dense_ffn",
    )(x, g, shift, scale, gate, rw, rb, w13, w13, w2, gf)


MOE_CAP = 512


def _top2_gates(logits, n_exp):
    lane = lax.broadcasted_iota(jnp.int32, logits.shape, 1).astype(_f32)
    logits = jnp.where(lane < n_exp, logits, -jnp.inf)
    v1 = jnp.max(logits, axis=1, keepdims=True)
    i1 = jnp.min(jnp.where(logits == v1, lane, float(LANES)), axis=1, keepdims=True)
    rest = jnp.where(lane == i1, -jnp.inf, logits)
    v2 = jnp.max(rest, axis=1, keepdims=True)
    i2 = jnp.min(jnp.where(rest == v2, lane, float(LANES)), axis=1, keepdims=True)
    e2 = jnp.exp(v2 - v1)
    den = 1.0 + e2
    gates = jnp.where(lane == i1, 1.0 / den, 0.0) + jnp.where(lane == i2, e2 / den, 0.0)
    routed = jnp.where((lane == i1) | (lane == i2), 1.0, 0.0)
    return gates, routed


def _moe_route_body(n_exp, x_ref, g_ref, sh_ref, sc_ref, rw_ref, rb_ref, ltri_ref,
                    xs_ref, dg_ref, slot_ref, flag_ref):
    tm = x_ref.shape[0]
    hb = _bf(_norm_mod(x_ref[...], g_ref[...], sh_ref[0], sc_ref[0]))
    gates, routed = _top2_gates(_dot(hb, rw_ref[...]) + rb_ref[...], n_exp)
    dg_ref[...] = gates
    rank = _dot(ltri_ref[...], _bf(routed))
    slot = jnp.where(routed > 0.5, rank, -1.0)
    slot_ref[...] = slot
    slot_t = slot.T
    cap = xs_ref.shape[2]
    want = lax.broadcasted_iota(jnp.int32, (LANES, tm), 0).astype(_f32)
    worst = jnp.zeros((1, 1), _f32)
    for e in range(n_exp):
        s_row = slot_t[e:e + 1, :]
        for c in range(cap // LANES):
            onehot = jnp.where(s_row == want + float(c * LANES), 1.0, 0.0)
            xs_ref[e, 0, c * LANES:(c + 1) * LANES, :] = _bf(_dot(_bf(onehot), hb))
        worst = jnp.maximum(worst, jnp.max(s_row, axis=1, keepdims=True))
    flag_ref[0] = jnp.broadcast_to(jnp.where(worst >= float(cap), 1.0, 0.0), (1, LANES))


def _moe_expert_body(x_ref, w1_ref, w3_ref, w2_ref, o_ref, acc_ref):
    f = pl.program_id(2)

    @pl.when(f == 0)
    def _():
        acc_ref[...] = jnp.zeros(acc_ref.shape, _f32)

    x = x_ref[0]
    a = _silu(_dot(x, w1_ref[0])) * _dot(x, w3_ref[0])
    acc_ref[...] += _dot(_bf(a), w2_ref[0])

    @pl.when(f == pl.num_programs(2) - 1)
    def _():
        o_ref[0] = _bf(acc_ref[...])


def _moe_combine_body(n_exp, final_norm, x_ref, gate_ref, dg_ref, slot_ref, ys_ref, gf_ref, o_ref):
    tm = x_ref.shape[0]
    cap = ys_ref.shape[2]
    lane = lax.broadcasted_iota(jnp.int32, (tm, LANES), 1)
    col = lax.broadcasted_iota(jnp.int32, (tm, cap), 1).astype(_f32)
    dg = dg_ref[...]
    slot = slot_ref[...]
    acc = jnp.zeros(x_ref.shape, _f32)
    for e in range(n_exp):
        slot_e = jnp.sum(jnp.where(lane == e, slot, 0.0), axis=1, keepdims=True)
        gate_e = jnp.sum(jnp.where(lane == e, dg, 0.0), axis=1, keepdims=True)
        scatter = _bf(jnp.where(slot_e == col, 1.0, 0.0))
        acc = acc + gate_e * _dot(scatter, ys_ref[e, 0])
    y = x_ref[...] + gate_ref[0] * acc
    if final_norm:
        y = y * lax.rsqrt(jnp.mean(y * y, axis=-1, keepdims=True) + EPS) * gf_ref[...]
    o_ref[...] = y


def _moe_sparse(x, g, shift, scale, gate, w13, w2, tm, tf, rows_per_group, router, final_g):
    m, d = x.shape
    n_exp, ff, _ = w2.shape
    rw, rb = router
    rw = _bf(jnp.pad(rw, ((0, 0), (0, LANES - n_exp))))
    rb = jnp.pad(rb.reshape(1, -1), ((0, 0), (0, LANES - n_exp)))
    final_norm = final_g is not None
    gf = final_g.reshape(1, -1) if final_norm else jnp.ones((1, d), _f32)
    tiles = m // tm
    cap = MOE_CAP
    ltri = (jnp.arange(tm, dtype=jnp.int32)[:, None] > jnp.arange(tm, dtype=jnp.int32)[None, :]).astype(_bf16)
    gs = _group_spec(tm, rows_per_group, d)
    rowd = pl.BlockSpec((tm, d), lambda i: (i, 0))
    rowl = pl.BlockSpec((tm, LANES), lambda i: (i, 0))
    full = lambda a: pl.BlockSpec(a.shape, lambda i: (0,) * a.ndim)
    xs_spec = pl.BlockSpec((n_exp, 1, cap, d), lambda i: (0, i, 0, 0))
    xs, dg, slot, flags = pl.pallas_call(
        functools.partial(_moe_route_body, n_exp),
        grid=(tiles,),
        in_specs=[rowd, full(g), gs(shift.shape[1]), gs(scale.shape[1]), full(rw), full(rb), full(ltri)],
        out_specs=[xs_spec, rowl, rowl, pl.BlockSpec((1, 1, LANES), lambda i: (i, 0, 0))],
        out_shape=[jax.ShapeDtypeStruct((n_exp, tiles, cap, d), _bf16), jax.ShapeDtypeStruct((m, LANES), _f32),
                   jax.ShapeDtypeStruct((m, LANES), _f32), jax.ShapeDtypeStruct((tiles, 1, LANES), _f32)],
        compiler_params=_params("arbitrary"),
        name="moe_route",
    )(x, g, shift, scale, rw, rb, ltri)

    rows = tiles * cap
    tr = _tile(rows, 1536)
    nf = ff // tf

    def dense_path():
        return _ffn(x, g, shift, scale, gate, w13, w2, tm, tf, rows_per_group, router=router, final_g=final_g)

    def sparse_path():
        ys = pl.pallas_call(
            _moe_expert_body,
            grid=(n_exp, rows // tr, nf),
            in_specs=[pl.BlockSpec((1, tr, d), lambda e, r, f: (e, r, 0)),
                      pl.BlockSpec((1, d, tf), lambda e, r, f: (e, 0, f)),
                      pl.BlockSpec((1, d, tf), lambda e, r, f: (e, 0, f + nf)),
                      pl.BlockSpec((1, tf, d), lambda e, r, f: (e, f, 0))],
            out_specs=pl.BlockSpec((1, tr, d), lambda e, r, f: (e, r, 0)),
            out_shape=jax.ShapeDtypeStruct((n_exp, rows, d), _bf16),
            scratch_shapes=[pltpu.VMEM((tr, d), _f32)],
            compiler_params=_params("arbitrary", "arbitrary", "arbitrary"),
            name="moe_experts",
        )(xs.reshape(n_exp, rows, d), w13, w13, w2)
        return pl.pallas_call(
            functools.partial(_moe_combine_body, n_exp, final_norm),
            grid=(tiles,),
            in_specs=[rowd, gs(gate.shape[1]), rowl, rowl, xs_spec, full(gf)],
            out_specs=rowd,
            out_shape=jax.ShapeDtypeStruct((m, d), _f32),
            compiler_params=_params("arbitrary"),
            name="moe_combine",
        )(x, gate, dg, slot, ys.reshape(n_exp, tiles, cap, d), gf)

    return lax.cond(jnp.max(flags) > 0.5, dense_path, sparse_path)


def _cmul(ar, ai, br, bi):
    return ar * br - ai * bi, ar * bi + ai * br


def _s5_io(u, b_ref, c_ref, d_ref, are_ref, aim_ref, h_of_bb):
    half = b_ref.shape[2] // 2
    bu = _dot(_bf(u), b_ref[0])
    bb_re, bb_im = _cmul(are_ref[0, 1:2], aim_ref[0, 1:2], bu[:, :half], bu[:, half:])
    h_re, h_im = h_of_bb(bb_re, bb_im)
    y = _dot(_bf(jnp.concatenate([h_re, h_im], axis=1)), c_ref[0]) + d_ref[0] * u
    return _bf(_gelu_tanh(y)), h_re, h_im


def _s5_scan_body(tc, u_ref, b_ref, c_ref, d_ref, are_ref, aim_ref, h0r_ref, h0i_ref,
                  y_ref, hr_ref, hi_ref, xr_ref, xi_ref):
    t_blk = pl.program_id(2)

    @pl.when(t_blk == 0)
    def _():
        hr_ref[0, 0] = h0r_ref[0, 0]
        hi_ref[0, 0] = h0i_ref[0, 0]

    width = are_ref.shape[2]
    a1 = (jnp.broadcast_to(are_ref[0, 0:1], (SUBLANES, width)), jnp.broadcast_to(aim_ref[0, 0:1], (SUBLANES, width)))
    a2 = _cmul(*a1, *a1)
    a3 = _cmul(*a2, *a1)
    a4 = _cmul(*a2, *a2)
    a5 = _cmul(*a4, *a1)
    a6 = _cmul(*a4, *a2)
    a7 = _cmul(*a4, *a3)
    a8 = _cmul(*a4, *a4)
    row = lax.broadcasted_iota(jnp.int32, (SUBLANES, width), 0)
    lvl = [tuple(jnp.where(row >= s, c, 0.0) for c in a) for s, a in ((1, a1), (2, a2), (4, a4))]
    pw = []
    for comp in range(2):
        acc = a8[comp]
        for s, a in ((6, a7), (5, a6), (4, a5), (3, a4), (2, a3), (1, a2), (0, a1)):
            acc = jnp.where(row == s, a[comp], acc)
        pw.append(acc)

    def scan_chunk(bb_re, bb_im):
        xr_ref[...] = bb_re
        xi_ref[...] = bb_im

        def tile(j, carry):
            cr, ci = carry
            r0 = pl.multiple_of(j * SUBLANES, SUBLANES)
            xr = xr_ref[pl.ds(r0, SUBLANES), :]
            xi = xi_ref[pl.ds(r0, SUBLANES), :]
            for (s, (fr, fi)) in zip((1, 2, 4), lvl):
                sr = pltpu.roll(xr, s, axis=0)
                si = pltpu.roll(xi, s, axis=0)
                pr, pi = _cmul(fr, fi, sr, si)
                xr = xr + pr
                xi = xi + pi
            pr, pi = _cmul(pw[0], pw[1], jnp.broadcast_to(cr, xr.shape), jnp.broadcast_to(ci, xi.shape))
            xr = xr + pr
            xi = xi + pi
            xr_ref[pl.ds(r0, SUBLANES), :] = xr
            xi_ref[pl.ds(r0, SUBLANES), :] = xi
            return xr[SUBLANES - 1:SUBLANES], xi[SUBLANES - 1:SUBLANES]

        cr, ci = lax.fori_loop(0, tc // SUBLANES, tile, (hr_ref[0, 0], hi_ref[0, 0]))
        hr_ref[0, 0] = cr
        hi_ref[0, 0] = ci
        return xr_ref[...], xi_ref[...]

    y, _, _ = _s5_io(u_ref[0], b_ref, c_ref, d_ref, are_ref, aim_ref, scan_chunk)
    y_ref[0] = y


def _s5_step_body(u_ref, b_ref, c_ref, d_ref, are_ref, aim_ref, h0r_ref, h0i_ref, y_ref, hr_ref, hi_ref):
    def one_step(bb_re, bb_im):
        pr, pi = _cmul(are_ref[0, 0:1], aim_ref[0, 0:1], h0r_ref[0], h0i_ref[0])
        return pr + bb_re, pi + bb_im

    y, h_re, h_im = _s5_io(u_ref[...], b_ref, c_ref, d_ref, are_ref, aim_ref, one_step)
    y_ref[...] = y
    hr_ref[0] = h_re
    hi_ref[0] = h_im


def _s5_tables(a_re, a_im, log_step, b_re, b_im, c_re, c_im, d):
    g_c, p_c, ch = b_re.shape
    gs = LANES // ch
    ns = g_c // gs
    dt = jnp.exp(log_step.astype(_f32))[:, None]
    mag = jnp.exp(dt * a_re)
    ab_re = mag * jnp.cos(dt * a_im)
    ab_im = mag * jnp.sin(dt * a_im)
    den = a_re * a_re + a_im * a_im
    nr = ab_re - 1.0
    co_re = (nr * a_re + ab_im * a_im) / den
    co_im = (ab_im * a_re - nr * a_im) / den
    eye = jnp.eye(gs, dtype=_f32)

    def pack_b(b):
        b = b.reshape(ns, gs, p_c, ch)
        return jnp.einsum('sgpc,gh->sgchp', b, eye).reshape(ns, gs * ch, gs * p_c)

    def pack_c(c):
        c = c.reshape(ns, gs, ch, p_c)
        return jnp.einsum('sgcp,gh->sgphc', c, eye).reshape(ns, gs * p_c, gs * ch)

    b_pack = _bf(jnp.concatenate([pack_b(b_re), pack_b(b_im)], axis=2))
    c_pack = _bf(jnp.concatenate([pack_c(c_re), -pack_c(c_im)], axis=1))
    vec = lambda a, c: jnp.stack([a.reshape(ns, gs * p_c), c.reshape(ns, gs * p_c)], axis=1)
    return b_pack, c_pack, d.reshape(ns, 1, gs * ch), vec(ab_re, co_re), vec(ab_im, co_im)


def _s5_scan(u, h0_re, h0_im, tables, tc):
    n, t, d = u.shape
    b_pack, c_pack, dvec, are, aim = tables
    ns, _, w2 = b_pack.shape
    w = w2 // 2
    h0r = h0_re.reshape(n, ns, 1, w)
    h0i = h0_im.reshape(n, ns, 1, w)
    tab = lambda a: pl.BlockSpec((1,) + a.shape[1:], lambda b, s, j: (s, 0, 0))
    st = pl.BlockSpec((1, 1, 1, w), lambda b, s, j: (b, s, 0, 0))
    seq = pl.BlockSpec((1, tc, LANES), lambda b, s, j: (b, j, s))
    y, hr, hi = pl.pallas_call(
        functools.partial(_s5_scan_body, tc),
        grid=(n, ns, t // tc),
        in_specs=[seq, tab(b_pack), tab(c_pack), tab(dvec), tab(are), tab(aim), st, st],
        out_specs=[seq, st, st],
        out_shape=[jax.ShapeDtypeStruct((n, t, d), _bf16),
                   jax.ShapeDtypeStruct((n, ns, 1, w), _f32), jax.ShapeDtypeStruct((n, ns, 1, w), _f32)],
        scratch_shapes=[pltpu.VMEM((tc, w), _f32), pltpu.VMEM((tc, w), _f32)],
        compiler_params=_params("arbitrary", "arbitrary", "arbitrary"),
        name="s5_scan",
    )(u, b_pack, c_pack, dvec, are, aim, h0r, h0i)
    return y, hr.reshape(n, ns * w), hi.reshape(n, ns * w)


def _s5_step(u, h0_re, h0_im, tables):
    b, d = u.shape
    b_pack, c_pack, dvec, are, aim = tables
    ns, _, w2 = b_pack.shape
    w = w2 // 2
    h0r = h0_re.reshape(b, ns, w).transpose(1, 0, 2)
    h0i = h0_im.reshape(b, ns, w).transpose(1, 0, 2)
    tab = lambda a: pl.BlockSpec((1,) + a.shape[1:], lambda s: (s, 0, 0))
    st = pl.BlockSpec((1, b, w), lambda s: (s, 0, 0))
    col = pl.BlockSpec((b, LANES), lambda s: (0, s))
    y, hr, hi = pl.pallas_call(
        _s5_step_body,
        grid=(ns,),
        in_specs=[col, tab(b_pack), tab(c_pack), tab(dvec), tab(are), tab(aim), st, st],
        out_specs=[col, st, st],
        out_shape=[jax.ShapeDtypeStruct((b, d), _bf16),
                   jax.ShapeDtypeStruct((ns, b, w), _f32), jax.ShapeDtypeStruct((ns, b, w), _f32)],
        compiler_params=_params("arbitrary"),
        name="s5_step",
    )(u, b_pack, c_pack, dvec, are, aim, h0r, h0i)
    return y, hr.transpose(1, 0, 2).reshape(b, ns * w), hi.transpose(1, 0, 2).reshape(b, ns * w)


def _glu_out_body(y_ref, x_ref, gate_ref, wl_ref, wr_ref, o_ref):
    y = y_ref[...]
    z = _dot(y, wl_ref[...]) * jax.nn.sigmoid(_dot(y, wr_ref[...]))
    o_ref[...] = x_ref[...] + gate_ref[0] * z


def _glu_out(yg, x, gate, w_glu, tm, rows_per_group):
    m, d = x.shape
    wl = _bf(w_glu[:, :d])
    wr = _bf(w_glu[:, d:])
    gs = _group_spec(tm, rows_per_group, d)
    rowd = pl.BlockSpec((tm, d), lambda i: (i, 0))
    full = lambda a: pl.BlockSpec(a.shape, lambda i: (0, 0))
    return pl.pallas_call(
        _glu_out_body,
        grid=(m // tm,),
        in_specs=[rowd, rowd, gs(gate.shape[1]), full(wl), full(wr)],
        out_specs=rowd,
        out_shape=jax.ShapeDtypeStruct((m, d), _f32),
        compiler_params=_params("arbitrary"),
        name="glu_out",
    )(yg, x, gate, wl, wr)


def _tile(m, target):
    if m <= target:
        return m
    t = target
    while m % t:
        t -= SUBLANES
    return t


def _run_group(x, mods, is_prompt, st, wts):
    n, t, d = x.shape
    m = n * t
    depth = wts["norm_mix"].shape[0]
    d_a = d // 2
    d_b = d // 2
    n_heads = d_a // HEAD_DIM
    rows_per_group = t if is_prompt else m
    tm = _tile(t, 256) if is_prompt else m
    xf = x.reshape(m, d).astype(_f32)

    def mod_vecs(l):
        parts = jnp.split(mods[l], 6, axis=-1)
        if is_prompt:
            return [p.reshape(n, 1, d) for p in parts]
        return [p.reshape(1, m, d) for p in parts]

    outs = {k: [] for k in ("shift", "wkv", "k", "v", "kidx", "re", "im")}
    for l in range(depth):
        sh_m, sc_m, g_m, sh_f, sc_f, g_f = mod_vecs(l)
        i = l // 2
        last = l == depth - 1
        final_g = wts["norm_final"] if last else None
        norm_mix = wts["norm_mix"][l].reshape(1, d)
        norm_ffn = wts["norm_ffn"][l].reshape(1, d)
        if l % 2 == 0:
            p_a, q, k, v, qi, ki, wi = _in_proj(xf, norm_mix, sh_m, sc_m, wts["e_w_in"][i], tm, rows_per_group)
            if is_prompt:
                prev0 = st["shift"][i].reshape(n, 1, -1)
            else:
                prev0 = st["shift"][i].reshape(1, m, -1)
            r, w, k_mod, v_a, av, bv, g, bonus = _rwkv_prep(
                p_a, prev0.astype(_f32), wts["e_mu"][i], wts["e_w0"][i], wts["e_w2"][i], wts["e_a0"][i],
                wts["e_a2"][i], wts["e_g2"][i], wts["e_k_k"][i], wts["e_k_a"][i], wts["e_r_k"][i],
                tm, rows_per_group)
            seq = lambda a: a.reshape(n, t, d_a)
            nb = n if is_prompt else _tile(n, 4)
            tc = _tile(t, 256)
            o, s_fin = _rwkv_scan(seq(r), seq(w), seq(k_mod), seq(v_a), seq(av), seq(bv),
                                  _pack_state(st["wkv"][i].astype(_f32)), nb, tc)
            if is_prompt:
                o_b = _dsa_prompt(q.reshape(n, t, d_b), k.reshape(n, t, d_b), v.reshape(n, t, d_b),
                                  qi.reshape(n, t, -1), ki.reshape(n, t, -1), wi.reshape(n, t, -1),
                                  wts["rel_bias"], 128)
            else:
                o_b = _dsa_sample(q.reshape(n, t, d_b), k.reshape(n, t, d_b), v.reshape(n, t, d_b),
                                  qi.reshape(n, t, -1), ki.reshape(n, t, -1), wi.reshape(n, t, -1),
                                  st["cache_k"], st["cache_v"], st["cache_kidx"], i,
                                  st["page_table"], wts["rel_bias"])
            xf = _mix_out(o.reshape(m, d_a), bonus, g, o_b.reshape(m, d_b), xf, g_m,
                          wts["e_lnx_w"][i], wts["e_lnx_b"][i], wts["e_w_out"][i], tm, rows_per_group)
            outs["shift"].append(p_a.reshape(n, t, -1)[:, -1].astype(x.dtype))
            outs["wkv"].append(_unpack_state(s_fin))
            outs["k"].append(k.reshape(n, t, n_heads, HEAD_DIM))
            outs["v"].append(v.reshape(n, t, n_heads, HEAD_DIM))
            outs["kidx"].append(ki.reshape(n, t, D_IDX))
            tm_f = _tile(t, 1024) if is_prompt else m
            ff = wts["ffn_w2"][i].shape[0]
            xf = _ffn(xf, norm_ffn, sh_f, sc_f, g_f, wts["ffn_w13"][i][None], wts["ffn_w2"][i][None],
                      tm_f, _tile(ff, 256) if ff % 256 == 0 else ff, rows_per_group, final_g=final_g)
        else:
            tables = _s5_tables(wts["o_a_re"][i], wts["o_a_im"][i], wts["o_log_step"][i], wts["o_b_re"][i],
                                wts["o_b_im"][i], wts["o_c_re"][i], wts["o_c_im"][i], wts["o_d"][i])
            u = _norm_mod_call(xf, norm_mix, sh_m, sc_m, tm, rows_per_group)
            h0r = st["ssm_re"][i].reshape(n, -1).astype(_f32)
            h0i = st["ssm_im"][i].reshape(n, -1).astype(_f32)
            if is_prompt:
                yg, hr, hi = _s5_scan(u.reshape(n, t, d), h0r, h0i, tables, _tile(t, 256))
            else:
                yg, hr, hi = _s5_step(u, h0r, h0i, tables)
            xf = _glu_out(yg.reshape(m, d), xf, g_m, wts["o_w_glu"][i], tm, rows_per_group)
            g_c = d // CH_G
            outs["re"].append(hr.reshape(n, g_c, P_C))
            outs["im"].append(hi.reshape(n, g_c, P_C))
            tm_f = _tile(t, 1024) if is_prompt else m
            ff = wts["moe_w2"][i].shape[1]
            moe = _moe_sparse if (is_prompt and tm_f == 1024) else _ffn
            xf = moe(xf, norm_ffn, sh_f, sc_f, g_f, wts["moe_w13"][i], wts["moe_w2"][i],
                     tm_f, _tile(ff, 512) if ff % 512 == 0 else ff, rows_per_group,
                     router=(wts["o_router_w"][i], wts["o_router_b"][i]), final_g=final_g)
    y = xf.reshape(n, t, d).astype(x.dtype)
    return (y, jnp.stack(outs["shift"]), jnp.stack(outs["wkv"]), jnp.stack(outs["k"]), jnp.stack(outs["v"]),
            jnp.stack(outs["kidx"]), jnp.stack(outs["re"]), jnp.stack(outs["im"]))


def kernel(x_prompt, x_sample, cache_k, cache_v, cache_kidx, state_shift, state_wkv, state_ssm_re, state_ssm_im, page_table, c_prompt, c_sample, norm_mix, norm_ffn, ada_w, ada_b, rel_bias, norm_final, e_w_in, e_mu, e_w0, e_w2, e_a0, e_a2, e_g2, e_k_k, e_k_a, e_r_k, e_lnx_w, e_lnx_b, e_w_out, e_ffn_w13, e_ffn_w2, o_a_re, o_a_im, o_log_step, o_b_re, o_b_im, o_c_re, o_c_im, o_d, o_w_glu, o_router_w, o_router_b, o_moe_w13, o_moe_w2):
    nb, _, d = x_prompt.shape
    n_dec = x_sample.shape[0]
    n_even = e_w_in.shape[0]
    n_odd = o_a_re.shape[0]
    d_a = d // 2
    n_heads = d_a // HEAD_DIM
    shift_w = state_shift.shape[-1]

    c_all = jnp.concatenate([c_prompt, c_sample], axis=0)
    pad = (-c_all.shape[0]) % SUBLANES
    mods = _ada(jnp.pad(c_all, ((0, pad), (0, 0))), ada_w, ada_b)
    mods_p = mods[:, :nb]
    mods_s = mods[:, nb:nb + n_dec]

    wts = dict(norm_mix=norm_mix, norm_ffn=norm_ffn, norm_final=norm_final, rel_bias=rel_bias,
               e_w_in=e_w_in, e_mu=e_mu, e_w0=e_w0, e_w2=e_w2, e_a0=e_a0, e_a2=e_a2, e_g2=e_g2, e_k_k=e_k_k,
               e_k_a=e_k_a, e_r_k=e_r_k, e_lnx_w=e_lnx_w, e_lnx_b=e_lnx_b, e_w_out=e_w_out,
               ffn_w13=_bf(e_ffn_w13), ffn_w2=_bf(e_ffn_w2),
               o_a_re=o_a_re, o_a_im=o_a_im, o_log_step=o_log_step, o_b_re=o_b_re, o_b_im=o_b_im,
               o_c_re=o_c_re, o_c_im=o_c_im, o_d=o_d, o_w_glu=o_w_glu, o_router_w=o_router_w,
               o_router_b=o_router_b, moe_w13=_bf(o_moe_w13), moe_w2=_bf(o_moe_w2))

    st_p = dict(shift=jnp.zeros((n_even, nb, shift_w), x_prompt.dtype),
                wkv=jnp.zeros((n_even, nb, n_heads, HEAD_DIM, HEAD_DIM), _f32),
                ssm_re=jnp.zeros((n_odd, nb, d // CH_G, P_C), _f32),
                ssm_im=jnp.zeros((n_odd, nb, d // CH_G, P_C), _f32))
    st_s = dict(shift=state_shift, wkv=state_wkv, ssm_re=state_ssm_re, ssm_im=state_ssm_im,
                cache_k=cache_k, cache_v=cache_v, cache_kidx=cache_kidx, page_table=page_table)
    out_p = _run_group(x_prompt, mods_p, True, st_p, wts)
    out_s = _run_group(x_sample, mods_s, False, st_s, wts)
    return (out_p[0], out_s[0]) + out_p[1:] + out_s[1:]
```

```python
import functools
import math

import jax
import jax.numpy as jnp
from jax import lax
from jax.experimental import pallas as pl
from jax.experimental.pallas import tpu as pltpu

HEAD_DIM = 64
LORA_W = 64
LORA_A = 64
LORA_G = 128
H_IDX = 8
D_IDX = 64
TOPK_MAX = 256
N_BUCKETS = 32
MAX_DIST = 128
CH_G = 16
P_C = 64
TOP_E = 2
EPS = 1e-6
LNX_EPS = 64e-5

LANES = 128
SUBLANES = 8
VMEM_LIMIT = 56 * 1024 * 1024
KEY_BLOCK = 256
COUNT_BLOCK = 1024
COUNT_ROWS = 64
NEG = -2.0e30
M_INIT = -1.0e30

_bf16 = jnp.bfloat16
_f32 = jnp.float32


def _bf(x):
    return x.astype(_bf16)


def _dot(a, b):
    return jnp.dot(a, b, preferred_element_type=_f32)


def _dot_nt(a, b):
    return lax.dot_general(a, b, (((1,), (1,)), ((), ())), preferred_element_type=_f32)


def _params(*sem):
    return pltpu.CompilerParams(dimension_semantics=sem, vmem_limit_bytes=VMEM_LIMIT)


def _split_bf16(x, parts):
    out = []
    for _ in range(parts - 1):
        hi = _bf(x)
        out.append(hi)
        x = x - hi.astype(_f32)
    out.append(_bf(x))
    return out


def _segsum(x, ones_bd):
    hi, mid, lo = _split_bf16(x, 3)
    return _dot(hi, ones_bd) + _dot(mid, ones_bd) + _dot(lo, ones_bd)


def _norm_mod(x, g, shift, scale):
    ms = jnp.mean(x * x, axis=-1, keepdims=True)
    return (x * lax.rsqrt(ms + EPS) * g) * (1.0 + scale) + shift


def _silu(x):
    return x * jax.nn.sigmoid(x)


def _gelu_tanh(x):
    return 0.5 * x * (1.0 + jnp.tanh(math.sqrt(2.0 / math.pi) * (x + 0.044715 * (x * x * x))))


def _softplus(x):
    return jnp.maximum(x, 0.0) + jnp.log(1.0 + jnp.exp(-jnp.abs(x)))


def _rel_bucket(dist):
    max_exact = N_BUCKETS // 2
    n = jnp.maximum(dist, 0)
    nf = jnp.maximum(n, 1).astype(_f32)
    large = max_exact + (jnp.log(nf / max_exact) / math.log(MAX_DIST / max_exact) * (N_BUCKETS - max_exact)).astype(jnp.int32)
    return jnp.where(n < max_exact, n, jnp.minimum(large, N_BUCKETS - 1))


def _group_spec(rows_per_block, rows_per_group, width):
    def spec(r):
        return pl.BlockSpec((1, r, width), lambda i, *_: ((i * rows_per_block) // rows_per_group, 0, 0))
    return spec


def _ada_body(c_ref, w_ref, b_ref, o_ref):
    o_ref[0] = _dot(_bf(_silu(c_ref[...])), _bf(w_ref[0])) + b_ref[0]


def _ada(c, ada_w, ada_b):
    depth, d, n6 = ada_w.shape
    rows = c.shape[0]
    tn = n6 // 4
    return pl.pallas_call(
        _ada_body,
        grid=(depth, n6 // tn),
        in_specs=[pl.BlockSpec((rows, d), lambda l, j: (0, 0)),
                  pl.BlockSpec((1, d, tn), lambda l, j: (l, 0, j)),
                  pl.BlockSpec((1, 1, tn), lambda l, j: (l, 0, j))],
        out_specs=pl.BlockSpec((1, rows, tn), lambda l, j: (l, 0, j)),
        out_shape=jax.ShapeDtypeStruct((depth, rows, n6), _f32),
        compiler_params=_params("arbitrary", "arbitrary"),
        name="ada_mod",
    )(c, ada_w, ada_b.reshape(depth, 1, n6))


def _norm_mod_body(x_ref, g_ref, sh_ref, sc_ref, o_ref):
    o_ref[...] = _norm_mod(x_ref[...], g_ref[...], sh_ref[0], sc_ref[0])


def _norm_mod_call(x, g, shift, scale, tm, rows_per_group):
    m, d = x.shape
    gs = _group_spec(tm, rows_per_group, d)
    return pl.pallas_call(
        _norm_mod_body,
        grid=(m // tm,),
        in_specs=[pl.BlockSpec((tm, d), lambda i: (i, 0)),
                  pl.BlockSpec((1, d), lambda i: (0, 0)),
                  gs(shift.shape[1]), gs(scale.shape[1])],
        out_specs=pl.BlockSpec((tm, d), lambda i: (i, 0)),
        out_shape=jax.ShapeDtypeStruct((m, d), _f32),
        compiler_params=_params("arbitrary"),
        name="norm_mod",
    )(x, g, shift, scale)


def _in_proj_body(x_ref, g_ref, sh_ref, sc_ref, wa_ref, wq_ref, wk_ref, wv_ref, wqi_ref, wkw_ref,
                  pa_ref, q_ref, k_ref, v_ref, qi_ref, ki_ref, wi_ref):
    h = _bf(_norm_mod(x_ref[...], g_ref[...], sh_ref[0], sc_ref[0]))
    pa_ref[...] = _dot(h, wa_ref[...])
    q_ref[...] = _bf(_dot(h, wq_ref[...]))
    k_ref[...] = _dot(h, wk_ref[...])
    v_ref[...] = _dot(h, wv_ref[...])
    qi_ref[...] = _bf(_dot(h, wqi_ref[...]))
    kw = _dot(h, wkw_ref[...])
    ki_ref[...] = kw[:, :D_IDX]
    wi_ref[...] = kw[:, D_IDX:D_IDX + H_IDX]


def _in_proj(x, g, shift, scale, w_in, tm, rows_per_group):
    m, d = x.shape
    d_a = d // 2
    d_b = d // 2
    shift_w = 3 * d_a + LORA_W + LORA_A + LORA_G
    c1 = shift_w + 3 * d_b
    c2 = c1 + H_IDX * D_IDX
    wa = _bf(w_in[:, :shift_w])
    wq = _bf(w_in[:, shift_w:shift_w + d_b])
    wk = _bf(w_in[:, shift_w + d_b:shift_w + 2 * d_b])
    wv = _bf(w_in[:, shift_w + 2 * d_b:c1])
    wqi = _bf(w_in[:, c1:c2])
    wkw = _bf(jnp.pad(w_in[:, c2:], ((0, 0), (0, LANES - D_IDX - H_IDX))))
    gs = _group_spec(tm, rows_per_group, d)
    full = lambda a: pl.BlockSpec(a.shape, lambda i: (0, 0))
    row = lambda w: pl.BlockSpec((tm, w), lambda i: (i, 0))
    widths = [(shift_w, _f32), (d_b, _bf16), (d_b, _f32), (d_b, _f32), (H_IDX * D_IDX, _bf16),
              (D_IDX, _f32), (H_IDX, _f32)]
    return pl.pallas_call(
        _in_proj_body,
        grid=(m // tm,),
        in_specs=[row(d), pl.BlockSpec((1, d), lambda i: (0, 0)), gs(shift.shape[1]), gs(scale.shape[1]),
                  full(wa), full(wq), full(wk), full(wv), full(wqi), full(wkw)],
        out_specs=[row(w) for w, _ in widths],
        out_shape=[jax.ShapeDtypeStruct((m, w), dt) for w, dt in widths],
        compiler_params=_params("arbitrary"),
        name="in_proj",
    )(x, g, shift, scale, wa, wq, wk, wv, wqi, wkw)


def _rwkv_prep_body(seq_is_one, rows_per_group, tm,
                    p_ref, pprev_ref, prev0_ref, mu_ref, w0_ref, a0_ref, kk_ref, ka_ref, rk_ref,
                    wwa_ref, g2_ref, ones_ref,
                    r_ref, w_ref, k_ref, v_ref, av_ref, bv_ref, g_ref, bonus_ref):
    d_a = r_ref.shape[1]
    p = p_ref[...]
    if seq_is_one:
        p_prev = prev0_ref[0]
    else:
        i = pl.program_id(0)
        first = (i * tm) % rows_per_group == 0
        prev_row = jnp.where(first, prev0_ref[0], pprev_ref[SUBLANES - 1:SUBLANES, :])
        rolled = pltpu.roll(p, 1, axis=0)
        row_id = lax.broadcasted_iota(jnp.int32, p.shape, 0)
        p_prev = jnp.where(row_id == 0, prev_row, rolled)
    ps = p + (p_prev - p) * mu_ref[...]
    r = ps[:, :d_a]
    k = ps[:, d_a:2 * d_a]
    v = ps[:, 2 * d_a:3 * d_a]
    xwa = ps[:, 3 * d_a:3 * d_a + LORA_W + LORA_A]
    xg = ps[:, 3 * d_a + LORA_W + LORA_A:]
    lane = lax.broadcasted_iota(jnp.int32, xwa.shape, 1)
    xwa = jnp.where(lane < LORA_W, jnp.tanh(xwa), xwa)
    lwa = _dot(_bf(xwa), wwa_ref[...])
    w_log = -_softplus(-(w0_ref[...] + lwa[:, :d_a])) - 0.5
    decay = jnp.exp(-jnp.exp(w_log))
    a = jax.nn.sigmoid(a0_ref[...] + lwa[:, d_a:])
    g_ref[...] = _dot(_bf(jax.nn.sigmoid(xg)), g2_ref[...])
    ones_bd = ones_ref[...]
    kk = k * kk_ref[...]
    kk = kk / jnp.maximum(jnp.sqrt(_segsum(kk * kk, ones_bd)), 1e-12)
    k_mod = k * (1.0 + (a - 1.0) * ka_ref[...])
    r_ref[...] = r
    w_ref[...] = decay
    k_ref[...] = k_mod
    v_ref[...] = v
    av_ref[...] = -kk
    bv_ref[...] = kk * a
    bonus_ref[...] = _segsum(r * k_mod * rk_ref[...], ones_bd) * v


def _ones_blockdiag(n, group):
    idx = jnp.arange(n, dtype=jnp.int32) // group
    return (idx[:, None] == idx[None, :]).astype(_bf16)


def _rwkv_prep(p_a, prev0, mu, w0, w2, a0, a2, g2, k_k, k_a, r_k, tm, rows_per_group):
    m, shift_w = p_a.shape
    d_a = w0.shape[-1]
    seq_is_one = rows_per_group == tm and prev0.shape[1] == tm
    wwa = jnp.zeros((LORA_W + LORA_A, 2 * d_a), _f32)
    wwa = _bf(wwa.at[:LORA_W, :d_a].set(w2).at[LORA_W:, d_a:].set(a2))
    ones_bd = _ones_blockdiag(d_a, HEAD_DIM)
    vec = lambda a: a.reshape(1, -1)
    gs = _group_spec(tm, rows_per_group, shift_w)
    full = lambda a: pl.BlockSpec(a.shape, lambda i: (0, 0))
    row = pl.BlockSpec((tm, d_a), lambda i: (i, 0))
    ins = [p_a, p_a, prev0, vec(mu), vec(w0), vec(a0), vec(k_k), vec(k_a), vec(r_k), wwa, _bf(g2), ones_bd]
    in_specs = [pl.BlockSpec((tm, shift_w), lambda i: (i, 0)),
                pl.BlockSpec((SUBLANES, shift_w), lambda i: (jnp.maximum(i * (tm // SUBLANES) - 1, 0), 0)),
                gs(prev0.shape[1])] + [full(a) for a in ins[3:]]
    return pl.pallas_call(
        functools.partial(_rwkv_prep_body, seq_is_one, rows_per_group, tm),
        grid=(m // tm,),
        in_specs=in_specs,
        out_specs=[row] * 8,
        out_shape=[jax.ShapeDtypeStruct((m, d_a), _f32)] * 8,
        compiler_params=_params("arbitrary"),
        name="rwkv_prep",
    )(*ins)


def _rwkv_scan_body(nb, n_pairs, tc,
                    r_ref, w_ref, k_ref, v_ref, av_ref, bv_ref, s0_ref, ones_ref, expand_ref, eye_ref,
                    o_ref, s_ref):
    t_blk = pl.program_id(1)

    @pl.when(t_blk == 0)
    def _():
        s_ref[...] = s0_ref[...]

    ones_k = ones_ref[...]
    parts = ones_k.shape[0] // LANES
    expand = expand_ref[...]
    eye = eye_ref[...] > 0.5
    units = [(n, hp) for n in range(nb) for hp in range(n_pairs)]

    def rowsum(xs, ones, parts):
        lhs = jnp.concatenate([jnp.concatenate(_split_bf16(x, parts), axis=1) for x in xs], axis=0)
        res = _dot(lhs, ones)
        return [res[u * HEAD_DIM:(u + 1) * HEAD_DIM] for u in range(len(xs))]

    sub = min(SUBLANES, tc)
    row_id = lax.broadcasted_iota(jnp.int32, (sub, LANES), 0)

    def tile_steps(j, carry):
        t0 = pl.multiple_of(j * sub, sub)
        tiles = {}
        for (n, hp) in units:
            sl = pl.ds(hp * LANES, LANES)
            tiles[(n, hp)] = tuple(ref[n, pl.ds(t0, sub), sl] for ref in (r_ref, w_ref, k_ref, v_ref, av_ref, bv_ref))
        states = [s_ref[n, hp] for (n, hp) in units]
        o_tiles = [jnp.zeros((sub, LANES), _f32) for _ in units]
        vcols = []
        for u in units:
            vt = tiles[u][3]
            by_head = jnp.concatenate([vt[:, :HEAD_DIM], vt[:, HEAD_DIM:]], axis=0)
            lhs = jnp.concatenate(_split_bf16(by_head, 3), axis=0)
            vcols.append(lax.dot_general(lhs, expand, (((0,), (0,)), ((), ())), preferred_element_type=_f32))
        for tt in range(sub):
            rows = {u: tuple(x[tt:tt + 1, :] for x in tiles[u]) for u in units}
            sa = rowsum([s * rows[u][4] for s, u in zip(states, units)], ones_k, parts)
            for idx, u in enumerate(units):
                r_t, w_t, k_t, v_t, av_t, bv_t = rows[u]
                vcol = vcols[idx][:, tt * LANES:(tt + 1) * LANES]
                states[idx] = states[idx] * w_t + sa[idx] * bv_t + vcol * k_t
            ob = rowsum([s * rows[u][0] for s, u in zip(states, units)], ones_k, parts)
            for idx in range(len(units)):
                o_row = jnp.sum(jnp.where(eye, ob[idx], 0.0), axis=0, keepdims=True)
                o_tiles[idx] = jnp.where(row_id == tt, jnp.broadcast_to(o_row, (sub, LANES)), o_tiles[idx])
        for idx, (n, hp) in enumerate(units):
            s_ref[n, hp] = states[idx]
            o_ref[n, pl.ds(t0, sub), pl.ds(hp * LANES, LANES)] = o_tiles[idx]
        return carry

    lax.fori_loop(0, tc // sub, tile_steps, 0)


def _rwkv_scan(r, w, k, v, av, bv, s0, nb, tc):
    n, t, d_a = r.shape
    n_pairs = d_a // LANES
    half = (jnp.arange(LANES, dtype=jnp.int32) // HEAD_DIM)
    ones_blk = (half[:, None] == half[None, :]).astype(_bf16)
    ones_k = jnp.concatenate([ones_blk] * (3 if t == 1 else 2), axis=0)
    sub = min(SUBLANES, tc)
    src_h = jnp.arange(2 * sub, dtype=jnp.int32) // sub
    src_t = jnp.arange(2 * sub, dtype=jnp.int32) % sub
    dst_t = jnp.arange(sub * LANES, dtype=jnp.int32) // LANES
    dst_h = (jnp.arange(sub * LANES, dtype=jnp.int32) % LANES) // HEAD_DIM
    expand = ((src_t[:, None] == dst_t[None, :]) & (src_h[:, None] == dst_h[None, :])).astype(_bf16)
    expand = jnp.concatenate([expand] * 3, axis=0)
    eye =(jnp.arange(HEAD_DIM, dtype=jnp.int32)[:, None]
           == (jnp.arange(LANES, dtype=jnp.int32) % HEAD_DIM)[None, :]).astype(_f32)
    seq = pl.BlockSpec((nb, tc, d_a), lambda b, j: (b, j, 0))
    st = pl.BlockSpec((nb, n_pairs, HEAD_DIM, LANES), lambda b, j: (b, 0, 0, 0))
    full = lambda a: pl.BlockSpec(a.shape, lambda b, j: (0, 0))
    return pl.pallas_call(
        functools.partial(_rwkv_scan_body, nb, n_pairs, tc),
        grid=(n // nb, t // tc),
        in_specs=[seq] * 6 + [st, full(ones_k), full(expand), full(eye)],
        out_specs=[seq, st],
        out_shape=[jax.ShapeDtypeStruct((n, t, d_a), _f32),
                   jax.ShapeDtypeStruct(s0.shape, _f32)],
        compiler_params=_params("arbitrary", "arbitrary"),
        name="rwkv_scan",
    )(r, w, k, v, av, bv, s0, ones_k, expand, eye)


def _pack_state(s):
    n, h = s.shape[:2]
    return s.reshape(n, h // 2, 2, HEAD_DIM, HEAD_DIM).transpose(0, 1, 3, 2, 4).reshape(n, h // 2, HEAD_DIM, LANES)


def _unpack_state(s):
    n, hp = s.shape[:2]
    return s.reshape(n, hp, HEAD_DIM, 2, HEAD_DIM).transpose(0, 1, 3, 2, 4).reshape(n, hp * 2, HEAD_DIM, HEAD_DIM)


def _mix_out_body(o_ref, bonus_ref, g_ref, ob_ref, x_ref, gate_ref, lw_ref, lb_ref, ones_ref, wa_ref, wb_ref,
                  out_ref):
    ones_bd = ones_ref[...]
    o = o_ref[...]
    inv = 1.0 / HEAD_DIM
    mean = _segsum(o, ones_bd) * inv
    dlt = o - mean
    var = _segsum(dlt * dlt, ones_bd) * inv
    on = dlt * lax.rsqrt(var + LNX_EPS) * lw_ref[...] + lb_ref[...]
    oa = (on + bonus_ref[...]) * g_ref[...]
    y = _dot(_bf(oa), wa_ref[...]) + _dot(ob_ref[...], wb_ref[...])
    out_ref[...] = x_ref[...] + gate_ref[0] * y


def _mix_out(o, bonus, g, o_b, x, gate, lnx_w, lnx_b, w_out, tm, rows_per_group):
    m, d = x.shape
    d_a = o.shape[1]
    ones_bd = _ones_blockdiag(d_a, HEAD_DIM)
    wa = _bf(w_out[:d_a])
    wb = _bf(w_out[d_a:])
    gs = _group_spec(tm, rows_per_group, d)
    full = lambda a: pl.BlockSpec(a.shape, lambda i: (0, 0))
    rowa = pl.BlockSpec((tm, d_a), lambda i: (i, 0))
    rowd = pl.BlockSpec((tm, d), lambda i: (i, 0))
    lw = lnx_w.reshape(1, -1)
    lb = lnx_b.reshape(1, -1)
    return pl.pallas_call(
        _mix_out_body,
        grid=(m // tm,),
        in_specs=[rowa, rowa, rowa, pl.BlockSpec((tm, o_b.shape[1]), lambda i: (i, 0)), rowd, gs(gate.shape[1]),
                  full(lw), full(lb), full(ones_bd), full(wa), full(wb)],
        out_specs=rowd,
        out_shape=jax.ShapeDtypeStruct((m, d), _f32),
        compiler_params=_params("arbitrary"),
        name="mix_out",
    )(o, bonus, g, o_b, x, gate, lw, lb, ones_bd, wa, wb)


def _key_to_float(u):
    key = u ^ jnp.int32(-2147483648)
    bits = jnp.where(key >= 0, key, key ^ jnp.int32(0x7FFFFFFF))
    return lax.bitcast_convert_type(bits, _f32)


def _kth_largest(count_ge, k_row, shape):
    def body(it, u):
        bit = jnp.left_shift(jnp.int32(1), 31 - it)
        cand_u = u | bit
        ok = count_ge(_key_to_float(cand_u)) >= k_row
        return jnp.where(ok, cand_u, u)

    u = lax.fori_loop(0, 32, body, jnp.zeros(shape, jnp.int32))
    return _key_to_float(u)


def _tie_cut(count_eq_lt, budget, n_bits, shape):
    def body(it, c):
        cand = c | jnp.left_shift(jnp.int32(1), n_bits - 1 - it)
        ok = count_eq_lt(cand) <= budget
        return jnp.where(ok, cand, c)

    return lax.fori_loop(0, n_bits, body, jnp.zeros(shape, jnp.int32))


def _dsa_prompt_body(n_heads, k_sel, t_len,
                     far_ref, qt_ref, qit_ref, wit_ref, k_ref, vt_ref, ki_ref, near_ref,
                     ot_ref, sc_ref, cut_ref, qz_ref, m_ref, l_ref, acc_ref):
    qb = qt_ref.shape[2]
    i = pl.program_id(1)
    q0 = i * qb
    qpos = q0 + lax.broadcasted_iota(jnp.int32, (1, qb), 1)
    n_kb = (q0 + qb + KEY_BLOCK - 1) // KEY_BLOCK
    row_kb = lax.broadcasted_iota(jnp.int32, (KEY_BLOCK, qb), 0)

    @pl.when(i == 0)
    def _():
        sc_ref[...] = jnp.full(sc_ref.shape, -jnp.inf, _f32)

    w8 = (wit_ref[0] * (H_IDX ** -0.5)) * (D_IDX ** -0.5)

    qi_all = jnp.concatenate([qit_ref[0, h * D_IDX:(h + 1) * D_IDX, :] for h in range(H_IDX)], axis=1)

    def score_block(kb, carry):
        c0 = pl.multiple_of(kb * KEY_BLOCK, KEY_BLOCK)
        s_all = _dot(ki_ref[0, pl.ds(c0, KEY_BLOCK), :], qi_all)
        acc = w8[0:1, :] * jnp.maximum(s_all[:, :qb], 0.0)
        for h in range(1, H_IDX):
            acc = acc + w8[h:h + 1, :] * jnp.maximum(s_all[:, h * qb:(h + 1) * qb], 0.0)
        sc_ref[pl.ds(c0, KEY_BLOCK), :] = jnp.where(c0 + row_kb <= qpos, acc, -jnp.inf)
        return carry

    lax.fori_loop(0, n_kb, score_block, 0)

    cb = min(COUNT_BLOCK, t_len)
    n_cb = (q0 + qb + cb - 1) // cb
    row_cb = lax.broadcasted_iota(jnp.int32, (cb, qb), 0)

    def count(pred):
        def body(kb, acc):
            c0 = pl.multiple_of(kb * cb, cb)
            hit = jnp.where(pred(sc_ref[pl.ds(c0, cb), :], c0 + row_cb), 1.0, 0.0)
            for j in range(cb // COUNT_ROWS):
                acc = acc + hit[j * COUNT_ROWS:(j + 1) * COUNT_ROWS, :]
            return acc
        acc = lax.fori_loop(0, n_cb, body, jnp.zeros((COUNT_ROWS, qb), _f32))
        return jnp.sum(acc, axis=0, keepdims=True)

    k_row = jnp.minimum(k_sel, qpos + 1).astype(_f32)
    thr = _kth_largest(lambda cand: count(lambda s, pos: s >= cand), k_row, (1, qb))
    n_gt = count(lambda s, pos: s > thr)
    n_eq = count(lambda s, pos: s == thr)
    budget = k_row - n_gt
    cut_ref[...] = jnp.full((1, qb), 2 * t_len, jnp.int32)

    @pl.when(jnp.max(n_eq - budget) > 0.5)
    def _():
        cut_ref[...] = _tie_cut(lambda c: count(lambda s, pos: (s == thr) & (pos < c)), budget,
                                (2 * t_len).bit_length(), (1, qb))

    cut = cut_ref[...]

    m_ref[...] = jnp.full(m_ref.shape, M_INIT, _f32)
    l_ref[...] = jnp.zeros(l_ref.shape, _f32)
    acc_ref[...] = jnp.zeros(acc_ref.shape, _f32)
    pair_row = lax.broadcasted_iota(jnp.int32, (LANES, qb), 0) // HEAD_DIM
    for hp in range(n_heads // 2):
        qp = qt_ref[0, hp * LANES:(hp + 1) * LANES, :] * (HEAD_DIM ** -0.5)
        zero = jnp.zeros_like(qp)
        qz_ref[hp] = jnp.concatenate([jnp.where(pair_row == 0, qp, zero), jnp.where(pair_row == 1, qp, zero)], axis=1)

    def attend(c0, width, bias_of_head):
        scb = sc_ref[pl.ds(c0, width), :]
        pos = c0 + lax.broadcasted_iota(jnp.int32, (width, qb), 0)
        sel = (scb > thr) | ((scb == thr) & (pos < cut))
        m_old = [m_ref[h] for h in range(n_heads)]
        l_old = [l_ref[h] for h in range(n_heads)]
        a_old = [acc_ref[h] for h in range(n_heads)]
        s2 = [_dot(k_ref[0, pl.ds(c0, width), hp * LANES:(hp + 1) * LANES], qz_ref[hp]) for hp in range(n_heads // 2)]
        m_out, l_out, a_out = [], [], []
        for h in range(n_heads):
            s = s2[h // 2][:, (h % 2) * qb:(h % 2 + 1) * qb] + bias_of_head(h)
            s = jnp.where(sel, s, NEG)
            m_new = jnp.maximum(m_old[h], jnp.max(s, axis=0, keepdims=True))
            alpha = jnp.exp(m_old[h] - m_new)
            p = jnp.exp(s - m_new)
            l_out.append(alpha * l_old[h] + jnp.sum(p, axis=0, keepdims=True))
            pv = _dot(vt_ref[0, h * HEAD_DIM:(h + 1) * HEAD_DIM, pl.ds(c0, width)], _bf(p))
            a_out.append(alpha * a_old[h] + pv)
            m_out.append(m_new)
        for h in range(n_heads):
            m_ref[h] = m_out[h]
            l_ref[h] = l_out[h]
            acc_ref[h] = a_out[h]

    n_far = jnp.maximum(i - 1, 0) // (KEY_BLOCK // qb)

    def far_block(kb, carry):
        attend(pl.multiple_of(kb * KEY_BLOCK, KEY_BLOCK), KEY_BLOCK, lambda h: far_ref[h])
        return carry

    lax.fori_loop(0, n_far, far_block, 0)

    def near_block(j, carry):
        attend(pl.multiple_of(j * qb, qb), qb, lambda h: near_ref[i - j, h])
        return carry

    lax.fori_loop(n_far * (KEY_BLOCK // qb), i + 1, near_block, 0)

    for h in range(n_heads):
        ot_ref[0, h * HEAD_DIM:(h + 1) * HEAD_DIM, :] = _bf(acc_ref[h] / l_ref[h])


def _dsa_prompt(q, k, v, qi, ki, wi, rel_bias, qb):
    n, t, d_b = q.shape
    n_heads = d_b // HEAD_DIM
    k_sel = min(TOPK_MAX, t // 4)
    tr = lambda a: a.transpose(0, 2, 1)
    n_near = KEY_BLOCK // qb + 1
    qq = jnp.arange(qb, dtype=jnp.int32)
    dist = (jnp.arange(n_near, dtype=jnp.int32)[:, None, None] * qb + qq[None, None, :] - qq[None, :, None])
    onehot = (_rel_bucket(dist)[..., None] == jnp.arange(N_BUCKETS, dtype=jnp.int32)).astype(_f32)
    near = jnp.einsum('dkqb,bh->dhkq', onehot, rel_bias, precision=lax.Precision.HIGHEST)
    far = rel_bias[N_BUCKETS - 1]
    res = lambda shape: pl.BlockSpec(shape, lambda b, i, *_: (b, 0, 0))
    blk = lambda w: pl.BlockSpec((1, w, qb), lambda b, i, *_: (b, 0, i))
    grid_spec = pltpu.PrefetchScalarGridSpec(
        num_scalar_prefetch=0,
        grid=(n, t // qb),
        in_specs=[pl.BlockSpec(memory_space=pltpu.SMEM),
                  blk(d_b), blk(H_IDX * D_IDX), blk(H_IDX),
                  res((1, t, d_b)), res((1, d_b, t)), res((1, t, D_IDX)),
                  pl.BlockSpec(near.shape, lambda b, i, *_: (0, 0, 0, 0))],
        out_specs=blk(d_b),
        scratch_shapes=[pltpu.VMEM((t, qb), _f32), pltpu.VMEM((1, qb), jnp.int32),
                        pltpu.VMEM((n_heads // 2, LANES, 2 * qb), _bf16),
                        pltpu.VMEM((n_heads, 1, qb), _f32), pltpu.VMEM((n_heads, 1, qb), _f32),
                        pltpu.VMEM((n_heads, HEAD_DIM, qb), _f32)],
    )
    o_t = pl.pallas_call(
        functools.partial(_dsa_prompt_body, n_heads, k_sel, t),
        grid_spec=grid_spec,
        out_shape=jax.ShapeDtypeStruct((n, d_b, t), _bf16),
        compiler_params=_params("arbitrary", "arbitrary"),
        name="dsa_prompt",
    )(far, tr(q), tr(qi), tr(wi), _bf(k), tr(_bf(v)), _bf(ki), near)
    return tr(o_t)


def _dsa_sample_index_body(n_pages, page, pg, pt_ref, qi_ref, wi_ref, kin_ref, *rest):
    cki_refs, (sc_ref, snew_ref) = rest[:pg], rest[pg:]
    step = pl.program_id(1)
    w8 = _bf(wi_ref[0] * (H_IDX ** -0.5)).astype(_f32)
    relu_bf = lambda s: _bf(jnp.maximum(s, 0.0)).astype(_f32)
    qi8 = qi_ref[0]
    for j in range(pg):
        c0 = pl.multiple_of((step * pg + j) * page, page)
        s8 = _dot(qi8, _bf(cki_refs[j][0, 0])) * (D_IDX ** -0.5)
        sc_ref[0, :, pl.ds(c0, page)] = jnp.sum(w8 * relu_bf(s8), axis=0, keepdims=True)

    @pl.when(step == n_pages // pg - 1)
    def _():
        s_new8 = jnp.sum(qi8.astype(_f32) * _bf(kin_ref[0]).astype(_f32), axis=1, keepdims=True) * (D_IDX ** -0.5)
        snew_ref[0] = jnp.sum(w8 * relu_bf(s_new8), axis=0, keepdims=True)


def _dsa_sample_select_body(k_sel, sc_ref, snew_ref, thr_ref, cut_ref):
    sc = sc_ref[...]
    s_new = snew_ref[...]
    rows, past = sc.shape
    pos = lax.broadcasted_iota(jnp.int32, sc.shape, 1)

    def count(pred):
        hits = jnp.sum(jnp.where(pred(sc, pos), 1.0, 0.0), axis=1, keepdims=True)
        return hits + jnp.where(pred(s_new, past), 1.0, 0.0)

    k_row = jnp.full((rows, 1), float(k_sel), _f32)
    thr = _kth_largest(lambda cand: count(lambda s, ps: s >= cand), k_row, (rows, 1))
    budget = k_row - count(lambda s, ps: s > thr)
    thr_ref[...] = thr
    cut_ref[...] = _tie_cut(lambda c: count(lambda s, ps: (s == thr) & (ps < c)), budget,
                            (2 * (past + 1)).bit_length(), (rows, 1))


def _dsa_sample_probs_body(n_heads, n_pages, page, pg, pt_ref, q_ref, kn_ref, sc_ref, snew_ref, thr_ref, cut_ref, *rest):
    ck_refs = rest[:pg]
    btab_ref, bnew_ref, p_ref, pnew_ref, lg_ref = rest[pg:]
    step = pl.program_id(1)
    past = n_pages * page
    q8 = q_ref[0]

    for j in range(pg):
        c0 = pl.multiple_of((step * pg + j) * page, page)
        rows = [_dot(q8, _bf(ck_refs[j][0, 0, h]))[h:h + 1] for h in range(n_heads)]
        lg = jnp.concatenate(rows, axis=0) * (HEAD_DIM ** -0.5)
        lg_ref[:, pl.ds(c0, page)] = lg + btab_ref[:, pl.ds(c0, page)]

    @pl.when(step == n_pages // pg - 1)
    def _():
        sc = sc_ref[0]
        s_new = snew_ref[0]
        thr = thr_ref[0]
        cut = cut_ref[0]
        pos = lax.broadcasted_iota(jnp.int32, sc.shape, 1)
        sel = (sc > thr) | ((sc == thr) & (pos < cut))
        sel_new = (s_new > thr) | ((s_new == thr) & (past < cut))

        kn = _bf(kn_ref[0]).astype(_f32)
        lg_new = jnp.sum(q8.astype(_f32) * kn, axis=1, keepdims=True) * (HEAD_DIM ** -0.5) + bnew_ref[...]
        lg_new = jnp.where(sel_new, lg_new, NEG)
        s_all = jnp.where(sel, lg_ref[...], NEG)
        m = jnp.maximum(jnp.maximum(jnp.max(s_all, axis=1, keepdims=True), lg_new), M_INIT)
        pr = jnp.exp(s_all - m)
        pr_new = jnp.exp(lg_new - m)
        l = jnp.sum(pr, axis=1, keepdims=True) + pr_new
        p_ref[0] = pr
        col = lax.broadcasted_iota(jnp.int32, (n_heads, 2), 1)
        pnew_ref[0] = jnp.where(col == 0, pr_new, l)


def _dsa_sample_pv_body(n_heads, n_pages, page, pg, pt_ref, p_ref, pnew_ref, vn_ref, *rest):
    cv_refs, (o_ref, acc_ref) = rest[:pg], rest[pg:]
    step = pl.program_id(1)

    @pl.when(step == 0)
    def _():
        acc_ref[...] = pnew_ref[0][:, 0:1] * _bf(vn_ref[0]).astype(_f32)

    acc = acc_ref[...]
    for j in range(pg):
        c0 = pl.multiple_of((step * pg + j) * page, page)
        pb = _bf(p_ref[0, :, pl.ds(c0, page)])
        rows = [_dot_nt(pb, _bf(cv_refs[j][0, 0, h]))[h:h + 1] for h in range(n_heads)]
        acc = acc + jnp.concatenate(rows, axis=0)
    acc_ref[...] = acc

    @pl.when(step == n_pages // pg - 1)
    def _():
        o_ref[0] = _bf(acc / pnew_ref[0][:, 1:2])


def _dsa_sample(q, k_new, v_new, qi, ki_new, wi, ck, cv, cki, layer, page_table, rel_bias):
    b, _, d_b = q.shape
    n_heads = d_b // HEAD_DIM
    n_pages = page_table.shape[1]
    page = ck.shape[2]
    ck, cv, cki = (jnp.moveaxis(a, 2, -1) for a in (ck, cv, cki))
    past = n_pages * page
    k_sel = min(TOPK_MAX, (past + 1) // 4)
    pg = max(g for g in (1, 2, 4, 8) if n_pages % g == 0)
    kpos = jnp.arange(past, dtype=jnp.int32)
    btab = rel_bias[_rel_bucket(past - kpos)].T
    bnew = rel_bias[_rel_bucket(jnp.zeros((1,), jnp.int32))].T
    heads = lambda a: a.reshape(b, n_heads, HEAD_DIM)
    pt = page_table.reshape(-1)
    per_b = lambda shape: pl.BlockSpec((1,) + shape, lambda i, s, pt: (i,) + (0,) * len(shape))
    const = lambda a: pl.BlockSpec(a.shape, lambda i, s, pt: (0,) * a.ndim)

    def paged(tail, j):
        return pl.BlockSpec((1, 1) + tail + (page,),
                            lambda i, s, pt: (layer, pt[i * n_pages + s * pg + j]) + (0,) * (len(tail) + 1))

    grid = (b, n_pages // pg)
    scores, s_new = pl.pallas_call(
        functools.partial(_dsa_sample_index_body, n_pages, page, pg),
        grid_spec=pltpu.PrefetchScalarGridSpec(
            num_scalar_prefetch=1, grid=grid,
            in_specs=[per_b((H_IDX, D_IDX)), per_b((H_IDX, 1)), per_b((1, D_IDX))]
                     + [paged((D_IDX,), j) for j in range(pg)],
            out_specs=[per_b((1, past)), per_b((1, 1))]),
        out_shape=[jax.ShapeDtypeStruct((b, 1, past), _f32), jax.ShapeDtypeStruct((b, 1, 1), _f32)],
        compiler_params=_params("arbitrary", "arbitrary"),
        name="dsa_sample_index",
    )(pt, qi.reshape(b, H_IDX, D_IDX), wi.reshape(b, H_IDX, 1), ki_new, *([cki] * pg))
    thr, cut = pl.pallas_call(
        functools.partial(_dsa_sample_select_body, k_sel),
        out_shape=[jax.ShapeDtypeStruct((b, 1), _f32), jax.ShapeDtypeStruct((b, 1), jnp.int32)],
        compiler_params=pltpu.CompilerParams(vmem_limit_bytes=VMEM_LIMIT),
        name="dsa_sample_select",
    )(scores.reshape(b, past), s_new.reshape(b, 1))
    probs, p_new = pl.pallas_call(
        functools.partial(_dsa_sample_probs_body, n_heads, n_pages, page, pg),
        grid_spec=pltpu.PrefetchScalarGridSpec(
            num_scalar_prefetch=1, grid=grid,
            in_specs=[per_b((n_heads, HEAD_DIM)), per_b((n_heads, HEAD_DIM)), per_b((1, past)), per_b((1, 1)),
                      per_b((1, 1)), per_b((1, 1))]
                     + [paged((n_heads, HEAD_DIM), j) for j in range(pg)]
                     + [const(btab), const(bnew)],
            out_specs=[per_b((n_heads, past)), per_b((n_heads, 2))],
            scratch_shapes=[pltpu.VMEM((n_heads, past), _f32)]),
        out_shape=[jax.ShapeDtypeStruct((b, n_heads, past), _f32), jax.ShapeDtypeStruct((b, n_heads, 2), _f32)],
        compiler_params=_params("arbitrary", "arbitrary"),
        name="dsa_sample_probs",
    )(pt, heads(q), heads(k_new), scores, s_new, thr.reshape(b, 1, 1), cut.reshape(b, 1, 1),
      *([ck] * pg), btab, bnew)
    out = pl.pallas_call(
        functools.partial(_dsa_sample_pv_body, n_heads, n_pages, page, pg),
        grid_spec=pltpu.PrefetchScalarGridSpec(
            num_scalar_prefetch=1, grid=grid,
            in_specs=[per_b((n_heads, past)), per_b((n_heads, 2)), per_b((n_heads, HEAD_DIM))]
                     + [paged((n_heads, HEAD_DIM), j) for j in range(pg)],
            out_specs=per_b((n_heads, HEAD_DIM)),
            scratch_shapes=[pltpu.VMEM((n_heads, HEAD_DIM), _f32)]),
        out_shape=jax.ShapeDtypeStruct((b, n_heads, HEAD_DIM), _bf16),
        compiler_params=_params("arbitrary", "arbitrary"),
        name="dsa_sample_pv",
    )(pt, probs, p_new, heads(v_new), *([cv] * pg))
    return out.reshape(b, 1, d_b)


def _ffn_body(routed, final_norm, n_exp,
              x_ref, g_ref, sh_ref, sc_ref, gate_ref, rw_ref, rb_ref, w1_ref, w3_ref, w2_ref, gf_ref,
              o_ref, h_ref, acc_ref, dg_ref):
    e = pl.program_id(1)
    f = pl.program_id(2)
    first = (e == 0) & (f == 0)
    last = (e == pl.num_programs(1) - 1) & (f == pl.num_programs(2) - 1)

    @pl.when(first)
    def _():
        h = _norm_mod(x_ref[...], g_ref[...], sh_ref[0], sc_ref[0])
        h_ref[...] = _bf(h)
        acc_ref[...] = jnp.zeros(acc_ref.shape, _f32)
        if routed:
            logits = _dot(_bf(h), rw_ref[...]) + rb_ref[...]
            lane = lax.broadcasted_iota(jnp.int32, logits.shape, 1).astype(_f32)
            logits = jnp.where(lane < n_exp, logits, -jnp.inf)
            v1 = jnp.max(logits, axis=1, keepdims=True)
            i1 = jnp.min(jnp.where(logits == v1, lane, float(LANES)), axis=1, keepdims=True)
            rest = jnp.where(lane == i1, -jnp.inf, logits)
            v2 = jnp.max(rest, axis=1, keepdims=True)
            i2 = jnp.min(jnp.where(rest == v2, lane, float(LANES)), axis=1, keepdims=True)
            e2 = jnp.exp(v2 - v1)
            den = 1.0 + e2
            dg_ref[...] = jnp.where(lane == i1, 1.0 / den, 0.0) + jnp.where(lane == i2, e2 / den, 0.0)

    h = h_ref[...]
    a = _silu(_dot(h, w1_ref[0])) * _dot(h, w3_ref[0])
    y = _dot(_bf(a), w2_ref[0])
    if routed:
        lane = lax.broadcasted_iota(jnp.int32, dg_ref.shape, 1)
        y = y * jnp.sum(jnp.where(lane == e, dg_ref[...], 0.0), axis=1, keepdims=True)
    acc_ref[...] += y

    @pl.when(last)
    def _():
        y = x_ref[...] + gate_ref[0] * acc_ref[...]
        if final_norm:
            y = y * lax.rsqrt(jnp.mean(y * y, axis=-1, keepdims=True) + EPS) * gf_ref[...]
        o_ref[...] = y


def _ffn(x, g, shift, scale, gate, w13, w2, tm, tf, rows_per_group, router=None, final_g=None):
    m, d = x.shape
    n_exp, ff, _ = w2.shape
    routed = router is not None
    final_norm = final_g is not None
    if routed:
        rw, rb = router
        rw = _bf(jnp.pad(rw, ((0, 0), (0, LANES - n_exp))))
        rb = jnp.pad(rb.reshape(1, -1), ((0, 0), (0, LANES - n_exp)))
    else:
        rw = jnp.zeros((d, LANES), _bf16)
        rb = jnp.zeros((1, LANES), _f32)
    gf = final_g.reshape(1, -1) if final_norm else jnp.ones((1, d), _f32)
    nf = ff // tf
    gs = _group_spec(tm, rows_per_group, d)
    rowd = pl.BlockSpec((tm, d), lambda i, e, f: (i, 0))
    full = lambda a: pl.BlockSpec(a.shape, lambda i, e, f: (0, 0))
    return pl.pallas_call(
        functools.partial(_ffn_body, routed, final_norm, n_exp),
        grid=(m // tm, n_exp, nf),
        in_specs=[rowd, full(g), gs(shift.shape[1]), gs(scale.shape[1]), gs(gate.shape[1]), full(rw), full(rb),
                  pl.BlockSpec((1, d, tf), lambda i, e, f: (e, 0, f)),
                  pl.BlockSpec((1, d, tf), lambda i, e, f: (e, 0, f + nf)),
                  pl.BlockSpec((1, tf, d), lambda i, e, f: (e, f, 0)),
                  full(gf)],
        out_specs=rowd,
        out_shape=jax.ShapeDtypeStruct((m, d), _f32),
        scratch_shapes=[pltpu.VMEM((tm, d), _bf16), pltpu.VMEM((tm, d), _f32), pltpu.VMEM((tm, LANES), _f32)],
        compiler_params=_params("arbitrary", "arbitrary", "arbitrary"),
        name="moe_ffn" if routed else "dense_ffn",
    )(x, g, shift, scale, gate, rw, rb, w13, w13, w2, gf)


MOE_CAP = 512


def _top2_gates(logits, n_exp):
    lane = lax.broadcasted_iota(jnp.int32, logits.shape, 1).astype(_f32)
    logits = jnp.where(lane < n_exp, logits, -jnp.inf)
    v1 = jnp.max(logits, axis=1, keepdims=True)
    i1 = jnp.min(jnp.where(logits == v1, lane, float(LANES)), axis=1, keepdims=True)
    rest = jnp.where(lane == i1, -jnp.inf, logits)
    v2 = jnp.max(rest, axis=1, keepdims=True)
    i2 = jnp.min(jnp.where(rest == v2, lane, float(LANES)), axis=1, keepdims=True)
    e2 = jnp.exp(v2 - v1)
    den = 1.0 + e2
    gates = jnp.where(lane == i1, 1.0 / den, 0.0) + jnp.where(lane == i2, e2 / den, 0.0)
    routed = jnp.where((lane == i1) | (lane == i2), 1.0, 0.0)
    return gates, routed


def _moe_route_body(n_exp, x_ref, g_ref, sh_ref, sc_ref, rw_ref, rb_ref, ltri_ref,
                    xs_ref, dg_ref, slot_ref, top_ref):
    tm = x_ref.shape[0]
    hb = _bf(_norm_mod(x_ref[...], g_ref[...], sh_ref[0], sc_ref[0]))
    gates, routed = _top2_gates(_dot(hb, rw_ref[...]) + rb_ref[...], n_exp)
    dg_ref[...] = gates
    rank = _dot(ltri_ref[...], _bf(routed))
    slot = jnp.where(routed > 0.5, rank, -1.0)
    slot_ref[...] = slot
    slot_t = slot.T
    want = lax.broadcasted_iota(jnp.int32, (LANES, tm), 0).astype(_f32)
    for e in range(n_exp):
        s_row = slot_t[e:e + 1, :]
        for c in range(xs_ref.shape[1]):
            onehot = jnp.where(s_row == want + float(c * LANES), 1.0, 0.0)
            xs_ref[e, c, 0] = _bf(_dot(_bf(onehot), hb))
    top_ref[0] = jnp.max(slot, axis=0, keepdims=True)


def _moe_expert_body(used_ref, x_ref, w1_ref, w3_ref, w2_ref, o_ref, acc_ref):
    f = pl.program_id(2)

    @pl.when(f == 0)
    def _():
        acc_ref[...] = jnp.zeros(acc_ref.shape, _f32)

    @pl.when(used_ref[pl.program_id(0) * pl.num_programs(1) + pl.program_id(1)] > 0)
    def _():
        x = x_ref[0]
        a = _silu(_dot(x, w1_ref[0])) * _dot(x, w3_ref[0])
        acc_ref[...] += _dot(_bf(a), w2_ref[0])

    @pl.when(f == pl.num_programs(2) - 1)
    def _():
        o_ref[0] = _bf(acc_ref[...])


def _moe_combine_body(n_exp, final_norm, x_ref, gate_ref, dg_ref, slot_ref, ys_ref, gf_ref, o_ref):
    tm = x_ref.shape[0]
    cap = ys_ref.shape[1] * LANES
    lane = lax.broadcasted_iota(jnp.int32, (tm, LANES), 1)
    col = lax.broadcasted_iota(jnp.int32, (tm, cap), 1).astype(_f32)
    dg = dg_ref[...]
    slot = slot_ref[...]
    acc = jnp.zeros(x_ref.shape, _f32)
    for e in range(n_exp):
        slot_e = jnp.sum(jnp.where(lane == e, slot, 0.0), axis=1, keepdims=True)
        gate_e = jnp.sum(jnp.where(lane == e, dg, 0.0), axis=1, keepdims=True)
        scatter = _bf(jnp.where(slot_e == col, 1.0, 0.0))
        acc = acc + gate_e * _dot(scatter, ys_ref[e, :, 0].reshape(cap, ys_ref.shape[4]))
    y = x_ref[...] + gate_ref[0] * acc
    if final_norm:
        y = y * lax.rsqrt(jnp.mean(y * y, axis=-1, keepdims=True) + EPS) * gf_ref[...]
    o_ref[...] = y


def _moe_sparse(x, g, shift, scale, gate, w13, w2, tm, tf, rows_per_group, router, final_g):
    m, d = x.shape
    n_exp, ff, _ = w2.shape
    rw, rb = router
    rw = _bf(jnp.pad(rw, ((0, 0), (0, LANES - n_exp))))
    rb = jnp.pad(rb.reshape(1, -1), ((0, 0), (0, LANES - n_exp)))
    final_norm = final_g is not None
    gf = final_g.reshape(1, -1) if final_norm else jnp.ones((1, d), _f32)
    tiles = m // tm
    cap = MOE_CAP
    ltri = (jnp.arange(tm, dtype=jnp.int32)[:, None] > jnp.arange(tm, dtype=jnp.int32)[None, :]).astype(_bf16)
    gs = _group_spec(tm, rows_per_group, d)
    rowd = pl.BlockSpec((tm, d), lambda i: (i, 0))
    rowl = pl.BlockSpec((tm, LANES), lambda i: (i, 0))
    full = lambda a: pl.BlockSpec(a.shape, lambda i: (0,) * a.ndim)
    n_chunks = cap // LANES
    xs_spec = pl.BlockSpec((n_exp, n_chunks, 1, LANES, d), lambda i: (0, 0, i, 0, 0))
    xs, dg, slot, top = pl.pallas_call(
        functools.partial(_moe_route_body, n_exp),
        grid=(tiles,),
        in_specs=[rowd, full(g), gs(shift.shape[1]), gs(scale.shape[1]), full(rw), full(rb), full(ltri)],
        out_specs=[xs_spec, rowl, rowl, pl.BlockSpec((1, 1, LANES), lambda i: (i, 0, 0))],
        out_shape=[jax.ShapeDtypeStruct((n_exp, n_chunks, tiles, LANES, d), _bf16), jax.ShapeDtypeStruct((m, LANES), _f32),
                   jax.ShapeDtypeStruct((m, LANES), _f32), jax.ShapeDtypeStruct((tiles, 1, LANES), _f32)],
        compiler_params=_params("arbitrary"),
        name="moe_route",
    )(x, g, shift, scale, rw, rb, ltri)

    tr = tiles * LANES
    nf = ff // tf
    top = top.reshape(tiles, LANES)[:, :n_exp]
    used = (jnp.max(top, axis=0)[:, None] >= (jnp.arange(n_chunks, dtype=_f32) * LANES)[None, :]).astype(jnp.int32)

    def dense_path():
        return _ffn(x, g, shift, scale, gate, w13, w2, tm, tf, rows_per_group, router=router, final_g=final_g)

    def sparse_path():
        ys = pl.pallas_call(
            _moe_expert_body,
            grid_spec=pltpu.PrefetchScalarGridSpec(
                num_scalar_prefetch=1, grid=(n_exp, n_chunks, nf),
                in_specs=[pl.BlockSpec((1, tr, d), lambda e, c, f, u: (e, c, 0)),
                          pl.BlockSpec((1, d, tf), lambda e, c, f, u: (e, 0, f)),
                          pl.BlockSpec((1, d, tf), lambda e, c, f, u: (e, 0, f + nf)),
                          pl.BlockSpec((1, tf, d), lambda e, c, f, u: (e, f, 0))],
                out_specs=pl.BlockSpec((1, tr, d), lambda e, c, f, u: (e, c, 0)),
                scratch_shapes=[pltpu.VMEM((tr, d), _f32)]),
            out_shape=jax.ShapeDtypeStruct((n_exp, n_chunks * tr, d), _bf16),
            compiler_params=_params("arbitrary", "arbitrary", "arbitrary"),
            name="moe_experts",
        )(used.reshape(-1), xs.reshape(n_exp, n_chunks * tr, d), w13, w13, w2)
        return pl.pallas_call(
            functools.partial(_moe_combine_body, n_exp, final_norm),
            grid=(tiles,),
            in_specs=[rowd, gs(gate.shape[1]), rowl, rowl, xs_spec, full(gf)],
            out_specs=rowd,
            out_shape=jax.ShapeDtypeStruct((m, d), _f32),
            compiler_params=_params("arbitrary"),
            name="moe_combine",
        )(x, gate, dg, slot, ys.reshape(n_exp, n_chunks, tiles, LANES, d), gf)

    return lax.cond(jnp.max(top) >= cap, dense_path, sparse_path)


def _cmul(ar, ai, br, bi):
    return ar * br - ai * bi, ar * bi + ai * br


def _s5_io(u, b_ref, c_ref, d_ref, are_ref, aim_ref, h_of_bb):
    half = b_ref.shape[2] // 2
    bu = _dot(_bf(u), b_ref[0])
    bb_re, bb_im = _cmul(are_ref[0, 1:2], aim_ref[0, 1:2], bu[:, :half], bu[:, half:])
    h_re, h_im = h_of_bb(bb_re, bb_im)
    y = _dot(_bf(jnp.concatenate([h_re, h_im], axis=1)), c_ref[0]) + d_ref[0] * u
    return _bf(_gelu_tanh(y)), h_re, h_im


def _s5_scan_body(tc, u_ref, b_ref, c_ref, d_ref, are_ref, aim_ref, h0r_ref, h0i_ref,
                  y_ref, hr_ref, hi_ref, xr_ref, xi_ref):
    t_blk = pl.program_id(2)

    @pl.when(t_blk == 0)
    def _():
        hr_ref[0, 0] = h0r_ref[0, 0]
        hi_ref[0, 0] = h0i_ref[0, 0]

    width = are_ref.shape[2]
    a1 = (jnp.broadcast_to(are_ref[0, 0:1], (SUBLANES, width)), jnp.broadcast_to(aim_ref[0, 0:1], (SUBLANES, width)))
    a2 = _cmul(*a1, *a1)
    a3 = _cmul(*a2, *a1)
    a4 = _cmul(*a2, *a2)
    a5 = _cmul(*a4, *a1)
    a6 = _cmul(*a4, *a2)
    a7 = _cmul(*a4, *a3)
    a8 = _cmul(*a4, *a4)
    row = lax.broadcasted_iota(jnp.int32, (SUBLANES, width), 0)
    lvl = [tuple(jnp.where(row >= s, c, 0.0) for c in a) for s, a in ((1, a1), (2, a2), (4, a4))]
    pw = []
    for comp in range(2):
        acc = a8[comp]
        for s, a in ((6, a7), (5, a6), (4, a5), (3, a4), (2, a3), (1, a2), (0, a1)):
            acc = jnp.where(row == s, a[comp], acc)
        pw.append(acc)

    def scan_chunk(bb_re, bb_im):
        xr_ref[...] = bb_re
        xi_ref[...] = bb_im

        def tile(j, carry):
            cr, ci = carry
            r0 = pl.multiple_of(j * SUBLANES, SUBLANES)
            xr = xr_ref[pl.ds(r0, SUBLANES), :]
            xi = xi_ref[pl.ds(r0, SUBLANES), :]
            for (s, (fr, fi)) in zip((1, 2, 4), lvl):
                sr = pltpu.roll(xr, s, axis=0)
                si = pltpu.roll(xi, s, axis=0)
                pr, pi = _cmul(fr, fi, sr, si)
                xr = xr + pr
                xi = xi + pi
            pr, pi = _cmul(pw[0], pw[1], jnp.broadcast_to(cr, xr.shape), jnp.broadcast_to(ci, xi.shape))
            xr = xr + pr
            xi = xi + pi
            xr_ref[pl.ds(r0, SUBLANES), :] = xr
            xi_ref[pl.ds(r0, SUBLANES), :] = xi
            return xr[SUBLANES - 1:SUBLANES], xi[SUBLANES - 1:SUBLANES]

        cr, ci = lax.fori_loop(0, tc // SUBLANES, tile, (hr_ref[0, 0], hi_ref[0, 0]))
        hr_ref[0, 0] = cr
        hi_ref[0, 0] = ci
        return xr_ref[...], xi_ref[...]

    y, _, _ = _s5_io(u_ref[0], b_ref, c_ref, d_ref, are_ref, aim_ref, scan_chunk)
    y_ref[0] = y


def _s5_step_body(u_ref, b_ref, c_ref, d_ref, are_ref, aim_ref, h0r_ref, h0i_ref, y_ref, hr_ref, hi_ref):
    def one_step(bb_re, bb_im):
        pr, pi = _cmul(are_ref[0, 0:1], aim_ref[0, 0:1], h0r_ref[0], h0i_ref[0])
        return pr + bb_re, pi + bb_im

    y, h_re, h_im = _s5_io(u_ref[...], b_ref, c_ref, d_ref, are_ref, aim_ref, one_step)
    y_ref[...] = y
    hr_ref[0] = h_re
    hi_ref[0] = h_im


def _s5_tables(a_re, a_im, log_step, b_re, b_im, c_re, c_im, d):
    g_c, p_c, ch = b_re.shape
    gs = LANES // ch
    ns = g_c // gs
    dt = jnp.exp(log_step.astype(_f32))[:, None]
    mag = jnp.exp(dt * a_re)
    ab_re = mag * jnp.cos(dt * a_im)
    ab_im = mag * jnp.sin(dt * a_im)
    den = a_re * a_re + a_im * a_im
    nr = ab_re - 1.0
    co_re = (nr * a_re + ab_im * a_im) / den
    co_im = (ab_im * a_re - nr * a_im) / den
    eye = jnp.eye(gs, dtype=_f32)

    def pack_b(b):
        b = b.reshape(ns, gs, p_c, ch)
        return jnp.einsum('sgpc,gh->sgchp', b, eye).reshape(ns, gs * ch, gs * p_c)

    def pack_c(c):
        c = c.reshape(ns, gs, ch, p_c)
        return jnp.einsum('sgcp,gh->sgphc', c, eye).reshape(ns, gs * p_c, gs * ch)

    b_pack = _bf(jnp.concatenate([pack_b(b_re), pack_b(b_im)], axis=2))
    c_pack = _bf(jnp.concatenate([pack_c(c_re), -pack_c(c_im)], axis=1))
    vec = lambda a, c: jnp.stack([a.reshape(ns, gs * p_c), c.reshape(ns, gs * p_c)], axis=1)
    return b_pack, c_pack, d.reshape(ns, 1, gs * ch), vec(ab_re, co_re), vec(ab_im, co_im)


def _s5_scan(u, h0_re, h0_im, tables, tc):
    n, t, d = u.shape
    b_pack, c_pack, dvec, are, aim = tables
    ns, _, w2 = b_pack.shape
    w = w2 // 2
    h0r = h0_re.reshape(n, ns, 1, w)
    h0i = h0_im.reshape(n, ns, 1, w)
    tab = lambda a: pl.BlockSpec((1,) + a.shape[1:], lambda b, s, j: (s, 0, 0))
    st = pl.BlockSpec((1, 1, 1, w), lambda b, s, j: (b, s, 0, 0))
    seq = pl.BlockSpec((1, tc, LANES), lambda b, s, j: (b, j, s))
    y, hr, hi = pl.pallas_call(
        functools.partial(_s5_scan_body, tc),
        grid=(n, ns, t // tc),
        in_specs=[seq, tab(b_pack), tab(c_pack), tab(dvec), tab(are), tab(aim), st, st],
        out_specs=[seq, st, st],
        out_shape=[jax.ShapeDtypeStruct((n, t, d), _bf16),
                   jax.ShapeDtypeStruct((n, ns, 1, w), _f32), jax.ShapeDtypeStruct((n, ns, 1, w), _f32)],
        scratch_shapes=[pltpu.VMEM((tc, w), _f32), pltpu.VMEM((tc, w), _f32)],
        compiler_params=_params("arbitrary", "arbitrary", "arbitrary"),
        name="s5_scan",
    )(u, b_pack, c_pack, dvec, are, aim, h0r, h0i)
    return y, hr.reshape(n, ns * w), hi.reshape(n, ns * w)


def _s5_step(u, h0_re, h0_im, tables):
    b, d = u.shape
    b_pack, c_pack, dvec, are, aim = tables
    ns, _, w2 = b_pack.shape
    w = w2 // 2
    h0r = h0_re.reshape(b, ns, w).transpose(1, 0, 2)
    h0i = h0_im.reshape(b, ns, w).transpose(1, 0, 2)
    tab = lambda a: pl.BlockSpec((1,) + a.shape[1:], lambda s: (s, 0, 0))
    st = pl.BlockSpec((1, b, w), lambda s: (s, 0, 0))
    col = pl.BlockSpec((b, LANES), lambda s: (0, s))
    y, hr, hi = pl.pallas_call(
        _s5_step_body,
        grid=(ns,),
        in_specs=[col, tab(b_pack), tab(c_pack), tab(dvec), tab(are), tab(aim), st, st],
        out_specs=[col, st, st],
        out_shape=[jax.ShapeDtypeStruct((b, d), _bf16),
                   jax.ShapeDtypeStruct((ns, b, w), _f32), jax.ShapeDtypeStruct((ns, b, w), _f32)],
        compiler_params=_params("arbitrary"),
        name="s5_step",
    )(u, b_pack, c_pack, dvec, are, aim, h0r, h0i)
    return y, hr.transpose(1, 0, 2).reshape(b, ns * w), hi.transpose(1, 0, 2).reshape(b, ns * w)


def _glu_out_body(y_ref, x_ref, gate_ref, wl_ref, wr_ref, o_ref):
    y = y_ref[...]
    z = _dot(y, wl_ref[...]) * jax.nn.sigmoid(_dot(y, wr_ref[...]))
    o_ref[...] = x_ref[...] + gate_ref[0] * z


def _glu_out(yg, x, gate, w_glu, tm, rows_per_group):
    m, d = x.shape
    wl = _bf(w_glu[:, :d])
    wr = _bf(w_glu[:, d:])
    gs = _group_spec(tm, rows_per_group, d)
    rowd = pl.BlockSpec((tm, d), lambda i: (i, 0))
    full = lambda a: pl.BlockSpec(a.shape, lambda i: (0, 0))
    return pl.pallas_call(
        _glu_out_body,
        grid=(m // tm,),
        in_specs=[rowd, rowd, gs(gate.shape[1]), full(wl), full(wr)],
        out_specs=rowd,
        out_shape=jax.ShapeDtypeStruct((m, d), _f32),
        compiler_params=_params("arbitrary"),
        name="glu_out",
    )(yg, x, gate, wl, wr)


def _tile(m, target):
    if m <= target:
        return m
    t = target
    while m % t:
        t -= SUBLANES
    return t


def _run_group(x, mods, is_prompt, st, wts):
    n, t, d = x.shape
    m = n * t
    depth = wts["norm_mix"].shape[0]
    d_a = d // 2
    d_b = d // 2
    n_heads = d_a // HEAD_DIM
    rows_per_group = t if is_prompt else m
    tm = _tile(t, 256) if is_prompt else m
    xf = x.reshape(m, d).astype(_f32)

    def mod_vecs(l):
        parts = jnp.split(mods[l], 6, axis=-1)
        if is_prompt:
            return [p.reshape(n, 1, d) for p in parts]
        return [p.reshape(1, m, d) for p in parts]

    outs = {k: [] for k in ("shift", "wkv", "k", "v", "kidx", "re", "im")}
    for l in range(depth):
        sh_m, sc_m, g_m, sh_f, sc_f, g_f = mod_vecs(l)
        i = l // 2
        last = l == depth - 1
        final_g = wts["norm_final"] if last else None
        norm_mix = wts["norm_mix"][l].reshape(1, d)
        norm_ffn = wts["norm_ffn"][l].reshape(1, d)
        if l % 2 == 0:
            p_a, q, k, v, qi, ki, wi = _in_proj(xf, norm_mix, sh_m, sc_m, wts["e_w_in"][i], tm, rows_per_group)
            if is_prompt:
                prev0 = st["shift"][i].reshape(n, 1, -1)
            else:
                prev0 = st["shift"][i].reshape(1, m, -1)
            r, w, k_mod, v_a, av, bv, g, bonus = _rwkv_prep(
                p_a, prev0.astype(_f32), wts["e_mu"][i], wts["e_w0"][i], wts["e_w2"][i], wts["e_a0"][i],
                wts["e_a2"][i], wts["e_g2"][i], wts["e_k_k"][i], wts["e_k_a"][i], wts["e_r_k"][i],
                tm, rows_per_group)
            seq = lambda a: a.reshape(n, t, d_a)
            nb = n if is_prompt else _tile(n, 4)
            tc = _tile(t, 256)
            o, s_fin = _rwkv_scan(seq(r), seq(w), seq(k_mod), seq(v_a), seq(av), seq(bv),
                                  _pack_state(st["wkv"][i].astype(_f32)), nb, tc)
            if is_prompt:
                o_b = _dsa_prompt(q.reshape(n, t, d_b), k.reshape(n, t, d_b), v.reshape(n, t, d_b),
                                  qi.reshape(n, t, -1), ki.reshape(n, t, -1), wi.reshape(n, t, -1),
                                  wts["rel_bias"], 128)
            else:
                o_b = _dsa_sample(q.reshape(n, t, d_b), k.reshape(n, t, d_b), v.reshape(n, t, d_b),
                                  qi.reshape(n, t, -1), ki.reshape(n, t, -1), wi.reshape(n, t, -1),
                                  st["cache_k"], st["cache_v"], st["cache_kidx"], i,
                                  st["page_table"], wts["rel_bias"])
            xf = _mix_out(o.reshape(m, d_a), bonus, g, o_b.reshape(m, d_b), xf, g_m,
                          wts["e_lnx_w"][i], wts["e_lnx_b"][i], wts["e_w_out"][i], tm, rows_per_group)
            outs["shift"].append(p_a.reshape(n, t, -1)[:, -1].astype(x.dtype))
            outs["wkv"].append(_unpack_state(s_fin))
            outs["k"].append(k.reshape(n, t, n_heads, HEAD_DIM))
            outs["v"].append(v.reshape(n, t, n_heads, HEAD_DIM))
            outs["kidx"].append(ki.reshape(n, t, D_IDX))
            tm_f = _tile(t, 1024) if is_prompt else m
            ff = wts["ffn_w2"][i].shape[0]
            xf = _ffn(xf, norm_ffn, sh_f, sc_f, g_f, wts["ffn_w13"][i][None], wts["ffn_w2"][i][None],
                      tm_f, _tile(ff, 256) if ff % 256 == 0 else ff, rows_per_group, final_g=final_g)
        else:
            tables = _s5_tables(wts["o_a_re"][i], wts["o_a_im"][i], wts["o_log_step"][i], wts["o_b_re"][i],
                                wts["o_b_im"][i], wts["o_c_re"][i], wts["o_c_im"][i], wts["o_d"][i])
            u = _norm_mod_call(xf, norm_mix, sh_m, sc_m, tm, rows_per_group)
            h0r = st["ssm_re"][i].reshape(n, -1).astype(_f32)
            h0i = st["ssm_im"][i].reshape(n, -1).astype(_f32)
            if is_prompt:
                yg, hr, hi = _s5_scan(u.reshape(n, t, d), h0r, h0i, tables, _tile(t, 256))
            else:
                yg, hr, hi = _s5_step(u, h0r, h0i, tables)
            xf = _glu_out(yg.reshape(m, d), xf, g_m, wts["o_w_glu"][i], tm, rows_per_group)
            g_c = d // CH_G
            outs["re"].append(hr.reshape(n, g_c, P_C))
            outs["im"].append(hi.reshape(n, g_c, P_C))
            tm_f = _tile(t, 1024) if is_prompt else m
            ff = wts["moe_w2"][i].shape[1]
            moe = _moe_sparse if (is_prompt and tm_f == 1024) else _ffn
            xf = moe(xf, norm_ffn, sh_f, sc_f, g_f, wts["moe_w13"][i], wts["moe_w2"][i],
                     tm_f, _tile(ff, 512) if ff % 512 == 0 else ff, rows_per_group,
                     router=(wts["o_router_w"][i], wts["o_router_b"][i]), final_g=final_g)
    y = xf.reshape(n, t, d).astype(x.dtype)
    return (y, jnp.stack(outs["shift"]), jnp.stack(outs["wkv"]), jnp.stack(outs["k"]), jnp.stack(outs["v"]),
            jnp.stack(outs["kidx"]), jnp.stack(outs["re"]), jnp.stack(outs["im"]))


def kernel(x_prompt, x_sample, cache_k, cache_v, cache_kidx, state_shift, state_wkv, state_ssm_re, state_ssm_im, page_table, c_prompt, c_sample, norm_mix, norm_ffn, ada_w, ada_b, rel_bias, norm_final, e_w_in, e_mu, e_w0, e_w2, e_a0, e_a2, e_g2, e_k_k, e_k_a, e_r_k, e_lnx_w, e_lnx_b, e_w_out, e_ffn_w13, e_ffn_w2, o_a_re, o_a_im, o_log_step, o_b_re, o_b_im, o_c_re, o_c_im, o_d, o_w_glu, o_router_w, o_router_b, o_moe_w13, o_moe_w2):
    nb, _, d = x_prompt.shape
    n_dec = x_sample.shape[0]
    n_even = e_w_in.shape[0]
    n_odd = o_a_re.shape[0]
    d_a = d // 2
    n_heads = d_a // HEAD_DIM
    shift_w = state_shift.shape[-1]

    c_all = jnp.concatenate([c_prompt, c_sample], axis=0)
    pad = (-c_all.shape[0]) % SUBLANES
    mods = _ada(jnp.pad(c_all, ((0, pad), (0, 0))), ada_w, ada_b)
    mods_p = mods[:, :nb]
    mods_s = mods[:, nb:nb + n_dec]

    wts = dict(norm_mix=norm_mix, norm_ffn=norm_ffn, norm_final=norm_final, rel_bias=rel_bias,
               e_w_in=e_w_in, e_mu=e_mu, e_w0=e_w0, e_w2=e_w2, e_a0=e_a0, e_a2=e_a2, e_g2=e_g2, e_k_k=e_k_k,
               e_k_a=e_k_a, e_r_k=e_r_k, e_lnx_w=e_lnx_w, e_lnx_b=e_lnx_b, e_w_out=e_w_out,
               ffn_w13=_bf(e_ffn_w13), ffn_w2=_bf(e_ffn_w2),
               o_a_re=o_a_re, o_a_im=o_a_im, o_log_step=o_log_step, o_b_re=o_b_re, o_b_im=o_b_im,
               o_c_re=o_c_re, o_c_im=o_c_im, o_d=o_d, o_w_glu=o_w_glu, o_router_w=o_router_w,
               o_router_b=o_router_b, moe_w13=_bf(o_moe_w13), moe_w2=_bf(o_moe_w2))

    st_p = dict(shift=jnp.zeros((n_even, nb, shift_w), x_prompt.dtype),
                wkv=jnp.zeros((n_even, nb, n_heads, HEAD_DIM, HEAD_DIM), _f32),
                ssm_re=jnp.zeros((n_odd, nb, d // CH_G, P_C), _f32),
                ssm_im=jnp.zeros((n_odd, nb, d // CH_G, P_C), _f32))
    st_s = dict(shift=state_shift, wkv=state_wkv, ssm_re=state_ssm_re, ssm_im=state_ssm_im,
                cache_k=cache_k, cache_v=cache_v, cache_kidx=cache_kidx, page_table=page_table)
    out_p = _run_group(x_prompt, mods_p, True, st_p, wts)
    out_s = _run_group(x_sample, mods_s, False, st_s, wts)
    return (out_p[0], out_s[0]) + out_p[1:] + out_s[1:]
```

```python
import functools
import math

import jax
import jax.numpy as jnp
from jax import lax
from jax.experimental import pallas as pl
from jax.experimental.pallas import tpu as pltpu

HEAD_DIM = 64
LORA_W = 64
LORA_A = 64
LORA_G = 128
H_IDX = 8
D_IDX = 64
TOPK_MAX = 256
N_BUCKETS = 32
MAX_DIST = 128
CH_G = 16
P_C = 64
TOP_E = 2
EPS = 1e-6
LNX_EPS = 64e-5

LANES = 128
SUBLANES = 8
VMEM_LIMIT = 56 * 1024 * 1024
KEY_BLOCK = 256
COUNT_BLOCK = 1024
COUNT_ROWS = 64
NEG = -2.0e30
M_INIT = -1.0e30

_bf16 = jnp.bfloat16
_f32 = jnp.float32


def _bf(x):
    return x.astype(_bf16)


def _dot(a, b):
    return jnp.dot(a, b, preferred_element_type=_f32)


def _dot_nt(a, b):
    return lax.dot_general(a, b, (((1,), (1,)), ((), ())), preferred_element_type=_f32)


def _params(*sem):
    return pltpu.CompilerParams(dimension_semantics=sem, vmem_limit_bytes=VMEM_LIMIT)


def _split_bf16(x, parts):
    out = []
    for _ in range(parts - 1):
        hi = _bf(x)
        out.append(hi)
        x = x - hi.astype(_f32)
    out.append(_bf(x))
    return out


def _segsum(x, ones_bd):
    hi, mid, lo = _split_bf16(x, 3)
    return _dot(hi, ones_bd) + _dot(mid, ones_bd) + _dot(lo, ones_bd)


def _norm_mod(x, g, shift, scale):
    ms = jnp.mean(x * x, axis=-1, keepdims=True)
    return (x * lax.rsqrt(ms + EPS) * g) * (1.0 + scale) + shift


def _silu(x):
    return x * jax.nn.sigmoid(x)


def _gelu_tanh(x):
    return 0.5 * x * (1.0 + jnp.tanh(math.sqrt(2.0 / math.pi) * (x + 0.044715 * (x * x * x))))


def _softplus(x):
    return jnp.maximum(x, 0.0) + jnp.log(1.0 + jnp.exp(-jnp.abs(x)))


def _rel_bucket(dist):
    max_exact = N_BUCKETS // 2
    n = jnp.maximum(dist, 0)
    nf = jnp.maximum(n, 1).astype(_f32)
    large = max_exact + (jnp.log(nf / max_exact) / math.log(MAX_DIST / max_exact) * (N_BUCKETS - max_exact)).astype(jnp.int32)
    return jnp.where(n < max_exact, n, jnp.minimum(large, N_BUCKETS - 1))


def _group_spec(rows_per_block, rows_per_group, width):
    def spec(r):
        return pl.BlockSpec((1, r, width), lambda i, *_: ((i * rows_per_block) // rows_per_group, 0, 0))
    return spec


def _ada_body(c_ref, w_ref, b_ref, o_ref):
    o_ref[0] = _dot(_bf(_silu(c_ref[...])), _bf(w_ref[0])) + b_ref[0]


def _ada(c, ada_w, ada_b):
    depth, d, n6 = ada_w.shape
    rows = c.shape[0]
    tn = n6 // 4
    return pl.pallas_call(
        _ada_body,
        grid=(depth, n6 // tn),
        in_specs=[pl.BlockSpec((rows, d), lambda l, j: (0, 0)),
                  pl.BlockSpec((1, d, tn), lambda l, j: (l, 0, j)),
                  pl.BlockSpec((1, 1, tn), lambda l, j: (l, 0, j))],
        out_specs=pl.BlockSpec((1, rows, tn), lambda l, j: (l, 0, j)),
        out_shape=jax.ShapeDtypeStruct((depth, rows, n6), _f32),
        compiler_params=_params("arbitrary", "arbitrary"),
        name="ada_mod",
    )(c, ada_w, ada_b.reshape(depth, 1, n6))


def _norm_mod_body(x_ref, g_ref, sh_ref, sc_ref, o_ref):
    o_ref[...] = _norm_mod(x_ref[...], g_ref[...], sh_ref[0], sc_ref[0])


def _norm_mod_call(x, g, shift, scale, tm, rows_per_group):
    m, d = x.shape
    gs = _group_spec(tm, rows_per_group, d)
    return pl.pallas_call(
        _norm_mod_body,
        grid=(m // tm,),
        in_specs=[pl.BlockSpec((tm, d), lambda i: (i, 0)),
                  pl.BlockSpec((1, d), lambda i: (0, 0)),
                  gs(shift.shape[1]), gs(scale.shape[1])],
        out_specs=pl.BlockSpec((tm, d), lambda i: (i, 0)),
        out_shape=jax.ShapeDtypeStruct((m, d), _f32),
        compiler_params=_params("arbitrary"),
        name="norm_mod",
    )(x, g, shift, scale)


def _in_proj_body(x_ref, g_ref, sh_ref, sc_ref, wa_ref, wq_ref, wk_ref, wv_ref, wqi_ref, wkw_ref,
                  pa_ref, q_ref, k_ref, v_ref, qi_ref, ki_ref, wi_ref):
    h = _bf(_norm_mod(x_ref[...], g_ref[...], sh_ref[0], sc_ref[0]))
    pa_ref[...] = _dot(h, wa_ref[...])
    q_ref[...] = _bf(_dot(h, wq_ref[...]))
    k_ref[...] = _dot(h, wk_ref[...])
    v_ref[...] = _dot(h, wv_ref[...])
    qi_ref[...] = _bf(_dot(h, wqi_ref[...]))
    kw = _dot(h, wkw_ref[...])
    ki_ref[...] = kw[:, :D_IDX]
    wi_ref[...] = kw[:, D_IDX:D_IDX + H_IDX]


def _in_proj(x, g, shift, scale, w_in, tm, rows_per_group):
    m, d = x.shape
    d_a = d // 2
    d_b = d // 2
    shift_w = 3 * d_a + LORA_W + LORA_A + LORA_G
    c1 = shift_w + 3 * d_b
    c2 = c1 + H_IDX * D_IDX
    wa = _bf(w_in[:, :shift_w])
    wq = _bf(w_in[:, shift_w:shift_w + d_b])
    wk = _bf(w_in[:, shift_w + d_b:shift_w + 2 * d_b])
    wv = _bf(w_in[:, shift_w + 2 * d_b:c1])
    wqi = _bf(w_in[:, c1:c2])
    wkw = _bf(jnp.pad(w_in[:, c2:], ((0, 0), (0, LANES - D_IDX - H_IDX))))
    gs = _group_spec(tm, rows_per_group, d)
    full = lambda a: pl.BlockSpec(a.shape, lambda i: (0, 0))
    row = lambda w: pl.BlockSpec((tm, w), lambda i: (i, 0))
    widths = [(shift_w, _f32), (d_b, _bf16), (d_b, _f32), (d_b, _f32), (H_IDX * D_IDX, _bf16),
              (D_IDX, _f32), (H_IDX, _f32)]
    return pl.pallas_call(
        _in_proj_body,
        grid=(m // tm,),
        in_specs=[row(d), pl.BlockSpec((1, d), lambda i: (0, 0)), gs(shift.shape[1]), gs(scale.shape[1]),
                  full(wa), full(wq), full(wk), full(wv), full(wqi), full(wkw)],
        out_specs=[row(w) for w, _ in widths],
        out_shape=[jax.ShapeDtypeStruct((m, w), dt) for w, dt in widths],
        compiler_params=_params("arbitrary"),
        name="in_proj",
    )(x, g, shift, scale, wa, wq, wk, wv, wqi, wkw)


def _rwkv_prep_body(seq_is_one, rows_per_group, tm,
                    p_ref, pprev_ref, prev0_ref, mu_ref, w0_ref, a0_ref, kk_ref, ka_ref, rk_ref,
                    wwa_ref, g2_ref, ones_ref,
                    r_ref, w_ref, k_ref, v_ref, av_ref, bv_ref, g_ref, bonus_ref):
    d_a = r_ref.shape[1]
    p = p_ref[...]
    if seq_is_one:
        p_prev = prev0_ref[0]
    else:
        i = pl.program_id(0)
        first = (i * tm) % rows_per_group == 0
        prev_row = jnp.where(first, prev0_ref[0], pprev_ref[SUBLANES - 1:SUBLANES, :])
        rolled = pltpu.roll(p, 1, axis=0)
        row_id = lax.broadcasted_iota(jnp.int32, p.shape, 0)
        p_prev = jnp.where(row_id == 0, prev_row, rolled)
    ps = p + (p_prev - p) * mu_ref[...]
    r = ps[:, :d_a]
    k = ps[:, d_a:2 * d_a]
    v = ps[:, 2 * d_a:3 * d_a]
    xwa = ps[:, 3 * d_a:3 * d_a + LORA_W + LORA_A]
    xg = ps[:, 3 * d_a + LORA_W + LORA_A:]
    lane = lax.broadcasted_iota(jnp.int32, xwa.shape, 1)
    xwa = jnp.where(lane < LORA_W, jnp.tanh(xwa), xwa)
    lwa = _dot(_bf(xwa), wwa_ref[...])
    w_log = -_softplus(-(w0_ref[...] + lwa[:, :d_a])) - 0.5
    decay = jnp.exp(-jnp.exp(w_log))
    a = jax.nn.sigmoid(a0_ref[...] + lwa[:, d_a:])
    g_ref[...] = _dot(_bf(jax.nn.sigmoid(xg)), g2_ref[...])
    ones_bd = ones_ref[...]
    kk = k * kk_ref[...]
    kk = kk / jnp.maximum(jnp.sqrt(_segsum(kk * kk, ones_bd)), 1e-12)
    k_mod = k * (1.0 + (a - 1.0) * ka_ref[...])
    r_ref[...] = r
    w_ref[...] = decay
    k_ref[...] = k_mod
    v_ref[...] = v
    av_ref[...] = -kk
    bv_ref[...] = kk * a
    bonus_ref[...] = _segsum(r * k_mod * rk_ref[...], ones_bd) * v


def _ones_blockdiag(n, group):
    idx = jnp.arange(n, dtype=jnp.int32) // group
    return (idx[:, None] == idx[None, :]).astype(_bf16)


def _rwkv_prep(p_a, prev0, mu, w0, w2, a0, a2, g2, k_k, k_a, r_k, tm, rows_per_group):
    m, shift_w = p_a.shape
    d_a = w0.shape[-1]
    seq_is_one = rows_per_group == tm and prev0.shape[1] == tm
    wwa = jnp.zeros((LORA_W + LORA_A, 2 * d_a), _f32)
    wwa = _bf(wwa.at[:LORA_W, :d_a].set(w2).at[LORA_W:, d_a:].set(a2))
    ones_bd = _ones_blockdiag(d_a, HEAD_DIM)
    vec = lambda a: a.reshape(1, -1)
    gs = _group_spec(tm, rows_per_group, shift_w)
    full = lambda a: pl.BlockSpec(a.shape, lambda i: (0, 0))
    row = pl.BlockSpec((tm, d_a), lambda i: (i, 0))
    ins = [p_a, p_a, prev0, vec(mu), vec(w0), vec(a0), vec(k_k), vec(k_a), vec(r_k), wwa, _bf(g2), ones_bd]
    in_specs = [pl.BlockSpec((tm, shift_w), lambda i: (i, 0)),
                pl.BlockSpec((SUBLANES, shift_w), lambda i: (jnp.maximum(i * (tm // SUBLANES) - 1, 0), 0)),
                gs(prev0.shape[1])] + [full(a) for a in ins[3:]]
    return pl.pallas_call(
        functools.partial(_rwkv_prep_body, seq_is_one, rows_per_group, tm),
        grid=(m // tm,),
        in_specs=in_specs,
        out_specs=[row] * 8,
        out_shape=[jax.ShapeDtypeStruct((m, d_a), _f32)] * 8,
        compiler_params=_params("arbitrary"),
        name="rwkv_prep",
    )(*ins)


def _rwkv_scan_body(nb, n_pairs, tc,
                    r_ref, w_ref, k_ref, v_ref, av_ref, bv_ref, s0_ref, ones_ref, expand_ref, eye_ref,
                    o_ref, s_ref):
    t_blk = pl.program_id(1)

    @pl.when(t_blk == 0)
    def _():
        s_ref[...] = s0_ref[...]

    ones_k = ones_ref[...]
    parts = ones_k.shape[0] // LANES
    expand = expand_ref[...]
    eye = eye_ref[...] > 0.5
    units = [(n, hp) for n in range(nb) for hp in range(n_pairs)]

    def rowsum(xs, ones, parts):
        lhs = jnp.concatenate([jnp.concatenate(_split_bf16(x, parts), axis=1) for x in xs], axis=0)
        res = _dot(lhs, ones)
        return [res[u * HEAD_DIM:(u + 1) * HEAD_DIM] for u in range(len(xs))]

    sub = min(SUBLANES, tc)
    row_id = lax.broadcasted_iota(jnp.int32, (sub, LANES), 0)

    def tile_steps(j, carry):
        t0 = pl.multiple_of(j * sub, sub)
        tiles = {}
        for (n, hp) in units:
            sl = pl.ds(hp * LANES, LANES)
            tiles[(n, hp)] = tuple(ref[n, pl.ds(t0, sub), sl] for ref in (r_ref, w_ref, k_ref, v_ref, av_ref, bv_ref))
        states = [s_ref[n, hp] for (n, hp) in units]
        o_tiles = [jnp.zeros((sub, LANES), _f32) for _ in units]
        vcols = []
        for u in units:
            vt = tiles[u][3]
            by_head = jnp.concatenate([vt[:, :HEAD_DIM], vt[:, HEAD_DIM:]], axis=0)
            lhs = jnp.concatenate(_split_bf16(by_head, 3), axis=0)
            vcols.append(lax.dot_general(lhs, expand, (((0,), (0,)), ((), ())), preferred_element_type=_f32))
        for tt in range(sub):
            rows = {u: tuple(x[tt:tt + 1, :] for x in tiles[u]) for u in units}
            sa = rowsum([s * rows[u][4] for s, u in zip(states, units)], ones_k, parts)
            for idx, u in enumerate(units):
                r_t, w_t, k_t, v_t, av_t, bv_t = rows[u]
                vcol = vcols[idx][:, tt * LANES:(tt + 1) * LANES]
                states[idx] = states[idx] * w_t + sa[idx] * bv_t + vcol * k_t
            ob = rowsum([s * rows[u][0] for s, u in zip(states, units)], ones_k, parts)
            for idx in range(len(units)):
                o_row = jnp.sum(jnp.where(eye, ob[idx], 0.0), axis=0, keepdims=True)
                o_tiles[idx] = jnp.where(row_id == tt, jnp.broadcast_to(o_row, (sub, LANES)), o_tiles[idx])
        for idx, (n, hp) in enumerate(units):
            s_ref[n, hp] = states[idx]
            o_ref[n, pl.ds(t0, sub), pl.ds(hp * LANES, LANES)] = o_tiles[idx]
        return carry

    lax.fori_loop(0, tc // sub, tile_steps, 0)


def _rwkv_scan(r, w, k, v, av, bv, s0, nb, tc):
    n, t, d_a = r.shape
    n_pairs = d_a // LANES
    half = (jnp.arange(LANES, dtype=jnp.int32) // HEAD_DIM)
    ones_blk = (half[:, None] == half[None, :]).astype(_bf16)
    ones_k = jnp.concatenate([ones_blk] * (3 if t == 1 else 2), axis=0)
    sub = min(SUBLANES, tc)
    src_h = jnp.arange(2 * sub, dtype=jnp.int32) // sub
    src_t = jnp.arange(2 * sub, dtype=jnp.int32) % sub
    dst_t = jnp.arange(sub * LANES, dtype=jnp.int32) // LANES
    dst_h = (jnp.arange(sub * LANES, dtype=jnp.int32) % LANES) // HEAD_DIM
    expand = ((src_t[:, None] == dst_t[None, :]) & (src_h[:, None] == dst_h[None, :])).astype(_bf16)
    expand = jnp.concatenate([expand] * 3, axis=0)
    eye =(jnp.arange(HEAD_DIM, dtype=jnp.int32)[:, None]
           == (jnp.arange(LANES, dtype=jnp.int32) % HEAD_DIM)[None, :]).astype(_f32)
    seq = pl.BlockSpec((nb, tc, d_a), lambda b, j: (b, j, 0))
    st = pl.BlockSpec((nb, n_pairs, HEAD_DIM, LANES), lambda b, j: (b, 0, 0, 0))
    full = lambda a: pl.BlockSpec(a.shape, lambda b, j: (0, 0))
    return pl.pallas_call(
        functools.partial(_rwkv_scan_body, nb, n_pairs, tc),
        grid=(n // nb, t // tc),
        in_specs=[seq] * 6 + [st, full(ones_k), full(expand), full(eye)],
        out_specs=[seq, st],
        out_shape=[jax.ShapeDtypeStruct((n, t, d_a), _f32),
                   jax.ShapeDtypeStruct(s0.shape, _f32)],
        compiler_params=_params("arbitrary", "arbitrary"),
        name="rwkv_scan",
    )(r, w, k, v, av, bv, s0, ones_k, expand, eye)


def _pack_state(s):
    n, h = s.shape[:2]
    return s.reshape(n, h // 2, 2, HEAD_DIM, HEAD_DIM).transpose(0, 1, 3, 2, 4).reshape(n, h // 2, HEAD_DIM, LANES)


def _unpack_state(s):
    n, hp = s.shape[:2]
    return s.reshape(n, hp, HEAD_DIM, 2, HEAD_DIM).transpose(0, 1, 3, 2, 4).reshape(n, hp * 2, HEAD_DIM, HEAD_DIM)


def _mix_out_body(o_ref, bonus_ref, g_ref, ob_ref, x_ref, gate_ref, lw_ref, lb_ref, ones_ref, wa_ref, wb_ref,
                  out_ref):
    ones_bd = ones_ref[...]
    o = o_ref[...]
    inv = 1.0 / HEAD_DIM
    mean = _segsum(o, ones_bd) * inv
    dlt = o - mean
    var = _segsum(dlt * dlt, ones_bd) * inv
    on = dlt * lax.rsqrt(var + LNX_EPS) * lw_ref[...] + lb_ref[...]
    oa = (on + bonus_ref[...]) * g_ref[...]
    y = _dot(_bf(oa), wa_ref[...]) + _dot(ob_ref[...], wb_ref[...])
    out_ref[...] = x_ref[...] + gate_ref[0] * y


def _mix_out(o, bonus, g, o_b, x, gate, lnx_w, lnx_b, w_out, tm, rows_per_group):
    m, d = x.shape
    d_a = o.shape[1]
    ones_bd = _ones_blockdiag(d_a, HEAD_DIM)
    wa = _bf(w_out[:d_a])
    wb = _bf(w_out[d_a:])
    gs = _group_spec(tm, rows_per_group, d)
    full = lambda a: pl.BlockSpec(a.shape, lambda i: (0, 0))
    rowa = pl.BlockSpec((tm, d_a), lambda i: (i, 0))
    rowd = pl.BlockSpec((tm, d), lambda i: (i, 0))
    lw = lnx_w.reshape(1, -1)
    lb = lnx_b.reshape(1, -1)
    return pl.pallas_call(
        _mix_out_body,
        grid=(m // tm,),
        in_specs=[rowa, rowa, rowa, pl.BlockSpec((tm, o_b.shape[1]), lambda i: (i, 0)), rowd, gs(gate.shape[1]),
                  full(lw), full(lb), full(ones_bd), full(wa), full(wb)],
        out_specs=rowd,
        out_shape=jax.ShapeDtypeStruct((m, d), _f32),
        compiler_params=_params("arbitrary"),
        name="mix_out",
    )(o, bonus, g, o_b, x, gate, lw, lb, ones_bd, wa, wb)


def _key_to_float(u):
    key = u ^ jnp.int32(-2147483648)
    bits = jnp.where(key >= 0, key, key ^ jnp.int32(0x7FFFFFFF))
    return lax.bitcast_convert_type(bits, _f32)


def _kth_largest(count_ge, k_row, shape):
    def body(it, u):
        bit = jnp.left_shift(jnp.int32(1), 31 - it)
        cand_u = u | bit
        ok = count_ge(_key_to_float(cand_u)) >= k_row
        return jnp.where(ok, cand_u, u)

    u = lax.fori_loop(0, 32, body, jnp.zeros(shape, jnp.int32))
    return _key_to_float(u)


def _tie_cut(count_eq_lt, budget, n_bits, shape):
    def body(it, c):
        cand = c | jnp.left_shift(jnp.int32(1), n_bits - 1 - it)
        ok = count_eq_lt(cand) <= budget
        return jnp.where(ok, cand, c)

    return lax.fori_loop(0, n_bits, body, jnp.zeros(shape, jnp.int32))


def _dsa_prompt_body(n_heads, k_sel, t_len,
                     far_ref, qt_ref, qit_ref, wit_ref, k_ref, vt_ref, ki_ref, near_ref,
                     ot_ref, sc_ref, cut_ref, qz_ref, m_ref, l_ref, acc_ref):
    qb = qt_ref.shape[2]
    i = pl.program_id(1)
    q0 = i * qb
    qpos = q0 + lax.broadcasted_iota(jnp.int32, (1, qb), 1)
    n_kb = (q0 + qb + KEY_BLOCK - 1) // KEY_BLOCK
    row_kb = lax.broadcasted_iota(jnp.int32, (KEY_BLOCK, qb), 0)

    @pl.when(i == 0)
    def _():
        sc_ref[...] = jnp.full(sc_ref.shape, -jnp.inf, _f32)

    w8 = (wit_ref[0] * (H_IDX ** -0.5)) * (D_IDX ** -0.5)

    qi_all = jnp.concatenate([qit_ref[0, h * D_IDX:(h + 1) * D_IDX, :] for h in range(H_IDX)], axis=1)

    def score_block(kb, carry):
        c0 = pl.multiple_of(kb * KEY_BLOCK, KEY_BLOCK)
        s_all = _dot(ki_ref[0, pl.ds(c0, KEY_BLOCK), :], qi_all)
        acc = w8[0:1, :] * jnp.maximum(s_all[:, :qb], 0.0)
        for h in range(1, H_IDX):
            acc = acc + w8[h:h + 1, :] * jnp.maximum(s_all[:, h * qb:(h + 1) * qb], 0.0)
        sc_ref[pl.ds(c0, KEY_BLOCK), :] = jnp.where(c0 + row_kb <= qpos, acc, -jnp.inf)
        return carry

    lax.fori_loop(0, n_kb, score_block, 0)

    cb = min(COUNT_BLOCK, t_len)
    n_cb = (q0 + qb + cb - 1) // cb
    row_cb = lax.broadcasted_iota(jnp.int32, (cb, qb), 0)

    def count(pred):
        def body(kb, acc):
            c0 = pl.multiple_of(kb * cb, cb)
            hit = jnp.where(pred(sc_ref[pl.ds(c0, cb), :], c0 + row_cb), 1.0, 0.0)
            for j in range(cb // COUNT_ROWS):
                acc = acc + hit[j * COUNT_ROWS:(j + 1) * COUNT_ROWS, :]
            return acc
        acc = lax.fori_loop(0, n_cb, body, jnp.zeros((COUNT_ROWS, qb), _f32))
        return jnp.sum(acc, axis=0, keepdims=True)

    k_row = jnp.minimum(k_sel, qpos + 1).astype(_f32)
    thr = _kth_largest(lambda cand: count(lambda s, pos: s >= cand), k_row, (1, qb))
    n_gt = count(lambda s, pos: s > thr)
    n_eq = count(lambda s, pos: s == thr)
    budget = k_row - n_gt
    cut_ref[...] = jnp.full((1, qb), 2 * t_len, jnp.int32)

    @pl.when(jnp.max(n_eq - budget) > 0.5)
    def _():
        cut_ref[...] = _tie_cut(lambda c: count(lambda s, pos: (s == thr) & (pos < c)), budget,
                                (2 * t_len).bit_length(), (1, qb))

    cut = cut_ref[...]

    m_ref[...] = jnp.full(m_ref.shape, M_INIT, _f32)
    l_ref[...] = jnp.zeros(l_ref.shape, _f32)
    acc_ref[...] = jnp.zeros(acc_ref.shape, _f32)
    pair_row = lax.broadcasted_iota(jnp.int32, (LANES, qb), 0) // HEAD_DIM
    for hp in range(n_heads // 2):
        qp = qt_ref[0, hp * LANES:(hp + 1) * LANES, :] * (HEAD_DIM ** -0.5)
        zero = jnp.zeros_like(qp)
        qz_ref[hp] = jnp.concatenate([jnp.where(pair_row == 0, qp, zero), jnp.where(pair_row == 1, qp, zero)], axis=1)

    def attend(c0, width, bias_of_head):
        scb = sc_ref[pl.ds(c0, width), :]
        pos = c0 + lax.broadcasted_iota(jnp.int32, (width, qb), 0)
        sel = (scb > thr) | ((scb == thr) & (pos < cut))
        m_old = [m_ref[h] for h in range(n_heads)]
        l_old = [l_ref[h] for h in range(n_heads)]
        a_old = [acc_ref[h] for h in range(n_heads)]
        s2 = [_dot(k_ref[0, pl.ds(c0, width), hp * LANES:(hp + 1) * LANES], qz_ref[hp]) for hp in range(n_heads // 2)]
        m_out, l_out, a_out = [], [], []
        for h in range(n_heads):
            s = s2[h // 2][:, (h % 2) * qb:(h % 2 + 1) * qb] + bias_of_head(h)
            s = jnp.where(sel, s, NEG)
            m_new = jnp.maximum(m_old[h], jnp.max(s, axis=0, keepdims=True))
            alpha = jnp.exp(m_old[h] - m_new)
            p = jnp.exp(s - m_new)
            l_out.append(alpha * l_old[h] + jnp.sum(p, axis=0, keepdims=True))
            pv = _dot(vt_ref[0, h * HEAD_DIM:(h + 1) * HEAD_DIM, pl.ds(c0, width)], _bf(p))
            a_out.append(alpha * a_old[h] + pv)
            m_out.append(m_new)
        for h in range(n_heads):
            m_ref[h] = m_out[h]
            l_ref[h] = l_out[h]
            acc_ref[h] = a_out[h]

    n_far = jnp.maximum(i - 1, 0) // (KEY_BLOCK // qb)

    def far_block(kb, carry):
        attend(pl.multiple_of(kb * KEY_BLOCK, KEY_BLOCK), KEY_BLOCK, lambda h: far_ref[h])
        return carry

    lax.fori_loop(0, n_far, far_block, 0)

    def near_block(j, carry):
        attend(pl.multiple_of(j * qb, qb), qb, lambda h: near_ref[i - j, h])
        return carry

    lax.fori_loop(n_far * (KEY_BLOCK // qb), i + 1, near_block, 0)

    for h in range(n_heads):
        ot_ref[0, h * HEAD_DIM:(h + 1) * HEAD_DIM, :] = _bf(acc_ref[h] / l_ref[h])


def _dsa_prompt(q, k, v, qi, ki, wi, rel_bias, qb):
    n, t, d_b = q.shape
    n_heads = d_b // HEAD_DIM
    k_sel = min(TOPK_MAX, t // 4)
    tr = lambda a: a.transpose(0, 2, 1)
    n_near = KEY_BLOCK // qb + 1
    qq = jnp.arange(qb, dtype=jnp.int32)
    dist = (jnp.arange(n_near, dtype=jnp.int32)[:, None, None] * qb + qq[None, None, :] - qq[None, :, None])
    onehot = (_rel_bucket(dist)[..., None] == jnp.arange(N_BUCKETS, dtype=jnp.int32)).astype(_f32)
    near = jnp.einsum('dkqb,bh->dhkq', onehot, rel_bias, precision=lax.Precision.HIGHEST)
    far = rel_bias[N_BUCKETS - 1]
    res = lambda shape: pl.BlockSpec(shape, lambda b, i, *_: (b, 0, 0))
    blk = lambda w: pl.BlockSpec((1, w, qb), lambda b, i, *_: (b, 0, i))
    grid_spec = pltpu.PrefetchScalarGridSpec(
        num_scalar_prefetch=0,
        grid=(n, t // qb),
        in_specs=[pl.BlockSpec(memory_space=pltpu.SMEM),
                  blk(d_b), blk(H_IDX * D_IDX), blk(H_IDX),
                  res((1, t, d_b)), res((1, d_b, t)), res((1, t, D_IDX)),
                  pl.BlockSpec(near.shape, lambda b, i, *_: (0, 0, 0, 0))],
        out_specs=blk(d_b),
        scratch_shapes=[pltpu.VMEM((t, qb), _f32), pltpu.VMEM((1, qb), jnp.int32),
                        pltpu.VMEM((n_heads // 2, LANES, 2 * qb), _bf16),
                        pltpu.VMEM((n_heads, 1, qb), _f32), pltpu.VMEM((n_heads, 1, qb), _f32),
                        pltpu.VMEM((n_heads, HEAD_DIM, qb), _f32)],
    )
    o_t = pl.pallas_call(
        functools.partial(_dsa_prompt_body, n_heads, k_sel, t),
        grid_spec=grid_spec,
        out_shape=jax.ShapeDtypeStruct((n, d_b, t), _bf16),
        compiler_params=_params("arbitrary", "arbitrary"),
        name="dsa_prompt",
    )(far, tr(q), tr(qi), tr(wi), _bf(k), tr(_bf(v)), _bf(ki), near)
    return tr(o_t)


def _dsa_sample_index_body(n_pages, page, pg, pt_ref, qi_ref, wi_ref, kin_ref, *rest):
    cki_refs, (sc_ref, snew_ref) = rest[:pg], rest[pg:]
    step = pl.program_id(1)
    w8 = _bf(wi_ref[0] * (H_IDX ** -0.5)).astype(_f32)
    relu_bf = lambda s: _bf(jnp.maximum(s, 0.0)).astype(_f32)
    qi8 = qi_ref[0]
    for j in range(pg):
        c0 = pl.multiple_of((step * pg + j) * page, page)
        s8 = _dot(qi8, _bf(cki_refs[j][0, 0])) * (D_IDX ** -0.5)
        sc_ref[0, :, pl.ds(c0, page)] = jnp.sum(w8 * relu_bf(s8), axis=0, keepdims=True)

    @pl.when(step == n_pages // pg - 1)
    def _():
        s_new8 = jnp.sum(qi8.astype(_f32) * _bf(kin_ref[0]).astype(_f32), axis=1, keepdims=True) * (D_IDX ** -0.5)
        snew_ref[0] = jnp.sum(w8 * relu_bf(s_new8), axis=0, keepdims=True)


def _dsa_sample_select_body(k_sel, sc_ref, snew_ref, thr_ref, cut_ref):
    sc = sc_ref[...]
    s_new = snew_ref[...]
    rows, past = sc.shape
    pos = lax.broadcasted_iota(jnp.int32, sc.shape, 1)

    def count(pred):
        hits = jnp.sum(jnp.where(pred(sc, pos), 1.0, 0.0), axis=1, keepdims=True)
        return hits + jnp.where(pred(s_new, past), 1.0, 0.0)

    k_row = jnp.full((rows, 1), float(k_sel), _f32)
    thr = _kth_largest(lambda cand: count(lambda s, ps: s >= cand), k_row, (rows, 1))
    budget = k_row - count(lambda s, ps: s > thr)
    thr_ref[...] = thr
    cut_ref[...] = _tie_cut(lambda c: count(lambda s, ps: (s == thr) & (ps < c)), budget,
                            (2 * (past + 1)).bit_length(), (rows, 1))


def _dsa_sample_probs_body(n_heads, n_pages, page, pg, pt_ref, q_ref, kn_ref, sc_ref, snew_ref, thr_ref, cut_ref, *rest):
    ck_refs = rest[:pg]
    btab_ref, bnew_ref, p_ref, pnew_ref, lg_ref = rest[pg:]
    step = pl.program_id(1)
    past = n_pages * page
    q8 = q_ref[0]

    for j in range(pg):
        c0 = pl.multiple_of((step * pg + j) * page, page)
        rows = [_dot(q8, _bf(ck_refs[j][0, 0, h]))[h:h + 1] for h in range(n_heads)]
        lg = jnp.concatenate(rows, axis=0) * (HEAD_DIM ** -0.5)
        lg_ref[:, pl.ds(c0, page)] = lg + btab_ref[:, pl.ds(c0, page)]

    @pl.when(step == n_pages // pg - 1)
    def _():
        sc = sc_ref[0]
        s_new = snew_ref[0]
        thr = thr_ref[0]
        cut = cut_ref[0]
        pos = lax.broadcasted_iota(jnp.int32, sc.shape, 1)
        sel = (sc > thr) | ((sc == thr) & (pos < cut))
        sel_new = (s_new > thr) | ((s_new == thr) & (past < cut))

        kn = _bf(kn_ref[0]).astype(_f32)
        lg_new = jnp.sum(q8.astype(_f32) * kn, axis=1, keepdims=True) * (HEAD_DIM ** -0.5) + bnew_ref[...]
        lg_new = jnp.where(sel_new, lg_new, NEG)
        s_all = jnp.where(sel, lg_ref[...], NEG)
        m = jnp.maximum(jnp.maximum(jnp.max(s_all, axis=1, keepdims=True), lg_new), M_INIT)
        pr = jnp.exp(s_all - m)
        pr_new = jnp.exp(lg_new - m)
        l = jnp.sum(pr, axis=1, keepdims=True) + pr_new
        p_ref[0] = pr
        col = lax.broadcasted_iota(jnp.int32, (n_heads, 2), 1)
        pnew_ref[0] = jnp.where(col == 0, pr_new, l)


def _dsa_sample_pv_body(n_heads, n_pages, page, pg, pt_ref, p_ref, pnew_ref, vn_ref, *rest):
    cv_refs, (o_ref, acc_ref) = rest[:pg], rest[pg:]
    step = pl.program_id(1)

    @pl.when(step == 0)
    def _():
        acc_ref[...] = pnew_ref[0][:, 0:1] * _bf(vn_ref[0]).astype(_f32)

    acc = acc_ref[...]
    for j in range(pg):
        c0 = pl.multiple_of((step * pg + j) * page, page)
        pb = _bf(p_ref[0, :, pl.ds(c0, page)])
        rows = [_dot_nt(pb, _bf(cv_refs[j][0, 0, h]))[h:h + 1] for h in range(n_heads)]
        acc = acc + jnp.concatenate(rows, axis=0)
    acc_ref[...] = acc

    @pl.when(step == n_pages // pg - 1)
    def _():
        o_ref[0] = _bf(acc / pnew_ref[0][:, 1:2])


def _dsa_sample(q, k_new, v_new, qi, ki_new, wi, ck, cv, cki, layer, page_table, rel_bias):
    b, _, d_b = q.shape
    n_heads = d_b // HEAD_DIM
    n_pages = page_table.shape[1]
    page = ck.shape[2]
    ck, cv, cki = (jnp.moveaxis(a, 2, -1) for a in (ck, cv, cki))
    past = n_pages * page
    k_sel = min(TOPK_MAX, (past + 1) // 4)
    pg = max(g for g in (1, 2, 4, 8, 16) if n_pages % g == 0)
    kpos = jnp.arange(past, dtype=jnp.int32)
    btab = rel_bias[_rel_bucket(past - kpos)].T
    bnew = rel_bias[_rel_bucket(jnp.zeros((1,), jnp.int32))].T
    heads = lambda a: a.reshape(b, n_heads, HEAD_DIM)
    pt = page_table.reshape(-1)
    per_b = lambda shape: pl.BlockSpec((1,) + shape, lambda i, s, pt: (i,) + (0,) * len(shape))
    const = lambda a: pl.BlockSpec(a.shape, lambda i, s, pt: (0,) * a.ndim)

    def paged(tail, j):
        return pl.BlockSpec((1, 1) + tail + (page,),
                            lambda i, s, pt: (layer, pt[i * n_pages + s * pg + j]) + (0,) * (len(tail) + 1))

    grid = (b, n_pages // pg)
    scores, s_new = pl.pallas_call(
        functools.partial(_dsa_sample_index_body, n_pages, page, pg),
        grid_spec=pltpu.PrefetchScalarGridSpec(
            num_scalar_prefetch=1, grid=grid,
            in_specs=[per_b((H_IDX, D_IDX)), per_b((H_IDX, 1)), per_b((1, D_IDX))]
                     + [paged((D_IDX,), j) for j in range(pg)],
            out_specs=[per_b((1, past)), per_b((1, 1))]),
        out_shape=[jax.ShapeDtypeStruct((b, 1, past), _f32), jax.ShapeDtypeStruct((b, 1, 1), _f32)],
        compiler_params=_params("arbitrary", "arbitrary"),
        name="dsa_sample_index",
    )(pt, qi.reshape(b, H_IDX, D_IDX), wi.reshape(b, H_IDX, 1), ki_new, *([cki] * pg))
    thr, cut = pl.pallas_call(
        functools.partial(_dsa_sample_select_body, k_sel),
        out_shape=[jax.ShapeDtypeStruct((b, 1), _f32), jax.ShapeDtypeStruct((b, 1), jnp.int32)],
        compiler_params=pltpu.CompilerParams(vmem_limit_bytes=VMEM_LIMIT),
        name="dsa_sample_select",
    )(scores.reshape(b, past), s_new.reshape(b, 1))
    probs, p_new = pl.pallas_call(
        functools.partial(_dsa_sample_probs_body, n_heads, n_pages, page, pg),
        grid_spec=pltpu.PrefetchScalarGridSpec(
            num_scalar_prefetch=1, grid=grid,
            in_specs=[per_b((n_heads, HEAD_DIM)), per_b((n_heads, HEAD_DIM)), per_b((1, past)), per_b((1, 1)),
                      per_b((1, 1)), per_b((1, 1))]
                     + [paged((n_heads, HEAD_DIM), j) for j in range(pg)]
                     + [const(btab), const(bnew)],
            out_specs=[per_b((n_heads, past)), per_b((n_heads, 2))],
            scratch_shapes=[pltpu.VMEM((n_heads, past), _f32)]),
        out_shape=[jax.ShapeDtypeStruct((b, n_heads, past), _f32), jax.ShapeDtypeStruct((b, n_heads, 2), _f32)],
        compiler_params=_params("arbitrary", "arbitrary"),
        name="dsa_sample_probs",
    )(pt, heads(q), heads(k_new), scores, s_new, thr.reshape(b, 1, 1), cut.reshape(b, 1, 1),
      *([ck] * pg), btab, bnew)
    out = pl.pallas_call(
        functools.partial(_dsa_sample_pv_body, n_heads, n_pages, page, pg),
        grid_spec=pltpu.PrefetchScalarGridSpec(
            num_scalar_prefetch=1, grid=grid,
            in_specs=[per_b((n_heads, past)), per_b((n_heads, 2)), per_b((n_heads, HEAD_DIM))]
                     + [paged((n_heads, HEAD_DIM), j) for j in range(pg)],
            out_specs=per_b((n_heads, HEAD_DIM)),
            scratch_shapes=[pltpu.VMEM((n_heads, HEAD_DIM), _f32)]),
        out_shape=jax.ShapeDtypeStruct((b, n_heads, HEAD_DIM), _bf16),
        compiler_params=_params("arbitrary", "arbitrary"),
        name="dsa_sample_pv",
    )(pt, probs, p_new, heads(v_new), *([cv] * pg))
    return out.reshape(b, 1, d_b)


def _ffn_body(routed, final_norm, n_exp,
              x_ref, g_ref, sh_ref, sc_ref, gate_ref, rw_ref, rb_ref, w1_ref, w3_ref, w2_ref, gf_ref,
              o_ref, h_ref, acc_ref, dg_ref):
    e = pl.program_id(1)
    f = pl.program_id(2)
    first = (e == 0) & (f == 0)
    last = (e == pl.num_programs(1) - 1) & (f == pl.num_programs(2) - 1)

    @pl.when(first)
    def _():
        h = _norm_mod(x_ref[...], g_ref[...], sh_ref[0], sc_ref[0])
        h_ref[...] = _bf(h)
        acc_ref[...] = jnp.zeros(acc_ref.shape, _f32)
        if routed:
            logits = _dot(_bf(h), rw_ref[...]) + rb_ref[...]
            lane = lax.broadcasted_iota(jnp.int32, logits.shape, 1).astype(_f32)
            logits = jnp.where(lane < n_exp, logits, -jnp.inf)
            v1 = jnp.max(logits, axis=1, keepdims=True)
            i1 = jnp.min(jnp.where(logits == v1, lane, float(LANES)), axis=1, keepdims=True)
            rest = jnp.where(lane == i1, -jnp.inf, logits)
            v2 = jnp.max(rest, axis=1, keepdims=True)
            i2 = jnp.min(jnp.where(rest == v2, lane, float(LANES)), axis=1, keepdims=True)
            e2 = jnp.exp(v2 - v1)
            den = 1.0 + e2
            dg_ref[...] = jnp.where(lane == i1, 1.0 / den, 0.0) + jnp.where(lane == i2, e2 / den, 0.0)

    h = h_ref[...]
    a = _silu(_dot(h, w1_ref[0])) * _dot(h, w3_ref[0])
    y = _dot(_bf(a), w2_ref[0])
    if routed:
        lane = lax.broadcasted_iota(jnp.int32, dg_ref.shape, 1)
        y = y * jnp.sum(jnp.where(lane == e, dg_ref[...], 0.0), axis=1, keepdims=True)
    acc_ref[...] += y

    @pl.when(last)
    def _():
        y = x_ref[...] + gate_ref[0] * acc_ref[...]
        if final_norm:
            y = y * lax.rsqrt(jnp.mean(y * y, axis=-1, keepdims=True) + EPS) * gf_ref[...]
        o_ref[...] = y


def _ffn(x, g, shift, scale, gate, w13, w2, tm, tf, rows_per_group, router=None, final_g=None):
    m, d = x.shape
    n_exp, ff, _ = w2.shape
    routed = router is not None
    final_norm = final_g is not None
    if routed:
        rw, rb = router
        rw = _bf(jnp.pad(rw, ((0, 0), (0, LANES - n_exp))))
        rb = jnp.pad(rb.reshape(1, -1), ((0, 0), (0, LANES - n_exp)))
    else:
        rw = jnp.zeros((d, LANES), _bf16)
        rb = jnp.zeros((1, LANES), _f32)
    gf = final_g.reshape(1, -1) if final_norm else jnp.ones((1, d), _f32)
    nf = ff // tf
    gs = _group_spec(tm, rows_per_group, d)
    rowd = pl.BlockSpec((tm, d), lambda i, e, f: (i, 0))
    full = lambda a: pl.BlockSpec(a.shape, lambda i, e, f: (0, 0))
    return pl.pallas_call(
        functools.partial(_ffn_body, routed, final_norm, n_exp),
        grid=(m // tm, n_exp, nf),
        in_specs=[rowd, full(g), gs(shift.shape[1]), gs(scale.shape[1]), gs(gate.shape[1]), full(rw), full(rb),
                  pl.BlockSpec((1, d, tf), lambda i, e, f: (e, 0, f)),
                  pl.BlockSpec((1, d, tf), lambda i, e, f: (e, 0, f + nf)),
                  pl.BlockSpec((1, tf, d), lambda i, e, f: (e, f, 0)),
                  full(gf)],
        out_specs=rowd,
        out_shape=jax.ShapeDtypeStruct((m, d), _f32),
        scratch_shapes=[pltpu.VMEM((tm, d), _bf16), pltpu.VMEM((tm, d), _f32), pltpu.VMEM((tm, LANES), _f32)],
        compiler_params=_params("arbitrary", "arbitrary", "arbitrary"),
        name="moe_ffn" if routed else "dense_ffn",
    )(x, g, shift, scale, gate, rw, rb, w13, w13, w2, gf)


MOE_CAP = 512


def _top2_gates(logits, n_exp):
    lane = lax.broadcasted_iota(jnp.int32, logits.shape, 1).astype(_f32)
    logits = jnp.where(lane < n_exp, logits, -jnp.inf)
    v1 = jnp.max(logits, axis=1, keepdims=True)
    i1 = jnp.min(jnp.where(logits == v1, lane, float(LANES)), axis=1, keepdims=True)
    rest = jnp.where(lane == i1, -jnp.inf, logits)
    v2 = jnp.max(rest, axis=1, keepdims=True)
    i2 = jnp.min(jnp.where(rest == v2, lane, float(LANES)), axis=1, keepdims=True)
    e2 = jnp.exp(v2 - v1)
    den = 1.0 + e2
    gates = jnp.where(lane == i1, 1.0 / den, 0.0) + jnp.where(lane == i2, e2 / den, 0.0)
    routed = jnp.where((lane == i1) | (lane == i2), 1.0, 0.0)
    return gates, routed


def _moe_route_body(n_exp, x_ref, g_ref, sh_ref, sc_ref, rw_ref, rb_ref, ltri_ref,
                    xs_ref, dg_ref, slot_ref, top_ref):
    tm = x_ref.shape[0]
    hb = _bf(_norm_mod(x_ref[...], g_ref[...], sh_ref[0], sc_ref[0]))
    gates, routed = _top2_gates(_dot(hb, rw_ref[...]) + rb_ref[...], n_exp)
    dg_ref[...] = gates
    rank = _dot(ltri_ref[...], _bf(routed))
    slot = jnp.where(routed > 0.5, rank, -1.0)
    slot_ref[...] = slot
    slot_t = slot.T
    want = lax.broadcasted_iota(jnp.int32, (LANES, tm), 0).astype(_f32)
    for e in range(n_exp):
        s_row = slot_t[e:e + 1, :]
        for c in range(xs_ref.shape[1]):
            onehot = jnp.where(s_row == want + float(c * LANES), 1.0, 0.0)
            xs_ref[e, c, 0] = _bf(_dot(_bf(onehot), hb))
    top_ref[0] = jnp.max(slot, axis=0, keepdims=True)


def _moe_expert_body(used_ref, x_ref, w1_ref, w3_ref, w2_ref, o_ref, acc_ref):
    f = pl.program_id(2)

    @pl.when(f == 0)
    def _():
        acc_ref[...] = jnp.zeros(acc_ref.shape, _f32)

    @pl.when(used_ref[pl.program_id(0) * pl.num_programs(1) + pl.program_id(1)] > 0)
    def _():
        x = x_ref[0]
        a = _silu(_dot(x, w1_ref[0])) * _dot(x, w3_ref[0])
        acc_ref[...] += _dot(_bf(a), w2_ref[0])

    @pl.when(f == pl.num_programs(2) - 1)
    def _():
        o_ref[0] = _bf(acc_ref[...])


def _moe_combine_body(n_exp, final_norm, x_ref, gate_ref, dg_ref, slot_ref, ys_ref, gf_ref, o_ref):
    tm = x_ref.shape[0]
    cap = ys_ref.shape[1] * LANES
    lane = lax.broadcasted_iota(jnp.int32, (tm, LANES), 1)
    col = lax.broadcasted_iota(jnp.int32, (tm, cap), 1).astype(_f32)
    dg = dg_ref[...]
    slot = slot_ref[...]
    acc = jnp.zeros(x_ref.shape, _f32)
    for e in range(n_exp):
        slot_e = jnp.sum(jnp.where(lane == e, slot, 0.0), axis=1, keepdims=True)
        gate_e = jnp.sum(jnp.where(lane == e, dg, 0.0), axis=1, keepdims=True)
        scatter = _bf(jnp.where(slot_e == col, 1.0, 0.0))
        acc = acc + gate_e * _dot(scatter, ys_ref[e, :, 0].reshape(cap, ys_ref.shape[4]))
    y = x_ref[...] + gate_ref[0] * acc
    if final_norm:
        y = y * lax.rsqrt(jnp.mean(y * y, axis=-1, keepdims=True) + EPS) * gf_ref[...]
    o_ref[...] = y


def _moe_sparse(x, g, shift, scale, gate, w13, w2, tm, tf, rows_per_group, router, final_g):
    m, d = x.shape
    n_exp, ff, _ = w2.shape
    rw, rb = router
    rw = _bf(jnp.pad(rw, ((0, 0), (0, LANES - n_exp))))
    rb = jnp.pad(rb.reshape(1, -1), ((0, 0), (0, LANES - n_exp)))
    final_norm = final_g is not None
    gf = final_g.reshape(1, -1) if final_norm else jnp.ones((1, d), _f32)
    tiles = m // tm
    cap = MOE_CAP
    ltri = (jnp.arange(tm, dtype=jnp.int32)[:, None] > jnp.arange(tm, dtype=jnp.int32)[None, :]).astype(_bf16)
    gs = _group_spec(tm, rows_per_group, d)
    rowd = pl.BlockSpec((tm, d), lambda i: (i, 0))
    rowl = pl.BlockSpec((tm, LANES), lambda i: (i, 0))
    full = lambda a: pl.BlockSpec(a.shape, lambda i: (0,) * a.ndim)
    n_chunks = cap // LANES
    xs_spec = pl.BlockSpec((n_exp, n_chunks, 1, LANES, d), lambda i: (0, 0, i, 0, 0))
    xs, dg, slot, top = pl.pallas_call(
        functools.partial(_moe_route_body, n_exp),
        grid=(tiles,),
        in_specs=[rowd, full(g), gs(shift.shape[1]), gs(scale.shape[1]), full(rw), full(rb), full(ltri)],
        out_specs=[xs_spec, rowl, rowl, pl.BlockSpec((1, 1, LANES), lambda i: (i, 0, 0))],
        out_shape=[jax.ShapeDtypeStruct((n_exp, n_chunks, tiles, LANES, d), _bf16), jax.ShapeDtypeStruct((m, LANES), _f32),
                   jax.ShapeDtypeStruct((m, LANES), _f32), jax.ShapeDtypeStruct((tiles, 1, LANES), _f32)],
        compiler_params=_params("arbitrary"),
        name="moe_route",
    )(x, g, shift, scale, rw, rb, ltri)

    tr = tiles * LANES
    nf = ff // tf
    top = top.reshape(tiles, LANES)[:, :n_exp]
    used = (jnp.max(top, axis=0)[:, None] >= (jnp.arange(n_chunks, dtype=_f32) * LANES)[None, :]).astype(jnp.int32)

    def dense_path():
        return _ffn(x, g, shift, scale, gate, w13, w2, tm, tf, rows_per_group, router=router, final_g=final_g)

    def sparse_path():
        ys = pl.pallas_call(
            _moe_expert_body,
            grid_spec=pltpu.PrefetchScalarGridSpec(
                num_scalar_prefetch=1, grid=(n_exp, n_chunks, nf),
                in_specs=[pl.BlockSpec((1, tr, d), lambda e, c, f, u: (e, c, 0)),
                          pl.BlockSpec((1, d, tf), lambda e, c, f, u: (e, 0, f)),
                          pl.BlockSpec((1, d, tf), lambda e, c, f, u: (e, 0, f + nf)),
                          pl.BlockSpec((1, tf, d), lambda e, c, f, u: (e, f, 0))],
                out_specs=pl.BlockSpec((1, tr, d), lambda e, c, f, u: (e, c, 0)),
                scratch_shapes=[pltpu.VMEM((tr, d), _f32)]),
            out_shape=jax.ShapeDtypeStruct((n_exp, n_chunks * tr, d), _bf16),
            compiler_params=_params("arbitrary", "arbitrary", "arbitrary"),
            name="moe_experts",
        )(used.reshape(-1), xs.reshape(n_exp, n_chunks * tr, d), w13, w13, w2)
        return pl.pallas_call(
            functools.partial(_moe_combine_body, n_exp, final_norm),
            grid=(tiles,),
            in_specs=[rowd, gs(gate.shape[1]), rowl, rowl, xs_spec, full(gf)],
            out_specs=rowd,
            out_shape=jax.ShapeDtypeStruct((m, d), _f32),
            compiler_params=_params("arbitrary"),
            name="moe_combine",
        )(x, gate, dg, slot, ys.reshape(n_exp, n_chunks, tiles, LANES, d), gf)

    return lax.cond(jnp.max(top) >= cap, dense_path, sparse_path)


def _cmul(ar, ai, br, bi):
    return ar * br - ai * bi, ar * bi + ai * br


def _s5_io(u, b_ref, c_ref, d_ref, are_ref, aim_ref, h_of_bb):
    half = b_ref.shape[2] // 2
    bu = _dot(_bf(u), b_ref[0])
    bb_re, bb_im = _cmul(are_ref[0, 1:2], aim_ref[0, 1:2], bu[:, :half], bu[:, half:])
    h_re, h_im = h_of_bb(bb_re, bb_im)
    y = _dot(_bf(jnp.concatenate([h_re, h_im], axis=1)), c_ref[0]) + d_ref[0] * u
    return _bf(_gelu_tanh(y)), h_re, h_im


def _s5_scan_body(tc, u_ref, b_ref, c_ref, d_ref, are_ref, aim_ref, h0r_ref, h0i_ref,
                  y_ref, hr_ref, hi_ref, xr_ref, xi_ref):
    t_blk = pl.program_id(2)

    @pl.when(t_blk == 0)
    def _():
        hr_ref[0, 0] = h0r_ref[0, 0]
        hi_ref[0, 0] = h0i_ref[0, 0]

    width = are_ref.shape[2]
    a1 = (jnp.broadcast_to(are_ref[0, 0:1], (SUBLANES, width)), jnp.broadcast_to(aim_ref[0, 0:1], (SUBLANES, width)))
    a2 = _cmul(*a1, *a1)
    a3 = _cmul(*a2, *a1)
    a4 = _cmul(*a2, *a2)
    a5 = _cmul(*a4, *a1)
    a6 = _cmul(*a4, *a2)
    a7 = _cmul(*a4, *a3)
    a8 = _cmul(*a4, *a4)
    row = lax.broadcasted_iota(jnp.int32, (SUBLANES, width), 0)
    lvl = [tuple(jnp.where(row >= s, c, 0.0) for c in a) for s, a in ((1, a1), (2, a2), (4, a4))]
    pw = []
    for comp in range(2):
        acc = a8[comp]
        for s, a in ((6, a7), (5, a6), (4, a5), (3, a4), (2, a3), (1, a2), (0, a1)):
            acc = jnp.where(row == s, a[comp], acc)
        pw.append(acc)

    def scan_chunk(bb_re, bb_im):
        xr_ref[...] = bb_re
        xi_ref[...] = bb_im

        def tile(j, carry):
            cr, ci = carry
            r0 = pl.multiple_of(j * SUBLANES, SUBLANES)
            xr = xr_ref[pl.ds(r0, SUBLANES), :]
            xi = xi_ref[pl.ds(r0, SUBLANES), :]
            for (s, (fr, fi)) in zip((1, 2, 4), lvl):
                sr = pltpu.roll(xr, s, axis=0)
                si = pltpu.roll(xi, s, axis=0)
                pr, pi = _cmul(fr, fi, sr, si)
                xr = xr + pr
                xi = xi + pi
            pr, pi = _cmul(pw[0], pw[1], jnp.broadcast_to(cr, xr.shape), jnp.broadcast_to(ci, xi.shape))
            xr = xr + pr
            xi = xi + pi
            xr_ref[pl.ds(r0, SUBLANES), :] = xr
            xi_ref[pl.ds(r0, SUBLANES), :] = xi
            return xr[SUBLANES - 1:SUBLANES], xi[SUBLANES - 1:SUBLANES]

        cr, ci = lax.fori_loop(0, tc // SUBLANES, tile, (hr_ref[0, 0], hi_ref[0, 0]))
        hr_ref[0, 0] = cr
        hi_ref[0, 0] = ci
        return xr_ref[...], xi_ref[...]

    y, _, _ = _s5_io(u_ref[0], b_ref, c_ref, d_ref, are_ref, aim_ref, scan_chunk)
    y_ref[0] = y


def _s5_step_body(u_ref, b_ref, c_ref, d_ref, are_ref, aim_ref, h0r_ref, h0i_ref, y_ref, hr_ref, hi_ref):
    def one_step(bb_re, bb_im):
        pr, pi = _cmul(are_ref[0, 0:1], aim_ref[0, 0:1], h0r_ref[0], h0i_ref[0])
        return pr + bb_re, pi + bb_im

    y, h_re, h_im = _s5_io(u_ref[...], b_ref, c_ref, d_ref, are_ref, aim_ref, one_step)
    y_ref[...] = y
    hr_ref[0] = h_re
    hi_ref[0] = h_im


def _s5_tables(a_re, a_im, log_step, b_re, b_im, c_re, c_im, d):
    g_c, p_c, ch = b_re.shape
    gs = LANES // ch
    ns = g_c // gs
    dt = jnp.exp(log_step.astype(_f32))[:, None]
    mag = jnp.exp(dt * a_re)
    ab_re = mag * jnp.cos(dt * a_im)
    ab_im = mag * jnp.sin(dt * a_im)
    den = a_re * a_re + a_im * a_im
    nr = ab_re - 1.0
    co_re = (nr * a_re + ab_im * a_im) / den
    co_im = (ab_im * a_re - nr * a_im) / den
    eye = jnp.eye(gs, dtype=_f32)

    def pack_b(b):
        b = b.reshape(ns, gs, p_c, ch)
        return jnp.einsum('sgpc,gh->sgchp', b, eye).reshape(ns, gs * ch, gs * p_c)

    def pack_c(c):
        c = c.reshape(ns, gs, ch, p_c)
        return jnp.einsum('sgcp,gh->sgphc', c, eye).reshape(ns, gs * p_c, gs * ch)

    b_pack = _bf(jnp.concatenate([pack_b(b_re), pack_b(b_im)], axis=2))
    c_pack = _bf(jnp.concatenate([pack_c(c_re), -pack_c(c_im)], axis=1))
    vec = lambda a, c: jnp.stack([a.reshape(ns, gs * p_c), c.reshape(ns, gs * p_c)], axis=1)
    return b_pack, c_pack, d.reshape(ns, 1, gs * ch), vec(ab_re, co_re), vec(ab_im, co_im)


def _s5_scan(u, h0_re, h0_im, tables, tc):
    n, t, d = u.shape
    b_pack, c_pack, dvec, are, aim = tables
    ns, _, w2 = b_pack.shape
    w = w2 // 2
    h0r = h0_re.reshape(n, ns, 1, w)
    h0i = h0_im.reshape(n, ns, 1, w)
    tab = lambda a: pl.BlockSpec((1,) + a.shape[1:], lambda b, s, j: (s, 0, 0))
    st = pl.BlockSpec((1, 1, 1, w), lambda b, s, j: (b, s, 0, 0))
    seq = pl.BlockSpec((1, tc, LANES), lambda b, s, j: (b, j, s))
    y, hr, hi = pl.pallas_call(
        functools.partial(_s5_scan_body, tc),
        grid=(n, ns, t // tc),
        in_specs=[seq, tab(b_pack), tab(c_pack), tab(dvec), tab(are), tab(aim), st, st],
        out_specs=[seq, st, st],
        out_shape=[jax.ShapeDtypeStruct((n, t, d), _bf16),
                   jax.ShapeDtypeStruct((n, ns, 1, w), _f32), jax.ShapeDtypeStruct((n, ns, 1, w), _f32)],
        scratch_shapes=[pltpu.VMEM((tc, w), _f32), pltpu.VMEM((tc, w), _f32)],
        compiler_params=_params("arbitrary", "arbitrary", "arbitrary"),
        name="s5_scan",
    )(u, b_pack, c_pack, dvec, are, aim, h0r, h0i)
    return y, hr.reshape(n, ns * w), hi.reshape(n, ns * w)


def _s5_step(u, h0_re, h0_im, tables):
    b, d = u.shape
    b_pack, c_pack, dvec, are, aim = tables
    ns, _, w2 = b_pack.shape
    w = w2 // 2
    h0r = h0_re.reshape(b, ns, w).transpose(1, 0, 2)
    h0i = h0_im.reshape(b, ns, w).transpose(1, 0, 2)
    tab = lambda a: pl.BlockSpec((1,) + a.shape[1:], lambda s: (s, 0, 0))
    st = pl.BlockSpec((1, b, w), lambda s: (s, 0, 0))
    col = pl.BlockSpec((b, LANES), lambda s: (0, s))
    y, hr, hi = pl.pallas_call(
        _s5_step_body,
        grid=(ns,),
        in_specs=[col, tab(b_pack), tab(c_pack), tab(dvec), tab(are), tab(aim), st, st],
        out_specs=[col, st, st],
        out_shape=[jax.ShapeDtypeStruct((b, d), _bf16),
                   jax.ShapeDtypeStruct((ns, b, w), _f32), jax.ShapeDtypeStruct((ns, b, w), _f32)],
        compiler_params=_params("arbitrary"),
        name="s5_step",
    )(u, b_pack, c_pack, dvec, are, aim, h0r, h0i)
    return y, hr.transpose(1, 0, 2).reshape(b, ns * w), hi.transpose(1, 0, 2).reshape(b, ns * w)


def _glu_out_body(y_ref, x_ref, gate_ref, wl_ref, wr_ref, o_ref):
    y = y_ref[...]
    z = _dot(y, wl_ref[...]) * jax.nn.sigmoid(_dot(y, wr_ref[...]))
    o_ref[...] = x_ref[...] + gate_ref[0] * z


def _glu_out(yg, x, gate, w_glu, tm, rows_per_group):
    m, d = x.shape
    wl = _bf(w_glu[:, :d])
    wr = _bf(w_glu[:, d:])
    gs = _group_spec(tm, rows_per_group, d)
    rowd = pl.BlockSpec((tm, d), lambda i: (i, 0))
    full = lambda a: pl.BlockSpec(a.shape, lambda i: (0, 0))
    return pl.pallas_call(
        _glu_out_body,
        grid=(m // tm,),
        in_specs=[rowd, rowd, gs(gate.shape[1]), full(wl), full(wr)],
        out_specs=rowd,
        out_shape=jax.ShapeDtypeStruct((m, d), _f32),
        compiler_params=_params("arbitrary"),
        name="glu_out",
    )(yg, x, gate, wl, wr)


def _tile(m, target):
    if m <= target:
        return m
    t = target
    while m % t:
        t -= SUBLANES
    return t


def _run_group(x, mods, is_prompt, st, wts):
    n, t, d = x.shape
    m = n * t
    depth = wts["norm_mix"].shape[0]
    d_a = d // 2
    d_b = d // 2
    n_heads = d_a // HEAD_DIM
    rows_per_group = t if is_prompt else m
    tm = _tile(t, 256) if is_prompt else m
    xf = x.reshape(m, d).astype(_f32)

    def mod_vecs(l):
        parts = jnp.split(mods[l], 6, axis=-1)
        if is_prompt:
            return [p.reshape(n, 1, d) for p in parts]
        return [p.reshape(1, m, d) for p in parts]

    outs = {k: [] for k in ("shift", "wkv", "k", "v", "kidx", "re", "im")}
    for l in range(depth):
        sh_m, sc_m, g_m, sh_f, sc_f, g_f = mod_vecs(l)
        i = l // 2
        last = l == depth - 1
        final_g = wts["norm_final"] if last else None
        norm_mix = wts["norm_mix"][l].reshape(1, d)
        norm_ffn = wts["norm_ffn"][l].reshape(1, d)
        if l % 2 == 0:
            p_a, q, k, v, qi, ki, wi = _in_proj(xf, norm_mix, sh_m, sc_m, wts["e_w_in"][i], tm, rows_per_group)
            if is_prompt:
                prev0 = st["shift"][i].reshape(n, 1, -1)
            else:
                prev0 = st["shift"][i].reshape(1, m, -1)
            r, w, k_mod, v_a, av, bv, g, bonus = _rwkv_prep(
                p_a, prev0.astype(_f32), wts["e_mu"][i], wts["e_w0"][i], wts["e_w2"][i], wts["e_a0"][i],
                wts["e_a2"][i], wts["e_g2"][i], wts["e_k_k"][i], wts["e_k_a"][i], wts["e_r_k"][i],
                tm, rows_per_group)
            seq = lambda a: a.reshape(n, t, d_a)
            nb = n if is_prompt else _tile(n, 4)
            tc = _tile(t, 256)
            o, s_fin = _rwkv_scan(seq(r), seq(w), seq(k_mod), seq(v_a), seq(av), seq(bv),
                                  _pack_state(st["wkv"][i].astype(_f32)), nb, tc)
            if is_prompt:
                o_b = _dsa_prompt(q.reshape(n, t, d_b), k.reshape(n, t, d_b), v.reshape(n, t, d_b),
                                  qi.reshape(n, t, -1), ki.reshape(n, t, -1), wi.reshape(n, t, -1),
                                  wts["rel_bias"], 128)
            else:
                o_b = _dsa_sample(q.reshape(n, t, d_b), k.reshape(n, t, d_b), v.reshape(n, t, d_b),
                                  qi.reshape(n, t, -1), ki.reshape(n, t, -1), wi.reshape(n, t, -1),
                                  st["cache_k"], st["cache_v"], st["cache_kidx"], i,
                                  st["page_table"], wts["rel_bias"])
            xf = _mix_out(o.reshape(m, d_a), bonus, g, o_b.reshape(m, d_b), xf, g_m,
                          wts["e_lnx_w"][i], wts["e_lnx_b"][i], wts["e_w_out"][i], tm, rows_per_group)
            outs["shift"].append(p_a.reshape(n, t, -1)[:, -1].astype(x.dtype))
            outs["wkv"].append(_unpack_state(s_fin))
            outs["k"].append(k.reshape(n, t, n_heads, HEAD_DIM))
            outs["v"].append(v.reshape(n, t, n_heads, HEAD_DIM))
            outs["kidx"].append(ki.reshape(n, t, D_IDX))
            tm_f = _tile(t, 1024) if is_prompt else m
            ff = wts["ffn_w2"][i].shape[0]
            xf = _ffn(xf, norm_ffn, sh_f, sc_f, g_f, wts["ffn_w13"][i][None], wts["ffn_w2"][i][None],
                      tm_f, _tile(ff, 256) if ff % 256 == 0 else ff, rows_per_group, final_g=final_g)
        else:
            tables = _s5_tables(wts["o_a_re"][i], wts["o_a_im"][i], wts["o_log_step"][i], wts["o_b_re"][i],
                                wts["o_b_im"][i], wts["o_c_re"][i], wts["o_c_im"][i], wts["o_d"][i])
            u = _norm_mod_call(xf, norm_mix, sh_m, sc_m, tm, rows_per_group)
            h0r = st["ssm_re"][i].reshape(n, -1).astype(_f32)
            h0i = st["ssm_im"][i].reshape(n, -1).astype(_f32)
            if is_prompt:
                yg, hr, hi = _s5_scan(u.reshape(n, t, d), h0r, h0i, tables, _tile(t, 512))
            else:
                yg, hr, hi = _s5_step(u, h0r, h0i, tables)
            xf = _glu_out(yg.reshape(m, d), xf, g_m, wts["o_w_glu"][i], tm, rows_per_group)
            g_c = d // CH_G
            outs["re"].append(hr.reshape(n, g_c, P_C))
            outs["im"].append(hi.reshape(n, g_c, P_C))
            tm_f = _tile(t, 1024) if is_prompt else m
            ff = wts["moe_w2"][i].shape[1]
            moe = _moe_sparse if (is_prompt and tm_f == 1024) else _ffn
            xf = moe(xf, norm_ffn, sh_f, sc_f, g_f, wts["moe_w13"][i], wts["moe_w2"][i],
                     tm_f, _tile(ff, 512) if ff % 512 == 0 else ff, rows_per_group,
                     router=(wts["o_router_w"][i], wts["o_router_b"][i]), final_g=final_g)
    y = xf.reshape(n, t, d).astype(x.dtype)
    return (y, jnp.stack(outs["shift"]), jnp.stack(outs["wkv"]), jnp.stack(outs["k"]), jnp.stack(outs["v"]),
            jnp.stack(outs["kidx"]), jnp.stack(outs["re"]), jnp.stack(outs["im"]))


def kernel(x_prompt, x_sample, cache_k, cache_v, cache_kidx, state_shift, state_wkv, state_ssm_re, state_ssm_im, page_table, c_prompt, c_sample, norm_mix, norm_ffn, ada_w, ada_b, rel_bias, norm_final, e_w_in, e_mu, e_w0, e_w2, e_a0, e_a2, e_g2, e_k_k, e_k_a, e_r_k, e_lnx_w, e_lnx_b, e_w_out, e_ffn_w13, e_ffn_w2, o_a_re, o_a_im, o_log_step, o_b_re, o_b_im, o_c_re, o_c_im, o_d, o_w_glu, o_router_w, o_router_b, o_moe_w13, o_moe_w2):
    nb, _, d = x_prompt.shape
    n_dec = x_sample.shape[0]
    n_even = e_w_in.shape[0]
    n_odd = o_a_re.shape[0]
    d_a = d // 2
    n_heads = d_a // HEAD_DIM
    shift_w = state_shift.shape[-1]

    c_all = jnp.concatenate([c_prompt, c_sample], axis=0)
    pad = (-c_all.shape[0]) % SUBLANES
    mods = _ada(jnp.pad(c_all, ((0, pad), (0, 0))), ada_w, ada_b)
    mods_p = mods[:, :nb]
    mods_s = mods[:, nb:nb + n_dec]

    wts = dict(norm_mix=norm_mix, norm_ffn=norm_ffn, norm_final=norm_final, rel_bias=rel_bias,
               e_w_in=e_w_in, e_mu=e_mu, e_w0=e_w0, e_w2=e_w2, e_a0=e_a0, e_a2=e_a2, e_g2=e_g2, e_k_k=e_k_k,
               e_k_a=e_k_a, e_r_k=e_r_k, e_lnx_w=e_lnx_w, e_lnx_b=e_lnx_b, e_w_out=e_w_out,
               ffn_w13=_bf(e_ffn_w13), ffn_w2=_bf(e_ffn_w2),
               o_a_re=o_a_re, o_a_im=o_a_im, o_log_step=o_log_step, o_b_re=o_b_re, o_b_im=o_b_im,
               o_c_re=o_c_re, o_c_im=o_c_im, o_d=o_d, o_w_glu=o_w_glu, o_router_w=o_router_w,
               o_router_b=o_router_b, moe_w13=_bf(o_moe_w13), moe_w2=_bf(o_moe_w2))

    st_p = dict(shift=jnp.zeros((n_even, nb, shift_w), x_prompt.dtype),
                wkv=jnp.zeros((n_even, nb, n_heads, HEAD_DIM, HEAD_DIM), _f32),
                ssm_re=jnp.zeros((n_odd, nb, d // CH_G, P_C), _f32),
                ssm_im=jnp.zeros((n_odd, nb, d // CH_G, P_C), _f32))
    st_s = dict(shift=state_shift, wkv=state_wkv, ssm_re=state_ssm_re, ssm_im=state_ssm_im,
                cache_k=cache_k, cache_v=cache_v, cache_kidx=cache_kidx, page_table=page_table)
    out_p = _run_group(x_prompt, mods_p, True, st_p, wts)
    out_s = _run_group(x_sample, mods_s, False, st_s, wts)
    return (out_p[0], out_s[0]) + out_p[1:] + out_s[1:]
```

```python
import functools
import math

import jax
import jax.numpy as jnp
from jax import lax
from jax.experimental import pallas as pl
from jax.experimental.pallas import tpu as pltpu

HEAD_DIM = 64
LORA_W = 64
LORA_A = 64
LORA_G = 128
H_IDX = 8
D_IDX = 64
TOPK_MAX = 256
N_BUCKETS = 32
MAX_DIST = 128
CH_G = 16
P_C = 64
TOP_E = 2
EPS = 1e-6
LNX_EPS = 64e-5

LANES = 128
SUBLANES = 8
VMEM_LIMIT = 56 * 1024 * 1024
KEY_BLOCK = 256
COUNT_BLOCK = 1024
COUNT_ROWS = 64
NEG = -2.0e30
M_INIT = -1.0e30

_bf16 = jnp.bfloat16
_f32 = jnp.float32


def _bf(x):
    return x.astype(_bf16)


def _dot(a, b):
    return jnp.dot(a, b, preferred_element_type=_f32)


def _dot_nt(a, b):
    return lax.dot_general(a, b, (((1,), (1,)), ((), ())), preferred_element_type=_f32)


def _params(*sem):
    return pltpu.CompilerParams(dimension_semantics=sem, vmem_limit_bytes=VMEM_LIMIT)


def _split_bf16(x, parts):
    out = []
    for _ in range(parts - 1):
        hi = _bf(x)
        out.append(hi)
        x = x - hi.astype(_f32)
    out.append(_bf(x))
    return out


def _segsum(x, ones_bd):
    hi, mid, lo = _split_bf16(x, 3)
    return _dot(hi, ones_bd) + _dot(mid, ones_bd) + _dot(lo, ones_bd)


def _norm_mod(x, g, shift, scale):
    ms = jnp.mean(x * x, axis=-1, keepdims=True)
    return (x * lax.rsqrt(ms + EPS) * g) * (1.0 + scale) + shift


def _silu(x):
    return x * jax.nn.sigmoid(x)


def _gelu_tanh(x):
    return 0.5 * x * (1.0 + jnp.tanh(math.sqrt(2.0 / math.pi) * (x + 0.044715 * (x * x * x))))


def _softplus(x):
    return jnp.maximum(x, 0.0) + jnp.log(1.0 + jnp.exp(-jnp.abs(x)))


def _rel_bucket(dist):
    max_exact = N_BUCKETS // 2
    n = jnp.maximum(dist, 0)
    nf = jnp.maximum(n, 1).astype(_f32)
    large = max_exact + (jnp.log(nf / max_exact) / math.log(MAX_DIST / max_exact) * (N_BUCKETS - max_exact)).astype(jnp.int32)
    return jnp.where(n < max_exact, n, jnp.minimum(large, N_BUCKETS - 1))


def _group_spec(rows_per_block, rows_per_group, width):
    def spec(r):
        return pl.BlockSpec((1, r, width), lambda i, *_: ((i * rows_per_block) // rows_per_group, 0, 0))
    return spec


def _ada_body(c_ref, w_ref, b_ref, o_ref):
    o_ref[0] = _dot(_bf(_silu(c_ref[...])), _bf(w_ref[0])) + b_ref[0]


def _ada(c, ada_w, ada_b):
    depth, d, n6 = ada_w.shape
    rows = c.shape[0]
    tn = n6 // 4
    return pl.pallas_call(
        _ada_body,
        grid=(depth, n6 // tn),
        in_specs=[pl.BlockSpec((rows, d), lambda l, j: (0, 0)),
                  pl.BlockSpec((1, d, tn), lambda l, j: (l, 0, j)),
                  pl.BlockSpec((1, 1, tn), lambda l, j: (l, 0, j))],
        out_specs=pl.BlockSpec((1, rows, tn), lambda l, j: (l, 0, j)),
        out_shape=jax.ShapeDtypeStruct((depth, rows, n6), _f32),
        compiler_params=_params("arbitrary", "arbitrary"),
        name="ada_mod",
    )(c, ada_w, ada_b.reshape(depth, 1, n6))


def _norm_mod_body(x_ref, g_ref, sh_ref, sc_ref, o_ref):
    o_ref[...] = _norm_mod(x_ref[...], g_ref[...], sh_ref[0], sc_ref[0])


def _norm_mod_call(x, g, shift, scale, tm, rows_per_group):
    m, d = x.shape
    gs = _group_spec(tm, rows_per_group, d)
    return pl.pallas_call(
        _norm_mod_body,
        grid=(m // tm,),
        in_specs=[pl.BlockSpec((tm, d), lambda i: (i, 0)),
                  pl.BlockSpec((1, d), lambda i: (0, 0)),
                  gs(shift.shape[1]), gs(scale.shape[1])],
        out_specs=pl.BlockSpec((tm, d), lambda i: (i, 0)),
        out_shape=jax.ShapeDtypeStruct((m, d), _f32),
        compiler_params=_params("arbitrary"),
        name="norm_mod",
    )(x, g, shift, scale)


def _in_proj_body(x_ref, g_ref, sh_ref, sc_ref, wa_ref, wq_ref, wk_ref, wv_ref, wqi_ref, wkw_ref,
                  pa_ref, q_ref, k_ref, v_ref, qi_ref, ki_ref, wi_ref):
    h = _bf(_norm_mod(x_ref[...], g_ref[...], sh_ref[0], sc_ref[0]))
    pa_ref[...] = _dot(h, wa_ref[...])
    q_ref[...] = _bf(_dot(h, wq_ref[...]))
    k_ref[...] = _dot(h, wk_ref[...])
    v_ref[...] = _dot(h, wv_ref[...])
    qi_ref[...] = _bf(_dot(h, wqi_ref[...]))
    kw = _dot(h, wkw_ref[...])
    ki_ref[...] = kw[:, :D_IDX]
    wi_ref[...] = kw[:, D_IDX:D_IDX + H_IDX]


def _in_proj(x, g, shift, scale, w_in, tm, rows_per_group):
    m, d = x.shape
    d_a = d // 2
    d_b = d // 2
    shift_w = 3 * d_a + LORA_W + LORA_A + LORA_G
    c1 = shift_w + 3 * d_b
    c2 = c1 + H_IDX * D_IDX
    wa = _bf(w_in[:, :shift_w])
    wq = _bf(w_in[:, shift_w:shift_w + d_b])
    wk = _bf(w_in[:, shift_w + d_b:shift_w + 2 * d_b])
    wv = _bf(w_in[:, shift_w + 2 * d_b:c1])
    wqi = _bf(w_in[:, c1:c2])
    wkw = _bf(jnp.pad(w_in[:, c2:], ((0, 0), (0, LANES - D_IDX - H_IDX))))
    gs = _group_spec(tm, rows_per_group, d)
    full = lambda a: pl.BlockSpec(a.shape, lambda i: (0, 0))
    row = lambda w: pl.BlockSpec((tm, w), lambda i: (i, 0))
    widths = [(shift_w, _f32), (d_b, _bf16), (d_b, _f32), (d_b, _f32), (H_IDX * D_IDX, _bf16),
              (D_IDX, _f32), (H_IDX, _f32)]
    return pl.pallas_call(
        _in_proj_body,
        grid=(m // tm,),
        in_specs=[row(d), pl.BlockSpec((1, d), lambda i: (0, 0)), gs(shift.shape[1]), gs(scale.shape[1]),
                  full(wa), full(wq), full(wk), full(wv), full(wqi), full(wkw)],
        out_specs=[row(w) for w, _ in widths],
        out_shape=[jax.ShapeDtypeStruct((m, w), dt) for w, dt in widths],
        compiler_params=_params("arbitrary"),
        name="in_proj",
    )(x, g, shift, scale, wa, wq, wk, wv, wqi, wkw)


def _rwkv_prep_body(seq_is_one, rows_per_group, tm,
                    p_ref, pprev_ref, prev0_ref, mu_ref, w0_ref, a0_ref, kk_ref, ka_ref, rk_ref,
                    wwa_ref, g2_ref, ones_ref,
                    r_ref, w_ref, k_ref, v_ref, av_ref, bv_ref, g_ref, bonus_ref):
    d_a = r_ref.shape[1]
    p = p_ref[...]
    if seq_is_one:
        p_prev = prev0_ref[0]
    else:
        i = pl.program_id(0)
        first = (i * tm) % rows_per_group == 0
        prev_row = jnp.where(first, prev0_ref[0], pprev_ref[SUBLANES - 1:SUBLANES, :])
        rolled = pltpu.roll(p, 1, axis=0)
        row_id = lax.broadcasted_iota(jnp.int32, p.shape, 0)
        p_prev = jnp.where(row_id == 0, prev_row, rolled)
    ps = p + (p_prev - p) * mu_ref[...]
    r = ps[:, :d_a]
    k = ps[:, d_a:2 * d_a]
    v = ps[:, 2 * d_a:3 * d_a]
    xwa = ps[:, 3 * d_a:3 * d_a + LORA_W + LORA_A]
    xg = ps[:, 3 * d_a + LORA_W + LORA_A:]
    lane = lax.broadcasted_iota(jnp.int32, xwa.shape, 1)
    xwa = jnp.where(lane < LORA_W, jnp.tanh(xwa), xwa)
    lwa = _dot(_bf(xwa), wwa_ref[...])
    w_log = -_softplus(-(w0_ref[...] + lwa[:, :d_a])) - 0.5
    decay = jnp.exp(-jnp.exp(w_log))
    a = jax.nn.sigmoid(a0_ref[...] + lwa[:, d_a:])
    g_ref[...] = _dot(_bf(jax.nn.sigmoid(xg)), g2_ref[...])
    ones_bd = ones_ref[...]
    kk = k * kk_ref[...]
    kk = kk / jnp.maximum(jnp.sqrt(_segsum(kk * kk, ones_bd)), 1e-12)
    k_mod = k * (1.0 + (a - 1.0) * ka_ref[...])
    r_ref[...] = r
    w_ref[...] = decay
    k_ref[...] = k_mod
    v_ref[...] = v
    av_ref[...] = -kk
    bv_ref[...] = kk * a
    bonus_ref[...] = _segsum(r * k_mod * rk_ref[...], ones_bd) * v


def _ones_blockdiag(n, group):
    idx = jnp.arange(n, dtype=jnp.int32) // group
    return (idx[:, None] == idx[None, :]).astype(_bf16)


def _rwkv_prep(p_a, prev0, mu, w0, w2, a0, a2, g2, k_k, k_a, r_k, tm, rows_per_group):
    m, shift_w = p_a.shape
    d_a = w0.shape[-1]
    seq_is_one = rows_per_group == tm and prev0.shape[1] == tm
    wwa = jnp.zeros((LORA_W + LORA_A, 2 * d_a), _f32)
    wwa = _bf(wwa.at[:LORA_W, :d_a].set(w2).at[LORA_W:, d_a:].set(a2))
    ones_bd = _ones_blockdiag(d_a, HEAD_DIM)
    vec = lambda a: a.reshape(1, -1)
    gs = _group_spec(tm, rows_per_group, shift_w)
    full = lambda a: pl.BlockSpec(a.shape, lambda i: (0, 0))
    row = pl.BlockSpec((tm, d_a), lambda i: (i, 0))
    ins = [p_a, p_a, prev0, vec(mu), vec(w0), vec(a0), vec(k_k), vec(k_a), vec(r_k), wwa, _bf(g2), ones_bd]
    in_specs = [pl.BlockSpec((tm, shift_w), lambda i: (i, 0)),
                pl.BlockSpec((SUBLANES, shift_w), lambda i: (jnp.maximum(i * (tm // SUBLANES) - 1, 0), 0)),
                gs(prev0.shape[1])] + [full(a) for a in ins[3:]]
    return pl.pallas_call(
        functools.partial(_rwkv_prep_body, seq_is_one, rows_per_group, tm),
        grid=(m // tm,),
        in_specs=in_specs,
        out_specs=[row] * 8,
        out_shape=[jax.ShapeDtypeStruct((m, d_a), _f32)] * 8,
        compiler_params=_params("arbitrary"),
        name="rwkv_prep",
    )(*ins)


def _rwkv_scan_body(nb, n_pairs, tc,
                    r_ref, w_ref, k_ref, v_ref, av_ref, bv_ref, s0_ref, ones_ref, expand_ref, eye_ref,
                    o_ref, s_ref):
    t_blk = pl.program_id(1)

    @pl.when(t_blk == 0)
    def _():
        s_ref[...] = s0_ref[...]

    ones_k = ones_ref[...]
    parts = ones_k.shape[0] // LANES
    expand = expand_ref[...]
    eye = eye_ref[...] > 0.5
    units = [(n, hp) for n in range(nb) for hp in range(n_pairs)]

    def rowsum(xs, ones, parts):
        lhs = jnp.concatenate([jnp.concatenate(_split_bf16(x, parts), axis=1) for x in xs], axis=0)
        res = _dot(lhs, ones)
        return [res[u * HEAD_DIM:(u + 1) * HEAD_DIM] for u in range(len(xs))]

    sub = min(SUBLANES, tc)
    row_id = lax.broadcasted_iota(jnp.int32, (sub, LANES), 0)

    def tile_steps(j, carry):
        t0 = pl.multiple_of(j * sub, sub)
        tiles = {}
        for (n, hp) in units:
            sl = pl.ds(hp * LANES, LANES)
            tiles[(n, hp)] = tuple(ref[n, pl.ds(t0, sub), sl] for ref in (r_ref, w_ref, k_ref, v_ref, av_ref, bv_ref))
        states = [s_ref[n, hp] for (n, hp) in units]
        o_tiles = [jnp.zeros((sub, LANES), _f32) for _ in units]
        vcols = []
        for u in units:
            vt = tiles[u][3]
            by_head = jnp.concatenate([vt[:, :HEAD_DIM], vt[:, HEAD_DIM:]], axis=0)
            lhs = jnp.concatenate(_split_bf16(by_head, 3), axis=0)
            vcols.append(lax.dot_general(lhs, expand, (((0,), (0,)), ((), ())), preferred_element_type=_f32))
        for tt in range(sub):
            rows = {u: tuple(x[tt:tt + 1, :] for x in tiles[u]) for u in units}
            sa = rowsum([s * rows[u][4] for s, u in zip(states, units)], ones_k, parts)
            for idx, u in enumerate(units):
                r_t, w_t, k_t, v_t, av_t, bv_t = rows[u]
                vcol = vcols[idx][:, tt * LANES:(tt + 1) * LANES]
                states[idx] = states[idx] * w_t + sa[idx] * bv_t + vcol * k_t
            ob = rowsum([s * rows[u][0] for s, u in zip(states, units)], ones_k, parts)
            for idx in range(len(units)):
                o_row = jnp.sum(jnp.where(eye, ob[idx], 0.0), axis=0, keepdims=True)
                o_tiles[idx] = jnp.where(row_id == tt, jnp.broadcast_to(o_row, (sub, LANES)), o_tiles[idx])
        for idx, (n, hp) in enumerate(units):
            s_ref[n, hp] = states[idx]
            o_ref[n, pl.ds(t0, sub), pl.ds(hp * LANES, LANES)] = o_tiles[idx]
        return carry

    lax.fori_loop(0, tc // sub, tile_steps, 0)


def _rwkv_scan(r, w, k, v, av, bv, s0, nb, tc):
    n, t, d_a = r.shape
    n_pairs = d_a // LANES
    half = (jnp.arange(LANES, dtype=jnp.int32) // HEAD_DIM)
    ones_blk = (half[:, None] == half[None, :]).astype(_bf16)
    ones_k = jnp.concatenate([ones_blk] * (3 if t == 1 else 2), axis=0)
    sub = min(SUBLANES, tc)
    src_h = jnp.arange(2 * sub, dtype=jnp.int32) // sub
    src_t = jnp.arange(2 * sub, dtype=jnp.int32) % sub
    dst_t = jnp.arange(sub * LANES, dtype=jnp.int32) // LANES
    dst_h = (jnp.arange(sub * LANES, dtype=jnp.int32) % LANES) // HEAD_DIM
    expand = ((src_t[:, None] == dst_t[None, :]) & (src_h[:, None] == dst_h[None, :])).astype(_bf16)
    expand = jnp.concatenate([expand] * 3, axis=0)
    eye =(jnp.arange(HEAD_DIM, dtype=jnp.int32)[:, None]
           == (jnp.arange(LANES, dtype=jnp.int32) % HEAD_DIM)[None, :]).astype(_f32)
    seq = pl.BlockSpec((nb, tc, d_a), lambda b, j: (b, j, 0))
    st = pl.BlockSpec((nb, n_pairs, HEAD_DIM, LANES), lambda b, j: (b, 0, 0, 0))
    full = lambda a: pl.BlockSpec(a.shape, lambda b, j: (0, 0))
    return pl.pallas_call(
        functools.partial(_rwkv_scan_body, nb, n_pairs, tc),
        grid=(n // nb, t // tc),
        in_specs=[seq] * 6 + [st, full(ones_k), full(expand), full(eye)],
        out_specs=[seq, st],
        out_shape=[jax.ShapeDtypeStruct((n, t, d_a), _f32),
                   jax.ShapeDtypeStruct(s0.shape, _f32)],
        compiler_params=_params("arbitrary", "arbitrary"),
        name="rwkv_scan",
    )(r, w, k, v, av, bv, s0, ones_k, expand, eye)


def _pack_state(s):
    n, h = s.shape[:2]
    return s.reshape(n, h // 2, 2, HEAD_DIM, HEAD_DIM).transpose(0, 1, 3, 2, 4).reshape(n, h // 2, HEAD_DIM, LANES)


def _unpack_state(s):
    n, hp = s.shape[:2]
    return s.reshape(n, hp, HEAD_DIM, 2, HEAD_DIM).transpose(0, 1, 3, 2, 4).reshape(n, hp * 2, HEAD_DIM, HEAD_DIM)


def _mix_out_body(o_ref, bonus_ref, g_ref, ob_ref, x_ref, gate_ref, lw_ref, lb_ref, ones_ref, wa_ref, wb_ref,
                  out_ref):
    ones_bd = ones_ref[...]
    o = o_ref[...]
    inv = 1.0 / HEAD_DIM
    mean = _segsum(o, ones_bd) * inv
    dlt = o - mean
    var = _segsum(dlt * dlt, ones_bd) * inv
    on = dlt * lax.rsqrt(var + LNX_EPS) * lw_ref[...] + lb_ref[...]
    oa = (on + bonus_ref[...]) * g_ref[...]
    y = _dot(_bf(oa), wa_ref[...]) + _dot(ob_ref[...], wb_ref[...])
    out_ref[...] = x_ref[...] + gate_ref[0] * y


def _mix_out(o, bonus, g, o_b, x, gate, lnx_w, lnx_b, w_out, tm, rows_per_group):
    m, d = x.shape
    d_a = o.shape[1]
    ones_bd = _ones_blockdiag(d_a, HEAD_DIM)
    wa = _bf(w_out[:d_a])
    wb = _bf(w_out[d_a:])
    gs = _group_spec(tm, rows_per_group, d)
    full = lambda a: pl.BlockSpec(a.shape, lambda i: (0, 0))
    rowa = pl.BlockSpec((tm, d_a), lambda i: (i, 0))
    rowd = pl.BlockSpec((tm, d), lambda i: (i, 0))
    lw = lnx_w.reshape(1, -1)
    lb = lnx_b.reshape(1, -1)
    return pl.pallas_call(
        _mix_out_body,
        grid=(m // tm,),
        in_specs=[rowa, rowa, rowa, pl.BlockSpec((tm, o_b.shape[1]), lambda i: (i, 0)), rowd, gs(gate.shape[1]),
                  full(lw), full(lb), full(ones_bd), full(wa), full(wb)],
        out_specs=rowd,
        out_shape=jax.ShapeDtypeStruct((m, d), _f32),
        compiler_params=_params("arbitrary"),
        name="mix_out",
    )(o, bonus, g, o_b, x, gate, lw, lb, ones_bd, wa, wb)


def _key_to_float(u):
    key = u ^ jnp.int32(-2147483648)
    bits = jnp.where(key >= 0, key, key ^ jnp.int32(0x7FFFFFFF))
    return lax.bitcast_convert_type(bits, _f32)


def _kth_largest(count_ge, k_row, shape):
    def body(it, u):
        bit = jnp.left_shift(jnp.int32(1), 31 - it)
        cand_u = u | bit
        ok = count_ge(_key_to_float(cand_u)) >= k_row
        return jnp.where(ok, cand_u, u)

    u = lax.fori_loop(0, 32, body, jnp.zeros(shape, jnp.int32))
    return _key_to_float(u)


def _tie_cut(count_eq_lt, budget, n_bits, shape):
    def body(it, c):
        cand = c | jnp.left_shift(jnp.int32(1), n_bits - 1 - it)
        ok = count_eq_lt(cand) <= budget
        return jnp.where(ok, cand, c)

    return lax.fori_loop(0, n_bits, body, jnp.zeros(shape, jnp.int32))


def _dsa_prompt_body(n_heads, k_sel, t_len,
                     far_ref, qt_ref, qit_ref, wit_ref, k_ref, vt_ref, ki_ref, near_ref,
                     ot_ref, sc_ref, cut_ref, qz_ref, m_ref, l_ref, acc_ref):
    qb = qt_ref.shape[2]
    i = pl.program_id(1)
    q0 = i * qb
    qpos = q0 + lax.broadcasted_iota(jnp.int32, (1, qb), 1)
    n_kb = (q0 + qb + KEY_BLOCK - 1) // KEY_BLOCK
    row_kb = lax.broadcasted_iota(jnp.int32, (KEY_BLOCK, qb), 0)

    @pl.when(i == 0)
    def _():
        sc_ref[...] = jnp.full(sc_ref.shape, -jnp.inf, _f32)

    w8 = (wit_ref[0] * (H_IDX ** -0.5)) * (D_IDX ** -0.5)

    qi_all = jnp.concatenate([qit_ref[0, h * D_IDX:(h + 1) * D_IDX, :] for h in range(H_IDX)], axis=1)

    def score_block(kb, carry):
        c0 = pl.multiple_of(kb * KEY_BLOCK, KEY_BLOCK)
        s_all = _dot(ki_ref[0, pl.ds(c0, KEY_BLOCK), :], qi_all)
        acc = w8[0:1, :] * jnp.maximum(s_all[:, :qb], 0.0)
        for h in range(1, H_IDX):
            acc = acc + w8[h:h + 1, :] * jnp.maximum(s_all[:, h * qb:(h + 1) * qb], 0.0)
        sc_ref[pl.ds(c0, KEY_BLOCK), :] = jnp.where(c0 + row_kb <= qpos, acc, -jnp.inf)
        return carry

    lax.fori_loop(0, n_kb, score_block, 0)

    cb = min(COUNT_BLOCK, t_len)
    n_cb = (q0 + qb + cb - 1) // cb
    row_cb = lax.broadcasted_iota(jnp.int32, (cb, qb), 0)

    def count(pred):
        def body(kb, acc):
            c0 = pl.multiple_of(kb * cb, cb)
            hit = jnp.where(pred(sc_ref[pl.ds(c0, cb), :], c0 + row_cb), 1.0, 0.0)
            for j in range(cb // COUNT_ROWS):
                acc = acc + hit[j * COUNT_ROWS:(j + 1) * COUNT_ROWS, :]
            return acc
        acc = lax.fori_loop(0, n_cb, body, jnp.zeros((COUNT_ROWS, qb), _f32))
        return jnp.sum(acc, axis=0, keepdims=True)

    k_row = jnp.minimum(k_sel, qpos + 1).astype(_f32)
    thr = _kth_largest(lambda cand: count(lambda s, pos: s >= cand), k_row, (1, qb))
    n_gt = count(lambda s, pos: s > thr)
    n_eq = count(lambda s, pos: s == thr)
    budget = k_row - n_gt
    cut_ref[...] = jnp.full((1, qb), 2 * t_len, jnp.int32)

    @pl.when(jnp.max(n_eq - budget) > 0.5)
    def _():
        cut_ref[...] = _tie_cut(lambda c: count(lambda s, pos: (s == thr) & (pos < c)), budget,
                                (2 * t_len).bit_length(), (1, qb))

    cut = cut_ref[...]

    m_ref[...] = jnp.full(m_ref.shape, M_INIT, _f32)
    l_ref[...] = jnp.zeros(l_ref.shape, _f32)
    acc_ref[...] = jnp.zeros(acc_ref.shape, _f32)
    pair_row = lax.broadcasted_iota(jnp.int32, (LANES, qb), 0) // HEAD_DIM
    for hp in range(n_heads // 2):
        qp = qt_ref[0, hp * LANES:(hp + 1) * LANES, :] * (HEAD_DIM ** -0.5)
        zero = jnp.zeros_like(qp)
        qz_ref[hp] = jnp.concatenate([jnp.where(pair_row == 0, qp, zero), jnp.where(pair_row == 1, qp, zero)], axis=1)

    def attend(c0, width, bias_of_head, uniform_bias=False):
        scb = sc_ref[pl.ds(c0, width), :]
        pos = c0 + lax.broadcasted_iota(jnp.int32, (width, qb), 0)
        sel = (scb > thr) | ((scb == thr) & (pos < cut))
        m_old = [m_ref[h] for h in range(n_heads)]
        l_old = [l_ref[h] for h in range(n_heads)]
        a_old = [acc_ref[h] for h in range(n_heads)]
        s2 = [_dot(k_ref[0, pl.ds(c0, width), hp * LANES:(hp + 1) * LANES], qz_ref[hp]) for hp in range(n_heads // 2)]
        m_out, l_out, a_out = [], [], []
        for h in range(n_heads):
            s = s2[h // 2][:, (h % 2) * qb:(h % 2 + 1) * qb]
            if uniform_bias:
                bias = bias_of_head(h)
                s = jnp.where(sel, s, NEG)
                m_new = jnp.maximum(m_old[h], jnp.max(s, axis=0, keepdims=True) + bias)
                p = jnp.exp(s - (m_new - bias))
            else:
                s = jnp.where(sel, s + bias_of_head(h), NEG)
                m_new = jnp.maximum(m_old[h], jnp.max(s, axis=0, keepdims=True))
                p = jnp.exp(s - m_new)
            alpha = jnp.exp(m_old[h] - m_new)
            l_out.append(alpha * l_old[h] + jnp.sum(p, axis=0, keepdims=True))
            pv = _dot(vt_ref[0, h * HEAD_DIM:(h + 1) * HEAD_DIM, pl.ds(c0, width)], _bf(p))
            a_out.append(alpha * a_old[h] + pv)
            m_out.append(m_new)
        for h in range(n_heads):
            m_ref[h] = m_out[h]
            l_ref[h] = l_out[h]
            acc_ref[h] = a_out[h]

    n_far = jnp.maximum(i - 1, 0) // (KEY_BLOCK // qb)

    def far_block(kb, carry):
        attend(pl.multiple_of(kb * KEY_BLOCK, KEY_BLOCK), KEY_BLOCK, lambda h: far_ref[h], uniform_bias=True)
        return carry

    lax.fori_loop(0, n_far, far_block, 0)

    def near_block(j, carry):
        attend(pl.multiple_of(j * qb, qb), qb, lambda h: near_ref[i - j, h])
        return carry

    lax.fori_loop(n_far * (KEY_BLOCK // qb), i + 1, near_block, 0)

    for h in range(n_heads):
        ot_ref[0, h * HEAD_DIM:(h + 1) * HEAD_DIM, :] = _bf(acc_ref[h] / l_ref[h])


def _dsa_prompt(q, k, v, qi, ki, wi, rel_bias, qb):
    n, t, d_b = q.shape
    n_heads = d_b // HEAD_DIM
    k_sel = min(TOPK_MAX, t // 4)
    tr = lambda a: a.transpose(0, 2, 1)
    n_near = KEY_BLOCK // qb + 1
    qq = jnp.arange(qb, dtype=jnp.int32)
    dist = (jnp.arange(n_near, dtype=jnp.int32)[:, None, None] * qb + qq[None, None, :] - qq[None, :, None])
    onehot = (_rel_bucket(dist)[..., None] == jnp.arange(N_BUCKETS, dtype=jnp.int32)).astype(_f32)
    near = jnp.einsum('dkqb,bh->dhkq', onehot, rel_bias, precision=lax.Precision.HIGHEST)
    far = rel_bias[N_BUCKETS - 1]
    res = lambda shape: pl.BlockSpec(shape, lambda b, i, *_: (b, 0, 0))
    blk = lambda w: pl.BlockSpec((1, w, qb), lambda b, i, *_: (b, 0, i))
    grid_spec = pltpu.PrefetchScalarGridSpec(
        num_scalar_prefetch=0,
        grid=(n, t // qb),
        in_specs=[pl.BlockSpec(memory_space=pltpu.SMEM),
                  blk(d_b), blk(H_IDX * D_IDX), blk(H_IDX),
                  res((1, t, d_b)), res((1, d_b, t)), res((1, t, D_IDX)),
                  pl.BlockSpec(near.shape, lambda b, i, *_: (0, 0, 0, 0))],
        out_specs=blk(d_b),
        scratch_shapes=[pltpu.VMEM((t, qb), _f32), pltpu.VMEM((1, qb), jnp.int32),
                        pltpu.VMEM((n_heads // 2, LANES, 2 * qb), _bf16),
                        pltpu.VMEM((n_heads, 1, qb), _f32), pltpu.VMEM((n_heads, 1, qb), _f32),
                        pltpu.VMEM((n_heads, HEAD_DIM, qb), _f32)],
    )
    o_t = pl.pallas_call(
        functools.partial(_dsa_prompt_body, n_heads, k_sel, t),
        grid_spec=grid_spec,
        out_shape=jax.ShapeDtypeStruct((n, d_b, t), _bf16),
        compiler_params=_params("arbitrary", "arbitrary"),
        name="dsa_prompt",
    )(far, tr(q), tr(qi), tr(wi), _bf(k), tr(_bf(v)), _bf(ki), near)
    return tr(o_t)


def _dsa_sample_index_body(n_pages, page, pg, pt_ref, qi_ref, wi_ref, kin_ref, *rest):
    cki_refs, (sc_ref, snew_ref) = rest[:pg], rest[pg:]
    step = pl.program_id(1)
    w8 = _bf(wi_ref[0] * (H_IDX ** -0.5)).astype(_f32)
    relu_bf = lambda s: _bf(jnp.maximum(s, 0.0)).astype(_f32)
    qi8 = qi_ref[0]
    for j in range(pg):
        c0 = pl.multiple_of((step * pg + j) * page, page)
        s8 = _dot(qi8, _bf(cki_refs[j][0, 0])) * (D_IDX ** -0.5)
        sc_ref[0, :, pl.ds(c0, page)] = jnp.sum(w8 * relu_bf(s8), axis=0, keepdims=True)

    @pl.when(step == n_pages // pg - 1)
    def _():
        s_new8 = jnp.sum(qi8.astype(_f32) * _bf(kin_ref[0]).astype(_f32), axis=1, keepdims=True) * (D_IDX ** -0.5)
        snew_ref[0] = jnp.sum(w8 * relu_bf(s_new8), axis=0, keepdims=True)


def _dsa_sample_select_body(k_sel, sc_ref, snew_ref, thr_ref, cut_ref):
    sc = sc_ref[...]
    s_new = snew_ref[...]
    rows, past = sc.shape
    pos = lax.broadcasted_iota(jnp.int32, sc.shape, 1)

    def count(pred):
        hits = jnp.sum(jnp.where(pred(sc, pos), 1.0, 0.0), axis=1, keepdims=True)
        return hits + jnp.where(pred(s_new, past), 1.0, 0.0)

    k_row = jnp.full((rows, 1), float(k_sel), _f32)
    thr = _kth_largest(lambda cand: count(lambda s, ps: s >= cand), k_row, (rows, 1))
    budget = k_row - count(lambda s, ps: s > thr)
    thr_ref[...] = thr
    cut_ref[...] = _tie_cut(lambda c: count(lambda s, ps: (s == thr) & (ps < c)), budget,
                            (2 * (past + 1)).bit_length(), (rows, 1))


def _dsa_sample_probs_body(n_heads, n_pages, page, pg, pt_ref, q_ref, kn_ref, sc_ref, snew_ref, thr_ref, cut_ref, *rest):
    ck_refs = rest[:pg]
    btab_ref, bnew_ref, p_ref, pnew_ref, lg_ref = rest[pg:]
    step = pl.program_id(1)
    past = n_pages * page
    q8 = q_ref[0]

    for j in range(pg):
        c0 = pl.multiple_of((step * pg + j) * page, page)
        rows = [_dot(q8, _bf(ck_refs[j][0, 0, h]))[h:h + 1] for h in range(n_heads)]
        lg = jnp.concatenate(rows, axis=0) * (HEAD_DIM ** -0.5)
        lg_ref[:, pl.ds(c0, page)] = lg + btab_ref[:, pl.ds(c0, page)]

    @pl.when(step == n_pages // pg - 1)
    def _():
        sc = sc_ref[0]
        s_new = snew_ref[0]
        thr = thr_ref[0]
        cut = cut_ref[0]
        pos = lax.broadcasted_iota(jnp.int32, sc.shape, 1)
        sel = (sc > thr) | ((sc == thr) & (pos < cut))
        sel_new = (s_new > thr) | ((s_new == thr) & (past < cut))

        kn = _bf(kn_ref[0]).astype(_f32)
        lg_new = jnp.sum(q8.astype(_f32) * kn, axis=1, keepdims=True) * (HEAD_DIM ** -0.5) + bnew_ref[...]
        lg_new = jnp.where(sel_new, lg_new, NEG)
        s_all = jnp.where(sel, lg_ref[...], NEG)
        m = jnp.maximum(jnp.maximum(jnp.max(s_all, axis=1, keepdims=True), lg_new), M_INIT)
        pr = jnp.exp(s_all - m)
        pr_new = jnp.exp(lg_new - m)
        l = jnp.sum(pr, axis=1, keepdims=True) + pr_new
        p_ref[0] = pr
        col = lax.broadcasted_iota(jnp.int32, (n_heads, 2), 1)
        pnew_ref[0] = jnp.where(col == 0, pr_new, l)


def _dsa_sample_pv_body(n_heads, n_pages, page, pg, pt_ref, p_ref, pnew_ref, vn_ref, *rest):
    cv_refs, (o_ref, acc_ref) = rest[:pg], rest[pg:]
    step = pl.program_id(1)

    @pl.when(step == 0)
    def _():
        acc_ref[...] = pnew_ref[0][:, 0:1] * _bf(vn_ref[0]).astype(_f32)

    acc = acc_ref[...]
    for j in range(pg):
        c0 = pl.multiple_of((step * pg + j) * page, page)
        pb = _bf(p_ref[0, :, pl.ds(c0, page)])
        rows = [_dot_nt(pb, _bf(cv_refs[j][0, 0, h]))[h:h + 1] for h in range(n_heads)]
        acc = acc + jnp.concatenate(rows, axis=0)
    acc_ref[...] = acc

    @pl.when(step == n_pages // pg - 1)
    def _():
        o_ref[0] = _bf(acc / pnew_ref[0][:, 1:2])


def _dsa_sample(q, k_new, v_new, qi, ki_new, wi, ck, cv, cki, layer, page_table, rel_bias):
    b, _, d_b = q.shape
    n_heads = d_b // HEAD_DIM
    n_pages = page_table.shape[1]
    page = ck.shape[2]
    ck, cv, cki = (jnp.moveaxis(a, 2, -1) for a in (ck, cv, cki))
    past = n_pages * page
    k_sel = min(TOPK_MAX, (past + 1) // 4)
    pg = max(g for g in (1, 2, 4, 8, 16) if n_pages % g == 0)
    kpos = jnp.arange(past, dtype=jnp.int32)
    btab = rel_bias[_rel_bucket(past - kpos)].T
    bnew = rel_bias[_rel_bucket(jnp.zeros((1,), jnp.int32))].T
    heads = lambda a: a.reshape(b, n_heads, HEAD_DIM)
    pt = page_table.reshape(-1)
    per_b = lambda shape: pl.BlockSpec((1,) + shape, lambda i, s, pt: (i,) + (0,) * len(shape))
    const = lambda a: pl.BlockSpec(a.shape, lambda i, s, pt: (0,) * a.ndim)

    def paged(tail, j):
        return pl.BlockSpec((1, 1) + tail + (page,),
                            lambda i, s, pt: (layer, pt[i * n_pages + s * pg + j]) + (0,) * (len(tail) + 1))

    grid = (b, n_pages // pg)
    scores, s_new = pl.pallas_call(
        functools.partial(_dsa_sample_index_body, n_pages, page, pg),
        grid_spec=pltpu.PrefetchScalarGridSpec(
            num_scalar_prefetch=1, grid=grid,
            in_specs=[per_b((H_IDX, D_IDX)), per_b((H_IDX, 1)), per_b((1, D_IDX))]
                     + [paged((D_IDX,), j) for j in range(pg)],
            out_specs=[per_b((1, past)), per_b((1, 1))]),
        out_shape=[jax.ShapeDtypeStruct((b, 1, past), _f32), jax.ShapeDtypeStruct((b, 1, 1), _f32)],
        compiler_params=_params("arbitrary", "arbitrary"),
        name="dsa_sample_index",
    )(pt, qi.reshape(b, H_IDX, D_IDX), wi.reshape(b, H_IDX, 1), ki_new, *([cki] * pg))
    thr, cut = pl.pallas_call(
        functools.partial(_dsa_sample_select_body, k_sel),
        out_shape=[jax.ShapeDtypeStruct((b, 1), _f32), jax.ShapeDtypeStruct((b, 1), jnp.int32)],
        compiler_params=pltpu.CompilerParams(vmem_limit_bytes=VMEM_LIMIT),
        name="dsa_sample_select",
    )(scores.reshape(b, past), s_new.reshape(b, 1))
    probs, p_new = pl.pallas_call(
        functools.partial(_dsa_sample_probs_body, n_heads, n_pages, page, pg),
        grid_spec=pltpu.PrefetchScalarGridSpec(
            num_scalar_prefetch=1, grid=grid,
            in_specs=[per_b((n_heads, HEAD_DIM)), per_b((n_heads, HEAD_DIM)), per_b((1, past)), per_b((1, 1)),
                      per_b((1, 1)), per_b((1, 1))]
                     + [paged((n_heads, HEAD_DIM), j) for j in range(pg)]
                     + [const(btab), const(bnew)],
            out_specs=[per_b((n_heads, past)), per_b((n_heads, 2))],
            scratch_shapes=[pltpu.VMEM((n_heads, past), _f32)]),
        out_shape=[jax.ShapeDtypeStruct((b, n_heads, past), _f32), jax.ShapeDtypeStruct((b, n_heads, 2), _f32)],
        compiler_params=_params("arbitrary", "arbitrary"),
        name="dsa_sample_probs",
    )(pt, heads(q), heads(k_new), scores, s_new, thr.reshape(b, 1, 1), cut.reshape(b, 1, 1),
      *([ck] * pg), btab, bnew)
    out = pl.pallas_call(
        functools.partial(_dsa_sample_pv_body, n_heads, n_pages, page, pg),
        grid_spec=pltpu.PrefetchScalarGridSpec(
            num_scalar_prefetch=1, grid=grid,
            in_specs=[per_b((n_heads, past)), per_b((n_heads, 2)), per_b((n_heads, HEAD_DIM))]
                     + [paged((n_heads, HEAD_DIM), j) for j in range(pg)],
            out_specs=per_b((n_heads, HEAD_DIM)),
            scratch_shapes=[pltpu.VMEM((n_heads, HEAD_DIM), _f32)]),
        out_shape=jax.ShapeDtypeStruct((b, n_heads, HEAD_DIM), _bf16),
        compiler_params=_params("arbitrary", "arbitrary"),
        name="dsa_sample_pv",
    )(pt, probs, p_new, heads(v_new), *([cv] * pg))
    return out.reshape(b, 1, d_b)


def _ffn_body(routed, final_norm, n_exp,
              x_ref, g_ref, sh_ref, sc_ref, gate_ref, rw_ref, rb_ref, w1_ref, w3_ref, w2_ref, gf_ref,
              o_ref, h_ref, acc_ref, dg_ref):
    e = pl.program_id(1)
    f = pl.program_id(2)
    first = (e == 0) & (f == 0)
    last = (e == pl.num_programs(1) - 1) & (f == pl.num_programs(2) - 1)

    @pl.when(first)
    def _():
        h = _norm_mod(x_ref[...], g_ref[...], sh_ref[0], sc_ref[0])
        h_ref[...] = _bf(h)
        acc_ref[...] = jnp.zeros(acc_ref.shape, _f32)
        if routed:
            logits = _dot(_bf(h), rw_ref[...]) + rb_ref[...]
            lane = lax.broadcasted_iota(jnp.int32, logits.shape, 1).astype(_f32)
            logits = jnp.where(lane < n_exp, logits, -jnp.inf)
            v1 = jnp.max(logits, axis=1, keepdims=True)
            i1 = jnp.min(jnp.where(logits == v1, lane, float(LANES)), axis=1, keepdims=True)
            rest = jnp.where(lane == i1, -jnp.inf, logits)
            v2 = jnp.max(rest, axis=1, keepdims=True)
            i2 = jnp.min(jnp.where(rest == v2, lane, float(LANES)), axis=1, keepdims=True)
            e2 = jnp.exp(v2 - v1)
            den = 1.0 + e2
            dg_ref[...] = jnp.where(lane == i1, 1.0 / den, 0.0) + jnp.where(lane == i2, e2 / den, 0.0)

    h = h_ref[...]
    a = _silu(_dot(h, w1_ref[0])) * _dot(h, w3_ref[0])
    y = _dot(_bf(a), w2_ref[0])
    if routed:
        lane = lax.broadcasted_iota(jnp.int32, dg_ref.shape, 1)
        y = y * jnp.sum(jnp.where(lane == e, dg_ref[...], 0.0), axis=1, keepdims=True)
    acc_ref[...] += y

    @pl.when(last)
    def _():
        y = x_ref[...] + gate_ref[0] * acc_ref[...]
        if final_norm:
            y = y * lax.rsqrt(jnp.mean(y * y, axis=-1, keepdims=True) + EPS) * gf_ref[...]
        o_ref[...] = y


def _ffn(x, g, shift, scale, gate, w13, w2, tm, tf, rows_per_group, router=None, final_g=None):
    m, d = x.shape
    n_exp, ff, _ = w2.shape
    routed = router is not None
    final_norm = final_g is not None
    if routed:
        rw, rb = router
        rw = _bf(jnp.pad(rw, ((0, 0), (0, LANES - n_exp))))
        rb = jnp.pad(rb.reshape(1, -1), ((0, 0), (0, LANES - n_exp)))
    else:
        rw = jnp.zeros((d, LANES), _bf16)
        rb = jnp.zeros((1, LANES), _f32)
    gf = final_g.reshape(1, -1) if final_norm else jnp.ones((1, d), _f32)
    nf = ff // tf
    gs = _group_spec(tm, rows_per_group, d)
    rowd = pl.BlockSpec((tm, d), lambda i, e, f: (i, 0))
    full = lambda a: pl.BlockSpec(a.shape, lambda i, e, f: (0, 0))
    return pl.pallas_call(
        functools.partial(_ffn_body, routed, final_norm, n_exp),
        grid=(m // tm, n_exp, nf),
        in_specs=[rowd, full(g), gs(shift.shape[1]), gs(scale.shape[1]), gs(gate.shape[1]), full(rw), full(rb),
                  pl.BlockSpec((1, d, tf), lambda i, e, f: (e, 0, f)),
                  pl.BlockSpec((1, d, tf), lambda i, e, f: (e, 0, f + nf)),
                  pl.BlockSpec((1, tf, d), lambda i, e, f: (e, f, 0)),
                  full(gf)],
        out_specs=rowd,
        out_shape=jax.ShapeDtypeStruct((m, d), _f32),
        scratch_shapes=[pltpu.VMEM((tm, d), _bf16), pltpu.VMEM((tm, d), _f32), pltpu.VMEM((tm, LANES), _f32)],
        compiler_params=_params("arbitrary", "arbitrary", "arbitrary"),
        name="moe_ffn" if routed else "dense_ffn",
    )(x, g, shift, scale, gate, rw, rb, w13, w13, w2, gf)


MOE_CAP = 512


def _top2_gates(logits, n_exp):
    lane = lax.broadcasted_iota(jnp.int32, logits.shape, 1).astype(_f32)
    logits = jnp.where(lane < n_exp, logits, -jnp.inf)
    v1 = jnp.max(logits, axis=1, keepdims=True)
    i1 = jnp.min(jnp.where(logits == v1, lane, float(LANES)), axis=1, keepdims=True)
    rest = jnp.where(lane == i1, -jnp.inf, logits)
    v2 = jnp.max(rest, axis=1, keepdims=True)
    i2 = jnp.min(jnp.where(rest == v2, lane, float(LANES)), axis=1, keepdims=True)
    e2 = jnp.exp(v2 - v1)
    den = 1.0 + e2
    gates = jnp.where(lane == i1, 1.0 / den, 0.0) + jnp.where(lane == i2, e2 / den, 0.0)
    routed = jnp.where((lane == i1) | (lane == i2), 1.0, 0.0)
    return gates, routed


def _moe_route_body(n_exp, x_ref, g_ref, sh_ref, sc_ref, rw_ref, rb_ref, ltri_ref,
                    xs_ref, dg_ref, slot_ref, top_ref):
    tm = x_ref.shape[0]
    hb = _bf(_norm_mod(x_ref[...], g_ref[...], sh_ref[0], sc_ref[0]))
    gates, routed = _top2_gates(_dot(hb, rw_ref[...]) + rb_ref[...], n_exp)
    dg_ref[...] = gates
    rank = _dot(ltri_ref[...], _bf(routed))
    slot = jnp.where(routed > 0.5, rank, -1.0)
    slot_ref[...] = slot
    slot_t = slot.T
    want = lax.broadcasted_iota(jnp.int32, (LANES, tm), 0).astype(_f32)
    for e in range(n_exp):
        s_row = slot_t[e:e + 1, :]
        for c in range(xs_ref.shape[1]):
            onehot = jnp.where(s_row == want + float(c * LANES), 1.0, 0.0)
            xs_ref[e, c, 0] = _bf(_dot(_bf(onehot), hb))
    top_ref[0] = jnp.max(slot, axis=0, keepdims=True)


def _moe_expert_body(used_ref, x_ref, w1_ref, w3_ref, w2_ref, o_ref, acc_ref):
    f = pl.program_id(2)

    @pl.when(f == 0)
    def _():
        acc_ref[...] = jnp.zeros(acc_ref.shape, _f32)

    @pl.when(used_ref[pl.program_id(0) * pl.num_programs(1) + pl.program_id(1)] > 0)
    def _():
        x = x_ref[0]
        a = _silu(_dot(x, w1_ref[0])) * _dot(x, w3_ref[0])
        acc_ref[...] += _dot(_bf(a), w2_ref[0])

    @pl.when(f == pl.num_programs(2) - 1)
    def _():
        o_ref[0] = _bf(acc_ref[...])


def _moe_combine_body(n_exp, final_norm, x_ref, gate_ref, dg_ref, slot_ref, ys_ref, gf_ref, o_ref):
    tm = x_ref.shape[0]
    cap = ys_ref.shape[1] * LANES
    lane = lax.broadcasted_iota(jnp.int32, (tm, LANES), 1)
    col = lax.broadcasted_iota(jnp.int32, (tm, cap), 1).astype(_f32)
    dg = dg_ref[...]
    slot = slot_ref[...]
    acc = jnp.zeros(x_ref.shape, _f32)
    for e in range(n_exp):
        slot_e = jnp.sum(jnp.where(lane == e, slot, 0.0), axis=1, keepdims=True)
        gate_e = jnp.sum(jnp.where(lane == e, dg, 0.0), axis=1, keepdims=True)
        scatter = _bf(jnp.where(slot_e == col, 1.0, 0.0))
        acc = acc + gate_e * _dot(scatter, ys_ref[e, :, 0].reshape(cap, ys_ref.shape[4]))
    y = x_ref[...] + gate_ref[0] * acc
    if final_norm:
        y = y * lax.rsqrt(jnp.mean(y * y, axis=-1, keepdims=True) + EPS) * gf_ref[...]
    o_ref[...] = y


def _moe_sparse(x, g, shift, scale, gate, w13, w2, tm, tf, rows_per_group, router, final_g):
    m, d = x.shape
    n_exp, ff, _ = w2.shape
    rw, rb = router
    rw = _bf(jnp.pad(rw, ((0, 0), (0, LANES - n_exp))))
    rb = jnp.pad(rb.reshape(1, -1), ((0, 0), (0, LANES - n_exp)))
    final_norm = final_g is not None
    gf = final_g.reshape(1, -1) if final_norm else jnp.ones((1, d), _f32)
    tiles = m // tm
    cap = MOE_CAP
    ltri = (jnp.arange(tm, dtype=jnp.int32)[:, None] > jnp.arange(tm, dtype=jnp.int32)[None, :]).astype(_bf16)
    gs = _group_spec(tm, rows_per_group, d)
    rowd = pl.BlockSpec((tm, d), lambda i: (i, 0))
    rowl = pl.BlockSpec((tm, LANES), lambda i: (i, 0))
    full = lambda a: pl.BlockSpec(a.shape, lambda i: (0,) * a.ndim)
    n_chunks = cap // LANES
    xs_spec = pl.BlockSpec((n_exp, n_chunks, 1, LANES, d), lambda i: (0, 0, i, 0, 0))
    xs, dg, slot, top = pl.pallas_call(
        functools.partial(_moe_route_body, n_exp),
        grid=(tiles,),
        in_specs=[rowd, full(g), gs(shift.shape[1]), gs(scale.shape[1]), full(rw), full(rb), full(ltri)],
        out_specs=[xs_spec, rowl, rowl, pl.BlockSpec((1, 1, LANES), lambda i: (i, 0, 0))],
        out_shape=[jax.ShapeDtypeStruct((n_exp, n_chunks, tiles, LANES, d), _bf16), jax.ShapeDtypeStruct((m, LANES), _f32),
                   jax.ShapeDtypeStruct((m, LANES), _f32), jax.ShapeDtypeStruct((tiles, 1, LANES), _f32)],
        compiler_params=_params("arbitrary"),
        name="moe_route",
    )(x, g, shift, scale, rw, rb, ltri)

    tr = tiles * LANES
    nf = ff // tf
    top = top.reshape(tiles, LANES)[:, :n_exp]
    used = (jnp.max(top, axis=0)[:, None] >= (jnp.arange(n_chunks, dtype=_f32) * LANES)[None, :]).astype(jnp.int32)

    def dense_path():
        return _ffn(x, g, shift, scale, gate, w13, w2, tm, tf, rows_per_group, router=router, final_g=final_g)

    def sparse_path():
        ys = pl.pallas_call(
            _moe_expert_body,
            grid_spec=pltpu.PrefetchScalarGridSpec(
                num_scalar_prefetch=1, grid=(n_exp, n_chunks, nf),
                in_specs=[pl.BlockSpec((1, tr, d), lambda e, c, f, u: (e, c, 0)),
                          pl.BlockSpec((1, d, tf), lambda e, c, f, u: (e, 0, f)),
                          pl.BlockSpec((1, d, tf), lambda e, c, f, u: (e, 0, f + nf)),
                          pl.BlockSpec((1, tf, d), lambda e, c, f, u: (e, f, 0))],
                out_specs=pl.BlockSpec((1, tr, d), lambda e, c, f, u: (e, c, 0)),
                scratch_shapes=[pltpu.VMEM((tr, d), _f32)]),
            out_shape=jax.ShapeDtypeStruct((n_exp, n_chunks * tr, d), _bf16),
            compiler_params=_params("arbitrary", "arbitrary", "arbitrary"),
            name="moe_experts",
        )(used.reshape(-1), xs.reshape(n_exp, n_chunks * tr, d), w13, w13, w2)
        return pl.pallas_call(
            functools.partial(_moe_combine_body, n_exp, final_norm),
            grid=(tiles,),
            in_specs=[rowd, gs(gate.shape[1]), rowl, rowl, xs_spec, full(gf)],
            out_specs=rowd,
            out_shape=jax.ShapeDtypeStruct((m, d), _f32),
            compiler_params=_params("arbitrary"),
            name="moe_combine",
        )(x, gate, dg, slot, ys.reshape(n_exp, n_chunks, tiles, LANES, d), gf)

    return lax.cond(jnp.max(top) >= cap, dense_path, sparse_path)


def _cmul(ar, ai, br, bi):
    return ar * br - ai * bi, ar * bi + ai * br


def _s5_io(u, b_ref, c_ref, d_ref, are_ref, aim_ref, h_of_bb):
    half = b_ref.shape[2] // 2
    bu = _dot(_bf(u), b_ref[0])
    bb_re, bb_im = _cmul(are_ref[0, 1:2], aim_ref[0, 1:2], bu[:, :half], bu[:, half:])
    h_re, h_im = h_of_bb(bb_re, bb_im)
    y = _dot(_bf(jnp.concatenate([h_re, h_im], axis=1)), c_ref[0]) + d_ref[0] * u
    return _bf(_gelu_tanh(y)), h_re, h_im


def _s5_scan_body(tc, u_ref, b_ref, c_ref, d_ref, are_ref, aim_ref, h0r_ref, h0i_ref,
                  y_ref, hr_ref, hi_ref, xr_ref, xi_ref):
    t_blk = pl.program_id(2)

    @pl.when(t_blk == 0)
    def _():
        hr_ref[0, 0] = h0r_ref[0, 0]
        hi_ref[0, 0] = h0i_ref[0, 0]

    width = are_ref.shape[2]
    a1 = (jnp.broadcast_to(are_ref[0, 0:1], (SUBLANES, width)), jnp.broadcast_to(aim_ref[0, 0:1], (SUBLANES, width)))
    a2 = _cmul(*a1, *a1)
    a3 = _cmul(*a2, *a1)
    a4 = _cmul(*a2, *a2)
    a5 = _cmul(*a4, *a1)
    a6 = _cmul(*a4, *a2)
    a7 = _cmul(*a4, *a3)
    a8 = _cmul(*a4, *a4)
    row = lax.broadcasted_iota(jnp.int32, (SUBLANES, width), 0)
    lvl = [tuple(jnp.where(row >= s, c, 0.0) for c in a) for s, a in ((1, a1), (2, a2), (4, a4))]
    pw = []
    for comp in range(2):
        acc = a8[comp]
        for s, a in ((6, a7), (5, a6), (4, a5), (3, a4), (2, a3), (1, a2), (0, a1)):
            acc = jnp.where(row == s, a[comp], acc)
        pw.append(acc)

    def scan_chunk(bb_re, bb_im):
        xr_ref[...] = bb_re
        xi_ref[...] = bb_im

        def tile(j, carry):
            cr, ci = carry
            r0 = pl.multiple_of(j * SUBLANES, SUBLANES)
            xr = xr_ref[pl.ds(r0, SUBLANES), :]
            xi = xi_ref[pl.ds(r0, SUBLANES), :]
            for (s, (fr, fi)) in zip((1, 2, 4), lvl):
                sr = pltpu.roll(xr, s, axis=0)
                si = pltpu.roll(xi, s, axis=0)
                pr, pi = _cmul(fr, fi, sr, si)
                xr = xr + pr
                xi = xi + pi
            pr, pi = _cmul(pw[0], pw[1], jnp.broadcast_to(cr, xr.shape), jnp.broadcast_to(ci, xi.shape))
            xr = xr + pr
            xi = xi + pi
            xr_ref[pl.ds(r0, SUBLANES), :] = xr
            xi_ref[pl.ds(r0, SUBLANES), :] = xi
            return xr[SUBLANES - 1:SUBLANES], xi[SUBLANES - 1:SUBLANES]

        cr, ci = lax.fori_loop(0, tc // SUBLANES, tile, (hr_ref[0, 0], hi_ref[0, 0]))
        hr_ref[0, 0] = cr
        hi_ref[0, 0] = ci
        return xr_ref[...], xi_ref[...]

    y, _, _ = _s5_io(u_ref[0], b_ref, c_ref, d_ref, are_ref, aim_ref, scan_chunk)
    y_ref[0] = y


def _s5_step_body(u_ref, b_ref, c_ref, d_ref, are_ref, aim_ref, h0r_ref, h0i_ref, y_ref, hr_ref, hi_ref):
    def one_step(bb_re, bb_im):
        pr, pi = _cmul(are_ref[0, 0:1], aim_ref[0, 0:1], h0r_ref[0], h0i_ref[0])
        return pr + bb_re, pi + bb_im

    y, h_re, h_im = _s5_io(u_ref[...], b_ref, c_ref, d_ref, are_ref, aim_ref, one_step)
    y_ref[...] = y
    hr_ref[0] = h_re
    hi_ref[0] = h_im


def _s5_tables(a_re, a_im, log_step, b_re, b_im, c_re, c_im, d):
    g_c, p_c, ch = b_re.shape
    gs = LANES // ch
    ns = g_c // gs
    dt = jnp.exp(log_step.astype(_f32))[:, None]
    mag = jnp.exp(dt * a_re)
    ab_re = mag * jnp.cos(dt * a_im)
    ab_im = mag * jnp.sin(dt * a_im)
    den = a_re * a_re + a_im * a_im
    nr = ab_re - 1.0
    co_re = (nr * a_re + ab_im * a_im) / den
    co_im = (ab_im * a_re - nr * a_im) / den
    eye = jnp.eye(gs, dtype=_f32)

    def pack_b(b):
        b = b.reshape(ns, gs, p_c, ch)
        return jnp.einsum('sgpc,gh->sgchp', b, eye).reshape(ns, gs * ch, gs * p_c)

    def pack_c(c):
        c = c.reshape(ns, gs, ch, p_c)
        return jnp.einsum('sgcp,gh->sgphc', c, eye).reshape(ns, gs * p_c, gs * ch)

    b_pack = _bf(jnp.concatenate([pack_b(b_re), pack_b(b_im)], axis=2))
    c_pack = _bf(jnp.concatenate([pack_c(c_re), -pack_c(c_im)], axis=1))
    vec = lambda a, c: jnp.stack([a.reshape(ns, gs * p_c), c.reshape(ns, gs * p_c)], axis=1)
    return b_pack, c_pack, d.reshape(ns, 1, gs * ch), vec(ab_re, co_re), vec(ab_im, co_im)


def _s5_scan(u, h0_re, h0_im, tables, tc):
    n, t, d = u.shape
    b_pack, c_pack, dvec, are, aim = tables
    ns, _, w2 = b_pack.shape
    w = w2 // 2
    h0r = h0_re.reshape(n, ns, 1, w)
    h0i = h0_im.reshape(n, ns, 1, w)
    tab = lambda a: pl.BlockSpec((1,) + a.shape[1:], lambda b, s, j: (s, 0, 0))
    st = pl.BlockSpec((1, 1, 1, w), lambda b, s, j: (b, s, 0, 0))
    seq = pl.BlockSpec((1, tc, LANES), lambda b, s, j: (b, j, s))
    y, hr, hi = pl.pallas_call(
        functools.partial(_s5_scan_body, tc),
        grid=(n, ns, t // tc),
        in_specs=[seq, tab(b_pack), tab(c_pack), tab(dvec), tab(are), tab(aim), st, st],
        out_specs=[seq, st, st],
        out_shape=[jax.ShapeDtypeStruct((n, t, d), _bf16),
                   jax.ShapeDtypeStruct((n, ns, 1, w), _f32), jax.ShapeDtypeStruct((n, ns, 1, w), _f32)],
        scratch_shapes=[pltpu.VMEM((tc, w), _f32), pltpu.VMEM((tc, w), _f32)],
        compiler_params=_params("arbitrary", "arbitrary", "arbitrary"),
        name="s5_scan",
    )(u, b_pack, c_pack, dvec, are, aim, h0r, h0i)
    return y, hr.reshape(n, ns * w), hi.reshape(n, ns * w)


def _s5_step(u, h0_re, h0_im, tables):
    b, d = u.shape
    b_pack, c_pack, dvec, are, aim = tables
    ns, _, w2 = b_pack.shape
    w = w2 // 2
    h0r = h0_re.reshape(b, ns, w).transpose(1, 0, 2)
    h0i = h0_im.reshape(b, ns, w).transpose(1, 0, 2)
    tab = lambda a: pl.BlockSpec((1,) + a.shape[1:], lambda s: (s, 0, 0))
    st = pl.BlockSpec((1, b, w), lambda s: (s, 0, 0))
    col = pl.BlockSpec((b, LANES), lambda s: (0, s))
    y, hr, hi = pl.pallas_call(
        _s5_step_body,
        grid=(ns,),
        in_specs=[col, tab(b_pack), tab(c_pack), tab(dvec), tab(are), tab(aim), st, st],
        out_specs=[col, st, st],
        out_shape=[jax.ShapeDtypeStruct((b, d), _bf16),
                   jax.ShapeDtypeStruct((ns, b, w), _f32), jax.ShapeDtypeStruct((ns, b, w), _f32)],
        compiler_params=_params("arbitrary"),
        name="s5_step",
    )(u, b_pack, c_pack, dvec, are, aim, h0r, h0i)
    return y, hr.transpose(1, 0, 2).reshape(b, ns * w), hi.transpose(1, 0, 2).reshape(b, ns * w)


def _glu_out_body(y_ref, x_ref, gate_ref, wl_ref, wr_ref, o_ref):
    y = y_ref[...]
    z = _dot(y, wl_ref[...]) * jax.nn.sigmoid(_dot(y, wr_ref[...]))
    o_ref[...] = x_ref[...] + gate_ref[0] * z


def _glu_out(yg, x, gate, w_glu, tm, rows_per_group):
    m, d = x.shape
    wl = _bf(w_glu[:, :d])
    wr = _bf(w_glu[:, d:])
    gs = _group_spec(tm, rows_per_group, d)
    rowd = pl.BlockSpec((tm, d), lambda i: (i, 0))
    full = lambda a: pl.BlockSpec(a.shape, lambda i: (0, 0))
    return pl.pallas_call(
        _glu_out_body,
        grid=(m // tm,),
        in_specs=[rowd, rowd, gs(gate.shape[1]), full(wl), full(wr)],
        out_specs=rowd,
        out_shape=jax.ShapeDtypeStruct((m, d), _f32),
        compiler_params=_params("arbitrary"),
        name="glu_out",
    )(yg, x, gate, wl, wr)


def _tile(m, target):
    if m <= target:
        return m
    t = target
    while m % t:
        t -= SUBLANES
    return t


def _run_group(x, mods, is_prompt, st, wts):
    n, t, d = x.shape
    m = n * t
    depth = wts["norm_mix"].shape[0]
    d_a = d // 2
    d_b = d // 2
    n_heads = d_a // HEAD_DIM
    rows_per_group = t if is_prompt else m
    tm = _tile(t, 256) if is_prompt else m
    xf = x.reshape(m, d).astype(_f32)

    def mod_vecs(l):
        parts = jnp.split(mods[l], 6, axis=-1)
        if is_prompt:
            return [p.reshape(n, 1, d) for p in parts]
        return [p.reshape(1, m, d) for p in parts]

    outs = {k: [] for k in ("shift", "wkv", "k", "v", "kidx", "re", "im")}
    for l in range(depth):
        sh_m, sc_m, g_m, sh_f, sc_f, g_f = mod_vecs(l)
        i = l // 2
        last = l == depth - 1
        final_g = wts["norm_final"] if last else None
        norm_mix = wts["norm_mix"][l].reshape(1, d)
        norm_ffn = wts["norm_ffn"][l].reshape(1, d)
        if l % 2 == 0:
            p_a, q, k, v, qi, ki, wi = _in_proj(xf, norm_mix, sh_m, sc_m, wts["e_w_in"][i], tm, rows_per_group)
            if is_prompt:
                prev0 = st["shift"][i].reshape(n, 1, -1)
            else:
                prev0 = st["shift"][i].reshape(1, m, -1)
            r, w, k_mod, v_a, av, bv, g, bonus = _rwkv_prep(
                p_a, prev0.astype(_f32), wts["e_mu"][i], wts["e_w0"][i], wts["e_w2"][i], wts["e_a0"][i],
                wts["e_a2"][i], wts["e_g2"][i], wts["e_k_k"][i], wts["e_k_a"][i], wts["e_r_k"][i],
                tm, rows_per_group)
            seq = lambda a: a.reshape(n, t, d_a)
            nb = n if is_prompt else _tile(n, 4)
            tc = _tile(t, 256)
            o, s_fin = _rwkv_scan(seq(r), seq(w), seq(k_mod), seq(v_a), seq(av), seq(bv),
                                  _pack_state(st["wkv"][i].astype(_f32)), nb, tc)
            if is_prompt:
                o_b = _dsa_prompt(q.reshape(n, t, d_b), k.reshape(n, t, d_b), v.reshape(n, t, d_b),
                                  qi.reshape(n, t, -1), ki.reshape(n, t, -1), wi.reshape(n, t, -1),
                                  wts["rel_bias"], 128)
            else:
                o_b = _dsa_sample(q.reshape(n, t, d_b), k.reshape(n, t, d_b), v.reshape(n, t, d_b),
                                  qi.reshape(n, t, -1), ki.reshape(n, t, -1), wi.reshape(n, t, -1),
                                  st["cache_k"], st["cache_v"], st["cache_kidx"], i,
                                  st["page_table"], wts["rel_bias"])
            xf = _mix_out(o.reshape(m, d_a), bonus, g, o_b.reshape(m, d_b), xf, g_m,
                          wts["e_lnx_w"][i], wts["e_lnx_b"][i], wts["e_w_out"][i], tm, rows_per_group)
            outs["shift"].append(p_a.reshape(n, t, -1)[:, -1].astype(x.dtype))
            outs["wkv"].append(_unpack_state(s_fin))
            outs["k"].append(k.reshape(n, t, n_heads, HEAD_DIM))
            outs["v"].append(v.reshape(n, t, n_heads, HEAD_DIM))
            outs["kidx"].append(ki.reshape(n, t, D_IDX))
            tm_f = _tile(t, 1024) if is_prompt else m
            ff = wts["ffn_w2"][i].shape[0]
            xf = _ffn(xf, norm_ffn, sh_f, sc_f, g_f, wts["ffn_w13"][i][None], wts["ffn_w2"][i][None],
                      tm_f, _tile(ff, 256) if ff % 256 == 0 else ff, rows_per_group, final_g=final_g)
        else:
            tables = _s5_tables(wts["o_a_re"][i], wts["o_a_im"][i], wts["o_log_step"][i], wts["o_b_re"][i],
                                wts["o_b_im"][i], wts["o_c_re"][i], wts["o_c_im"][i], wts["o_d"][i])
            u = _norm_mod_call(xf, norm_mix, sh_m, sc_m, tm, rows_per_group)
            h0r = st["ssm_re"][i].reshape(n, -1).astype(_f32)
            h0i = st["ssm_im"][i].reshape(n, -1).astype(_f32)
            if is_prompt:
                yg, hr, hi = _s5_scan(u.reshape(n, t, d), h0r, h0i, tables, _tile(t, 512))
            else:
                yg, hr, hi = _s5_step(u, h0r, h0i, tables)
            xf = _glu_out(yg.reshape(m, d), xf, g_m, wts["o_w_glu"][i], tm, rows_per_group)
            g_c = d // CH_G
            outs["re"].append(hr.reshape(n, g_c, P_C))
            outs["im"].append(hi.reshape(n, g_c, P_C))
            tm_f = _tile(t, 1024) if is_prompt else m
            ff = wts["moe_w2"][i].shape[1]
            moe = _moe_sparse if (is_prompt and tm_f == 1024) else _ffn
            xf = moe(xf, norm_ffn, sh_f, sc_f, g_f, wts["moe_w13"][i], wts["moe_w2"][i],
                     tm_f, _tile(ff, 512) if ff % 512 == 0 else ff, rows_per_group,
                     router=(wts["o_router_w"][i], wts["o_router_b"][i]), final_g=final_g)
    y = xf.reshape(n, t, d).astype(x.dtype)
    return (y, jnp.stack(outs["shift"]), jnp.stack(outs["wkv"]), jnp.stack(outs["k"]), jnp.stack(outs["v"]),
            jnp.stack(outs["kidx"]), jnp.stack(outs["re"]), jnp.stack(outs["im"]))


def kernel(x_prompt, x_sample, cache_k, cache_v, cache_kidx, state_shift, state_wkv, state_ssm_re, state_ssm_im, page_table, c_prompt, c_sample, norm_mix, norm_ffn, ada_w, ada_b, rel_bias, norm_final, e_w_in, e_mu, e_w0, e_w2, e_a0, e_a2, e_g2, e_k_k, e_k_a, e_r_k, e_lnx_w, e_lnx_b, e_w_out, e_ffn_w13, e_ffn_w2, o_a_re, o_a_im, o_log_step, o_b_re, o_b_im, o_c_re, o_c_im, o_d, o_w_glu, o_router_w, o_router_b, o_moe_w13, o_moe_w2):
    nb, _, d = x_prompt.shape
    n_dec = x_sample.shape[0]
    n_even = e_w_in.shape[0]
    n_odd = o_a_re.shape[0]
    d_a = d // 2
    n_heads = d_a // HEAD_DIM
    shift_w = state_shift.shape[-1]

    c_all = jnp.concatenate([c_prompt, c_sample], axis=0)
    pad = (-c_all.shape[0]) % SUBLANES
    mods = _ada(jnp.pad(c_all, ((0, pad), (0, 0))), ada_w, ada_b)
    mods_p = mods[:, :nb]
    mods_s = mods[:, nb:nb + n_dec]

    wts = dict(norm_mix=norm_mix, norm_ffn=norm_ffn, norm_final=norm_final, rel_bias=rel_bias,
               e_w_in=e_w_in, e_mu=e_mu, e_w0=e_w0, e_w2=e_w2, e_a0=e_a0, e_a2=e_a2, e_g2=e_g2, e_k_k=e_k_k,
               e_k_a=e_k_a, e_r_k=e_r_k, e_lnx_w=e_lnx_w, e_lnx_b=e_lnx_b, e_w_out=e_w_out,
               ffn_w13=_bf(e_ffn_w13), ffn_w2=_bf(e_ffn_w2),
               o_a_re=o_a_re, o_a_im=o_a_im, o_log_step=o_log_step, o_b_re=o_b_re, o_b_im=o_b_im,
               o_c_re=o_c_re, o_c_im=o_c_im, o_d=o_d, o_w_glu=o_w_glu, o_router_w=o_router_w,
               o_router_b=o_router_b, moe_w13=_bf(o_moe_w13), moe_w2=_bf(o_moe_w2))

    st_p = dict(shift=jnp.zeros((n_even, nb, shift_w), x_prompt.dtype),
                wkv=jnp.zeros((n_even, nb, n_heads, HEAD_DIM, HEAD_DIM), _f32),
                ssm_re=jnp.zeros((n_odd, nb, d // CH_G, P_C), _f32),
                ssm_im=jnp.zeros((n_odd, nb, d // CH_G, P_C), _f32))
    st_s = dict(shift=state_shift, wkv=state_wkv, ssm_re=state_ssm_re, ssm_im=state_ssm_im,
                cache_k=cache_k, cache_v=cache_v, cache_kidx=cache_kidx, page_table=page_table)
    out_p = _run_group(x_prompt, mods_p, True, st_p, wts)
    out_s = _run_group(x_sample, mods_s, False, st_s, wts)
    return (out_p[0], out_s[0]) + out_p[1:] + out_s[1:]
```
